```python
import math
import jax, jax.numpy as jnp
from jax import lax
import numpy as np

D_MODEL = 1024
BATCH = 16
SEQ = 256
DEPTH = 4
DEC_BATCH = 2
DEC_SEQ = 1024
PAST_LEN = 256

GRID_W = 64
MIX_W = D_MODEL
RET_HEADS = 4
RET_DH = 96
RET_W = RET_HEADS * RET_DH
HY_W = 256
HY_ORDER = 2
ML_HEADS = 4
ML_DH = 96
ML_W = ML_HEADS * ML_DH
IN_W = 4 * RET_W + 3 * HY_W + 4 * ML_W + 2 * 2 * ML_HEADS
SPLIT_POINTS = (4 * RET_W, 4 * RET_W + 3 * HY_W, 4 * RET_W + 3 * HY_W + 4 * ML_W)
D_FF = 4 * D_MODEL
CHUNK = 128
N_BANDS = 16
FEAT_W = 1 + 2 * N_BANDS
FILT_W = 64
HY_SHIFT = 0.05
HY_TARGET = 1e-2
HY_SHORT_DECAY_PCT = 0.3
HY_LONG_DECAY_PCT = 1.5
ROPE_BASE = 10000.0
EPS = 1e-6

kernel_name = "hybrid_diffusion_trunk_step"


def rmsnorm(x, g):
    xf = x.astype(jnp.float32)
    return xf * lax.rsqrt(jnp.mean(xf * xf, axis=-1, keepdims=True) + EPS) * g


def head_norm(x, g):
    xn = x * lax.rsqrt(jnp.mean(x * x, axis=-1, keepdims=True) + EPS)
    return xn.reshape(x.shape[:2] + (-1,)) * g


def dwconv3(x, w, b):
    xpad = jnp.pad(x, ((0, 0), (1, 1), (0, 0)))
    return xpad[:, :-2] * w[0] + xpad[:, 1:-1] * w[1] + xpad[:, 2:] * w[2] + b


def rope_2d(x):
    L, dh = x.shape[1], x.shape[-1]
    rows = L // GRID_W
    row = jnp.repeat(jnp.arange(rows, dtype=jnp.float32), GRID_W)
    col = jnp.tile(jnp.arange(GRID_W, dtype=jnp.float32), rows)
    half = dh // 2
    n_freq = half // 2
    freqs = ROPE_BASE ** (-jnp.arange(n_freq, dtype=jnp.float32) / n_freq)
    ang = jnp.concatenate([row[:, None] * freqs, col[:, None] * freqs], axis=-1)
    cos = jnp.cos(ang)[None, :, None, :]
    sin = jnp.sin(ang)[None, :, None, :]
    x1, x2 = x[..., :half], x[..., half:]
    return jnp.concatenate([x1 * cos - x2 * sin, x2 * cos + x1 * sin], axis=-1)


def _chunks(x):
    B, L, H = x.shape[:3]
    xr = x.reshape((B, L // CHUNK, CHUNK, H) + x.shape[3:])
    return jnp.moveaxis(xr, (1, 3), (0, 2))


def _unchunks(y):
    yr = jnp.moveaxis(y, (0, 2), (1, 3))
    n, c = yr.shape[1], yr.shape[2]
    return yr.reshape((yr.shape[0], n * c) + yr.shape[3:])


def retention_chunked(q, k, v, log_gamma, s0):
    idx = jnp.arange(CHUNK, dtype=jnp.float32)
    rel = idx[:, None] - idx[None, :]
    causal = rel >= 0
    decay_mask = jnp.where(causal[None], jnp.exp(jnp.maximum(rel, 0.0)[None] * log_gamma[:, None, None]), 0.0)
    q_decay = jnp.exp((idx + 1.0)[None, :] * log_gamma[:, None])
    k_decay = jnp.exp((CHUNK - 1.0 - idx)[None, :] * log_gamma[:, None])
    chunk_decay = jnp.exp(CHUNK * log_gamma)

    def step(s, inp):
        qb, kb, vb = inp
        scores = jnp.einsum('bhid,bhjd->bhij', qb, kb) * decay_mask
        intra = jnp.einsum('bhij,bhjd->bhid', scores, vb)
        inter = jnp.einsum('bhid,bhde->bhie', qb, s) * q_decay[None, :, :, None]
        s_new = s * chunk_decay[None, :, None, None] + jnp.einsum('bhjd,bhje->bhde', kb * k_decay[None, :, :, None], vb)
        return s_new, intra + inter

    s_fin, out = lax.scan(step, s0, (_chunks(q), _chunks(k), _chunks(v)))
    return _unchunks(out), s_fin


def mlstm_chunked(q, k, v, i_pre, log_f, c0, n0, m0):
    idx = jnp.arange(CHUNK)
    causal = idx[:, None] >= idx[None, :]

    def step(carry, inp):
        c, nv, m = carry
        qb, kb, vb, ib, fb = inp
        b = jnp.cumsum(fb, axis=-1)
        d_log = jnp.where(causal, b[..., :, None] - b[..., None, :] + ib[..., None, :], -jnp.inf)
        a = b + m[..., None]
        m_t = jnp.maximum(a, jnp.max(d_log, axis=-1))
        w_intra = jnp.exp(d_log - m_t[..., None])
        w_inter = jnp.exp(a - m_t)
        s = jnp.einsum('bhid,bhjd->bhij', qb, kb) * w_intra
        num = jnp.einsum('bhij,bhjd->bhid', s, vb) + w_inter[..., None] * jnp.einsum('bhid,bhde->bhie', qb, c)
        den = jnp.sum(s, axis=-1) + w_inter * jnp.einsum('bhid,bhd->bhi', qb, nv)
        h = num / jnp.maximum(jnp.abs(den), jnp.exp(-m_t))[..., None]
        b_last = b[..., -1]
        g_log = b_last[..., None] - b + ib
        m_new = jnp.maximum(b_last + m, jnp.max(g_log, axis=-1))
        w_prev = jnp.exp(b_last + m - m_new)
        kw = kb * jnp.exp(g_log - m_new[..., None])[..., None]
        c_new = w_prev[..., None, None] * c + jnp.einsum('bhjd,bhje->bhde', kw, vb)
        n_new = w_prev[..., None] * nv + jnp.sum(kw, axis=-2)
        return (c_new, n_new, m_new), h

    fin, out = lax.scan(step, (c0, n0, m0), (_chunks(q), _chunks(k), _chunks(v), _chunks(i_pre), _chunks(log_f)))
    return _unchunks(out), fin


def hyena_filter_spectra(L, lp):
    tn = jnp.arange(L, dtype=jnp.float32) / L
    bands = jnp.linspace(1e-4, N_BANDS - 1, N_BANDS, dtype=jnp.float32)
    ang = 2.0 * math.pi * tn[:, None] * bands[None, :]
    feat = jnp.concatenate([tn[:, None], jnp.cos(ang), jnp.sin(ang)], axis=-1)
    fr = lp['hy_sin_freq']
    hdn = jnp.sin(fr * (feat @ lp['hy_f_w1'] + lp['hy_f_b1']))
    hdn = jnp.sin(fr * (hdn @ lp['hy_f_w2'] + lp['hy_f_b2']))
    filt = (hdn @ lp['hy_f_w3'] + lp['hy_f_b3']).astype(jnp.float32).reshape(L, 2, HY_ORDER, HY_W)
    deltas = jnp.abs(jnp.linspace(math.log(HY_TARGET) / HY_LONG_DECAY_PCT,
                                  math.log(HY_TARGET) / HY_SHORT_DECAY_PCT, HY_W, dtype=jnp.float32))
    window = jnp.exp(-tn[:, None] * deltas[None, :]) + HY_SHIFT
    filt = filt * window[:, None, None, :]
    fwd, bwd = filt[:, 0], filt[:, 1]
    full = jnp.concatenate([fwd, jnp.zeros((1, HY_ORDER, HY_W), jnp.float32), bwd[:0:-1]], axis=0)
    return jnp.fft.rfft(full, axis=0)


def long_conv(z, kf):
    L = z.shape[1]
    zf = jnp.fft.rfft(z, n=2 * L, axis=1)
    return jnp.fft.irfft(zf * kf[None], n=2 * L, axis=1)[:, :L]


def mixer(h, lp, init, latent):
    B, L, _ = h.shape
    s_ret0, c0, n0, m0 = [t.astype(jnp.float32) for t in init]
    proj = jnp.einsum('bld,de->ble', h, lp['w_in']).astype(jnp.float32)
    r_part, hy_part, ml_part, g_part = jnp.split(proj, SPLIT_POINTS, axis=-1)

    rq, rk, rv, rg = jnp.split(r_part, 4, axis=-1)
    rq = rq.reshape(B, L, RET_HEADS, RET_DH)
    rk = rk.reshape(B, L, RET_HEADS, RET_DH)
    rv = rv.reshape(B, L, RET_HEADS, RET_DH)
    if latent:
        rq, rk = rope_2d(rq), rope_2d(rk)
    rk = rk * RET_DH ** -0.5
    log_gamma = jax.nn.log_sigmoid(lp['ret_decay_logit'].astype(jnp.float32))
    o_f, s_f = retention_chunked(rq, rk, rv, log_gamma[0], s_ret0[:, 0])
    o_b, s_b = retention_chunked(rq[:, ::-1], rk[:, ::-1], rv[:, ::-1], log_gamma[1], s_ret0[:, 1])
    ret = head_norm(o_f + o_b[:, ::-1], lp['ret_norm_g']) * jax.nn.silu(rg)

    hy = dwconv3(hy_part, lp['hy_conv_w'], lp['hy_conv_b'])
    hv, hx1, hx2 = jnp.split(hy, 3, axis=-1)
    kf = hyena_filter_spectra(L, lp)
    z = hx1 * (long_conv(hv, kf[:, 0]) + lp['hy_bias'][0] * hv)
    hy_out = hx2 * (long_conv(z, kf[:, 1]) + lp['hy_bias'][1] * z)

    mq, mk, mv, mo = jnp.split(ml_part, 4, axis=-1)
    mq = mq.reshape(B, L, ML_HEADS, ML_DH)
    mk = mk.reshape(B, L, ML_HEADS, ML_DH) * ML_DH ** -0.5
    mv = mv.reshape(B, L, ML_HEADS, ML_DH)
    gates = g_part.reshape(B, L, 2, 2, ML_HEADS) + lp['ml_gate_bias']
    i_pre = gates[:, :, :, 0]
    log_f = jax.nn.log_sigmoid(gates[:, :, :, 1])
    h_f, st_f = mlstm_chunked(mq, mk, mv, i_pre[:, :, 0], log_f[:, :, 0], c0[:, 0], n0[:, 0], m0[:, 0])
    h_b, st_b = mlstm_chunked(mq[:, ::-1], mk[:, ::-1], mv[:, ::-1], i_pre[:, ::-1, 1], log_f[:, ::-1, 1],
                              c0[:, 1], n0[:, 1], m0[:, 1])
    ml = head_norm(h_f + h_b[:, ::-1], lp['ml_norm_g']) * jax.nn.sigmoid(mo)

    mixed = jnp.concatenate([ret, hy_out, ml], axis=-1)
    out = jnp.einsum('ble,ed->bld', mixed, lp['w_out'])
    states = (jnp.stack([s_f, s_b], axis=1), jnp.stack([st_f[0], st_b[0]], axis=1),
              jnp.stack([st_f[1], st_b[1]], axis=1), jnp.stack([st_f[2], st_b[2]], axis=1))
    return out, states


def conv_ffn(h, lp):
    up = jnp.einsum('bld,df->blf', h, lp['w_up'])
    a, b = jnp.split(up, 2, axis=-1)
    a = dwconv3(a, lp['ffn_conv_w'], lp['ffn_conv_b'])
    return jnp.einsum('blf,fd->bld', jax.nn.gelu(a, approximate=True) * b, lp['w_down'])


def trunk_layer(x, cvec, lp, init, latent):
    mod = jax.nn.silu(cvec.astype(jnp.float32)) @ lp['w_mod'] + lp['b_mod']
    sh1, sc1, g1, sh2, sc2, g2 = jnp.split(mod[:, None, :], 6, axis=-1)
    h = rmsnorm(x, lp['norm_mix_pre']) * (1.0 + sc1) + sh1
    mix, states = mixer(h, lp, init, latent)
    x1 = x + g1 * rmsnorm(mix, lp['norm_mix_post'])
    h = rmsnorm(x1, lp['norm_ffn_pre']) * (1.0 + sc2) + sh2
    out = x1 + g2 * rmsnorm(conv_ffn(h, lp), lp['norm_ffn_post'])
    return out.astype(x.dtype), states


def setup_inputs(seed: int = 0) -> dict:
    key = jax.random.key(seed)
    keys = jax.random.split(key, 48)
    counter = [0]

    def nrm(shape, scale):
        k = keys[counter[0]]
        counter[0] += 1
        return jax.random.normal(k, shape, jnp.float32) * scale

    D = D_MODEL
    gain = lambda shape: 1.0 + nrm(shape, 0.05)
    ret_logit = jnp.log(2.0 ** (5.0 + jnp.arange(RET_HEADS, dtype=jnp.float32)) - 1.0)
    gate_base = jnp.stack([jnp.zeros((ML_HEADS,), jnp.float32), jnp.linspace(3.0, 6.0, ML_HEADS, dtype=jnp.float32)])
    return {
        'x_prompt': nrm((BATCH, SEQ, D), 1.0),
        'x_sample': nrm((DEC_BATCH, DEC_SEQ, D), 1.0),
        'c': nrm((DEC_BATCH, D), 1.0),
        'state_ret': nrm((DEC_BATCH, DEPTH, 2, RET_HEADS, RET_DH, RET_DH), 0.5),
        'state_mlstm_c': nrm((DEC_BATCH, DEPTH, 2, ML_HEADS, ML_DH, ML_DH), 0.5),
        'state_mlstm_n': nrm((DEC_BATCH, DEPTH, 2, ML_HEADS, ML_DH), 0.5),
        'state_mlstm_m': nrm((DEC_BATCH, DEPTH, 2, ML_HEADS), 0.5),
        'c_ctx': nrm((D,), 1.0),
        'norm_mix_pre': gain((DEPTH, D)),
        'norm_mix_post': gain((DEPTH, D)),
        'norm_ffn_pre': gain((DEPTH, D)),
        'norm_ffn_post': gain((DEPTH, D)),
        'w_mod': nrm((DEPTH, D, 6 * D), 0.3 * D ** -0.5),
        'b_mod': nrm((DEPTH, 6 * D), 0.02),
        'w_in': nrm((DEPTH, D, IN_W), D ** -0.5),
        'w_out': nrm((DEPTH, MIX_W, D), MIX_W ** -0.5),
        'ret_decay_logit': jnp.broadcast_to(ret_logit, (DEPTH, 2, RET_HEADS)) + nrm((DEPTH, 2, RET_HEADS), 0.1),
        'ret_norm_g': gain((DEPTH, RET_W)),
        'hy_conv_w': nrm((DEPTH, 3, 3 * HY_W), 0.5),
        'hy_conv_b': nrm((DEPTH, 3 * HY_W), 0.02),
        'hy_f_w1': nrm((DEPTH, FEAT_W, FILT_W), FEAT_W ** -0.5),
        'hy_f_b1': nrm((DEPTH, FILT_W), 0.1),
        'hy_f_w2': nrm((DEPTH, FILT_W, FILT_W), FILT_W ** -0.5),
        'hy_f_b2': nrm((DEPTH, FILT_W), 0.1),
        'hy_f_w3': nrm((DEPTH, FILT_W, 2 * HY_ORDER * HY_W), 0.1 * FILT_W ** -0.5),
        'hy_f_b3': nrm((DEPTH, 2 * HY_ORDER * HY_W), 0.01),
        'hy_sin_freq': 1.0 + nrm((DEPTH, FILT_W), 0.1),
        'hy_bias': nrm((DEPTH, HY_ORDER, HY_W), 0.5),
        'ml_gate_bias': jnp.broadcast_to(gate_base, (DEPTH, 2, 2, ML_HEADS)) + nrm((DEPTH, 2, 2, ML_HEADS), 0.1),
        'ml_norm_g': gain((DEPTH, ML_W)),
        'w_up': nrm((DEPTH, D, 2 * D_FF), D ** -0.5),
        'ffn_conv_w': nrm((DEPTH, 3, D_FF), 0.5),
        'ffn_conv_b': nrm((DEPTH, D_FF), 0.02),
        'w_down': nrm((DEPTH, D_FF, D), D_FF ** -0.5),
    }


def reference(x_prompt, x_sample, c, state_ret, state_mlstm_c, state_mlstm_n, state_mlstm_m, c_ctx,
              norm_mix_pre, norm_mix_post, norm_ffn_pre, norm_ffn_post, w_mod, b_mod, w_in, w_out,
              ret_decay_logit, ret_norm_g, hy_conv_w, hy_conv_b, hy_f_w1, hy_f_b1, hy_f_w2, hy_f_b2,
              hy_f_w3, hy_f_b3, hy_sin_freq, hy_bias, ml_gate_bias, ml_norm_g,
              w_up, ffn_conv_w, ffn_conv_b, w_down):
    f32 = jnp.float32
    bp = x_prompt.shape[0]
    c_prompt = jnp.broadcast_to(c_ctx[None, :], (bp, D_MODEL))
    zero_init = (jnp.zeros((bp, 2, RET_HEADS, RET_DH, RET_DH), f32),
                 jnp.zeros((bp, 2, ML_HEADS, ML_DH, ML_DH), f32),
                 jnp.zeros((bp, 2, ML_HEADS, ML_DH), f32),
                 jnp.zeros((bp, 2, ML_HEADS), f32))
    xp, xs = x_prompt, x_sample
    new_ret, new_c, new_n, new_m = [], [], [], []
    for l in range(DEPTH):
        lp = {
            'norm_mix_pre': norm_mix_pre[l], 'norm_mix_post': norm_mix_post[l],
            'norm_ffn_pre': norm_ffn_pre[l], 'norm_ffn_post': norm_ffn_post[l],
            'w_mod': w_mod[l], 'b_mod': b_mod[l], 'w_in': w_in[l], 'w_out': w_out[l],
            'ret_decay_logit': ret_decay_logit[l], 'ret_norm_g': ret_norm_g[l],
            'hy_conv_w': hy_conv_w[l], 'hy_conv_b': hy_conv_b[l],
            'hy_f_w1': hy_f_w1[l], 'hy_f_b1': hy_f_b1[l], 'hy_f_w2': hy_f_w2[l], 'hy_f_b2': hy_f_b2[l],
            'hy_f_w3': hy_f_w3[l], 'hy_f_b3': hy_f_b3[l], 'hy_sin_freq': hy_sin_freq[l], 'hy_bias': hy_bias[l],
            'ml_gate_bias': ml_gate_bias[l], 'ml_norm_g': ml_norm_g[l],
            'w_up': w_up[l], 'ffn_conv_w': ffn_conv_w[l], 'ffn_conv_b': ffn_conv_b[l], 'w_down': w_down[l],
        }
        xp, st = trunk_layer(xp, c_prompt, lp, zero_init, False)
        new_ret.append(st[0])
        new_c.append(st[1])
        new_n.append(st[2])
        new_m.append(st[3])
        cache_init = (state_ret[:, l], state_mlstm_c[:, l], state_mlstm_n[:, l], state_mlstm_m[:, l])
        xs, _ = trunk_layer(xs, c, lp, cache_init, True)
    return (xp, xs, jnp.stack(new_ret, axis=1), jnp.stack(new_c, axis=1), jnp.stack(new_n, axis=1), jnp.stack(new_m, axis=1))
```

```python
import functools
import math

import numpy as np
import jax
import jax.numpy as jnp
from jax import lax
from jax.experimental import pallas as pl
from jax.experimental.pallas import tpu as pltpu

D_MODEL = 1024
DEPTH = 4
GRID_W = 64
HEADS = 4
DH = 96
DHP = 128
HW = HEADS * DH
HWP = HEADS * DHP
HY_W = 256
D_FF = 4 * D_MODEL
CHUNK = 128
N_BANDS = 16
FEAT_W = 1 + 2 * N_BANDS
HY_SHIFT = 0.05
HY_TARGET = 1e-2
HY_SHORT_DECAY_PCT = 0.3
HY_LONG_DECAY_PCT = 1.5
ROPE_BASE = 10000.0
EPS = 1e-6
NORM_LANE = DH

MXU_DT = jnp.bfloat16
F32 = jnp.float32
VMEM_LIMIT = 56 * 1024 * 1024


def _dot(a, b):
    return jnp.dot(a.astype(MXU_DT), b.astype(MXU_DT), preferred_element_type=F32)


def _dot_nt(a, b):
    return lax.dot_general(a.astype(MXU_DT), b.astype(MXU_DT), (((1,), (1,)), ((), ())),
                           preferred_element_type=F32)


def _dot_tn(a, b):
    return lax.dot_general(a.astype(MXU_DT), b.astype(MXU_DT), (((0,), (0,)), ((), ())),
                           preferred_element_type=F32)


def _split2(x):
    hi = x.astype(MXU_DT)
    lo = (x - hi.astype(F32)).astype(MXU_DT)
    return hi, lo


def _dot_split(a_hi, a_lo, b):
    b_hi, b_lo = _split2(b)
    return (jnp.dot(a_hi, b_hi, preferred_element_type=F32)
            + jnp.dot(a_hi, b_lo, preferred_element_type=F32)
            + jnp.dot(a_lo, b_hi, preferred_element_type=F32))


def _dot3(a, b):
    a_hi, a_lo = _split2(a)
    return _dot_split(a_hi, a_lo, b)


def _dot_exact_lhs(t, x):
    x1 = x.astype(MXU_DT)
    r1 = x - x1.astype(F32)
    x2 = r1.astype(MXU_DT)
    x3 = (r1 - x2.astype(F32)).astype(MXU_DT)
    return (jnp.dot(t, x1, preferred_element_type=F32) + jnp.dot(t, x2, preferred_element_type=F32)
            + jnp.dot(t, x3, preferred_element_type=F32))


def _rms(x, n):
    return x * lax.rsqrt(jnp.sum(x * x, axis=-1, keepdims=True) * (1.0 / n) + EPS)


def _log_sigmoid(x):
    return jnp.minimum(x, 0.0) - jnp.log1p(jnp.exp(-jnp.abs(x)))


def _resident(shape, index_map):
    return pl.BlockSpec(shape, index_map, pipeline_mode=pl.Buffered(1))


def _params(**kw):
    return pltpu.CompilerParams(vmem_limit_bytes=VMEM_LIMIT, **kw)


@functools.lru_cache(maxsize=None)
def _dft_tables(L):
    f = np.arange(L, dtype=np.int64)[:, None]
    s = np.arange(L, dtype=np.int64)[None, :]
    ang = np.pi * ((f * s) % (2 * L)).astype(np.float64) / L
    fr = np.cos(ang)
    fi = -np.sin(ang)
    fi[0, :] = np.where(np.arange(L) % 2 == 0, 1.0, -1.0)
    fwd = np.concatenate([fr, fi], axis=0)
    gr = np.cos(ang.T) * (2.0 / (2 * L))
    gr[:, 0] = 1.0 / (2 * L)
    gi = -np.sin(ang.T) * (2.0 / (2 * L))
    gi[:, 0] = np.where(np.arange(L) % 2 == 0, 1.0, -1.0) / (2 * L)
    inv = np.concatenate([gr, gi], axis=1)

    def split(m):
        m32 = m.astype(np.float32)
        hi = m32.astype(jnp.bfloat16)
        lo = (m32 - hi.astype(np.float32)).astype(jnp.bfloat16)
        return hi, lo

    return split(fwd) + split(inv)


@functools.lru_cache(maxsize=None)
def _filter_tables(L):
    tn = np.arange(L, dtype=np.float64) / L
    bands = np.linspace(1e-4, N_BANDS - 1, N_BANDS)
    ang = 2.0 * math.pi * tn[:, None] * bands[None, :]
    feat = np.zeros((L, 128), np.float32)
    feat[:, 0] = tn
    feat[:, 1:1 + N_BANDS] = np.cos(ang)
    feat[:, 1 + N_BANDS:FEAT_W] = np.sin(ang)
    deltas = np.abs(np.linspace(math.log(HY_TARGET) / HY_LONG_DECAY_PCT,
                                math.log(HY_TARGET) / HY_SHORT_DECAY_PCT, HY_W))
    window = (np.exp(-tn[:, None] * deltas[None, :]) + HY_SHIFT).astype(np.float32)
    return feat, window


@functools.lru_cache(maxsize=None)
def _rope_tables(L):
    rows = L // GRID_W
    row = np.repeat(np.arange(rows, dtype=np.float64), GRID_W)
    col = np.tile(np.arange(GRID_W, dtype=np.float64), rows)
    half = DH // 2
    n_freq = half // 2
    freqs = ROPE_BASE ** (-np.arange(n_freq, dtype=np.float64) / n_freq)
    ang = np.concatenate([row[:, None] * freqs, col[:, None] * freqs], axis=-1)
    cos = np.zeros((L, DHP), np.float32)
    sin = np.zeros((L, DHP), np.float32)
    cos[:, :half] = np.cos(ang)
    cos[:, 64:64 + half] = np.cos(ang)
    sin[:, :half] = -np.sin(ang)
    sin[:, 64:64 + half] = np.sin(ang)
    return cos, sin


@functools.lru_cache(maxsize=None)
def _tri_tables():
    i = np.arange(CHUNK)
    lower = (i[:, None] >= i[None, :]).astype(np.float32)
    upper = (i[:, None] <= i[None, :]).astype(np.float32)
    return lower, upper


def _pad_heads(w, axis=-1, rope=False):
    axis = axis % w.ndim
    parts = []

    def zeros(n):
        shp = list(w.shape)
        shp[axis] = n
        return jnp.zeros(shp, w.dtype)

    for h in range(HEADS):
        blk = lax.slice_in_dim(w, h * DH, (h + 1) * DH, axis=axis)
        if rope:
            half = DH // 2
            parts += [lax.slice_in_dim(blk, 0, half, axis=axis), zeros(64 - half),
                      lax.slice_in_dim(blk, half, DH, axis=axis), zeros(64 - half)]
        else:
            parts += [blk, zeros(DHP - DH)]
    return jnp.concatenate(parts, axis=axis)


def _unpad_heads_axis(x, axis, rope=False):
    if rope:
        half = DH // 2
        return jnp.concatenate([lax.slice_in_dim(x, 0, half, axis=axis),
                                lax.slice_in_dim(x, 64, 64 + half, axis=axis)], axis=axis)
    return lax.slice_in_dim(x, 0, DH, axis=axis)


def _mod_body(c_ref, w_ref, b_ref, o_ref):
    c = c_ref[...]
    s = c * (1.0 / (1.0 + jnp.exp(-c)))
    o_ref[...] = _dot(s, w_ref[...]) + b_ref[...]


def _mod_all(cvec8, w_mod, b_mod):
    tn = 1536
    nj = 6 * D_MODEL // tn
    return pl.pallas_call(
        _mod_body,
        grid=(DEPTH, nj),
        in_specs=[pl.BlockSpec((8, D_MODEL), lambda l, j: (0, 0)),
                  pl.BlockSpec((None, D_MODEL, tn), lambda l, j: (l, 0, j)),
                  pl.BlockSpec((None, 1, tn), lambda l, j: (l, 0, j))],
        out_specs=pl.BlockSpec((None, 8, tn), lambda l, j: (l, 0, j)),
        out_shape=jax.ShapeDtypeStruct((DEPTH, 8, 6 * D_MODEL), F32),
        compiler_params=_params(),
        name="mod_all",
    )(cvec8, w_mod, b_mod.reshape(DEPTH, 1, 6 * D_MODEL))


def _filt_body(feat_ref, win_ref, fh_ref, fl_ref, w1_ref, b1_ref, w2_ref, b2_ref, w3_ref, b3_ref,
               fr_ref, hb_ref, p_ref, q_ref, p2_ref, *, L):
    fr = fr_ref[...]
    h = jnp.sin(fr * (_dot3(feat_ref[...], w1_ref[...]) + b1_ref[...]))
    h = jnp.sin(fr * (_dot3(h, w2_ref[...]) + b2_ref[...]))
    filt = _dot3(h, w3_ref[...]) + b3_ref[...]
    win = win_ref[...]
    row = lax.broadcasted_iota(jnp.int32, (L, 1), 0)
    fh = fh_ref[...]
    fl = fl_ref[...]
    for o in range(2):
        fwd = filt[:, o * HY_W:(o + 1) * HY_W] * win
        bwd = filt[:, (2 + o) * HY_W:(3 + o) * HY_W] * win
        bwd = jnp.where(row == 0, 0.0, bwd)
        a = _dot_split(fh, fl, fwd)
        b = _dot_split(fh, fl, bwd)
        kr = a[:L] + b[:L]
        ki = a[L:] - b[L:]
        nyq = a[L:L + 1] + b[L:L + 1]
        bias = hb_ref[o:o + 1, :]
        p_ref[o] = kr + bias
        q_ref[o] = jnp.where(row == 0, 0.0, ki)
        p2_ref[o] = jnp.where(row == 0, nyq, kr) + bias


def _filter_spectra(L, fw1, fb1, fw2, fb2, fw3, fb3, ffr, hy_bias):
    feat, window = _filter_tables(L)
    fh, fl, _, _ = _dft_tables(L)
    const = lambda shape: _resident(shape, lambda l: (0,) * len(shape))
    per_layer = lambda shape: pl.BlockSpec((None,) + shape, lambda l: (l,) + (0,) * len(shape))
    out_spec = per_layer((2, L, HY_W))
    out_shape = jax.ShapeDtypeStruct((DEPTH, 2, L, HY_W), F32)
    return pl.pallas_call(
        functools.partial(_filt_body, L=L),
        grid=(DEPTH,),
        in_specs=[const((L, 128)), const((L, HY_W)), const((2 * L, L)), const((2 * L, L)),
                  per_layer((128, 128)), per_layer((1, 128)), per_layer((128, 128)), per_layer((1, 128)),
                  per_layer((128, 4 * HY_W)), per_layer((1, 4 * HY_W)), per_layer((1, 128)),
                  per_layer((2, HY_W))],
        out_specs=[out_spec, out_spec, out_spec],
        out_shape=[out_shape, out_shape, out_shape],
        compiler_params=_params(),
        name="hyena_filter_%d" % L,
    )(feat, window, fh, fl, fw1, fb1, fw2, fb2, fw3, fb3, ffr, hy_bias)


W_R0, W_H0, W_M0, W_G0, W_END = 0, 4 * HWP, 4 * HWP + 3 * HY_W, 8 * HWP + 3 * HY_W, 8 * HWP + 3 * HY_W + 128


def _in_body(x_ref, sc_ref, sh_ref, g_ref, w_ref, pr_ref, ph_ref, pm_ref, pg_ref):
    h = _rms(x_ref[...], D_MODEL) * g_ref[...]
    h = (h * (1.0 + sc_ref[...]) + sh_ref[...]).astype(MXU_DT)
    pr_ref[...] = jnp.dot(h, w_ref[:, W_R0:W_H0], preferred_element_type=F32)
    ph_ref[...] = jnp.dot(h, w_ref[:, W_H0:W_M0], preferred_element_type=F32)
    pm_ref[...] = jnp.dot(h, w_ref[:, W_M0:W_G0], preferred_element_type=F32)
    pg_ref[...] = jnp.dot(h, w_ref[:, W_G0:W_END], preferred_element_type=F32)


def _mod_spec(l, which, row_of_tile):
    return pl.BlockSpec((None, None, None, 1, D_MODEL), lambda i: (l, which, row_of_tile(i), 0, 0))


def _in_proj(x, mod, gain, w_in, l, row_of_tile, tm):
    n = x.shape[0]
    shapes = [4 * HWP, 3 * HY_W, 4 * HWP, 128]
    return pl.pallas_call(
        _in_body,
        grid=(n // tm,),
        in_specs=[pl.BlockSpec((tm, D_MODEL), lambda i: (i, 0)),
                  _mod_spec(l, 1, row_of_tile), _mod_spec(l, 0, row_of_tile),
                  pl.BlockSpec((None, 1, D_MODEL), lambda i: (l, 0, 0)),
                  _resident((None, D_MODEL, W_END), lambda i: (l, 0, 0))],
        out_specs=[pl.BlockSpec((tm, w), lambda i: (i, 0)) for w in shapes],
        out_shape=[jax.ShapeDtypeStruct((n, w), F32) for w in shapes],
        compiler_params=_params(),
        name="in_proj",
    )(x, mod, mod, gain, w_in)


def _ret_body(*refs, L, latent):
    if latent:
        (q_ref, k_ref, v_ref, gate_ref, dl_ref, gain_ref, s0_ref, cos_ref, sin_ref,
         o_ref, qs, ks, acc) = refs
    else:
        (q_ref, k_ref, v_ref, gate_ref, dl_ref, gain_ref, o_ref, sfin_ref, qs, ks, acc) = refs
    C = CHUNK
    nchunk = L // C
    lg = _log_sigmoid(dl_ref[...])
    rel = (lax.broadcasted_iota(jnp.int32, (C, C), 0) - lax.broadcasted_iota(jnp.int32, (C, C), 1)).astype(F32)
    ri = lax.broadcasted_iota(jnp.int32, (C, 1), 0).astype(F32)

    for h in range(HEADS):
        cols = slice(h * DHP, (h + 1) * DHP)
        q = q_ref[:, cols]
        k = k_ref[:, cols]
        if latent:
            cos = cos_ref[...]
            sin = sin_ref[...]
            q = q * cos + pltpu.roll(q, 64, 1) * sin
            k = k * cos + pltpu.roll(k, 64, 1) * sin
        qs[:, cols] = q
        ks[:, cols] = k * (DH ** -0.5)

    for h in range(HEADS):
        cols = slice(h * DHP, (h + 1) * DHP)
        lgf = lg[h:h + 1, :]
        lgb = lg[HEADS + h:HEADS + h + 1, :]
        lgf1 = lgf[:, 0:1]
        lgb1 = lgb[:, 0:1]
        mask = (jnp.where(rel >= 0, jnp.exp(rel * lgf), 0.0)
                + jnp.where(rel <= 0, jnp.exp(-rel * lgb), 0.0))
        qdec_f = jnp.exp((ri + 1.0) * lgf1)
        qdec_b = jnp.exp((C - ri) * lgb1)
        kdec_f = jnp.exp((C - 1.0 - ri) * lgf1)
        kdec_b = jnp.exp(ri * lgb1)
        cdec_f = jnp.exp(C * lgf1)
        cdec_b = jnp.exp(C * lgb1)
        if latent:
            s_f0 = s0_ref[0, h]
            s_b0 = s0_ref[1, h]
        else:
            s_f0 = jnp.zeros((DHP, DHP), F32)
            s_b0 = jnp.zeros((DHP, DHP), F32)

        def fwd_step(c, s_f):
            rows = pl.ds(pl.multiple_of(c * C, C), C)
            q = qs[rows, cols]
            k = ks[rows, cols]
            v = v_ref[rows, cols]
            sc = _dot_nt(q, k) * mask
            acc[rows, cols] = _dot(sc, v) + _dot(q, s_f) * qdec_f
            return s_f * cdec_f + _dot_tn(k * kdec_f, v)

        s_f = lax.fori_loop(0, nchunk, fwd_step, s_f0)

        def bwd_step(j, s_b):
            c = nchunk - 1 - j
            rows = pl.ds(pl.multiple_of(c * C, C), C)
            q = qs[rows, cols]
            k = ks[rows, cols]
            v = v_ref[rows, cols]
            acc[rows, cols] += _dot(q, s_b) * qdec_b
            return s_b * cdec_b + _dot_tn(k * kdec_b, v)

        s_b = lax.fori_loop(0, nchunk, bwd_step, s_b0)
        if not latent:
            sfin_ref[0, h] = s_f
            sfin_ref[1, h] = s_b

    for h in range(HEADS):
        cols = slice(h * DHP, (h + 1) * DHP)
        g = gate_ref[:, cols]
        y = _rms(acc[:, cols], DH) * gain_ref[:, cols] * (g * (1.0 / (1.0 + jnp.exp(-g))))
        o_ref[:, cols] = y.astype(o_ref.dtype)


def _retention(pr, dl, gain, l, B, L, latent, s0=None):
    blk = lambda j: pl.BlockSpec((L, HWP), lambda b, j=j: (b, j))
    in_specs = [blk(0), blk(1), blk(2), blk(3),
                pl.BlockSpec((None, 8, 128), lambda b: (l, 0, 0)),
                pl.BlockSpec((None, 1, HWP), lambda b: (l, 0, 0))]
    args = [pr, pr, pr, pr, dl, gain]
    out_specs = [pl.BlockSpec((L, HWP), lambda b: (b, 0))]
    out_shape = [jax.ShapeDtypeStruct((B * L, HWP), MXU_DT)]
    if latent:
        cos, sin = _rope_tables(L)
        in_specs += [pl.BlockSpec((None, None, 2, HEADS, DHP, DHP), lambda b: (b, l, 0, 0, 0, 0)),
                     pl.BlockSpec((L, DHP), lambda b: (0, 0)), pl.BlockSpec((L, DHP), lambda b: (0, 0))]
        args += [s0, cos, sin]
    else:
        out_specs.append(pl.BlockSpec((None, 2, HEADS, DHP, DHP), lambda b: (b, 0, 0, 0, 0)))
        out_shape.append(jax.ShapeDtypeStruct((B, 2, HEADS, DHP, DHP), F32))
    return pl.pallas_call(
        functools.partial(_ret_body, L=L, latent=latent),
        grid=(B,),
        in_specs=in_specs, out_specs=out_specs, out_shape=out_shape,
        scratch_shapes=[pltpu.VMEM((L, HWP), F32), pltpu.VMEM((L, HWP), F32), pltpu.VMEM((L, HWP), F32)],
        compiler_params=_params(),
        name="retention_%d" % L,
    )(*args)


def _ml_body(*refs, L, latent):
    if latent:
        (q_ref, k_ref, v_ref, og_ref, gates_ref, gb_ref, gain_ref, tl_ref, tu_ref, c0_ref, m0_ref,
         o_ref, acc, bcol_s, ibcol_s, brow_s, ibrow_s) = refs
    else:
        (q_ref, k_ref, v_ref, og_ref, gates_ref, gb_ref, gain_ref, tl_ref, tu_ref,
         o_ref, cfin_ref, mfin_ref, acc, bcol_s, ibcol_s, brow_s, ibrow_s) = refs
    C = CHUNK
    nchunk = L // C
    lane = lax.broadcasted_iota(jnp.int32, (C, DHP), 1)
    tri_r = lax.broadcasted_iota(jnp.int32, (C, C), 0)
    tri_c = lax.broadcasted_iota(jnp.int32, (C, C), 1)
    tl = tl_ref[...]
    tu = tu_ref[...]

    for c in range(nchunk):
        rows = slice(c * C, (c + 1) * C)
        g = gates_ref[rows, :] + gb_ref[...]
        lf = _log_sigmoid(g)
        pre = _dot_exact_lhs(tl, lf)
        suf = _dot_exact_lhs(tu, lf)
        bc = jnp.where(lane < 8, pre, suf)
        bcol_s[rows, :] = bc
        ibcol_s[rows, :] = g
        brow_s[c * 16:(c + 1) * 16, :] = bc.T[0:16, :]
        ibrow_s[c * 16:(c + 1) * 16, :] = g.T[0:16, :]

    acc[...] = jnp.zeros_like(acc)
    for h in range(HEADS):
        cols = slice(h * DHP, (h + 1) * DHP)
        for d in range(2):
            ic = d * 8 + h
            fc = d * 8 + 4 + h
            causal = (tri_r >= tri_c) if d == 0 else (tri_r <= tri_c)
            if latent:
                c_init = c0_ref[d, h]
                m_init = m0_ref[d * HEADS + h:d * HEADS + h + 1, 0:1]
            else:
                c_init = jnp.zeros((DHP, DHP), F32)
                m_init = jnp.zeros((1, 1), F32)

            def step(j, carry, d=d, ic=ic, fc=fc, causal=causal, cols=cols):
                cst, m = carry
                c = j if d == 0 else nchunk - 1 - j
                rows = pl.ds(pl.multiple_of(c * C, C), C)
                r16 = pl.ds(pl.multiple_of(c * 16, 16), 16)
                q = q_ref[rows, cols]
                k = k_ref[rows, cols] * (DH ** -0.5)
                v = jnp.where(lane == NORM_LANE, 1.0, v_ref[rows, cols])
                bcs = bcol_s[rows, :]
                ibs = ibcol_s[rows, :]
                brs = brow_s[r16, :]
                ibrs = ibrow_s[r16, :]
                bcol = bcs[:, fc:fc + 1]
                ibcol = ibs[:, ic:ic + 1]
                brow = brs[fc:fc + 1, :]
                ibrow = ibrs[ic:ic + 1, :]
                dlog = jnp.where(causal, bcol - brow + ibrow, -jnp.inf)
                a = bcol + m
                m_t = jnp.maximum(a, jnp.max(dlog, axis=1, keepdims=True))
                w_intra = jnp.exp(dlog - m_t)
                w_inter = jnp.exp(a - m_t)
                s = _dot_nt(q, k) * w_intra
                tot = _dot(s, v) + w_inter * _dot(q, cst)
                den = tot[:, NORM_LANE:NORM_LANE + 1]
                hout = tot / jnp.maximum(jnp.abs(den), jnp.exp(-m_t))
                acc[rows, cols] += hout
                b_last = bcol[C - 1:C, :] if d == 0 else bcol[0:1, :]
                g_log = b_last - bcol + ibcol
                m_new = jnp.maximum(b_last + m, jnp.max(g_log, axis=0, keepdims=True))
                w_prev = jnp.exp(b_last + m - m_new)
                kw = k * jnp.exp(g_log - m_new)
                return w_prev * cst + _dot_tn(kw, v), m_new

            cst, m = lax.fori_loop(0, nchunk, step, (c_init, m_init))
            if not latent:
                cfin_ref[d, h] = cst
                mfin_ref[d * HEADS + h:d * HEADS + h + 1, :] = jnp.broadcast_to(m, (1, 128))

    lane_l = lax.broadcasted_iota(jnp.int32, (L, DHP), 1)
    for h in range(HEADS):
        cols = slice(h * DHP, (h + 1) * DHP)
        x = jnp.where(lane_l < DH, acc[:, cols], 0.0)
        g = og_ref[:, cols]
        y = _rms(x, DH) * gain_ref[:, cols] * (1.0 / (1.0 + jnp.exp(-g)))
        o_ref[:, cols] = y.astype(o_ref.dtype)


def _mlstm(pm, pg, gb, gain, l, B, L, latent, c0=None, m0=None):
    tl, tu = _tri_tables()
    tl = tl.astype(MXU_DT)
    tu = tu.astype(MXU_DT)
    blk = lambda j: pl.BlockSpec((L, HWP), lambda b, j=j: (b, j))
    in_specs = [blk(0), blk(1), blk(2), blk(3),
                pl.BlockSpec((L, 128), lambda b: (b, 0)),
                pl.BlockSpec((None, 1, 128), lambda b: (l, 0, 0)),
                pl.BlockSpec((None, 1, HWP), lambda b: (l, 0, 0)),
                pl.BlockSpec((CHUNK, CHUNK), lambda b: (0, 0)),
                pl.BlockSpec((CHUNK, CHUNK), lambda b: (0, 0))]
    args = [pm, pm, pm, pm, pg, gb, gain, tl, tu]
    out_specs = [pl.BlockSpec((L, HWP), lambda b: (b, 0))]
    out_shape = [jax.ShapeDtypeStruct((B * L, HWP), MXU_DT)]
    if latent:
        in_specs += [pl.BlockSpec((None, None, 2, HEADS, DHP, DHP), lambda b: (b, l, 0, 0, 0, 0)),
                     pl.BlockSpec((None, None, 8, 128), lambda b: (b, l, 0, 0))]
        args += [c0, m0]
    else:
        out_specs += [pl.BlockSpec((None, 2, HEADS, DHP, DHP), lambda b: (b, 0, 0, 0, 0)),
                      pl.BlockSpec((None, 8, 128), lambda b: (b, 0, 0))]
        out_shape += [jax.ShapeDtypeStruct((B, 2, HEADS, DHP, DHP), F32),
                      jax.ShapeDtypeStruct((B, 8, 128), F32)]
    nchunk = L // CHUNK
    return pl.pallas_call(
        functools.partial(_ml_body, L=L, latent=latent),
        grid=(B,),
        in_specs=in_specs, out_specs=out_specs, out_shape=out_shape,
        scratch_shapes=[pltpu.VMEM((L, HWP), F32), pltpu.VMEM((L, 128), F32), pltpu.VMEM((L, 128), F32),
                        pltpu.VMEM((nchunk * 16, 128), F32), pltpu.VMEM((nchunk * 16, 128), F32)],
        compiler_params=_params(),
        name="mlstm_%d" % L,
    )(*args)


def _shift_rows(x, seq_len, n_rows):
    pos = lax.broadcasted_iota(jnp.int32, (n_rows, 1), 0) & (seq_len - 1)
    prev = jnp.where(pos == 0, 0.0, pltpu.roll(x, 1, 0))
    nxt = jnp.where(pos == seq_len - 1, 0.0, pltpu.roll(x, n_rows - 1, 0))
    return prev, nxt


def _hy_body(p_ref, cw_ref, cb_ref, fh_ref, fl_ref, gh_ref, gl_ref, ps_ref, qs_ref, p2s_ref, o_ref, *, L):
    x = p_ref[...]
    prev, nxt = _shift_rows(x, L, L)
    hy = prev * cw_ref[0:1, :] + x * cw_ref[1:2, :] + nxt * cw_ref[2:3, :] + cb_ref[...]
    hv = hy[:, 0:HY_W]
    hx1 = hy[:, HY_W:2 * HY_W]
    hx2 = hy[:, 2 * HY_W:3 * HY_W]

    def conv(z, o):
        spec = _dot_split(fh_ref[...], fl_ref[...], z)
        xr = spec[:L]
        xi = spec[L:]
        p = ps_ref[o]
        q = qs_ref[o]
        yr = xr * p - xi * q
        yi = xr * q + xi * p2s_ref[o]
        return _dot_split(gh_ref[...], gl_ref[...], jnp.concatenate([yr, yi], axis=0))

    z = hx1 * conv(hv, 0)
    o_ref[...] = (hx2 * conv(z, 1)).astype(o_ref.dtype)


def _hyena(ph, cw, cb, spectra, l, B, L):
    fh, fl, gh, gl = _dft_tables(L)
    ps, qs, p2s = spectra
    const = lambda shape: _resident(shape, lambda b: (0,) * len(shape))
    spec_blk = _resident((None, 2, L, HY_W), lambda b: (l, 0, 0, 0))
    return pl.pallas_call(
        functools.partial(_hy_body, L=L),
        grid=(B,),
        in_specs=[pl.BlockSpec((L, 3 * HY_W), lambda b: (b, 0)),
                  pl.BlockSpec((None, 8, 3 * HY_W), lambda b: (l, 0, 0)),
                  pl.BlockSpec((None, 1, 3 * HY_W), lambda b: (l, 0, 0)),
                  const((2 * L, L)), const((2 * L, L)), const((L, 2 * L)), const((L, 2 * L)),
                  spec_blk, spec_blk, spec_blk],
        out_specs=pl.BlockSpec((L, HY_W), lambda b: (b, 0)),
        out_shape=jax.ShapeDtypeStruct((B * L, HY_W), MXU_DT),
        compiler_params=_params(),
        name="hyena_%d" % L,
    )(ph, cw, cb, fh, fl, gh, gl, ps, qs, p2s)


def _out_body(mr_ref, mh_ref, mm_ref, wr_ref, wh_ref, wm_ref, x_ref, g1_ref, sc2_ref, sh2_ref,
              gpost_ref, gpre_ref, x1_ref, h2_ref):
    mix = (jnp.dot(mr_ref[...], wr_ref[...], preferred_element_type=F32)
           + jnp.dot(mh_ref[...], wh_ref[...], preferred_element_type=F32)
           + jnp.dot(mm_ref[...], wm_ref[...], preferred_element_type=F32))
    x1 = x_ref[...] + g1_ref[...] * (_rms(mix, D_MODEL) * gpost_ref[...])
    x1_ref[...] = x1
    h2 = _rms(x1, D_MODEL) * gpre_ref[...] * (1.0 + sc2_ref[...]) + sh2_ref[...]
    h2_ref[...] = h2.astype(h2_ref.dtype)


def _out_proj(mr, mh, mm, wr, wh, wm, x, mod, gpost, gpre, l, row_of_tile, tm):
    n = x.shape[0]
    tile = lambda w: pl.BlockSpec((tm, w), lambda i: (i, 0))
    layer = lambda r, c: pl.BlockSpec((None, r, c), lambda i: (l, 0, 0))
    return pl.pallas_call(
        _out_body,
        grid=(n // tm,),
        in_specs=[tile(HWP), tile(HY_W), tile(HWP),
                  layer(HWP, D_MODEL), layer(HY_W, D_MODEL), layer(HWP, D_MODEL),
                  tile(D_MODEL),
                  _mod_spec(l, 2, row_of_tile), _mod_spec(l, 4, row_of_tile), _mod_spec(l, 3, row_of_tile),
                  layer(1, D_MODEL), layer(1, D_MODEL)],
        out_specs=[tile(D_MODEL), tile(D_MODEL)],
        out_shape=[jax.ShapeDtypeStruct((n, D_MODEL), F32), jax.ShapeDtypeStruct((n, D_MODEL), MXU_DT)],
        compiler_params=_params(),
        name="out_proj",
    )(mr, mh, mm, wr, wh, wm, x, mod, mod, mod, gpost, gpre)


def _ffn_body(h_ref, wa_ref, wb_ref, cw_ref, cb_ref, wd_ref, x1_ref, g2_ref, gpost_ref, o_ref, acc,
              *, seq_len, tm):
    j = pl.program_id(1)
    h = h_ref[...]
    a = jnp.dot(h, wa_ref[...], preferred_element_type=F32)
    b = jnp.dot(h, wb_ref[...], preferred_element_type=F32)
    prev, nxt = _shift_rows(a, seq_len, tm)
    a = prev * cw_ref[0:1, :] + a * cw_ref[1:2, :] + nxt * cw_ref[2:3, :] + cb_ref[...]
    gelu = 0.5 * a * (1.0 + jnp.tanh(math.sqrt(2.0 / math.pi) * (a + 0.044715 * (a * a * a))))
    part = jnp.dot((gelu * b).astype(MXU_DT), wd_ref[...], preferred_element_type=F32)

    @pl.when(j == 0)
    def _():
        acc[...] = part

    @pl.when(j > 0)
    def _():
        acc[...] += part

    @pl.when(j == pl.num_programs(1) - 1)
    def _():
        o_ref[...] = x1_ref[...] + g2_ref[...] * (_rms(acc[...], D_MODEL) * gpost_ref[...])


def _conv_ffn(h2, w_up, cw, cb, w_down, x1, mod, gpost, l, row_of_tile, seq_len, tm, tf):
    n = h2.shape[0]
    nf = D_FF // tf
    return pl.pallas_call(
        functools.partial(_ffn_body, seq_len=seq_len, tm=tm),
        grid=(n // tm, nf),
        in_specs=[pl.BlockSpec((tm, D_MODEL), lambda i, j: (i, 0)),
                  pl.BlockSpec((None, D_MODEL, tf), lambda i, j: (l, 0, j)),
                  pl.BlockSpec((None, D_MODEL, tf), lambda i, j: (l, 0, j + nf)),
                  pl.BlockSpec((None, 8, tf), lambda i, j: (l, 0, j)),
                  pl.BlockSpec((None, 1, tf), lambda i, j: (l, 0, j)),
                  pl.BlockSpec((None, tf, D_MODEL), lambda i, j: (l, j, 0)),
                  pl.BlockSpec((tm, D_MODEL), lambda i, j: (i, 0)),
                  pl.BlockSpec((None, None, None, 1, D_MODEL), lambda i, j: (l, 5, row_of_tile(i), 0, 0)),
                  pl.BlockSpec((None, 1, D_MODEL), lambda i, j: (l, 0, 0))],
        out_specs=pl.BlockSpec((tm, D_MODEL), lambda i, j: (i, 0)),
        out_shape=jax.ShapeDtypeStruct((n, D_MODEL), F32),
        scratch_shapes=[pltpu.VMEM((tm, D_MODEL), F32)],
        compiler_params=_params(),
        name="conv_ffn",
    )(h2, w_up, w_up, cw, cb, w_down, x1, mod, gpost)


def _pad_rows8(w):
    return jnp.concatenate([w, jnp.zeros((w.shape[0], 8 - w.shape[1], w.shape[2]), w.dtype)], axis=1)


def kernel(x_prompt, x_sample, c, state_ret, state_mlstm_c, state_mlstm_n, state_mlstm_m, c_ctx,
           norm_mix_pre, norm_mix_post, norm_ffn_pre, norm_ffn_post, w_mod, b_mod, w_in, w_out,
           ret_decay_logit, ret_norm_g, hy_conv_w, hy_conv_b, hy_f_w1, hy_f_b1, hy_f_w2, hy_f_b2,
           hy_f_w3, hy_f_b3, hy_sin_freq, hy_bias, ml_gate_bias, ml_norm_g,
           w_up, ffn_conv_w, ffn_conv_b, w_down):
    BP, LP, _ = x_prompt.shape
    BS, LS, _ = x_sample.shape

    sec = lambda a, i, w: lax.slice_in_dim(a, i, i + w, axis=2)
    w_in_p = jnp.concatenate(
        [_pad_heads(sec(w_in, 0, HW), rope=True), _pad_heads(sec(w_in, HW, HW), rope=True),
         _pad_heads(sec(w_in, 2 * HW, HW)), _pad_heads(sec(w_in, 3 * HW, HW)),
         sec(w_in, 4 * HW, 3 * HY_W)]
        + [_pad_heads(sec(w_in, 4 * HW + 3 * HY_W + i * HW, HW)) for i in range(4)]
        + [sec(w_in, 8 * HW + 3 * HY_W, 16), jnp.zeros((DEPTH, D_MODEL, 112), w_in.dtype)],
        axis=2).astype(MXU_DT)
    w_out_r = _pad_heads(lax.slice_in_dim(w_out, 0, HW, axis=1), axis=1).astype(MXU_DT)
    w_out_h = lax.slice_in_dim(w_out, HW, HW + HY_W, axis=1).astype(MXU_DT)
    w_out_m = _pad_heads(lax.slice_in_dim(w_out, HW + HY_W, 2 * HW + HY_W, axis=1), axis=1).astype(MXU_DT)
    w_up_b = w_up.astype(MXU_DT)
    w_down_b = w_down.astype(MXU_DT)
    ret_gain = _pad_heads(ret_norm_g).reshape(DEPTH, 1, HWP)
    ml_gain = _pad_heads(ml_norm_g).reshape(DEPTH, 1, HWP)
    dl = jnp.broadcast_to(ret_decay_logit.reshape(DEPTH, 8, 1), (DEPTH, 8, 128))
    gate_bias = jnp.concatenate([ml_gate_bias.reshape(DEPTH, 1, 16), jnp.zeros((DEPTH, 1, 112), F32)], axis=2)
    hy_cw = _pad_rows8(hy_conv_w)
    hy_cb = hy_conv_b.reshape(DEPTH, 1, 3 * HY_W)
    ffn_cw = _pad_rows8(ffn_conv_w)
    ffn_cb = ffn_conv_b.reshape(DEPTH, 1, D_FF)
    g_mix_pre = norm_mix_pre.reshape(DEPTH, 1, D_MODEL)
    g_mix_post = norm_mix_post.reshape(DEPTH, 1, D_MODEL)
    g_ffn_pre = norm_ffn_pre.reshape(DEPTH, 1, D_MODEL)
    g_ffn_post = norm_ffn_post.reshape(DEPTH, 1, D_MODEL)
    pad2 = lambda a, r, cdim: jnp.pad(a, ((0, 0), (0, r - a.shape[1]), (0, cdim - a.shape[2])))
    fw1 = pad2(hy_f_w1, 128, 128)
    fb1 = pad2(hy_f_b1.reshape(DEPTH, 1, -1), 1, 128)
    fw2 = pad2(hy_f_w2, 128, 128)
    fb2 = pad2(hy_f_b2.reshape(DEPTH, 1, -1), 1, 128)
    fw3 = pad2(hy_f_w3, 128, 4 * HY_W)
    fb3 = hy_f_b3.reshape(DEPTH, 1, 4 * HY_W)
    ffr = pad2(hy_sin_freq.reshape(DEPTH, 1, -1), 1, 128)

    sr = jnp.zeros(state_ret.shape[:4] + (DHP, DHP), F32)
    half = DH // 2
    sr = sr.at[..., 0:half, 0:DH].set(state_ret[..., 0:half, :])
    sr = sr.at[..., 64:64 + half, 0:DH].set(state_ret[..., half:DH, :])
    c0 = jnp.zeros(state_mlstm_c.shape[:4] + (DHP, DHP), F32)
    c0 = c0.at[..., 0:DH, 0:DH].set(state_mlstm_c)
    c0 = c0.at[..., 0:DH, NORM_LANE].set(state_mlstm_n)
    m0 = jnp.broadcast_to(state_mlstm_m.reshape(BS, DEPTH, 8, 1), (BS, DEPTH, 8, 128))

    cvec8 = jnp.concatenate([c_ctx.reshape(1, D_MODEL), c, jnp.zeros((8 - 1 - BS, D_MODEL), F32)], axis=0)
    mod = _mod_all(cvec8, w_mod, b_mod)
    mod = mod.reshape(DEPTH, 8, 6, 1, D_MODEL).transpose(0, 2, 1, 3, 4)
    spectra_p = _filter_spectra(LP, fw1, fb1, fw2, fb2, fw3, fb3, ffr, hy_bias)
    spectra_s = _filter_spectra(LS, fw1, fb1, fw2, fb2, fw3, fb3, ffr, hy_bias)

    xp = x_prompt.reshape(BP * LP, D_MODEL)
    xs = x_sample.reshape(BS * LS, D_MODEL)
    tm = 512
    ffn_tm = 1024
    row_ctx = lambda i: 0
    row_lat_tm = lambda i: 1 + i // (LS // tm)
    row_lat_ffn = lambda i: 1 + i // (LS // ffn_tm)
    new_ret, new_c, new_m = [], [], []

    def layer(x, l, B, L, latent, row_tm, row_ffn, spectra):
        pr, ph, pm, pg = _in_proj(x, mod, g_mix_pre, w_in_p, l, row_tm, tm)
        if latent:
            (mr,) = _retention(pr, dl, ret_gain, l, B, L, True, s0=sr)
            (mm,) = _mlstm(pm, pg, gate_bias, ml_gain, l, B, L, True, c0=c0, m0=m0)
            states = None
        else:
            mr, s_ret = _retention(pr, dl, ret_gain, l, B, L, False)
            mm, c_fin, m_fin = _mlstm(pm, pg, gate_bias, ml_gain, l, B, L, False)
            states = (s_ret, c_fin, m_fin)
        mh = _hyena(ph, hy_cw, hy_cb, spectra, l, B, L)
        x1, h2 = _out_proj(mr, mh, mm, w_out_r, w_out_h, w_out_m, x, mod, g_mix_post, g_ffn_pre, l, row_tm, tm)
        out = _conv_ffn(h2, w_up_b, ffn_cw, ffn_cb, w_down_b, x1, mod, g_ffn_post, l, row_ffn, L, ffn_tm, 512)
        return out, states

    for l in range(DEPTH):
        xp, st = layer(xp, l, BP, LP, False, row_ctx, row_ctx, spectra_p)
        new_ret.append(st[0])
        new_c.append(st[1])
        new_m.append(st[2])
        xs, _ = layer(xs, l, BS, LS, True, row_lat_tm, row_lat_ffn, spectra_s)

    s_ret = jnp.stack(new_ret, axis=1)
    s_ret = _unpad_heads_axis(_unpad_heads_axis(s_ret, 4, rope=True), 5)
    c_aug = jnp.stack(new_c, axis=1)
    out_c = c_aug[..., 0:DH, 0:DH]
    out_n = c_aug[..., 0:DH, NORM_LANE]
    out_m = jnp.stack(new_m, axis=1)[..., 0].reshape(BP, DEPTH, 2, HEADS)
    return (xp.reshape(BP, LP, D_MODEL), xs.reshape(BS, LS, D_MODEL), s_ret, out_c, out_n, out_m)
```

```python
import functools
import math

import numpy as np
import jax
import jax.numpy as jnp
from jax import lax
from jax.experimental import pallas as pl
from jax.experimental.pallas import tpu as pltpu

D_MODEL = 1024
DEPTH = 4
GRID_W = 64
HEADS = 4
DH = 96
DHP = 128
HW = HEADS * DH
HWP = HEADS * DHP
HY_W = 256
D_FF = 4 * D_MODEL
CHUNK = 128
N_BANDS = 16
FEAT_W = 1 + 2 * N_BANDS
HY_SHIFT = 0.05
HY_TARGET = 1e-2
HY_SHORT_DECAY_PCT = 0.3
HY_LONG_DECAY_PCT = 1.5
ROPE_BASE = 10000.0
EPS = 1e-6

MXU_DT = jnp.bfloat16
F32 = jnp.float32
VMEM_LIMIT = 56 * 1024 * 1024


def _dot(a, b):
    return jnp.dot(a.astype(MXU_DT), b.astype(MXU_DT), preferred_element_type=F32)


def _dot_nt(a, b):
    return lax.dot_general(a.astype(MXU_DT), b.astype(MXU_DT), (((1,), (1,)), ((), ())),
                           preferred_element_type=F32)


def _dot_tn(a, b):
    return lax.dot_general(a.astype(MXU_DT), b.astype(MXU_DT), (((0,), (0,)), ((), ())),
                           preferred_element_type=F32)


def _split2(x):
    hi = x.astype(MXU_DT)
    lo = (x - hi.astype(F32)).astype(MXU_DT)
    return hi, lo


def _dot_split(a_hi, a_lo, b):
    b_hi, b_lo = _split2(b)
    return (jnp.dot(a_hi, b_hi, preferred_element_type=F32)
            + jnp.dot(a_hi, b_lo, preferred_element_type=F32)
            + jnp.dot(a_lo, b_hi, preferred_element_type=F32))


def _dot3(a, b):
    a_hi, a_lo = _split2(a)
    return _dot_split(a_hi, a_lo, b)


def _dot_exact_lhs(t, x):
    x1 = x.astype(MXU_DT)
    r1 = x - x1.astype(F32)
    x2 = r1.astype(MXU_DT)
    x3 = (r1 - x2.astype(F32)).astype(MXU_DT)
    return (jnp.dot(t, x1, preferred_element_type=F32) + jnp.dot(t, x2, preferred_element_type=F32)
            + jnp.dot(t, x3, preferred_element_type=F32))


def _rms(x, n):
    return x * lax.rsqrt(jnp.sum(x * x, axis=-1, keepdims=True) * (1.0 / n) + EPS)


def _chunk_rows(c, size):
    if isinstance(c, int):
        return slice(c * size, (c + 1) * size)
    return pl.ds(pl.multiple_of(c * size, size), size)


def _log_sigmoid(x):
    return jnp.minimum(x, 0.0) - jnp.log1p(jnp.exp(-jnp.abs(x)))


def _resident(shape, index_map):
    return pl.BlockSpec(shape, index_map, pipeline_mode=pl.Buffered(1))


def _params(**kw):
    return pltpu.CompilerParams(vmem_limit_bytes=VMEM_LIMIT, **kw)


@functools.lru_cache(maxsize=None)
def _dft_tables(L):
    f = np.arange(L, dtype=np.int64)[:, None]
    s = np.arange(L, dtype=np.int64)[None, :]
    ang = np.pi * ((f * s) % (2 * L)).astype(np.float64) / L
    fr = np.cos(ang)
    fi = -np.sin(ang)
    fi[0, :] = np.where(np.arange(L) % 2 == 0, 1.0, -1.0)
    fwd = np.concatenate([fr, fi], axis=0)
    gr = np.cos(ang.T) * (2.0 / (2 * L))
    gr[:, 0] = 1.0 / (2 * L)
    gi = -np.sin(ang.T) * (2.0 / (2 * L))
    gi[:, 0] = np.where(np.arange(L) % 2 == 0, 1.0, -1.0) / (2 * L)
    inv = np.concatenate([gr, gi], axis=1)

    def split(m):
        m32 = m.astype(np.float32)
        hi = m32.astype(jnp.bfloat16)
        lo = (m32 - hi.astype(np.float32)).astype(jnp.bfloat16)
        return hi, lo

    return split(fwd) + split(inv)


@functools.lru_cache(maxsize=None)
def _filter_tables(L):
    tn = np.arange(L, dtype=np.float64) / L
    bands = np.linspace(1e-4, N_BANDS - 1, N_BANDS)
    ang = 2.0 * math.pi * tn[:, None] * bands[None, :]
    feat = np.zeros((L, 128), np.float32)
    feat[:, 0] = tn
    feat[:, 1:1 + N_BANDS] = np.cos(ang)
    feat[:, 1 + N_BANDS:FEAT_W] = np.sin(ang)
    deltas = np.abs(np.linspace(math.log(HY_TARGET) / HY_LONG_DECAY_PCT,
                                math.log(HY_TARGET) / HY_SHORT_DECAY_PCT, HY_W))
    window = (np.exp(-tn[:, None] * deltas[None, :]) + HY_SHIFT).astype(np.float32)
    return feat, window


@functools.lru_cache(maxsize=None)
def _rope_tables(L):
    rows = L // GRID_W
    row = np.repeat(np.arange(rows, dtype=np.float64), GRID_W)
    col = np.tile(np.arange(GRID_W, dtype=np.float64), rows)
    half = DH // 2
    n_freq = half // 2
    freqs = ROPE_BASE ** (-np.arange(n_freq, dtype=np.float64) / n_freq)
    ang = np.concatenate([row[:, None] * freqs, col[:, None] * freqs], axis=-1)
    cos = np.zeros((L, DHP), np.float32)
    sin = np.zeros((L, DHP), np.float32)
    cos[:, :half] = np.cos(ang)
    cos[:, 64:64 + half] = np.cos(ang)
    sin[:, :half] = -np.sin(ang)
    sin[:, 64:64 + half] = np.sin(ang)
    return cos, sin


@functools.lru_cache(maxsize=None)
def _tri_tables():
    i = np.arange(CHUNK)
    lower = (i[:, None] >= i[None, :]).astype(np.float32)
    upper = (i[:, None] <= i[None, :]).astype(np.float32)
    return lower, upper


def _pad_heads(w, axis=-1, rope=False):
    axis = axis % w.ndim
    parts = []

    def zeros(n):
        shp = list(w.shape)
        shp[axis] = n
        return jnp.zeros(shp, w.dtype)

    for h in range(HEADS):
        blk = lax.slice_in_dim(w, h * DH, (h + 1) * DH, axis=axis)
        if rope:
            half = DH // 2
            parts += [lax.slice_in_dim(blk, 0, half, axis=axis), zeros(64 - half),
                      lax.slice_in_dim(blk, half, DH, axis=axis), zeros(64 - half)]
        else:
            parts += [blk, zeros(DHP - DH)]
    return jnp.concatenate(parts, axis=axis)


def _unpad_heads_axis(x, axis, rope=False):
    if rope:
        half = DH // 2
        return jnp.concatenate([lax.slice_in_dim(x, 0, half, axis=axis),
                                lax.slice_in_dim(x, 64, 64 + half, axis=axis)], axis=axis)
    return lax.slice_in_dim(x, 0, DH, axis=axis)


def _mod_body(c_ref, w_ref, b_ref, o_ref):
    c = c_ref[...]
    s = c * (1.0 / (1.0 + jnp.exp(-c)))
    o_ref[...] = _dot(s, w_ref[...]) + b_ref[...]


def _mod_all(cvec8, w_mod, b_mod):
    tn = 1536
    nj = 6 * D_MODEL // tn
    return pl.pallas_call(
        _mod_body,
        grid=(DEPTH, nj),
        in_specs=[pl.BlockSpec((8, D_MODEL), lambda l, j: (0, 0)),
                  pl.BlockSpec((None, D_MODEL, tn), lambda l, j: (l, 0, j)),
                  pl.BlockSpec((None, 1, tn), lambda l, j: (l, 0, j))],
        out_specs=pl.BlockSpec((None, 8, tn), lambda l, j: (l, 0, j)),
        out_shape=jax.ShapeDtypeStruct((DEPTH, 8, 6 * D_MODEL), F32),
        compiler_params=_params(),
        name="mod_all",
    )(cvec8, w_mod, b_mod.reshape(DEPTH, 1, 6 * D_MODEL))


def _filt_body(feat_ref, win_ref, fh_ref, fl_ref, w1_ref, b1_ref, w2_ref, b2_ref, w3_ref, b3_ref,
               fr_ref, hb_ref, p_ref, q_ref, p2_ref, *, L):
    fr = fr_ref[...]
    h = jnp.sin(fr * (_dot3(feat_ref[...], w1_ref[...]) + b1_ref[...]))
    h = jnp.sin(fr * (_dot3(h, w2_ref[...]) + b2_ref[...]))
    filt = _dot3(h, w3_ref[...]) + b3_ref[...]
    win = win_ref[...]
    row = lax.broadcasted_iota(jnp.int32, (L, 1), 0)
    fh = fh_ref[...]
    fl = fl_ref[...]
    for o in range(2):
        fwd = filt[:, o * HY_W:(o + 1) * HY_W] * win
        bwd = filt[:, (2 + o) * HY_W:(3 + o) * HY_W] * win
        bwd = jnp.where(row == 0, 0.0, bwd)
        a = _dot_split(fh, fl, fwd)
        b = _dot_split(fh, fl, bwd)
        kr = a[:L] + b[:L]
        ki = a[L:] - b[L:]
        nyq = a[L:L + 1] + b[L:L + 1]
        bias = hb_ref[o:o + 1, :]
        p_ref[o] = kr + bias
        q_ref[o] = jnp.where(row == 0, 0.0, ki)
        p2_ref[o] = jnp.where(row == 0, nyq, kr) + bias


def _filter_spectra(L, fw1, fb1, fw2, fb2, fw3, fb3, ffr, hy_bias):
    feat, window = _filter_tables(L)
    fh, fl, _, _ = _dft_tables(L)
    const = lambda shape: _resident(shape, lambda l: (0,) * len(shape))
    per_layer = lambda shape: pl.BlockSpec((None,) + shape, lambda l: (l,) + (0,) * len(shape))
    out_spec = per_layer((2, L, HY_W))
    out_shape = jax.ShapeDtypeStruct((DEPTH, 2, L, HY_W), F32)
    return pl.pallas_call(
        functools.partial(_filt_body, L=L),
        grid=(DEPTH,),
        in_specs=[const((L, 128)), const((L, HY_W)), const((2 * L, L)), const((2 * L, L)),
                  per_layer((128, 128)), per_layer((1, 128)), per_layer((128, 128)), per_layer((1, 128)),
                  per_layer((128, 4 * HY_W)), per_layer((1, 4 * HY_W)), per_layer((1, 128)),
                  per_layer((2, HY_W))],
        out_specs=[out_spec, out_spec, out_spec],
        out_shape=[out_shape, out_shape, out_shape],
        compiler_params=_params(),
        name="hyena_filter_%d" % L,
    )(feat, window, fh, fl, fw1, fb1, fw2, fb2, fw3, fb3, ffr, hy_bias)


W_R0, W_H0, W_M0, W_G0, W_END = 0, 4 * HWP, 4 * HWP + 3 * HY_W, 8 * HWP + 3 * HY_W, 8 * HWP + 3 * HY_W + 128


def _in_body(x_ref, sc_ref, sh_ref, g_ref, w_ref, pr_ref, ph_ref, pm_ref, pg_ref):
    h = _rms(x_ref[...], D_MODEL) * g_ref[...]
    h = (h * (1.0 + sc_ref[...]) + sh_ref[...]).astype(MXU_DT)
    pr_ref[...] = jnp.dot(h, w_ref[:, W_R0:W_H0], preferred_element_type=F32)
    ph_ref[...] = jnp.dot(h, w_ref[:, W_H0:W_M0], preferred_element_type=F32)
    pm_ref[...] = jnp.dot(h, w_ref[:, W_M0:W_G0], preferred_element_type=F32)
    pg_ref[...] = jnp.dot(h, w_ref[:, W_G0:W_END], preferred_element_type=F32)


def _mod_spec(l, which, row_of_tile):
    return pl.BlockSpec((None, None, None, 1, D_MODEL), lambda i: (l, which, row_of_tile(i), 0, 0))


def _in_proj(x, mod, gain, w_in, l, row_of_tile, tm):
    n = x.shape[0]
    shapes = [4 * HWP, 3 * HY_W, 4 * HWP, 128]
    return pl.pallas_call(
        _in_body,
        grid=(n // tm,),
        in_specs=[pl.BlockSpec((tm, D_MODEL), lambda i: (i, 0)),
                  _mod_spec(l, 1, row_of_tile), _mod_spec(l, 0, row_of_tile),
                  pl.BlockSpec((None, 1, D_MODEL), lambda i: (l, 0, 0)),
                  _resident((None, D_MODEL, W_END), lambda i: (l, 0, 0))],
        out_specs=[pl.BlockSpec((tm, w), lambda i: (i, 0)) for w in shapes],
        out_shape=[jax.ShapeDtypeStruct((n, w), F32) for w in shapes],
        compiler_params=_params(),
        name="in_proj",
    )(x, mod, mod, gain, w_in)


def _ret_body(*refs, L, latent):
    if latent:
        (q_ref, k_ref, v_ref, gate_ref, dl_ref, gain_ref, s0_ref, cos_ref, sin_ref,
         o_ref, qs, ks, acc, sst) = refs
    else:
        (q_ref, k_ref, v_ref, gate_ref, dl_ref, gain_ref, o_ref, sfin_ref, qs, ks, acc, sst) = refs
    C = CHUNK
    nchunk = L // C
    lg = _log_sigmoid(dl_ref[...])
    rel = (lax.broadcasted_iota(jnp.int32, (C, C), 0) - lax.broadcasted_iota(jnp.int32, (C, C), 1)).astype(F32)
    ri = lax.broadcasted_iota(jnp.int32, (C, 1), 0).astype(F32)

    for h in range(HEADS):
        cols = slice(h * DHP, (h + 1) * DHP)
        q = q_ref[:, cols]
        k = k_ref[:, cols]
        if latent:
            cos = cos_ref[...]
            sin = sin_ref[...]
            q = q * cos + pltpu.roll(q, 64, 1) * sin
            k = k * cos + pltpu.roll(k, 64, 1) * sin
        qs[:, cols] = q
        ks[:, cols] = k * (DH ** -0.5)

    acc[...] = jnp.zeros_like(acc)
    consts = []
    for h in range(HEADS):
        lgf = lg[h:h + 1, :]
        lgb = lg[HEADS + h:HEADS + h + 1, :]
        lgf1 = lgf[:, 0:1]
        lgb1 = lgb[:, 0:1]
        mask = (jnp.where(rel >= 0, jnp.exp(rel * lgf), 0.0)
                + jnp.where(rel <= 0, jnp.exp(-rel * lgb), 0.0))
        consts.append(dict(
            mask=mask,
            qdec_f=jnp.exp((ri + 1.0) * lgf1), qdec_b=jnp.exp((C - ri) * lgb1),
            kdec_f=jnp.exp((C - 1.0 - ri) * lgf1), kdec_b=jnp.exp(ri * lgb1),
            cdec_f=jnp.exp(C * lgf1), cdec_b=jnp.exp(C * lgb1)))
        for d in range(2):
            sst[d, h] = s0_ref[d, h] if latent else jnp.zeros((DHP, DHP), F32)

    def step(j):
        rows_f = _chunk_rows(j, C)
        rows_b = _chunk_rows(nchunk - 1 - j, C)
        for h in range(HEADS):
            cols = slice(h * DHP, (h + 1) * DHP)
            cn = consts[h]
            q = qs[rows_f, cols]
            k = ks[rows_f, cols]
            v = v_ref[rows_f, cols]
            s_f = sst[0, h]
            sc = _dot_nt(q, k) * cn["mask"]
            acc[rows_f, cols] += _dot(sc, v) + _dot(q, s_f) * cn["qdec_f"]
            sst[0, h] = s_f * cn["cdec_f"] + _dot_tn(k * cn["kdec_f"], v)
            q = qs[rows_b, cols]
            k = ks[rows_b, cols]
            v = v_ref[rows_b, cols]
            s_b = sst[1, h]
            acc[rows_b, cols] += _dot(q, s_b) * cn["qdec_b"]
            sst[1, h] = s_b * cn["cdec_b"] + _dot_tn(k * cn["kdec_b"], v)

    if nchunk <= 2:
        for j in range(nchunk):
            step(j)
    else:
        pl.loop(0, nchunk)(step)

    if not latent:
        sfin_ref[...] = sst[...]

    for h in range(HEADS):
        cols = slice(h * DHP, (h + 1) * DHP)
        g = gate_ref[:, cols]
        y = _rms(acc[:, cols], DH) * gain_ref[:, cols] * (g * (1.0 / (1.0 + jnp.exp(-g))))
        o_ref[:, cols] = y.astype(o_ref.dtype)


def _retention(pr, dl, gain, l, B, L, latent, s0=None):
    blk = lambda j: pl.BlockSpec((L, HWP), lambda b, j=j: (b, j))
    in_specs = [blk(0), blk(1), blk(2), blk(3),
                pl.BlockSpec((None, 8, 128), lambda b: (l, 0, 0)),
                pl.BlockSpec((None, 1, HWP), lambda b: (l, 0, 0))]
    args = [pr, pr, pr, pr, dl, gain]
    out_specs = [pl.BlockSpec((L, HWP), lambda b: (b, 0))]
    out_shape = [jax.ShapeDtypeStruct((B * L, HWP), MXU_DT)]
    if latent:
        cos, sin = _rope_tables(L)
        in_specs += [pl.BlockSpec((None, None, 2, HEADS, DHP, DHP), lambda b: (b, l, 0, 0, 0, 0)),
                     pl.BlockSpec((L, DHP), lambda b: (0, 0)), pl.BlockSpec((L, DHP), lambda b: (0, 0))]
        args += [s0, cos, sin]
    else:
        out_specs.append(pl.BlockSpec((None, 2, HEADS, DHP, DHP), lambda b: (b, 0, 0, 0, 0)))
        out_shape.append(jax.ShapeDtypeStruct((B, 2, HEADS, DHP, DHP), F32))
    return pl.pallas_call(
        functools.partial(_ret_body, L=L, latent=latent),
        grid=(B,),
        in_specs=in_specs, out_specs=out_specs, out_shape=out_shape,
        scratch_shapes=[pltpu.VMEM((L, HWP), F32), pltpu.VMEM((L, HWP), F32), pltpu.VMEM((L, HWP), F32),
                        pltpu.VMEM((2, HEADS, DHP, DHP), F32)],
        compiler_params=_params(),
        name="retention_%d" % L,
    )(*args)


def _ml_body(*refs, L, latent):
    if latent:
        (q_ref, k_ref, v_ref, og_ref, gates_ref, gb_ref, gain_ref, tl_ref, tu_ref, c0_ref, m0_ref,
         o_ref, acc, bb_s, ub_s, pm_s, brow_s, ibrow_s, mc_s, mx_s, cn_s) = refs
    else:
        (q_ref, k_ref, v_ref, og_ref, gates_ref, gb_ref, gain_ref, tl_ref, tu_ref,
         o_ref, cfin_ref, mfin_ref, acc, bb_s, ub_s, pm_s, brow_s, ibrow_s, mc_s, mx_s, cn_s) = refs
    C = CHUNK
    nchunk = L // C
    lane = lax.broadcasted_iota(jnp.int32, (C, DHP), 1)
    tri_r = lax.broadcasted_iota(jnp.int32, (C, C), 0)
    tri_c = lax.broadcasted_iota(jnp.int32, (C, C), 1)
    tl = tl_ref[...]
    tu = tu_ref[...]

    row_i = lax.broadcasted_iota(jnp.int32, (C, 128), 0)

    def cummax_rows(x, suffix):
        s = 1
        while s < C:
            if suffix:
                x = jnp.where(row_i < C - s, jnp.maximum(x, pltpu.roll(x, C - s, 0)), x)
            else:
                x = jnp.where(row_i >= s, jnp.maximum(x, pltpu.roll(x, s, 0)), x)
            s *= 2
        return x

    for c in range(nchunk):
        rows = slice(c * C, (c + 1) * C)
        g = gates_ref[rows, :] + gb_ref[...]
        lf = _log_sigmoid(g)
        pre = _dot_exact_lhs(tl, lf)
        suf = _dot_exact_lhs(tu, lf)
        bc = jnp.where(lane < 8, pre, suf)
        brow_s[c * 16:(c + 1) * 16, :] = bc.T[0:16, :]
        ibrow_s[c * 16:(c + 1) * 16, :] = g.T[0:16, :]
        for d in range(2):
            for h in range(HEADS):
                sr = d * HEADS + h
                b_b = jnp.broadcast_to(bc[:, d * 8 + 4 + h:d * 8 + 5 + h], (C, 128))
                u_b = jnp.broadcast_to(g[:, d * 8 + h:d * 8 + h + 1], (C, 128)) - b_b
                bb_s[sr, rows, :] = b_b
                ub_s[sr, rows, :] = u_b
                pm_s[sr, rows, :] = cummax_rows(u_b, suffix=(d == 1))

    for d in range(2):
        for h in range(HEADS):
            sr = d * HEADS + h
            m = m0_ref[sr:sr + 1, :] if latent else jnp.zeros((1, 128), F32)
            for j in range(nchunk):
                c = j if d == 0 else nchunk - 1 - j
                last = c * C + (C - 1 if d == 0 else 0)
                mx = jnp.maximum(m, pm_s[sr, last:last + 1, :])
                mc_s[c * 8 + sr:c * 8 + sr + 1, :] = m
                mx_s[c * 8 + sr:c * 8 + sr + 1, :] = mx
                m = bb_s[sr, last:last + 1, :] + mx
            if not latent:
                mfin_ref[sr:sr + 1, :] = m

    acc[...] = jnp.zeros_like(acc)
    if latent:
        cn_s[...] = c0_ref[...]
    else:
        cn_s[...] = jnp.zeros_like(cn_s)
    ones_blk = jnp.ones((C, DHP), MXU_DT)

    def step(j):
        for d in range(2):
            c = j if d == 0 else nchunk - 1 - j
            rows = _chunk_rows(c, C)
            brs = brow_s[_chunk_rows(c, 16), :]
            ibrs = ibrow_s[_chunk_rows(c, 16), :]
            mcs = mc_s[_chunk_rows(c, 8), :]
            mxs = mx_s[_chunk_rows(c, 8), :]
            causal = (tri_r >= tri_c) if d == 0 else (tri_r <= tri_c)
            for h in range(HEADS):
                cols = slice(h * DHP, (h + 1) * DHP)
                ic = d * 8 + h
                fc = d * 8 + 4 + h
                sr = d * HEADS + h
                m_c = mcs[sr:sr + 1, :]
                mx = mxs[sr:sr + 1, :]
                q = q_ref[rows, cols].astype(MXU_DT)
                k = k_ref[rows, cols] * (DH ** -0.5)
                v1 = jnp.concatenate([v_ref[rows, cols].astype(MXU_DT), ones_blk], axis=1)
                cn = cn_s[d, h]
                urow = ibrs[ic:ic + 1, :] - brs[fc:fc + 1, :]
                mb = jnp.maximum(m_c, pm_s[sr, rows, :])
                w_intra = jnp.exp(jnp.where(causal, urow - mb, -jnp.inf))
                w_inter = jnp.exp(m_c - mb)
                s = _dot_nt(q, k) * w_intra
                tot = (jnp.dot(s.astype(MXU_DT), v1, preferred_element_type=F32)
                       + jnp.concatenate([w_inter, w_inter], axis=1) * _dot(q, cn))
                den = tot[:, DHP:]
                floor = jnp.exp(-(bb_s[sr, rows, :] + mb))
                acc[rows, cols] += tot[:, :DHP] * (1.0 / jnp.maximum(jnp.abs(den), floor))
                kw = k * jnp.exp(ub_s[sr, rows, :] - mx)
                w_prev = jnp.exp(m_c - mx)
                cn_s[d, h] = jnp.concatenate([w_prev, w_prev], axis=1) * cn + _dot_tn(kw, v1)

    if nchunk <= 2:
        for j in range(nchunk):
            step(j)
    else:
        pl.loop(0, nchunk)(step)

    if not latent:
        cfin_ref[...] = cn_s[...]

    for h in range(HEADS):
        cols = slice(h * DHP, (h + 1) * DHP)
        g = og_ref[:, cols]
        y = _rms(acc[:, cols], DH) * gain_ref[:, cols] * (1.0 / (1.0 + jnp.exp(-g)))
        o_ref[:, cols] = y.astype(o_ref.dtype)


def _mlstm(pm, pg, gb, gain, l, B, L, latent, c0=None, m0=None):
    tl, tu = _tri_tables()
    tl = tl.astype(MXU_DT)
    tu = tu.astype(MXU_DT)
    blk = lambda j: pl.BlockSpec((L, HWP), lambda b, j=j: (b, j))
    in_specs = [blk(0), blk(1), blk(2), blk(3),
                pl.BlockSpec((L, 128), lambda b: (b, 0)),
                pl.BlockSpec((None, 1, 128), lambda b: (l, 0, 0)),
                pl.BlockSpec((None, 1, HWP), lambda b: (l, 0, 0)),
                pl.BlockSpec((CHUNK, CHUNK), lambda b: (0, 0)),
                pl.BlockSpec((CHUNK, CHUNK), lambda b: (0, 0))]
    args = [pm, pm, pm, pm, pg, gb, gain, tl, tu]
    out_specs = [pl.BlockSpec((L, HWP), lambda b: (b, 0))]
    out_shape = [jax.ShapeDtypeStruct((B * L, HWP), MXU_DT)]
    if latent:
        in_specs += [pl.BlockSpec((None, None, 2, HEADS, DHP, 2 * DHP), lambda b: (b, l, 0, 0, 0, 0)),
                     pl.BlockSpec((None, None, 8, 128), lambda b: (b, l, 0, 0))]
        args += [c0, m0]
    else:
        out_specs += [pl.BlockSpec((None, 2, HEADS, DHP, 2 * DHP), lambda b: (b, 0, 0, 0, 0)),
                      pl.BlockSpec((None, 8, 128), lambda b: (b, 0, 0))]
        out_shape += [jax.ShapeDtypeStruct((B, 2, HEADS, DHP, 2 * DHP), F32),
                      jax.ShapeDtypeStruct((B, 8, 128), F32)]
    nchunk = L // CHUNK
    return pl.pallas_call(
        functools.partial(_ml_body, L=L, latent=latent),
        grid=(B,),
        in_specs=in_specs, out_specs=out_specs, out_shape=out_shape,
        scratch_shapes=[pltpu.VMEM((L, HWP), F32),
                        pltpu.VMEM((8, L, 128), F32), pltpu.VMEM((8, L, 128), F32), pltpu.VMEM((8, L, 128), F32),
                        pltpu.VMEM((nchunk * 16, 128), F32), pltpu.VMEM((nchunk * 16, 128), F32),
                        pltpu.VMEM((nchunk * 8, 128), F32), pltpu.VMEM((nchunk * 8, 128), F32),
                        pltpu.VMEM((2, HEADS, DHP, 2 * DHP), F32)],
        compiler_params=_params(),
        name="mlstm_%d" % L,
    )(*args)


def _shift_rows(x, seq_len, n_rows):
    pos = lax.broadcasted_iota(jnp.int32, (n_rows, 1), 0) & (seq_len - 1)
    prev = jnp.where(pos == 0, 0.0, pltpu.roll(x, 1, 0))
    nxt = jnp.where(pos == seq_len - 1, 0.0, pltpu.roll(x, n_rows - 1, 0))
    return prev, nxt


def _hy_body(p_ref, cw_ref, cb_ref, fh_ref, fl_ref, gh_ref, gl_ref, ps_ref, qs_ref, p2s_ref, o_ref, *, L):
    x = p_ref[...]
    prev, nxt = _shift_rows(x, L, L)
    hy = prev * cw_ref[0:1, :] + x * cw_ref[1:2, :] + nxt * cw_ref[2:3, :] + cb_ref[...]
    hv = hy[:, 0:HY_W]
    hx1 = hy[:, HY_W:2 * HY_W]
    hx2 = hy[:, 2 * HY_W:3 * HY_W]

    def conv(z, o):
        spec = _dot_split(fh_ref[...], fl_ref[...], z)
        xr = spec[:L]
        xi = spec[L:]
        p = ps_ref[o]
        q = qs_ref[o]
        yr = xr * p - xi * q
        yi = xr * q + xi * p2s_ref[o]
        return _dot_split(gh_ref[...], gl_ref[...], jnp.concatenate([yr, yi], axis=0))

    z = hx1 * conv(hv, 0)
    o_ref[...] = (hx2 * conv(z, 1)).astype(o_ref.dtype)


def _hyena(ph, cw, cb, spectra, l, B, L):
    fh, fl, gh, gl = _dft_tables(L)
    ps, qs, p2s = spectra
    const = lambda shape: _resident(shape, lambda b: (0,) * len(shape))
    spec_blk = _resident((None, 2, L, HY_W), lambda b: (l, 0, 0, 0))
    return pl.pallas_call(
        functools.partial(_hy_body, L=L),
        grid=(B,),
        in_specs=[pl.BlockSpec((L, 3 * HY_W), lambda b: (b, 0)),
                  pl.BlockSpec((None, 8, 3 * HY_W), lambda b: (l, 0, 0)),
                  pl.BlockSpec((None, 1, 3 * HY_W), lambda b: (l, 0, 0)),
                  const((2 * L, L)), const((2 * L, L)), const((L, 2 * L)), const((L, 2 * L)),
                  spec_blk, spec_blk, spec_blk],
        out_specs=pl.BlockSpec((L, HY_W), lambda b: (b, 0)),
        out_shape=jax.ShapeDtypeStruct((B * L, HY_W), MXU_DT),
        compiler_params=_params(),
        name="hyena_%d" % L,
    )(ph, cw, cb, fh, fl, gh, gl, ps, qs, p2s)


def _out_body(mr_ref, mh_ref, mm_ref, wr_ref, wh_ref, wm_ref, x_ref, g1_ref, sc2_ref, sh2_ref,
              gpost_ref, gpre_ref, x1_ref, h2_ref):
    mix = (jnp.dot(mr_ref[...], wr_ref[...], preferred_element_type=F32)
           + jnp.dot(mh_ref[...], wh_ref[...], preferred_element_type=F32)
           + jnp.dot(mm_ref[...], wm_ref[...], preferred_element_type=F32))
    x1 = x_ref[...] + g1_ref[...] * (_rms(mix, D_MODEL) * gpost_ref[...])
    x1_ref[...] = x1
    h2 = _rms(x1, D_MODEL) * gpre_ref[...] * (1.0 + sc2_ref[...]) + sh2_ref[...]
    h2_ref[...] = h2.astype(h2_ref.dtype)


def _out_proj(mr, mh, mm, wr, wh, wm, x, mod, gpost, gpre, l, row_of_tile, tm):
    n = x.shape[0]
    tile = lambda w: pl.BlockSpec((tm, w), lambda i: (i, 0))
    layer = lambda r, c: pl.BlockSpec((None, r, c), lambda i: (l, 0, 0))
    return pl.pallas_call(
        _out_body,
        grid=(n // tm,),
        in_specs=[tile(HWP), tile(HY_W), tile(HWP),
                  layer(HWP, D_MODEL), layer(HY_W, D_MODEL), layer(HWP, D_MODEL),
                  tile(D_MODEL),
                  _mod_spec(l, 2, row_of_tile), _mod_spec(l, 4, row_of_tile), _mod_spec(l, 3, row_of_tile),
                  layer(1, D_MODEL), layer(1, D_MODEL)],
        out_specs=[tile(D_MODEL), tile(D_MODEL)],
        out_shape=[jax.ShapeDtypeStruct((n, D_MODEL), F32), jax.ShapeDtypeStruct((n, D_MODEL), MXU_DT)],
        compiler_params=_params(),
        name="out_proj",
    )(mr, mh, mm, wr, wh, wm, x, mod, mod, mod, gpost, gpre)


def _ffn_body(h_ref, wa_ref, wb_ref, cw_ref, cb_ref, wd_ref, x1_ref, g2_ref, gpost_ref, o_ref, acc,
              *, seq_len, tm):
    j = pl.program_id(1)
    h = h_ref[...]
    a = jnp.dot(h, wa_ref[...], preferred_element_type=F32)
    b = jnp.dot(h, wb_ref[...], preferred_element_type=F32)
    prev, nxt = _shift_rows(a, seq_len, tm)
    a = prev * cw_ref[0:1, :] + a * cw_ref[1:2, :] + nxt * cw_ref[2:3, :] + cb_ref[...]
    gelu = 0.5 * a * (1.0 + jnp.tanh(math.sqrt(2.0 / math.pi) * (a + 0.044715 * (a * a * a))))
    part = jnp.dot((gelu * b).astype(MXU_DT), wd_ref[...], preferred_element_type=F32)

    @pl.when(j == 0)
    def _():
        acc[...] = part

    @pl.when(j > 0)
    def _():
        acc[...] += part

    @pl.when(j == pl.num_programs(1) - 1)
    def _():
        o_ref[...] = x1_ref[...] + g2_ref[...] * (_rms(acc[...], D_MODEL) * gpost_ref[...])


def _conv_ffn(h2, w_up, cw, cb, w_down, x1, mod, gpost, l, row_of_tile, seq_len, tm, tf):
    n = h2.shape[0]
    nf = D_FF // tf
    return pl.pallas_call(
        functools.partial(_ffn_body, seq_len=seq_len, tm=tm),
        grid=(n // tm, nf),
        in_specs=[pl.BlockSpec((tm, D_MODEL), lambda i, j: (i, 0)),
                  pl.BlockSpec((None, D_MODEL, tf), lambda i, j: (l, 0, j)),
                  pl.BlockSpec((None, D_MODEL, tf), lambda i, j: (l, 0, j + nf)),
                  pl.BlockSpec((None, 8, tf), lambda i, j: (l, 0, j)),
                  pl.BlockSpec((None, 1, tf), lambda i, j: (l, 0, j)),
                  pl.BlockSpec((None, tf, D_MODEL), lambda i, j: (l, j, 0)),
                  pl.BlockSpec((tm, D_MODEL), lambda i, j: (i, 0)),
                  pl.BlockSpec((None, None, None, 1, D_MODEL), lambda i, j: (l, 5, row_of_tile(i), 0, 0)),
                  pl.BlockSpec((None, 1, D_MODEL), lambda i, j: (l, 0, 0))],
        out_specs=pl.BlockSpec((tm, D_MODEL), lambda i, j: (i, 0)),
        out_shape=jax.ShapeDtypeStruct((n, D_MODEL), F32),
        scratch_shapes=[pltpu.VMEM((tm, D_MODEL), F32)],
        compiler_params=_params(),
        name="conv_ffn",
    )(h2, w_up, w_up, cw, cb, w_down, x1, mod, gpost)


def _pad_rows8(w):
    return jnp.concatenate([w, jnp.zeros((w.shape[0], 8 - w.shape[1], w.shape[2]), w.dtype)], axis=1)


def kernel(x_prompt, x_sample, c, state_ret, state_mlstm_c, state_mlstm_n, state_mlstm_m, c_ctx,
           norm_mix_pre, norm_mix_post, norm_ffn_pre, norm_ffn_post, w_mod, b_mod, w_in, w_out,
           ret_decay_logit, ret_norm_g, hy_conv_w, hy_conv_b, hy_f_w1, hy_f_b1, hy_f_w2, hy_f_b2,
           hy_f_w3, hy_f_b3, hy_sin_freq, hy_bias, ml_gate_bias, ml_norm_g,
           w_up, ffn_conv_w, ffn_conv_b, w_down):
    BP, LP, _ = x_prompt.shape
    BS, LS, _ = x_sample.shape

    sec = lambda a, i, w: lax.slice_in_dim(a, i, i + w, axis=2)
    w_in_p = jnp.concatenate(
        [_pad_heads(sec(w_in, 0, HW), rope=True), _pad_heads(sec(w_in, HW, HW), rope=True),
         _pad_heads(sec(w_in, 2 * HW, HW)), _pad_heads(sec(w_in, 3 * HW, HW)),
         sec(w_in, 4 * HW, 3 * HY_W)]
        + [_pad_heads(sec(w_in, 4 * HW + 3 * HY_W + i * HW, HW)) for i in range(4)]
        + [sec(w_in, 8 * HW + 3 * HY_W, 16), jnp.zeros((DEPTH, D_MODEL, 112), w_in.dtype)],
        axis=2).astype(MXU_DT)
    w_out_r = _pad_heads(lax.slice_in_dim(w_out, 0, HW, axis=1), axis=1).astype(MXU_DT)
    w_out_h = lax.slice_in_dim(w_out, HW, HW + HY_W, axis=1).astype(MXU_DT)
    w_out_m = _pad_heads(lax.slice_in_dim(w_out, HW + HY_W, 2 * HW + HY_W, axis=1), axis=1).astype(MXU_DT)
    w_up_b = w_up.astype(MXU_DT)
    w_down_b = w_down.astype(MXU_DT)
    ret_gain = _pad_heads(ret_norm_g).reshape(DEPTH, 1, HWP)
    ml_gain = _pad_heads(ml_norm_g).reshape(DEPTH, 1, HWP)
    dl = jnp.broadcast_to(ret_decay_logit.reshape(DEPTH, 8, 1), (DEPTH, 8, 128))
    gate_bias = jnp.concatenate([ml_gate_bias.reshape(DEPTH, 1, 16), jnp.zeros((DEPTH, 1, 112), F32)], axis=2)
    hy_cw = _pad_rows8(hy_conv_w)
    hy_cb = hy_conv_b.reshape(DEPTH, 1, 3 * HY_W)
    ffn_cw = _pad_rows8(ffn_conv_w)
    ffn_cb = ffn_conv_b.reshape(DEPTH, 1, D_FF)
    g_mix_pre = norm_mix_pre.reshape(DEPTH, 1, D_MODEL)
    g_mix_post = norm_mix_post.reshape(DEPTH, 1, D_MODEL)
    g_ffn_pre = norm_ffn_pre.reshape(DEPTH, 1, D_MODEL)
    g_ffn_post = norm_ffn_post.reshape(DEPTH, 1, D_MODEL)
    pad2 = lambda a, r, cdim: jnp.pad(a, ((0, 0), (0, r - a.shape[1]), (0, cdim - a.shape[2])))
    fw1 = pad2(hy_f_w1, 128, 128)
    fb1 = pad2(hy_f_b1.reshape(DEPTH, 1, -1), 1, 128)
    fw2 = pad2(hy_f_w2, 128, 128)
    fb2 = pad2(hy_f_b2.reshape(DEPTH, 1, -1), 1, 128)
    fw3 = pad2(hy_f_w3, 128, 4 * HY_W)
    fb3 = hy_f_b3.reshape(DEPTH, 1, 4 * HY_W)
    ffr = pad2(hy_sin_freq.reshape(DEPTH, 1, -1), 1, 128)

    sr = jnp.zeros(state_ret.shape[:4] + (DHP, DHP), F32)
    half = DH // 2
    sr = sr.at[..., 0:half, 0:DH].set(state_ret[..., 0:half, :])
    sr = sr.at[..., 64:64 + half, 0:DH].set(state_ret[..., half:DH, :])
    c0 = jnp.zeros(state_mlstm_c.shape[:4] + (DHP, 2 * DHP), F32)
    c0 = c0.at[..., 0:DH, 0:DH].set(state_mlstm_c)
    c0 = c0.at[..., 0:DH, DHP:].set(jnp.broadcast_to(state_mlstm_n[..., None], state_mlstm_n.shape + (DHP,)))
    m0 = jnp.broadcast_to(state_mlstm_m.reshape(BS, DEPTH, 8, 1), (BS, DEPTH, 8, 128))

    cvec8 = jnp.concatenate([c_ctx.reshape(1, D_MODEL), c, jnp.zeros((8 - 1 - BS, D_MODEL), F32)], axis=0)
    mod = _mod_all(cvec8, w_mod, b_mod)
    mod = mod.reshape(DEPTH, 8, 6, 1, D_MODEL).transpose(0, 2, 1, 3, 4)
    spectra_p = _filter_spectra(LP, fw1, fb1, fw2, fb2, fw3, fb3, ffr, hy_bias)
    spectra_s = _filter_spectra(LS, fw1, fb1, fw2, fb2, fw3, fb3, ffr, hy_bias)

    xp = x_prompt.reshape(BP * LP, D_MODEL)
    xs = x_sample.reshape(BS * LS, D_MODEL)
    tm = 512
    ffn_tm = 1024
    row_ctx = lambda i: 0
    row_lat_tm = lambda i: 1 + i // (LS // tm)
    row_lat_ffn = lambda i: 1 + i // (LS // ffn_tm)
    new_ret, new_c, new_m = [], [], []

    def layer(x, l, B, L, latent, row_tm, row_ffn, spectra):
        pr, ph, pm, pg = _in_proj(x, mod, g_mix_pre, w_in_p, l, row_tm, tm)
        if latent:
            (mr,) = _retention(pr, dl, ret_gain, l, B, L, True, s0=sr)
            (mm,) = _mlstm(pm, pg, gate_bias, ml_gain, l, B, L, True, c0=c0, m0=m0)
            states = None
        else:
            mr, s_ret = _retention(pr, dl, ret_gain, l, B, L, False)
            mm, c_fin, m_fin = _mlstm(pm, pg, gate_bias, ml_gain, l, B, L, False)
            states = (s_ret, c_fin, m_fin)
        mh = _hyena(ph, hy_cw, hy_cb, spectra, l, B, L)
        x1, h2 = _out_proj(mr, mh, mm, w_out_r, w_out_h, w_out_m, x, mod, g_mix_post, g_ffn_pre, l, row_tm, tm)
        out = _conv_ffn(h2, w_up_b, ffn_cw, ffn_cb, w_down_b, x1, mod, g_ffn_post, l, row_ffn, L, ffn_tm, 512)
        return out, states

    for l in range(DEPTH):
        xp, st = layer(xp, l, BP, LP, False, row_ctx, row_ctx, spectra_p)
        new_ret.append(st[0])
        new_c.append(st[1])
        new_m.append(st[2])
        xs, _ = layer(xs, l, BS, LS, True, row_lat_tm, row_lat_ffn, spectra_s)

    s_ret = jnp.stack(new_ret, axis=1)
    s_ret = _unpad_heads_axis(_unpad_heads_axis(s_ret, 4, rope=True), 5)
    c_aug = jnp.stack(new_c, axis=1)
    out_c = c_aug[..., 0:DH, 0:DH]
    out_n = c_aug[..., 0:DH, DHP]
    out_m = jnp.stack(new_m, axis=1)[..., 0].reshape(BP, DEPTH, 2, HEADS)
    return (xp.reshape(BP, LP, D_MODEL), xs.reshape(BS, LS, D_MODEL), s_ret, out_c, out_n, out_m)
```

```python
import functools
import math

import numpy as np
import jax
import jax.numpy as jnp
from jax import lax
from jax.experimental import pallas as pl
from jax.experimental.pallas import tpu as pltpu

D_MODEL = 1024
DEPTH = 4
GRID_W = 64
HEADS = 4
DH = 96
DHP = 128
HW = HEADS * DH
HWP = HEADS * DHP
HY_W = 256
D_FF = 4 * D_MODEL
CHUNK = 128
N_BANDS = 16
FEAT_W = 1 + 2 * N_BANDS
HY_SHIFT = 0.05
HY_TARGET = 1e-2
HY_SHORT_DECAY_PCT = 0.3
HY_LONG_DECAY_PCT = 1.5
ROPE_BASE = 10000.0
EPS = 1e-6

MXU_DT = jnp.bfloat16
F32 = jnp.float32
VMEM_LIMIT = 56 * 1024 * 1024


def _dot(a, b):
    return jnp.dot(a.astype(MXU_DT), b.astype(MXU_DT), preferred_element_type=F32)


def _dot_nt(a, b):
    return lax.dot_general(a.astype(MXU_DT), b.astype(MXU_DT), (((1,), (1,)), ((), ())),
                           preferred_element_type=F32)


def _dot_tn(a, b):
    return lax.dot_general(a.astype(MXU_DT), b.astype(MXU_DT), (((0,), (0,)), ((), ())),
                           preferred_element_type=F32)


def _split2(x):
    hi = x.astype(MXU_DT)
    lo = (x - hi.astype(F32)).astype(MXU_DT)
    return hi, lo


def _dot_split(a_hi, a_lo, b):
    b_hi, b_lo = _split2(b)
    return (jnp.dot(a_hi, b_hi, preferred_element_type=F32)
            + jnp.dot(a_hi, b_lo, preferred_element_type=F32)
            + jnp.dot(a_lo, b_hi, preferred_element_type=F32))


def _dot3(a, b):
    a_hi, a_lo = _split2(a)
    return _dot_split(a_hi, a_lo, b)


def _dot_exact_lhs(t, x):
    x1 = x.astype(MXU_DT)
    r1 = x - x1.astype(F32)
    x2 = r1.astype(MXU_DT)
    x3 = (r1 - x2.astype(F32)).astype(MXU_DT)
    return (jnp.dot(t, x1, preferred_element_type=F32) + jnp.dot(t, x2, preferred_element_type=F32)
            + jnp.dot(t, x3, preferred_element_type=F32))


def _rms(x, n):
    return x * lax.rsqrt(jnp.sum(x * x, axis=-1, keepdims=True) * (1.0 / n) + EPS)


def _chunk_rows(c, size):
    if isinstance(c, int):
        return slice(c * size, (c + 1) * size)
    return pl.ds(pl.multiple_of(c * size, size), size)


def _log_sigmoid(x):
    return jnp.minimum(x, 0.0) - jnp.log1p(jnp.exp(-jnp.abs(x)))


def _resident(shape, index_map):
    return pl.BlockSpec(shape, index_map, pipeline_mode=pl.Buffered(1))


def _params(**kw):
    return pltpu.CompilerParams(vmem_limit_bytes=VMEM_LIMIT, **kw)


@functools.lru_cache(maxsize=None)
def _dft_tables(L):
    f = np.arange(L, dtype=np.int64)[:, None]
    s = np.arange(L, dtype=np.int64)[None, :]
    ang = np.pi * ((f * s) % (2 * L)).astype(np.float64) / L
    fr = np.cos(ang)
    fi = -np.sin(ang)
    fi[0, :] = np.where(np.arange(L) % 2 == 0, 1.0, -1.0)
    fwd = np.concatenate([fr, fi], axis=0)
    gr = np.cos(ang.T) * (2.0 / (2 * L))
    gr[:, 0] = 1.0 / (2 * L)
    gi = -np.sin(ang.T) * (2.0 / (2 * L))
    gi[:, 0] = np.where(np.arange(L) % 2 == 0, 1.0, -1.0) / (2 * L)
    inv = np.concatenate([gr, gi], axis=1)

    def split(m):
        m32 = m.astype(np.float32)
        hi = m32.astype(jnp.bfloat16)
        lo = (m32 - hi.astype(np.float32)).astype(jnp.bfloat16)
        return hi, lo

    return split(fwd) + split(inv)


@functools.lru_cache(maxsize=None)
def _filter_tables(L):
    tn = np.arange(L, dtype=np.float64) / L
    bands = np.linspace(1e-4, N_BANDS - 1, N_BANDS)
    ang = 2.0 * math.pi * tn[:, None] * bands[None, :]
    feat = np.zeros((L, 128), np.float32)
    feat[:, 0] = tn
    feat[:, 1:1 + N_BANDS] = np.cos(ang)
    feat[:, 1 + N_BANDS:FEAT_W] = np.sin(ang)
    deltas = np.abs(np.linspace(math.log(HY_TARGET) / HY_LONG_DECAY_PCT,
                                math.log(HY_TARGET) / HY_SHORT_DECAY_PCT, HY_W))
    window = (np.exp(-tn[:, None] * deltas[None, :]) + HY_SHIFT).astype(np.float32)
    return feat, window


@functools.lru_cache(maxsize=None)
def _rope_tables(L):
    rows = L // GRID_W
    row = np.repeat(np.arange(rows, dtype=np.float64), GRID_W)
    col = np.tile(np.arange(GRID_W, dtype=np.float64), rows)
    half = DH // 2
    n_freq = half // 2
    freqs = ROPE_BASE ** (-np.arange(n_freq, dtype=np.float64) / n_freq)
    ang = np.concatenate([row[:, None] * freqs, col[:, None] * freqs], axis=-1)
    cos = np.zeros((L, DHP), np.float32)
    sin = np.zeros((L, DHP), np.float32)
    cos[:, :half] = np.cos(ang)
    cos[:, 64:64 + half] = np.cos(ang)
    sin[:, :half] = -np.sin(ang)
    sin[:, 64:64 + half] = np.sin(ang)
    return cos, sin


@functools.lru_cache(maxsize=None)
def _tri_tables():
    i = np.arange(CHUNK)
    lower = (i[:, None] >= i[None, :]).astype(np.float32)
    upper = (i[:, None] <= i[None, :]).astype(np.float32)
    return lower, upper


def _pad_heads(w, axis=-1, rope=False):
    axis = axis % w.ndim
    parts = []

    def zeros(n):
        shp = list(w.shape)
        shp[axis] = n
        return jnp.zeros(shp, w.dtype)

    for h in range(HEADS):
        blk = lax.slice_in_dim(w, h * DH, (h + 1) * DH, axis=axis)
        if rope:
            half = DH // 2
            parts += [lax.slice_in_dim(blk, 0, half, axis=axis), zeros(64 - half),
                      lax.slice_in_dim(blk, half, DH, axis=axis), zeros(64 - half)]
        else:
            parts += [blk, zeros(DHP - DH)]
    return jnp.concatenate(parts, axis=axis)


def _unpad_heads_axis(x, axis, rope=False):
    if rope:
        half = DH // 2
        return jnp.concatenate([lax.slice_in_dim(x, 0, half, axis=axis),
                                lax.slice_in_dim(x, 64, 64 + half, axis=axis)], axis=axis)
    return lax.slice_in_dim(x, 0, DH, axis=axis)


def _mod_body(c_ref, w_ref, b_ref, o_ref):
    c = c_ref[...]
    s = c * (1.0 / (1.0 + jnp.exp(-c)))
    o_ref[...] = _dot(s, w_ref[...]) + b_ref[...]


def _mod_all(cvec8, w_mod, b_mod):
    tn = 1536
    nj = 6 * D_MODEL // tn
    return pl.pallas_call(
        _mod_body,
        grid=(DEPTH, nj),
        in_specs=[pl.BlockSpec((8, D_MODEL), lambda l, j: (0, 0)),
                  pl.BlockSpec((None, D_MODEL, tn), lambda l, j: (l, 0, j)),
                  pl.BlockSpec((None, 1, tn), lambda l, j: (l, 0, j))],
        out_specs=pl.BlockSpec((None, 8, tn), lambda l, j: (l, 0, j)),
        out_shape=jax.ShapeDtypeStruct((DEPTH, 8, 6 * D_MODEL), F32),
        compiler_params=_params(),
        name="mod_all",
    )(cvec8, w_mod, b_mod.reshape(DEPTH, 1, 6 * D_MODEL))


def _filt_body(feat_ref, win_ref, fh_ref, fl_ref, w1_ref, b1_ref, w2_ref, b2_ref, w3_ref, b3_ref,
               fr_ref, hb_ref, p_ref, q_ref, p2_ref, *, L):
    fr = fr_ref[...]
    h = jnp.sin(fr * (_dot3(feat_ref[...], w1_ref[...]) + b1_ref[...]))
    h = jnp.sin(fr * (_dot3(h, w2_ref[...]) + b2_ref[...]))
    filt = _dot3(h, w3_ref[...]) + b3_ref[...]
    win = win_ref[...]
    row = lax.broadcasted_iota(jnp.int32, (L, 1), 0)
    fh = fh_ref[...]
    fl = fl_ref[...]
    for o in range(2):
        fwd = filt[:, o * HY_W:(o + 1) * HY_W] * win
        bwd = filt[:, (2 + o) * HY_W:(3 + o) * HY_W] * win
        bwd = jnp.where(row == 0, 0.0, bwd)
        a = _dot_split(fh, fl, fwd)
        b = _dot_split(fh, fl, bwd)
        kr = a[:L] + b[:L]
        ki = a[L:] - b[L:]
        nyq = a[L:L + 1] + b[L:L + 1]
        bias = hb_ref[o:o + 1, :]
        p_ref[o] = kr + bias
        q_ref[o] = jnp.where(row == 0, 0.0, ki)
        p2_ref[o] = jnp.where(row == 0, nyq, kr) + bias


def _filter_spectra(L, fw1, fb1, fw2, fb2, fw3, fb3, ffr, hy_bias):
    feat, window = _filter_tables(L)
    fh, fl, _, _ = _dft_tables(L)
    const = lambda shape: _resident(shape, lambda l: (0,) * len(shape))
    per_layer = lambda shape: pl.BlockSpec((None,) + shape, lambda l: (l,) + (0,) * len(shape))
    out_spec = per_layer((2, L, HY_W))
    out_shape = jax.ShapeDtypeStruct((DEPTH, 2, L, HY_W), F32)
    return pl.pallas_call(
        functools.partial(_filt_body, L=L),
        grid=(DEPTH,),
        in_specs=[const((L, 128)), const((L, HY_W)), const((2 * L, L)), const((2 * L, L)),
                  per_layer((128, 128)), per_layer((1, 128)), per_layer((128, 128)), per_layer((1, 128)),
                  per_layer((128, 4 * HY_W)), per_layer((1, 4 * HY_W)), per_layer((1, 128)),
                  per_layer((2, HY_W))],
        out_specs=[out_spec, out_spec, out_spec],
        out_shape=[out_shape, out_shape, out_shape],
        compiler_params=_params(),
        name="hyena_filter_%d" % L,
    )(feat, window, fh, fl, fw1, fb1, fw2, fb2, fw3, fb3, ffr, hy_bias)


W_R0, W_H0, W_M0, W_G0, W_END = 0, 4 * HWP, 4 * HWP + 3 * HY_W, 8 * HWP + 3 * HY_W, 8 * HWP + 3 * HY_W + 128
IN_W = 8 * HW + 3 * HY_W + 16


def _prep_in_body(w_ref, o_ref):
    rows = w_ref.shape[0]
    half = DH // 2

    def zeros(n):
        return jnp.zeros((rows, n), F32)

    def head_tile(src, rope):
        if rope:
            return jnp.concatenate([w_ref[:, src:src + half], zeros(64 - half),
                                    w_ref[:, src + half:src + DH], zeros(64 - half)], axis=1)
        return jnp.concatenate([w_ref[:, src:src + DH], zeros(DHP - DH)], axis=1)

    dst = 0
    for sec in range(4):
        for h in range(HEADS):
            o_ref[:, dst:dst + DHP] = head_tile(sec * HW + h * DH, rope=sec < 2).astype(o_ref.dtype)
            dst += DHP
    o_ref[:, dst:dst + 3 * HY_W] = w_ref[:, 4 * HW:4 * HW + 3 * HY_W].astype(o_ref.dtype)
    dst += 3 * HY_W
    for sec in range(4):
        for h in range(HEADS):
            o_ref[:, dst:dst + DHP] = head_tile(4 * HW + 3 * HY_W + sec * HW + h * DH, False).astype(o_ref.dtype)
            dst += DHP
    g0 = 8 * HW + 3 * HY_W
    o_ref[:, dst:dst + 128] = jnp.concatenate([w_ref[:, g0:g0 + 16], zeros(112)], axis=1).astype(o_ref.dtype)


def _prep_w_in(w_in):
    tr = 256
    return pl.pallas_call(
        _prep_in_body,
        grid=(DEPTH, D_MODEL // tr),
        in_specs=[pl.BlockSpec((None, tr, IN_W), lambda l, i: (l, i, 0))],
        out_specs=pl.BlockSpec((None, tr, W_END), lambda l, i: (l, i, 0)),
        out_shape=jax.ShapeDtypeStruct((DEPTH, D_MODEL, W_END), MXU_DT),
        compiler_params=_params(),
        name="prep_w_in",
    )(w_in)


def _in_body(x_ref, sc_ref, sh_ref, g_ref, w_ref, pr_ref, ph_ref, pm_ref, pg_ref):
    h = _rms(x_ref[...], D_MODEL) * g_ref[...]
    h = (h * (1.0 + sc_ref[...]) + sh_ref[...]).astype(MXU_DT)
    pr_ref[...] = jnp.dot(h, w_ref[:, W_R0:W_H0], preferred_element_type=F32)
    ph_ref[...] = jnp.dot(h, w_ref[:, W_H0:W_M0], preferred_element_type=F32)
    pm_ref[...] = jnp.dot(h, w_ref[:, W_M0:W_G0], preferred_element_type=F32)
    pg_ref[...] = jnp.dot(h, w_ref[:, W_G0:W_END], preferred_element_type=F32)


def _mod_spec(l, which, row_of_tile):
    return pl.BlockSpec((None, None, None, 1, D_MODEL), lambda i: (l, which, row_of_tile(i), 0, 0))


def _in_proj(x, mod, gain, w_in, l, row_of_tile, tm):
    n = x.shape[0]
    shapes = [4 * HWP, 3 * HY_W, 4 * HWP, 128]
    return pl.pallas_call(
        _in_body,
        grid=(n // tm,),
        in_specs=[pl.BlockSpec((tm, D_MODEL), lambda i: (i, 0)),
                  _mod_spec(l, 1, row_of_tile), _mod_spec(l, 0, row_of_tile),
                  pl.BlockSpec((None, 1, D_MODEL), lambda i: (l, 0, 0)),
                  _resident((None, D_MODEL, W_END), lambda i: (l, 0, 0))],
        out_specs=[pl.BlockSpec((tm, w), lambda i: (i, 0)) for w in shapes],
        out_shape=[jax.ShapeDtypeStruct((n, w), F32) for w in shapes],
        compiler_params=_params(),
        name="in_proj",
    )(x, mod, mod, gain, w_in)


def _ret_body(*refs, L, latent):
    if latent:
        (q_ref, k_ref, v_ref, gate_ref, dl_ref, gain_ref, s0_ref, cos_ref, sin_ref,
         o_ref, qs, ks, acc, sst) = refs
    else:
        (q_ref, k_ref, v_ref, gate_ref, dl_ref, gain_ref, _, o_ref, sfin_ref, qs, ks, acc, sst) = refs
    half = DH // 2
    C = CHUNK
    nchunk = L // C
    lg = _log_sigmoid(dl_ref[...])
    rel = (lax.broadcasted_iota(jnp.int32, (C, C), 0) - lax.broadcasted_iota(jnp.int32, (C, C), 1)).astype(F32)
    ri = lax.broadcasted_iota(jnp.int32, (C, 1), 0).astype(F32)

    for h in range(HEADS):
        cols = slice(h * DHP, (h + 1) * DHP)
        q = q_ref[:, cols]
        k = k_ref[:, cols]
        if latent:
            cos = cos_ref[...]
            sin = sin_ref[...]
            q = q * cos + pltpu.roll(q, 64, 1) * sin
            k = k * cos + pltpu.roll(k, 64, 1) * sin
        qs[:, cols] = q
        ks[:, cols] = k * (DH ** -0.5)

    acc[...] = jnp.zeros_like(acc)
    consts = []
    for h in range(HEADS):
        lgf = lg[h:h + 1, :]
        lgb = lg[HEADS + h:HEADS + h + 1, :]
        lgf1 = lgf[:, 0:1]
        lgb1 = lgb[:, 0:1]
        mask = (jnp.where(rel >= 0, jnp.exp(rel * lgf), 0.0)
                + jnp.where(rel <= 0, jnp.exp(-rel * lgb), 0.0))
        consts.append(dict(
            mask=mask,
            qdec_f=jnp.exp((ri + 1.0) * lgf1), qdec_b=jnp.exp((C - ri) * lgb1),
            kdec_f=jnp.exp((C - 1.0 - ri) * lgf1), kdec_b=jnp.exp(ri * lgb1),
            cdec_f=jnp.exp(C * lgf1), cdec_b=jnp.exp(C * lgb1)))
        for d in range(2):
            sst[d, h] = jnp.zeros((DHP, DHP), F32)
            if latent:
                sst[d, h, 0:half, 0:DH] = s0_ref[d, h, 0:half, :]
                sst[d, h, 64:64 + half, 0:DH] = s0_ref[d, h, half:DH, :]

    def step(j):
        rows_f = _chunk_rows(j, C)
        rows_b = _chunk_rows(nchunk - 1 - j, C)
        for h in range(HEADS):
            cols = slice(h * DHP, (h + 1) * DHP)
            cn = consts[h]
            q = qs[rows_f, cols]
            k = ks[rows_f, cols]
            v = v_ref[rows_f, cols]
            s_f = sst[0, h]
            sc = _dot_nt(q, k) * cn["mask"]
            acc[rows_f, cols] += _dot(sc, v) + _dot(q, s_f) * cn["qdec_f"]
            sst[0, h] = s_f * cn["cdec_f"] + _dot_tn(k * cn["kdec_f"], v)
            q = qs[rows_b, cols]
            k = ks[rows_b, cols]
            v = v_ref[rows_b, cols]
            s_b = sst[1, h]
            acc[rows_b, cols] += _dot(q, s_b) * cn["qdec_b"]
            sst[1, h] = s_b * cn["cdec_b"] + _dot_tn(k * cn["kdec_b"], v)

    if nchunk <= 2:
        for j in range(nchunk):
            step(j)
    else:
        pl.loop(0, nchunk)(step)

    if not latent:
        for d in range(2):
            for h in range(HEADS):
                sfin_ref[d, h, 0:half, :] = sst[d, h, 0:half, 0:DH]
                sfin_ref[d, h, half:DH, :] = sst[d, h, 64:64 + half, 0:DH]

    for h in range(HEADS):
        cols = slice(h * DHP, (h + 1) * DHP)
        g = gate_ref[:, cols]
        y = _rms(acc[:, cols], DH) * gain_ref[:, cols] * (g * (1.0 / (1.0 + jnp.exp(-g))))
        o_ref[:, cols] = y.astype(o_ref.dtype)


def _state_spec(l, *tail):
    shape = (None, None, 2, HEADS) + tail
    return pl.BlockSpec(shape, lambda b: (b, l) + (0,) * (len(shape) - 2))


def _retention(pr, dl, gain, l, B, L, latent, state):
    blk = lambda j: pl.BlockSpec((L, HWP), lambda b, j=j: (b, j))
    in_specs = [blk(0), blk(1), blk(2), blk(3),
                pl.BlockSpec((None, 8, 128), lambda b: (l, 0, 0)),
                pl.BlockSpec((None, 1, HWP), lambda b: (l, 0, 0))]
    args = [pr, pr, pr, pr, dl, gain]
    out_specs = [pl.BlockSpec((L, HWP), lambda b: (b, 0))]
    out_shape = [jax.ShapeDtypeStruct((B * L, HWP), MXU_DT)]
    aliases = {}
    if latent:
        cos, sin = _rope_tables(L)
        in_specs += [_state_spec(l, DH, DH),
                     pl.BlockSpec((L, DHP), lambda b: (0, 0)), pl.BlockSpec((L, DHP), lambda b: (0, 0))]
        args += [state, cos, sin]
    else:
        in_specs.append(pl.BlockSpec(memory_space=pl.ANY))
        args.append(state)
        aliases = {len(args) - 1: 1}
        out_specs.append(_state_spec(l, DH, DH))
        out_shape.append(jax.ShapeDtypeStruct(state.shape, F32))
    return pl.pallas_call(
        functools.partial(_ret_body, L=L, latent=latent),
        grid=(B,),
        in_specs=in_specs, out_specs=out_specs, out_shape=out_shape,
        input_output_aliases=aliases,
        scratch_shapes=[pltpu.VMEM((L, HWP), F32), pltpu.VMEM((L, HWP), F32), pltpu.VMEM((L, HWP), F32),
                        pltpu.VMEM((2, HEADS, DHP, DHP), F32)],
        compiler_params=_params(),
        name="retention_%d" % L,
    )(*args)


def _ml_body(*refs, L, latent):
    if latent:
        (q_ref, k_ref, v_ref, og_ref, gates_ref, gb_ref, gain_ref, tl_ref, tu_ref, c0_ref, n0_ref, m0_ref,
         o_ref, acc, bb_s, ub_s, pm_s, brow_s, ibrow_s, mc_s, mx_s, cn_s) = refs
    else:
        (q_ref, k_ref, v_ref, og_ref, gates_ref, gb_ref, gain_ref, tl_ref, tu_ref, _, _,
         o_ref, cfin_ref, nfin_ref, mfin_ref, acc, bb_s, ub_s, pm_s, brow_s, ibrow_s, mc_s, mx_s, cn_s) = refs
    C = CHUNK
    nchunk = L // C
    lane = lax.broadcasted_iota(jnp.int32, (C, DHP), 1)
    tri_r = lax.broadcasted_iota(jnp.int32, (C, C), 0)
    tri_c = lax.broadcasted_iota(jnp.int32, (C, C), 1)
    tl = tl_ref[...]
    tu = tu_ref[...]

    row_i = lax.broadcasted_iota(jnp.int32, (C, 128), 0)

    def cummax_rows(x, suffix):
        s = 1
        while s < C:
            if suffix:
                x = jnp.where(row_i < C - s, jnp.maximum(x, pltpu.roll(x, C - s, 0)), x)
            else:
                x = jnp.where(row_i >= s, jnp.maximum(x, pltpu.roll(x, s, 0)), x)
            s *= 2
        return x

    for c in range(nchunk):
        rows = slice(c * C, (c + 1) * C)
        g = gates_ref[rows, :] + gb_ref[...]
        lf = _log_sigmoid(g)
        pre = _dot_exact_lhs(tl, lf)
        suf = _dot_exact_lhs(tu, lf)
        bc = jnp.where(lane < 8, pre, suf)
        brow_s[c * 16:(c + 1) * 16, :] = bc.T[0:16, :]
        ibrow_s[c * 16:(c + 1) * 16, :] = g.T[0:16, :]
        for d in range(2):
            for h in range(HEADS):
                sr = d * HEADS + h
                b_b = jnp.broadcast_to(bc[:, d * 8 + 4 + h:d * 8 + 5 + h], (C, 128))
                u_b = jnp.broadcast_to(g[:, d * 8 + h:d * 8 + h + 1], (C, 128)) - b_b
                bb_s[sr, rows, :] = b_b
                ub_s[sr, rows, :] = u_b
                pm_s[sr, rows, :] = cummax_rows(u_b, suffix=(d == 1))

    for d in range(2):
        for h in range(HEADS):
            sr = d * HEADS + h
            m = m0_ref[sr:sr + 1, :] if latent else jnp.zeros((1, 128), F32)
            for j in range(nchunk):
                c = j if d == 0 else nchunk - 1 - j
                last = c * C + (C - 1 if d == 0 else 0)
                mx = jnp.maximum(m, pm_s[sr, last:last + 1, :])
                mc_s[c * 8 + sr:c * 8 + sr + 1, :] = m
                mx_s[c * 8 + sr:c * 8 + sr + 1, :] = mx
                m = bb_s[sr, last:last + 1, :] + mx
            if not latent:
                mfin_ref[sr:sr + 1, :] = m

    acc[...] = jnp.zeros_like(acc)
    cn_s[...] = jnp.zeros_like(cn_s)
    if latent:
        for d in range(2):
            for h in range(HEADS):
                cn_s[d, h, 0:DH, 0:DH] = c0_ref[d, h]
                n_row = jnp.concatenate([n0_ref[d, h:h + 1, :], jnp.zeros((1, DHP - DH), F32)], axis=1)
                cn_s[d, h, :, DHP:] = jnp.broadcast_to(n_row, (DHP, DHP)).T
    ones_blk = jnp.ones((C, DHP), MXU_DT)

    def step(j):
        for d in range(2):
            c = j if d == 0 else nchunk - 1 - j
            rows = _chunk_rows(c, C)
            brs = brow_s[_chunk_rows(c, 16), :]
            ibrs = ibrow_s[_chunk_rows(c, 16), :]
            mcs = mc_s[_chunk_rows(c, 8), :]
            mxs = mx_s[_chunk_rows(c, 8), :]
            causal = (tri_r >= tri_c) if d == 0 else (tri_r <= tri_c)
            for h in range(HEADS):
                cols = slice(h * DHP, (h + 1) * DHP)
                ic = d * 8 + h
                fc = d * 8 + 4 + h
                sr = d * HEADS + h
                m_c = mcs[sr:sr + 1, :]
                mx = mxs[sr:sr + 1, :]
                q = q_ref[rows, cols].astype(MXU_DT)
                k = k_ref[rows, cols] * (DH ** -0.5)
                v1 = jnp.concatenate([v_ref[rows, cols].astype(MXU_DT), ones_blk], axis=1)
                cn = cn_s[d, h]
                urow = ibrs[ic:ic + 1, :] - brs[fc:fc + 1, :]
                mb = jnp.maximum(m_c, pm_s[sr, rows, :])
                w_intra = jnp.exp(jnp.where(causal, urow - mb, -jnp.inf))
                w_inter = jnp.exp(m_c - mb)
                s = _dot_nt(q, k) * w_intra
                tot = (jnp.dot(s.astype(MXU_DT), v1, preferred_element_type=F32)
                       + jnp.concatenate([w_inter, w_inter], axis=1) * _dot(q, cn))
                den = tot[:, DHP:]
                floor = jnp.exp(-(bb_s[sr, rows, :] + mb))
                acc[rows, cols] += tot[:, :DHP] * (1.0 / jnp.maximum(jnp.abs(den), floor))
                kw = k * jnp.exp(ub_s[sr, rows, :] - mx)
                w_prev = jnp.exp(m_c - mx)
                cn_s[d, h] = jnp.concatenate([w_prev, w_prev], axis=1) * cn + _dot_tn(kw, v1)

    if nchunk <= 2:
        for j in range(nchunk):
            step(j)
    else:
        pl.loop(0, nchunk)(step)

    if not latent:
        for d in range(2):
            for h in range(HEADS):
                cfin_ref[d, h] = cn_s[d, h, 0:DH, 0:DH]
                nfin_ref[d, h:h + 1, :] = cn_s[d, h, :, DHP:].T[0:1, 0:DH]

    for h in range(HEADS):
        cols = slice(h * DHP, (h + 1) * DHP)
        g = og_ref[:, cols]
        y = _rms(acc[:, cols], DH) * gain_ref[:, cols] * (1.0 / (1.0 + jnp.exp(-g)))
        o_ref[:, cols] = y.astype(o_ref.dtype)


def _mlstm(pm, pg, gb, gain, l, B, L, latent, c_state, n_state, m0=None):
    tl, tu = _tri_tables()
    tl = tl.astype(MXU_DT)
    tu = tu.astype(MXU_DT)
    blk = lambda j: pl.BlockSpec((L, HWP), lambda b, j=j: (b, j))
    in_specs = [blk(0), blk(1), blk(2), blk(3),
                pl.BlockSpec((L, 128), lambda b: (b, 0)),
                pl.BlockSpec((None, 1, 128), lambda b: (l, 0, 0)),
                pl.BlockSpec((None, 1, HWP), lambda b: (l, 0, 0)),
                pl.BlockSpec((CHUNK, CHUNK), lambda b: (0, 0)),
                pl.BlockSpec((CHUNK, CHUNK), lambda b: (0, 0))]
    args = [pm, pm, pm, pm, pg, gb, gain, tl, tu]
    out_specs = [pl.BlockSpec((L, HWP), lambda b: (b, 0))]
    out_shape = [jax.ShapeDtypeStruct((B * L, HWP), MXU_DT)]
    aliases = {}
    if latent:
        in_specs += [_state_spec(l, DH, DH), _state_spec(l, DH),
                     pl.BlockSpec((None, None, 8, 128), lambda b: (b, l, 0, 0))]
        args += [c_state, n_state, m0]
    else:
        in_specs += [pl.BlockSpec(memory_space=pl.ANY), pl.BlockSpec(memory_space=pl.ANY)]
        args += [c_state, n_state]
        aliases = {len(args) - 2: 1, len(args) - 1: 2}
        out_specs += [_state_spec(l, DH, DH), _state_spec(l, DH),
                      pl.BlockSpec((None, 8, 128), lambda b: (b, 0, 0))]
        out_shape += [jax.ShapeDtypeStruct(c_state.shape, F32), jax.ShapeDtypeStruct(n_state.shape, F32),
                      jax.ShapeDtypeStruct((B, 8, 128), F32)]
    nchunk = L // CHUNK
    return pl.pallas_call(
        functools.partial(_ml_body, L=L, latent=latent),
        grid=(B,),
        in_specs=in_specs, out_specs=out_specs, out_shape=out_shape,
        input_output_aliases=aliases,
        scratch_shapes=[pltpu.VMEM((L, HWP), F32),
                        pltpu.VMEM((8, L, 128), F32), pltpu.VMEM((8, L, 128), F32), pltpu.VMEM((8, L, 128), F32),
                        pltpu.VMEM((nchunk * 16, 128), F32), pltpu.VMEM((nchunk * 16, 128), F32),
                        pltpu.VMEM((nchunk * 8, 128), F32), pltpu.VMEM((nchunk * 8, 128), F32),
                        pltpu.VMEM((2, HEADS, DHP, 2 * DHP), F32)],
        compiler_params=_params(),
        name="mlstm_%d" % L,
    )(*args)


def _shift_rows(x, seq_len, n_rows):
    pos = lax.broadcasted_iota(jnp.int32, (n_rows, 1), 0) & (seq_len - 1)
    prev = jnp.where(pos == 0, 0.0, pltpu.roll(x, 1, 0))
    nxt = jnp.where(pos == seq_len - 1, 0.0, pltpu.roll(x, n_rows - 1, 0))
    return prev, nxt


def _hy_body(p_ref, cw_ref, cb_ref, fh_ref, fl_ref, gh_ref, gl_ref, ps_ref, qs_ref, p2s_ref, o_ref, *, L):
    x = p_ref[...]
    prev, nxt = _shift_rows(x, L, L)
    hy = prev * cw_ref[0:1, :] + x * cw_ref[1:2, :] + nxt * cw_ref[2:3, :] + cb_ref[...]
    hv = hy[:, 0:HY_W]
    hx1 = hy[:, HY_W:2 * HY_W]
    hx2 = hy[:, 2 * HY_W:3 * HY_W]

    def conv(z, o):
        spec = _dot_split(fh_ref[...], fl_ref[...], z)
        xr = spec[:L]
        xi = spec[L:]
        p = ps_ref[o]
        q = qs_ref[o]
        yr = xr * p - xi * q
        yi = xr * q + xi * p2s_ref[o]
        return _dot_split(gh_ref[...], gl_ref[...], jnp.concatenate([yr, yi], axis=0))

    z = hx1 * conv(hv, 0)
    o_ref[...] = (hx2 * conv(z, 1)).astype(o_ref.dtype)


def _hyena(ph, cw, cb, spectra, l, B, L):
    fh, fl, gh, gl = _dft_tables(L)
    ps, qs, p2s = spectra
    const = lambda shape: _resident(shape, lambda b: (0,) * len(shape))
    spec_blk = _resident((None, 2, L, HY_W), lambda b: (l, 0, 0, 0))
    return pl.pallas_call(
        functools.partial(_hy_body, L=L),
        grid=(B,),
        in_specs=[pl.BlockSpec((L, 3 * HY_W), lambda b: (b, 0)),
                  pl.BlockSpec((None, 8, 3 * HY_W), lambda b: (l, 0, 0)),
                  pl.BlockSpec((None, 1, 3 * HY_W), lambda b: (l, 0, 0)),
                  const((2 * L, L)), const((2 * L, L)), const((L, 2 * L)), const((L, 2 * L)),
                  spec_blk, spec_blk, spec_blk],
        out_specs=pl.BlockSpec((L, HY_W), lambda b: (b, 0)),
        out_shape=jax.ShapeDtypeStruct((B * L, HY_W), MXU_DT),
        compiler_params=_params(),
        name="hyena_%d" % L,
    )(ph, cw, cb, fh, fl, gh, gl, ps, qs, p2s)


def _out_body(mr_ref, mh_ref, mm_ref, wr_ref, wh_ref, wm_ref, x_ref, g1_ref, sc2_ref, sh2_ref,
              gpost_ref, gpre_ref, x1_ref, h2_ref):
    mix = (jnp.dot(mr_ref[...], wr_ref[...], preferred_element_type=F32)
           + jnp.dot(mh_ref[...], wh_ref[...], preferred_element_type=F32)
           + jnp.dot(mm_ref[...], wm_ref[...], preferred_element_type=F32))
    x1 = x_ref[...] + g1_ref[...] * (_rms(mix, D_MODEL) * gpost_ref[...])
    x1_ref[...] = x1
    h2 = _rms(x1, D_MODEL) * gpre_ref[...] * (1.0 + sc2_ref[...]) + sh2_ref[...]
    h2_ref[...] = h2.astype(h2_ref.dtype)


def _out_proj(mr, mh, mm, wr, wh, wm, x, mod, gpost, gpre, l, row_of_tile, tm):
    n = x.shape[0]
    tile = lambda w: pl.BlockSpec((tm, w), lambda i: (i, 0))
    layer = lambda r, c: pl.BlockSpec((None, r, c), lambda i: (l, 0, 0))
    return pl.pallas_call(
        _out_body,
        grid=(n // tm,),
        in_specs=[tile(HWP), tile(HY_W), tile(HWP),
                  layer(HWP, D_MODEL), layer(HY_W, D_MODEL), layer(HWP, D_MODEL),
                  tile(D_MODEL),
                  _mod_spec(l, 2, row_of_tile), _mod_spec(l, 4, row_of_tile), _mod_spec(l, 3, row_of_tile),
                  layer(1, D_MODEL), layer(1, D_MODEL)],
        out_specs=[tile(D_MODEL), tile(D_MODEL)],
        out_shape=[jax.ShapeDtypeStruct((n, D_MODEL), F32), jax.ShapeDtypeStruct((n, D_MODEL), MXU_DT)],
        compiler_params=_params(),
        name="out_proj",
    )(mr, mh, mm, wr, wh, wm, x, mod, mod, mod, gpost, gpre)


def _ffn_body(h_ref, wa_ref, wb_ref, cw_ref, cb_ref, wd_ref, x1_ref, g2_ref, gpost_ref, o_ref, acc,
              *, seq_len, tm):
    j = pl.program_id(1)
    h = h_ref[...]
    a = jnp.dot(h, wa_ref[...].astype(MXU_DT), preferred_element_type=F32)
    b = jnp.dot(h, wb_ref[...].astype(MXU_DT), preferred_element_type=F32)
    prev, nxt = _shift_rows(a, seq_len, tm)
    a = prev * cw_ref[0:1, :] + a * cw_ref[1:2, :] + nxt * cw_ref[2:3, :] + cb_ref[...]
    gelu = 0.5 * a * (1.0 + jnp.tanh(math.sqrt(2.0 / math.pi) * (a + 0.044715 * (a * a * a))))
    part = jnp.dot((gelu * b).astype(MXU_DT), wd_ref[...].astype(MXU_DT), preferred_element_type=F32)

    @pl.when(j == 0)
    def _():
        acc[...] = part

    @pl.when(j > 0)
    def _():
        acc[...] += part

    @pl.when(j == pl.num_programs(1) - 1)
    def _():
        o_ref[...] = x1_ref[...] + g2_ref[...] * (_rms(acc[...], D_MODEL) * gpost_ref[...])


def _conv_ffn(h2, w_up, cw, cb, w_down, x1, mod, gpost, l, row_of_tile, seq_len, tm, tf):
    n = h2.shape[0]
    nf = D_FF // tf
    return pl.pallas_call(
        functools.partial(_ffn_body, seq_len=seq_len, tm=tm),
        grid=(n // tm, nf),
        in_specs=[pl.BlockSpec((tm, D_MODEL), lambda i, j: (i, 0)),
                  pl.BlockSpec((None, D_MODEL, tf), lambda i, j: (l, 0, j)),
                  pl.BlockSpec((None, D_MODEL, tf), lambda i, j: (l, 0, j + nf)),
                  pl.BlockSpec((None, 8, tf), lambda i, j: (l, 0, j)),
                  pl.BlockSpec((None, 1, tf), lambda i, j: (l, 0, j)),
                  pl.BlockSpec((None, tf, D_MODEL), lambda i, j: (l, j, 0)),
                  pl.BlockSpec((tm, D_MODEL), lambda i, j: (i, 0)),
                  pl.BlockSpec((None, None, None, 1, D_MODEL), lambda i, j: (l, 5, row_of_tile(i), 0, 0)),
                  pl.BlockSpec((None, 1, D_MODEL), lambda i, j: (l, 0, 0))],
        out_specs=pl.BlockSpec((tm, D_MODEL), lambda i, j: (i, 0)),
        out_shape=jax.ShapeDtypeStruct((n, D_MODEL), F32),
        scratch_shapes=[pltpu.VMEM((tm, D_MODEL), F32)],
        compiler_params=_params(),
        name="conv_ffn",
    )(h2, w_up, w_up, cw, cb, w_down, x1, mod, gpost)


def _pad_rows8(w):
    return jnp.concatenate([w, jnp.zeros((w.shape[0], 8 - w.shape[1], w.shape[2]), w.dtype)], axis=1)


def kernel(x_prompt, x_sample, c, state_ret, state_mlstm_c, state_mlstm_n, state_mlstm_m, c_ctx,
           norm_mix_pre, norm_mix_post, norm_ffn_pre, norm_ffn_post, w_mod, b_mod, w_in, w_out,
           ret_decay_logit, ret_norm_g, hy_conv_w, hy_conv_b, hy_f_w1, hy_f_b1, hy_f_w2, hy_f_b2,
           hy_f_w3, hy_f_b3, hy_sin_freq, hy_bias, ml_gate_bias, ml_norm_g,
           w_up, ffn_conv_w, ffn_conv_b, w_down):
    BP, LP, _ = x_prompt.shape
    BS, LS, _ = x_sample.shape

    w_in_p = _prep_w_in(w_in)
    w_out_r = _pad_heads(lax.slice_in_dim(w_out, 0, HW, axis=1), axis=1).astype(MXU_DT)
    w_out_h = lax.slice_in_dim(w_out, HW, HW + HY_W, axis=1).astype(MXU_DT)
    w_out_m = _pad_heads(lax.slice_in_dim(w_out, HW + HY_W, 2 * HW + HY_W, axis=1), axis=1).astype(MXU_DT)
    ret_gain = _pad_heads(ret_norm_g).reshape(DEPTH, 1, HWP)
    ml_gain = _pad_heads(ml_norm_g).reshape(DEPTH, 1, HWP)
    dl = jnp.broadcast_to(ret_decay_logit.reshape(DEPTH, 8, 1), (DEPTH, 8, 128))
    gate_bias = jnp.concatenate([ml_gate_bias.reshape(DEPTH, 1, 16), jnp.zeros((DEPTH, 1, 112), F32)], axis=2)
    hy_cw = _pad_rows8(hy_conv_w)
    hy_cb = hy_conv_b.reshape(DEPTH, 1, 3 * HY_W)
    ffn_cw = _pad_rows8(ffn_conv_w)
    ffn_cb = ffn_conv_b.reshape(DEPTH, 1, D_FF)
    g_mix_pre = norm_mix_pre.reshape(DEPTH, 1, D_MODEL)
    g_mix_post = norm_mix_post.reshape(DEPTH, 1, D_MODEL)
    g_ffn_pre = norm_ffn_pre.reshape(DEPTH, 1, D_MODEL)
    g_ffn_post = norm_ffn_post.reshape(DEPTH, 1, D_MODEL)
    pad2 = lambda a, r, cdim: jnp.pad(a, ((0, 0), (0, r - a.shape[1]), (0, cdim - a.shape[2])))
    fw1 = pad2(hy_f_w1, 128, 128)
    fb1 = pad2(hy_f_b1.reshape(DEPTH, 1, -1), 1, 128)
    fw2 = pad2(hy_f_w2, 128, 128)
    fb2 = pad2(hy_f_b2.reshape(DEPTH, 1, -1), 1, 128)
    fw3 = pad2(hy_f_w3, 128, 4 * HY_W)
    fb3 = hy_f_b3.reshape(DEPTH, 1, 4 * HY_W)
    ffr = pad2(hy_sin_freq.reshape(DEPTH, 1, -1), 1, 128)

    m0 = jnp.broadcast_to(state_mlstm_m.reshape(BS, DEPTH, 8, 1), (BS, DEPTH, 8, 128))
    new_ret = jnp.zeros((BP, DEPTH, 2, HEADS, DH, DH), F32)
    new_c = jnp.zeros((BP, DEPTH, 2, HEADS, DH, DH), F32)
    new_n = jnp.zeros((BP, DEPTH, 2, HEADS, DH), F32)

    cvec8 = jnp.concatenate([c_ctx.reshape(1, D_MODEL), c, jnp.zeros((8 - 1 - BS, D_MODEL), F32)], axis=0)
    mod = _mod_all(cvec8, w_mod, b_mod)
    mod = mod.reshape(DEPTH, 8, 6, 1, D_MODEL).transpose(0, 2, 1, 3, 4)
    spectra_p = _filter_spectra(LP, fw1, fb1, fw2, fb2, fw3, fb3, ffr, hy_bias)
    spectra_s = _filter_spectra(LS, fw1, fb1, fw2, fb2, fw3, fb3, ffr, hy_bias)

    xp = x_prompt.reshape(BP * LP, D_MODEL)
    xs = x_sample.reshape(BS * LS, D_MODEL)
    tm = 512
    ffn_tm = 1024
    row_ctx = lambda i: 0
    row_lat_tm = lambda i: 1 + i // (LS // tm)
    row_lat_ffn = lambda i: 1 + i // (LS // ffn_tm)
    new_m = []

    def mix_and_ffn(x, pr_ph, mr, mm, l, B, L, row_tm, row_ffn, spectra):
        mh = _hyena(pr_ph, hy_cw, hy_cb, spectra, l, B, L)
        x1, h2 = _out_proj(mr, mh, mm, w_out_r, w_out_h, w_out_m, x, mod, g_mix_post, g_ffn_pre, l, row_tm, tm)
        return _conv_ffn(h2, w_up, ffn_cw, ffn_cb, w_down, x1, mod, g_ffn_post, l, row_ffn, L, ffn_tm, 512)

    for l in range(DEPTH):
        pr, ph, pm, pg = _in_proj(xp, mod, g_mix_pre, w_in_p, l, row_ctx, tm)
        mr, new_ret = _retention(pr, dl, ret_gain, l, BP, LP, False, new_ret)
        mm, new_c, new_n, m_fin = _mlstm(pm, pg, gate_bias, ml_gain, l, BP, LP, False, new_c, new_n)
        new_m.append(m_fin)
        xp = mix_and_ffn(xp, ph, mr, mm, l, BP, LP, row_ctx, row_ctx, spectra_p)
        pr, ph, pm, pg = _in_proj(xs, mod, g_mix_pre, w_in_p, l, row_lat_tm, tm)
        (mr,) = _retention(pr, dl, ret_gain, l, BS, LS, True, state_ret)
        (mm,) = _mlstm(pm, pg, gate_bias, ml_gain, l, BS, LS, True, state_mlstm_c, state_mlstm_n, m0)
        xs = mix_and_ffn(xs, ph, mr, mm, l, BS, LS, row_lat_tm, row_lat_ffn, spectra_s)

    out_m = jnp.stack(new_m, axis=1)[..., 0].reshape(BP, DEPTH, 2, HEADS)
    return (xp.reshape(BP, LP, D_MODEL), xs.reshape(BS, LS, D_MODEL), new_ret, new_c, new_n, out_m)
```

```python
import functools
import math

import numpy as np
import jax
import jax.numpy as jnp
from jax import lax
from jax.experimental import pallas as pl
from jax.experimental.pallas import tpu as pltpu

D_MODEL = 1024
DEPTH = 4
GRID_W = 64
HEADS = 4
DH = 96
DHP = 128
HW = HEADS * DH
HWP = HEADS * DHP
HY_W = 256
D_FF = 4 * D_MODEL
CHUNK = 128
N_BANDS = 16
FEAT_W = 1 + 2 * N_BANDS
HY_SHIFT = 0.05
HY_TARGET = 1e-2
HY_SHORT_DECAY_PCT = 0.3
HY_LONG_DECAY_PCT = 1.5
ROPE_BASE = 10000.0
EPS = 1e-6

MXU_DT = jnp.bfloat16
F32 = jnp.float32
VMEM_LIMIT = 56 * 1024 * 1024


def _dot(a, b):
    return jnp.dot(a.astype(MXU_DT), b.astype(MXU_DT), preferred_element_type=F32)


def _dot_nt(a, b):
    return lax.dot_general(a.astype(MXU_DT), b.astype(MXU_DT), (((1,), (1,)), ((), ())),
                           preferred_element_type=F32)


def _dot_tn(a, b):
    return lax.dot_general(a.astype(MXU_DT), b.astype(MXU_DT), (((0,), (0,)), ((), ())),
                           preferred_element_type=F32)


def _split2(x):
    hi = x.astype(MXU_DT)
    lo = (x - hi.astype(F32)).astype(MXU_DT)
    return hi, lo


def _dot_split(a_hi, a_lo, b):
    b_hi, b_lo = _split2(b)
    return (jnp.dot(a_hi, b_hi, preferred_element_type=F32)
            + jnp.dot(a_hi, b_lo, preferred_element_type=F32)
            + jnp.dot(a_lo, b_hi, preferred_element_type=F32))


def _dot3(a, b):
    a_hi, a_lo = _split2(a)
    return _dot_split(a_hi, a_lo, b)


def _dot_exact_lhs(t, x):
    x1 = x.astype(MXU_DT)
    r1 = x - x1.astype(F32)
    x2 = r1.astype(MXU_DT)
    x3 = (r1 - x2.astype(F32)).astype(MXU_DT)
    return (jnp.dot(t, x1, preferred_element_type=F32) + jnp.dot(t, x2, preferred_element_type=F32)
            + jnp.dot(t, x3, preferred_element_type=F32))


def _rms(x, n):
    return x * lax.rsqrt(jnp.sum(x * x, axis=-1, keepdims=True) * (1.0 / n) + EPS)


def _chunk_rows(c, size):
    if isinstance(c, int):
        return slice(c * size, (c + 1) * size)
    return pl.ds(pl.multiple_of(c * size, size), size)


def _log_sigmoid(x):
    return jnp.minimum(x, 0.0) - jnp.log1p(jnp.exp(-jnp.abs(x)))


def _resident(shape, index_map):
    return pl.BlockSpec(shape, index_map, pipeline_mode=pl.Buffered(1))


def _params(**kw):
    return pltpu.CompilerParams(vmem_limit_bytes=VMEM_LIMIT, **kw)


@functools.lru_cache(maxsize=None)
def _dft_tables(L):
    f = np.arange(L, dtype=np.int64)[:, None]
    s = np.arange(L, dtype=np.int64)[None, :]
    ang = np.pi * ((f * s) % (2 * L)).astype(np.float64) / L
    fr = np.cos(ang)
    fi = -np.sin(ang)
    fi[0, :] = np.where(np.arange(L) % 2 == 0, 1.0, -1.0)
    fwd = np.concatenate([fr, fi], axis=0)
    gr = np.cos(ang.T) * (2.0 / (2 * L))
    gr[:, 0] = 1.0 / (2 * L)
    gi = -np.sin(ang.T) * (2.0 / (2 * L))
    gi[:, 0] = np.where(np.arange(L) % 2 == 0, 1.0, -1.0) / (2 * L)
    inv = np.concatenate([gr, gi], axis=1)
    return fwd.astype(np.float32), inv.astype(np.float32)


@functools.lru_cache(maxsize=None)
def _filter_tables(L):
    tn = np.arange(L, dtype=np.float64) / L
    bands = np.linspace(1e-4, N_BANDS - 1, N_BANDS)
    ang = 2.0 * math.pi * tn[:, None] * bands[None, :]
    feat = np.zeros((L, 128), np.float32)
    feat[:, 0] = tn
    feat[:, 1:1 + N_BANDS] = np.cos(ang)
    feat[:, 1 + N_BANDS:FEAT_W] = np.sin(ang)
    deltas = np.abs(np.linspace(math.log(HY_TARGET) / HY_LONG_DECAY_PCT,
                                math.log(HY_TARGET) / HY_SHORT_DECAY_PCT, HY_W))
    window = (np.exp(-tn[:, None] * deltas[None, :]) + HY_SHIFT).astype(np.float32)
    return feat, window


@functools.lru_cache(maxsize=None)
def _rope_tables(L):
    rows = L // GRID_W
    row = np.repeat(np.arange(rows, dtype=np.float64), GRID_W)
    col = np.tile(np.arange(GRID_W, dtype=np.float64), rows)
    half = DH // 2
    n_freq = half // 2
    freqs = ROPE_BASE ** (-np.arange(n_freq, dtype=np.float64) / n_freq)
    ang = np.concatenate([row[:, None] * freqs, col[:, None] * freqs], axis=-1)
    cos = np.zeros((L, DHP), np.float32)
    sin = np.zeros((L, DHP), np.float32)
    cos[:, :half] = np.cos(ang)
    cos[:, 64:64 + half] = np.cos(ang)
    sin[:, :half] = -np.sin(ang)
    sin[:, 64:64 + half] = np.sin(ang)
    return cos, sin


@functools.lru_cache(maxsize=None)
def _tri_tables():
    i = np.arange(CHUNK)
    lower = (i[:, None] >= i[None, :]).astype(np.float32)
    upper = (i[:, None] <= i[None, :]).astype(np.float32)
    return lower, upper


def _pad_heads(w, axis=-1, rope=False):
    axis = axis % w.ndim
    parts = []

    def zeros(n):
        shp = list(w.shape)
        shp[axis] = n
        return jnp.zeros(shp, w.dtype)

    for h in range(HEADS):
        blk = lax.slice_in_dim(w, h * DH, (h + 1) * DH, axis=axis)
        if rope:
            half = DH // 2
            parts += [lax.slice_in_dim(blk, 0, half, axis=axis), zeros(64 - half),
                      lax.slice_in_dim(blk, half, DH, axis=axis), zeros(64 - half)]
        else:
            parts += [blk, zeros(DHP - DH)]
    return jnp.concatenate(parts, axis=axis)


def _unpad_heads_axis(x, axis, rope=False):
    if rope:
        half = DH // 2
        return jnp.concatenate([lax.slice_in_dim(x, 0, half, axis=axis),
                                lax.slice_in_dim(x, 64, 64 + half, axis=axis)], axis=axis)
    return lax.slice_in_dim(x, 0, DH, axis=axis)


def _mod_body(c_ref, w_ref, b_ref, o_ref):
    c = c_ref[...]
    s = c * (1.0 / (1.0 + jnp.exp(-c)))
    o_ref[...] = _dot(s, w_ref[...]) + b_ref[...]


def _mod_all(cvec8, w_mod, b_mod):
    tn = 1536
    nj = 6 * D_MODEL // tn
    return pl.pallas_call(
        _mod_body,
        grid=(DEPTH, nj),
        in_specs=[pl.BlockSpec((8, D_MODEL), lambda l, j: (0, 0)),
                  pl.BlockSpec((None, D_MODEL, tn), lambda l, j: (l, 0, j)),
                  pl.BlockSpec((None, 1, tn), lambda l, j: (l, 0, j))],
        out_specs=pl.BlockSpec((None, 8, tn), lambda l, j: (l, 0, j)),
        out_shape=jax.ShapeDtypeStruct((DEPTH, 8, 6 * D_MODEL), F32),
        compiler_params=_params(),
        name="mod_all",
    )(cvec8, w_mod, b_mod.reshape(DEPTH, 1, 6 * D_MODEL))


def _filt_body(feat_ref, win_ref, f_ref, w1_ref, b1_ref, w2_ref, b2_ref, w3_ref, b3_ref,
               fr_ref, hb_ref, p_ref, q_ref, p2_ref, *, L):
    fr = fr_ref[...]
    h = jnp.sin(fr * (_dot3(feat_ref[...], w1_ref[...]) + b1_ref[...]))
    h = jnp.sin(fr * (_dot3(h, w2_ref[...]) + b2_ref[...]))
    filt = _dot3(h, w3_ref[...]) + b3_ref[...]
    win = win_ref[...]
    row = lax.broadcasted_iota(jnp.int32, (L, 1), 0)
    taps = filt * jnp.concatenate([win, win, win, win], axis=1)
    lane = lax.broadcasted_iota(jnp.int32, (L, 4 * HY_W), 1)
    taps = jnp.where((row == 0) & (lane >= 2 * HY_W), 0.0, taps)
    spec = _dot(f_ref[...], taps)
    for o in range(2):
        a = spec[:, o * HY_W:(o + 1) * HY_W]
        b = spec[:, (2 + o) * HY_W:(3 + o) * HY_W]
        kr = a[:L] + b[:L]
        ki = a[L:] - b[L:]
        nyq = a[L:L + 1] + b[L:L + 1]
        bias = hb_ref[o:o + 1, :]
        p_ref[o] = kr + bias
        q_ref[o] = jnp.where(row == 0, 0.0, ki)
        p2_ref[o] = jnp.where(row == 0, nyq, kr) + bias


def _filter_spectra(L, fw1, fb1, fw2, fb2, fw3, fb3, ffr, hy_bias):
    feat, window = _filter_tables(L)
    fwd_table, _ = _dft_tables(L)
    const = lambda shape: _resident(shape, lambda l: (0,) * len(shape))
    per_layer = lambda shape: pl.BlockSpec((None,) + shape, lambda l: (l,) + (0,) * len(shape))
    out_spec = per_layer((2, L, HY_W))
    out_shape = jax.ShapeDtypeStruct((DEPTH, 2, L, HY_W), F32)
    return pl.pallas_call(
        functools.partial(_filt_body, L=L),
        grid=(DEPTH,),
        in_specs=[const((L, 128)), const((L, HY_W)), const((2 * L, L)),
                  per_layer((128, 128)), per_layer((1, 128)), per_layer((128, 128)), per_layer((1, 128)),
                  per_layer((128, 4 * HY_W)), per_layer((1, 4 * HY_W)), per_layer((1, 128)),
                  per_layer((2, HY_W))],
        out_specs=[out_spec, out_spec, out_spec],
        out_shape=[out_shape, out_shape, out_shape],
        compiler_params=_params(),
        name="hyena_filter_%d" % L,
    )(feat, window, fwd_table, fw1, fb1, fw2, fb2, fw3, fb3, ffr, hy_bias)


W_R0, W_H0, W_M0, W_G0, W_END = 0, 4 * HWP, 4 * HWP + 3 * HY_W, 8 * HWP + 3 * HY_W, 8 * HWP + 3 * HY_W + 128
IN_W = 8 * HW + 3 * HY_W + 16
FFN_SUB = 256
ROW_SUB = 256


def _prep_in_body(wt_ref, o_ref):
    kb = wt_ref.shape[1]
    half = DH // 2

    def zeros(n):
        return jnp.zeros((n, kb), F32)

    def head_tile(src, rope):
        if rope:
            return jnp.concatenate([wt_ref[src:src + half, :], zeros(64 - half),
                                    wt_ref[src + half:src + DH, :], zeros(64 - half)], axis=0)
        return jnp.concatenate([wt_ref[src:src + DH, :], zeros(DHP - DH)], axis=0)

    tiles = []
    for sec in range(4):
        tiles += [head_tile(sec * HW + h * DH, sec < 2) for h in range(HEADS)]
    tiles += [wt_ref[4 * HW + i * 128:4 * HW + (i + 1) * 128, :] for i in range(3 * HY_W // 128)]
    for sec in range(4):
        tiles += [head_tile(4 * HW + 3 * HY_W + sec * HW + h * DH, False) for h in range(HEADS)]
    g0 = 8 * HW + 3 * HY_W
    tiles.append(jnp.concatenate([wt_ref[g0:g0 + 16, :], zeros(112)], axis=0))
    for t, tile in enumerate(tiles):
        o_ref[:, t * 128:(t + 1) * 128] = tile.T.astype(o_ref.dtype)


def _prep_w_in(w_in_t):
    kb = 256
    return pl.pallas_call(
        _prep_in_body,
        grid=(DEPTH, D_MODEL // kb),
        in_specs=[pl.BlockSpec((None, IN_W, kb), lambda l, i: (l, 0, i))],
        out_specs=pl.BlockSpec((None, kb, W_END), lambda l, i: (l, i, 0)),
        out_shape=jax.ShapeDtypeStruct((DEPTH, D_MODEL, W_END), MXU_DT),
        compiler_params=_params(),
        name="prep_w_in",
    )(w_in_t)


def _in_body(x_ref, sc_ref, sh_ref, g_ref, w_ref, pr_ref, ph_ref, pm_ref, pg_ref):
    h = _rms(x_ref[...], D_MODEL) * g_ref[...]
    h = (h * (1.0 + sc_ref[...]) + sh_ref[...]).astype(MXU_DT)
    pr_ref[...] = jnp.dot(h, w_ref[:, W_R0:W_H0], preferred_element_type=F32)
    ph_ref[...] = jnp.dot(h, w_ref[:, W_H0:W_M0], preferred_element_type=F32)
    pm_ref[...] = jnp.dot(h, w_ref[:, W_M0:W_G0], preferred_element_type=F32)
    pg_ref[...] = jnp.dot(h, w_ref[:, W_G0:W_END], preferred_element_type=F32)


def _mod_spec(l, which, row_of_tile):
    return pl.BlockSpec((None, None, None, 1, D_MODEL), lambda i: (l, which, row_of_tile(i), 0, 0))


def _in_proj(x, mod, gain, w_in, l, row_of_tile, tm):
    n = x.shape[0]
    shapes = [4 * HWP, 3 * HY_W, 4 * HWP, 128]
    return pl.pallas_call(
        _in_body,
        grid=(n // tm,),
        in_specs=[pl.BlockSpec((tm, D_MODEL), lambda i: (i, 0)),
                  _mod_spec(l, 1, row_of_tile), _mod_spec(l, 0, row_of_tile),
                  pl.BlockSpec((None, 1, D_MODEL), lambda i: (l, 0, 0)),
                  _resident((None, D_MODEL, W_END), lambda i: (l, 0, 0))],
        out_specs=[pl.BlockSpec((tm, w), lambda i: (i, 0)) for w in shapes],
        out_shape=[jax.ShapeDtypeStruct((n, w), F32) for w in shapes],
        compiler_params=_params(),
        name="in_proj",
    )(x, mod, mod, gain, w_in)


def _ret_body(*refs, L, latent):
    if latent:
        (q_ref, k_ref, v_ref, gate_ref, dl_ref, gain_ref, s0_ref, cos_ref, sin_ref,
         o_ref, qs, ks, acc, sst) = refs
    else:
        (q_ref, k_ref, v_ref, gate_ref, dl_ref, gain_ref, _, o_ref, sfin_ref, qs, ks, acc, sst) = refs
    half = DH // 2
    C = CHUNK
    nchunk = L // C
    lg = _log_sigmoid(dl_ref[...])
    rel = (lax.broadcasted_iota(jnp.int32, (C, C), 0) - lax.broadcasted_iota(jnp.int32, (C, C), 1)).astype(F32)
    ri = lax.broadcasted_iota(jnp.int32, (C, 1), 0).astype(F32)

    for h in range(HEADS):
        cols = slice(h * DHP, (h + 1) * DHP)
        q = q_ref[:, cols]
        k = k_ref[:, cols]
        if latent:
            cos = cos_ref[...]
            sin = sin_ref[...]
            q = q * cos + pltpu.roll(q, 64, 1) * sin
            k = k * cos + pltpu.roll(k, 64, 1) * sin
        qs[:, cols] = q
        ks[:, cols] = k * (DH ** -0.5)

    acc[...] = jnp.zeros_like(acc)
    consts = []
    for h in range(HEADS):
        lgf = lg[h:h + 1, :]
        lgb = lg[HEADS + h:HEADS + h + 1, :]
        lgf1 = lgf[:, 0:1]
        lgb1 = lgb[:, 0:1]
        mask = (jnp.where(rel >= 0, jnp.exp(rel * lgf), 0.0)
                + jnp.where(rel <= 0, jnp.exp(-rel * lgb), 0.0))
        consts.append(dict(
            mask=mask,
            qdec_f=jnp.exp((ri + 1.0) * lgf1), qdec_b=jnp.exp((C - ri) * lgb1),
            kdec_f=jnp.exp((C - 1.0 - ri) * lgf1), kdec_b=jnp.exp(ri * lgb1),
            cdec_f=jnp.exp(C * lgf1), cdec_b=jnp.exp(C * lgb1)))
        for d in range(2):
            sst[d, h] = jnp.zeros((DHP, DHP), F32)
            if latent:
                sst[d, h, 0:half, 0:DH] = s0_ref[d, h, 0:half, :]
                sst[d, h, 64:64 + half, 0:DH] = s0_ref[d, h, half:DH, :]

    def step(j):
        rows_f = _chunk_rows(j, C)
        rows_b = _chunk_rows(nchunk - 1 - j, C)
        for h in range(HEADS):
            cols = slice(h * DHP, (h + 1) * DHP)
            cn = consts[h]
            q = qs[rows_f, cols]
            k = ks[rows_f, cols]
            v = v_ref[rows_f, cols]
            s_f = sst[0, h]
            sc = _dot_nt(q, k) * cn["mask"]
            acc[rows_f, cols] += _dot(sc, v) + _dot(q, s_f) * cn["qdec_f"]
            sst[0, h] = s_f * cn["cdec_f"] + _dot_tn(k * cn["kdec_f"], v)
            q = qs[rows_b, cols]
            k = ks[rows_b, cols]
            v = v_ref[rows_b, cols]
            s_b = sst[1, h]
            acc[rows_b, cols] += _dot(q, s_b) * cn["qdec_b"]
            sst[1, h] = s_b * cn["cdec_b"] + _dot_tn(k * cn["kdec_b"], v)

    if nchunk <= 2:
        for j in range(nchunk):
            step(j)
    else:
        pl.loop(0, nchunk)(step)

    if not latent:
        for d in range(2):
            for h in range(HEADS):
                sfin_ref[d, h, 0:half, :] = sst[d, h, 0:half, 0:DH]
                sfin_ref[d, h, half:DH, :] = sst[d, h, 64:64 + half, 0:DH]

    for h in range(HEADS):
        cols = slice(h * DHP, (h + 1) * DHP)
        g = gate_ref[:, cols]
        y = _rms(acc[:, cols], DH) * gain_ref[:, cols] * (g * (1.0 / (1.0 + jnp.exp(-g))))
        o_ref[:, cols] = y.astype(o_ref.dtype)


def _state_spec(l, *tail):
    shape = (None, None, 2, HEADS) + tail
    return pl.BlockSpec(shape, lambda b: (b, l) + (0,) * (len(shape) - 2))


def _retention(pr, dl, gain, l, B, L, latent, state):
    blk = lambda j: pl.BlockSpec((L, HWP), lambda b, j=j: (b, j))
    in_specs = [blk(0), blk(1), blk(2), blk(3),
                pl.BlockSpec((None, 8, 128), lambda b: (l, 0, 0)),
                pl.BlockSpec((None, 1, HWP), lambda b: (l, 0, 0))]
    args = [pr, pr, pr, pr, dl, gain]
    out_specs = [pl.BlockSpec((L, HWP), lambda b: (b, 0))]
    out_shape = [jax.ShapeDtypeStruct((B * L, HWP), MXU_DT)]
    aliases = {}
    if latent:
        cos, sin = _rope_tables(L)
        in_specs += [_state_spec(l, DH, DH),
                     pl.BlockSpec((L, DHP), lambda b: (0, 0)), pl.BlockSpec((L, DHP), lambda b: (0, 0))]
        args += [state, cos, sin]
    else:
        in_specs.append(pl.BlockSpec(memory_space=pl.ANY))
        args.append(state)
        aliases = {len(args) - 1: 1}
        out_specs.append(_state_spec(l, DH, DH))
        out_shape.append(jax.ShapeDtypeStruct(state.shape, F32))
    return pl.pallas_call(
        functools.partial(_ret_body, L=L, latent=latent),
        grid=(B,),
        in_specs=in_specs, out_specs=out_specs, out_shape=out_shape,
        input_output_aliases=aliases,
        scratch_shapes=[pltpu.VMEM((L, HWP), F32), pltpu.VMEM((L, HWP), F32), pltpu.VMEM((L, HWP), F32),
                        pltpu.VMEM((2, HEADS, DHP, DHP), F32)],
        compiler_params=_params(),
        name="retention_%d" % L,
    )(*args)


def _ml_body(*refs, L, latent):
    if latent:
        (q_ref, k_ref, v_ref, og_ref, gates_ref, gb_ref, gain_ref, tl_ref, tu_ref, c0_ref, n0_ref, m0_ref,
         o_ref, acc, bb_s, ub_s, pm_s, brow_s, ibrow_s, mc_s, mx_s, cn_s) = refs
    else:
        (q_ref, k_ref, v_ref, og_ref, gates_ref, gb_ref, gain_ref, tl_ref, tu_ref, _, _,
         o_ref, cfin_ref, nfin_ref, mfin_ref, acc, bb_s, ub_s, pm_s, brow_s, ibrow_s, mc_s, mx_s, cn_s) = refs
    C = CHUNK
    nchunk = L // C
    lane = lax.broadcasted_iota(jnp.int32, (C, DHP), 1)
    tri_r = lax.broadcasted_iota(jnp.int32, (C, C), 0)
    tri_c = lax.broadcasted_iota(jnp.int32, (C, C), 1)
    tl = tl_ref[...]
    tu = tu_ref[...]

    row_i = lax.broadcasted_iota(jnp.int32, (C, 128), 0)

    def cummax_rows(x, suffix):
        s = 1
        while s < C:
            if suffix:
                x = jnp.where(row_i < C - s, jnp.maximum(x, pltpu.roll(x, C - s, 0)), x)
            else:
                x = jnp.where(row_i >= s, jnp.maximum(x, pltpu.roll(x, s, 0)), x)
            s *= 2
        return x

    for c in range(nchunk):
        rows = slice(c * C, (c + 1) * C)
        g = gates_ref[rows, :] + gb_ref[...]
        lf = _log_sigmoid(g)
        pre = _dot_exact_lhs(tl, lf)
        suf = _dot_exact_lhs(tu, lf)
        bc = jnp.where(lane < 8, pre, suf)
        brow_s[c * 16:(c + 1) * 16, :] = bc.T[0:16, :]
        ibrow_s[c * 16:(c + 1) * 16, :] = g.T[0:16, :]
        for d in range(2):
            for h in range(HEADS):
                sr = d * HEADS + h
                b_b = jnp.broadcast_to(bc[:, d * 8 + 4 + h:d * 8 + 5 + h], (C, 128))
                u_b = jnp.broadcast_to(g[:, d * 8 + h:d * 8 + h + 1], (C, 128)) - b_b
                bb_s[sr, rows, :] = b_b
                ub_s[sr, rows, :] = u_b
                pm_s[sr, rows, :] = cummax_rows(u_b, suffix=(d == 1))

    for d in range(2):
        for h in range(HEADS):
            sr = d * HEADS + h
            m = m0_ref[sr:sr + 1, :] if latent else jnp.zeros((1, 128), F32)
            for j in range(nchunk):
                c = j if d == 0 else nchunk - 1 - j
                last = c * C + (C - 1 if d == 0 else 0)
                mx = jnp.maximum(m, pm_s[sr, last:last + 1, :])
                mc_s[c * 8 + sr:c * 8 + sr + 1, :] = m
                mx_s[c * 8 + sr:c * 8 + sr + 1, :] = mx
                m = bb_s[sr, last:last + 1, :] + mx
            if not latent:
                mfin_ref[sr:sr + 1, :] = m

    acc[...] = jnp.zeros_like(acc)
    cn_s[...] = jnp.zeros_like(cn_s)
    if latent:
        for d in range(2):
            for h in range(HEADS):
                cn_s[d, h, 0:DH, 0:DH] = c0_ref[d, h]
                n_row = jnp.concatenate([n0_ref[d, h:h + 1, :], jnp.zeros((1, DHP - DH), F32)], axis=1)
                cn_s[d, h, :, DHP:] = jnp.broadcast_to(n_row, (DHP, DHP)).T
    ones_blk = jnp.ones((C, DHP), MXU_DT)

    def step(j):
        for d in range(2):
            c = j if d == 0 else nchunk - 1 - j
            rows = _chunk_rows(c, C)
            brs = brow_s[_chunk_rows(c, 16), :]
            ibrs = ibrow_s[_chunk_rows(c, 16), :]
            mcs = mc_s[_chunk_rows(c, 8), :]
            mxs = mx_s[_chunk_rows(c, 8), :]
            causal = (tri_r >= tri_c) if d == 0 else (tri_r <= tri_c)
            for h in range(HEADS):
                cols = slice(h * DHP, (h + 1) * DHP)
                ic = d * 8 + h
                fc = d * 8 + 4 + h
                sr = d * HEADS + h
                m_c = mcs[sr:sr + 1, :]
                mx = mxs[sr:sr + 1, :]
                q = q_ref[rows, cols].astype(MXU_DT)
                k = k_ref[rows, cols] * (DH ** -0.5)
                v1 = jnp.concatenate([v_ref[rows, cols].astype(MXU_DT), ones_blk], axis=1)
                cn = cn_s[d, h]
                urow = ibrs[ic:ic + 1, :] - brs[fc:fc + 1, :]
                mb = jnp.maximum(m_c, pm_s[sr, rows, :])
                w_intra = jnp.exp(jnp.where(causal, urow - mb, -jnp.inf))
                w_inter = jnp.exp(m_c - mb)
                s = _dot_nt(q, k) * w_intra
                tot = (jnp.dot(s.astype(MXU_DT), v1, preferred_element_type=F32)
                       + jnp.concatenate([w_inter, w_inter], axis=1) * _dot(q, cn))
                den = tot[:, DHP:]
                floor = jnp.exp(-(bb_s[sr, rows, :] + mb))
                acc[rows, cols] += tot[:, :DHP] * (1.0 / jnp.maximum(jnp.abs(den), floor))
                kw = k * jnp.exp(ub_s[sr, rows, :] - mx)
                w_prev = jnp.exp(m_c - mx)
                cn_s[d, h] = jnp.concatenate([w_prev, w_prev], axis=1) * cn + _dot_tn(kw, v1)

    if nchunk <= 2:
        for j in range(nchunk):
            step(j)
    else:
        pl.loop(0, nchunk)(step)

    if not latent:
        for d in range(2):
            for h in range(HEADS):
                cfin_ref[d, h] = cn_s[d, h, 0:DH, 0:DH]
                nfin_ref[d, h:h + 1, :] = cn_s[d, h, :, DHP:].T[0:1, 0:DH]

    for h in range(HEADS):
        cols = slice(h * DHP, (h + 1) * DHP)
        g = og_ref[:, cols]
        y = _rms(acc[:, cols], DH) * gain_ref[:, cols] * (1.0 / (1.0 + jnp.exp(-g)))
        o_ref[:, cols] = y.astype(o_ref.dtype)


def _mlstm(pm, pg, gb, gain, l, B, L, latent, c_state, n_state, m0=None):
    tl, tu = _tri_tables()
    tl = tl.astype(MXU_DT)
    tu = tu.astype(MXU_DT)
    blk = lambda j: pl.BlockSpec((L, HWP), lambda b, j=j: (b, j))
    in_specs = [blk(0), blk(1), blk(2), blk(3),
                pl.BlockSpec((L, 128), lambda b: (b, 0)),
                pl.BlockSpec((None, 1, 128), lambda b: (l, 0, 0)),
                pl.BlockSpec((None, 1, HWP), lambda b: (l, 0, 0)),
                pl.BlockSpec((CHUNK, CHUNK), lambda b: (0, 0)),
                pl.BlockSpec((CHUNK, CHUNK), lambda b: (0, 0))]
    args = [pm, pm, pm, pm, pg, gb, gain, tl, tu]
    out_specs = [pl.BlockSpec((L, HWP), lambda b: (b, 0))]
    out_shape = [jax.ShapeDtypeStruct((B * L, HWP), MXU_DT)]
    aliases = {}
    if latent:
        in_specs += [_state_spec(l, DH, DH), _state_spec(l, DH),
                     pl.BlockSpec((None, None, 8, 128), lambda b: (b, l, 0, 0))]
        args += [c_state, n_state, m0]
    else:
        in_specs += [pl.BlockSpec(memory_space=pl.ANY), pl.BlockSpec(memory_space=pl.ANY)]
        args += [c_state, n_state]
        aliases = {len(args) - 2: 1, len(args) - 1: 2}
        out_specs += [_state_spec(l, DH, DH), _state_spec(l, DH),
                      pl.BlockSpec((None, 8, 128), lambda b: (b, 0, 0))]
        out_shape += [jax.ShapeDtypeStruct(c_state.shape, F32), jax.ShapeDtypeStruct(n_state.shape, F32),
                      jax.ShapeDtypeStruct((B, 8, 128), F32)]
    nchunk = L // CHUNK
    return pl.pallas_call(
        functools.partial(_ml_body, L=L, latent=latent),
        grid=(B,),
        in_specs=in_specs, out_specs=out_specs, out_shape=out_shape,
        input_output_aliases=aliases,
        scratch_shapes=[pltpu.VMEM((L, HWP), F32),
                        pltpu.VMEM((8, L, 128), F32), pltpu.VMEM((8, L, 128), F32), pltpu.VMEM((8, L, 128), F32),
                        pltpu.VMEM((nchunk * 16, 128), F32), pltpu.VMEM((nchunk * 16, 128), F32),
                        pltpu.VMEM((nchunk * 8, 128), F32), pltpu.VMEM((nchunk * 8, 128), F32),
                        pltpu.VMEM((2, HEADS, DHP, 2 * DHP), F32)],
        compiler_params=_params(),
        name="mlstm_%d" % L,
    )(*args)


def _shift_rows(x, seq_len, n_rows):
    pos = lax.broadcasted_iota(jnp.int32, (n_rows, 1), 0) & (seq_len - 1)
    prev = jnp.where(pos == 0, 0.0, pltpu.roll(x, 1, 0))
    nxt = jnp.where(pos == seq_len - 1, 0.0, pltpu.roll(x, n_rows - 1, 0))
    return prev, nxt


def _hy_body(p_ref, cw_ref, cb_ref, f_ref, g_ref, ps_ref, qs_ref, p2s_ref, o_ref, fh_ref, gh_ref, *, L):
    @pl.when(pl.program_id(0) == 0)
    def _():
        fh_ref[...] = f_ref[...].astype(MXU_DT)
        gh_ref[...] = g_ref[...].astype(MXU_DT)

    x = p_ref[...]
    prev, nxt = _shift_rows(x, L, L)
    hy = prev * cw_ref[0:1, :] + x * cw_ref[1:2, :] + nxt * cw_ref[2:3, :] + cb_ref[...]
    hv = hy[:, 0:HY_W]
    hx1 = hy[:, HY_W:2 * HY_W]
    hx2 = hy[:, 2 * HY_W:3 * HY_W]

    def conv(z, o):
        spec = _dot(fh_ref[...], z)
        xr = spec[:L]
        xi = spec[L:]
        p = ps_ref[o]
        q = qs_ref[o]
        yr = xr * p - xi * q
        yi = xr * q + xi * p2s_ref[o]
        return _dot(gh_ref[...], jnp.concatenate([yr, yi], axis=0))

    z = hx1 * conv(hv, 0)
    o_ref[...] = (hx2 * conv(z, 1)).astype(o_ref.dtype)


def _hyena(ph, cw, cb, spectra, l, B, L):
    fwd_table, inv_table = _dft_tables(L)
    ps, qs, p2s = spectra
    const = lambda shape: _resident(shape, lambda b: (0,) * len(shape))
    spec_blk = _resident((None, 2, L, HY_W), lambda b: (l, 0, 0, 0))
    return pl.pallas_call(
        functools.partial(_hy_body, L=L),
        grid=(B,),
        in_specs=[pl.BlockSpec((L, 3 * HY_W), lambda b: (b, 0)),
                  pl.BlockSpec((None, 8, 3 * HY_W), lambda b: (l, 0, 0)),
                  pl.BlockSpec((None, 1, 3 * HY_W), lambda b: (l, 0, 0)),
                  const((2 * L, L)), const((L, 2 * L)),
                  spec_blk, spec_blk, spec_blk],
        out_specs=pl.BlockSpec((L, HY_W), lambda b: (b, 0)),
        out_shape=jax.ShapeDtypeStruct((B * L, HY_W), MXU_DT),
        scratch_shapes=[pltpu.VMEM((2 * L, L), MXU_DT), pltpu.VMEM((L, 2 * L), MXU_DT)],
        compiler_params=_params(),
        name="hyena_%d" % L,
    )(ph, cw, cb, fwd_table, inv_table, ps, qs, p2s)


def _out_body(mr_ref, mh_ref, mm_ref, wr_ref, wh_ref, wm_ref, x_ref, g1_ref, sc2_ref, sh2_ref,
              gpost_ref, gpre_ref, x1_ref, h2_ref):
    for r in range(x_ref.shape[0] // ROW_SUB):
        rs = slice(r * ROW_SUB, (r + 1) * ROW_SUB)
        mix = (jnp.dot(mr_ref[rs, :], wr_ref[...], preferred_element_type=F32)
               + jnp.dot(mh_ref[rs, :], wh_ref[...], preferred_element_type=F32)
               + jnp.dot(mm_ref[rs, :], wm_ref[...], preferred_element_type=F32))
        x1 = x_ref[rs, :] + g1_ref[...] * (_rms(mix, D_MODEL) * gpost_ref[...])
        x1_ref[rs, :] = x1
        h2 = _rms(x1, D_MODEL) * gpre_ref[...] * (1.0 + sc2_ref[...]) + sh2_ref[...]
        h2_ref[rs, :] = h2.astype(h2_ref.dtype)


def _out_proj(mr, mh, mm, wr, wh, wm, x, mod, gpost, gpre, l, row_of_tile, tm):
    n = x.shape[0]
    tile = lambda w: pl.BlockSpec((tm, w), lambda i: (i, 0))
    layer = lambda r, c: pl.BlockSpec((None, r, c), lambda i: (l, 0, 0))
    return pl.pallas_call(
        _out_body,
        grid=(n // tm,),
        in_specs=[tile(HWP), tile(HY_W), tile(HWP),
                  layer(HWP, D_MODEL), layer(HY_W, D_MODEL), layer(HWP, D_MODEL),
                  tile(D_MODEL),
                  _mod_spec(l, 2, row_of_tile), _mod_spec(l, 4, row_of_tile), _mod_spec(l, 3, row_of_tile),
                  layer(1, D_MODEL), layer(1, D_MODEL)],
        out_specs=[tile(D_MODEL), tile(D_MODEL)],
        out_shape=[jax.ShapeDtypeStruct((n, D_MODEL), F32), jax.ShapeDtypeStruct((n, D_MODEL), MXU_DT)],
        compiler_params=_params(),
        name="out_proj",
    )(mr, mh, mm, wr, wh, wm, x, mod, mod, mod, gpost, gpre)


def _ffn_body(h_ref, wa_ref, wb_ref, cw_ref, cb_ref, wd_ref, x1_ref, g2_ref, gpost_ref, o_ref, acc,
              *, seq_len, tm):
    j = pl.program_id(1)
    @pl.when(j == 0)
    def _():
        acc[...] = jnp.zeros_like(acc)

    h = h_ref[...]
    part = None
    for s in range(wa_ref.shape[1] // FFN_SUB):
        cs = slice(s * FFN_SUB, (s + 1) * FFN_SUB)
        a = jnp.dot(h, wa_ref[:, cs].astype(MXU_DT), preferred_element_type=F32)
        b = jnp.dot(h, wb_ref[:, cs].astype(MXU_DT), preferred_element_type=F32)
        prev, nxt = _shift_rows(a, seq_len, tm)
        a = prev * cw_ref[0:1, cs] + a * cw_ref[1:2, cs] + nxt * cw_ref[2:3, cs] + cb_ref[:, cs]
        gelu = 0.5 * a * (1.0 + jnp.tanh(math.sqrt(2.0 / math.pi) * (a + 0.044715 * (a * a * a))))
        p = jnp.dot((gelu * b).astype(MXU_DT), wd_ref[cs, :].astype(MXU_DT), preferred_element_type=F32)
        part = p if part is None else part + p
    acc[...] += part

    @pl.when(j == pl.num_programs(1) - 1)
    def _():
        o_ref[...] = x1_ref[...] + g2_ref[...] * (_rms(acc[...], D_MODEL) * gpost_ref[...])


def _conv_ffn(h2, w_up, cw, cb, w_down, x1, mod, gpost, l, row_of_tile, seq_len, tm, tf):
    n = h2.shape[0]
    nf = D_FF // tf
    return pl.pallas_call(
        functools.partial(_ffn_body, seq_len=seq_len, tm=tm),
        grid=(n // tm, nf),
        in_specs=[pl.BlockSpec((tm, D_MODEL), lambda i, j: (i, 0)),
                  pl.BlockSpec((None, D_MODEL, tf), lambda i, j: (l, 0, j)),
                  pl.BlockSpec((None, D_MODEL, tf), lambda i, j: (l, 0, j + nf)),
                  pl.BlockSpec((None, 8, tf), lambda i, j: (l, 0, j)),
                  pl.BlockSpec((None, 1, tf), lambda i, j: (l, 0, j)),
                  pl.BlockSpec((None, tf, D_MODEL), lambda i, j: (l, j, 0)),
                  pl.BlockSpec((tm, D_MODEL), lambda i, j: (i, 0)),
                  pl.BlockSpec((None, None, None, 1, D_MODEL), lambda i, j: (l, 5, row_of_tile(i), 0, 0)),
                  pl.BlockSpec((None, 1, D_MODEL), lambda i, j: (l, 0, 0))],
        out_specs=pl.BlockSpec((tm, D_MODEL), lambda i, j: (i, 0)),
        out_shape=jax.ShapeDtypeStruct((n, D_MODEL), F32),
        scratch_shapes=[pltpu.VMEM((tm, D_MODEL), F32)],
        compiler_params=_params(),
        name="conv_ffn",
    )(h2, w_up, w_up, cw, cb, w_down, x1, mod, gpost)


def _pad_rows8(w):
    return jnp.concatenate([w, jnp.zeros((w.shape[0], 8 - w.shape[1], w.shape[2]), w.dtype)], axis=1)


def kernel(x_prompt, x_sample, c, state_ret, state_mlstm_c, state_mlstm_n, state_mlstm_m, c_ctx,
           norm_mix_pre, norm_mix_post, norm_ffn_pre, norm_ffn_post, w_mod, b_mod, w_in, w_out,
           ret_decay_logit, ret_norm_g, hy_conv_w, hy_conv_b, hy_f_w1, hy_f_b1, hy_f_w2, hy_f_b2,
           hy_f_w3, hy_f_b3, hy_sin_freq, hy_bias, ml_gate_bias, ml_norm_g,
           w_up, ffn_conv_w, ffn_conv_b, w_down):
    BP, LP, _ = x_prompt.shape
    BS, LS, _ = x_sample.shape

    w_in_p = _prep_w_in(jnp.swapaxes(w_in, 1, 2))
    w_out_r = _pad_heads(lax.slice_in_dim(w_out, 0, HW, axis=1), axis=1).astype(MXU_DT)
    w_out_h = lax.slice_in_dim(w_out, HW, HW + HY_W, axis=1).astype(MXU_DT)
    w_out_m = _pad_heads(lax.slice_in_dim(w_out, HW + HY_W, 2 * HW + HY_W, axis=1), axis=1).astype(MXU_DT)
    ret_gain = _pad_heads(ret_norm_g).reshape(DEPTH, 1, HWP)
    ml_gain = _pad_heads(ml_norm_g).reshape(DEPTH, 1, HWP)
    dl = jnp.broadcast_to(ret_decay_logit.reshape(DEPTH, 8, 1), (DEPTH, 8, 128))
    gate_bias = jnp.concatenate([ml_gate_bias.reshape(DEPTH, 1, 16), jnp.zeros((DEPTH, 1, 112), F32)], axis=2)
    hy_cw = _pad_rows8(hy_conv_w)
    hy_cb = hy_conv_b.reshape(DEPTH, 1, 3 * HY_W)
    ffn_cw = _pad_rows8(ffn_conv_w)
    ffn_cb = ffn_conv_b.reshape(DEPTH, 1, D_FF)
    g_mix_pre = norm_mix_pre.reshape(DEPTH, 1, D_MODEL)
    g_mix_post = norm_mix_post.reshape(DEPTH, 1, D_MODEL)
    g_ffn_pre = norm_ffn_pre.reshape(DEPTH, 1, D_MODEL)
    g_ffn_post = norm_ffn_post.reshape(DEPTH, 1, D_MODEL)
    pad2 = lambda a, r, cdim: jnp.pad(a, ((0, 0), (0, r - a.shape[1]), (0, cdim - a.shape[2])))
    fw1 = pad2(hy_f_w1, 128, 128)
    fb1 = pad2(hy_f_b1.reshape(DEPTH, 1, -1), 1, 128)
    fw2 = pad2(hy_f_w2, 128, 128)
    fb2 = pad2(hy_f_b2.reshape(DEPTH, 1, -1), 1, 128)
    fw3 = pad2(hy_f_w3, 128, 4 * HY_W)
    fb3 = hy_f_b3.reshape(DEPTH, 1, 4 * HY_W)
    ffr = pad2(hy_sin_freq.reshape(DEPTH, 1, -1), 1, 128)

    m0 = jnp.broadcast_to(state_mlstm_m.reshape(BS, DEPTH, 8, 1), (BS, DEPTH, 8, 128))
    new_ret = jnp.zeros((BP, DEPTH, 2, HEADS, DH, DH), F32)
    new_c = jnp.zeros((BP, DEPTH, 2, HEADS, DH, DH), F32)
    new_n = jnp.zeros((BP, DEPTH, 2, HEADS, DH), F32)

    cvec8 = jnp.concatenate([c_ctx.reshape(1, D_MODEL), c, jnp.zeros((8 - 1 - BS, D_MODEL), F32)], axis=0)
    mod = _mod_all(cvec8, w_mod, b_mod)
    mod = mod.reshape(DEPTH, 8, 6, 1, D_MODEL).transpose(0, 2, 1, 3, 4)
    spectra_p = _filter_spectra(LP, fw1, fb1, fw2, fb2, fw3, fb3, ffr, hy_bias)
    spectra_s = _filter_spectra(LS, fw1, fb1, fw2, fb2, fw3, fb3, ffr, hy_bias)

    xp = x_prompt.reshape(BP * LP, D_MODEL)
    xs = x_sample.reshape(BS * LS, D_MODEL)
    tm = 512
    ffn_tm = 1024
    row_ctx = lambda i: 0
    row_lat_tm = lambda i: 1 + i // (LS // tm)
    row_lat_ffn = lambda i: 1 + i // (LS // ffn_tm)
    new_m = []

    def mix_and_ffn(x, pr_ph, mr, mm, l, B, L, row_tm, row_ffn, spectra):
        mh = _hyena(pr_ph, hy_cw, hy_cb, spectra, l, B, L)
        x1, h2 = _out_proj(mr, mh, mm, w_out_r, w_out_h, w_out_m, x, mod, g_mix_post, g_ffn_pre, l, row_tm, tm)
        return _conv_ffn(h2, w_up, ffn_cw, ffn_cb, w_down, x1, mod, g_ffn_post, l, row_ffn, L, ffn_tm, 512)

    for l in range(DEPTH):
        pr, ph, pm, pg = _in_proj(xp, mod, g_mix_pre, w_in_p, l, row_ctx, tm)
        mr, new_ret = _retention(pr, dl, ret_gain, l, BP, LP, False, new_ret)
        mm, new_c, new_n, m_fin = _mlstm(pm, pg, gate_bias, ml_gain, l, BP, LP, False, new_c, new_n)
        new_m.append(m_fin)
        xp = mix_and_ffn(xp, ph, mr, mm, l, BP, LP, row_ctx, row_ctx, spectra_p)
        pr, ph, pm, pg = _in_proj(xs, mod, g_mix_pre, w_in_p, l, row_lat_tm, tm)
        (mr,) = _retention(pr, dl, ret_gain, l, BS, LS, True, state_ret)
        (mm,) = _mlstm(pm, pg, gate_bias, ml_gain, l, BS, LS, True, state_mlstm_c, state_mlstm_n, m0)
        xs = mix_and_ffn(xs, ph, mr, mm, l, BS, LS, row_lat_tm, row_lat_ffn, spectra_s)

    out_m = jnp.stack(new_m, axis=1)[..., 0].reshape(BP, DEPTH, 2, HEADS)
    return (xp.reshape(BP, LP, D_MODEL), xs.reshape(BS, LS, D_MODEL), new_ret, new_c, new_n, out_m)
```

```python
import functools
import math

import numpy as np
import jax
import jax.numpy as jnp
from jax import lax
from jax.experimental import pallas as pl
from jax.experimental.pallas import tpu as pltpu

D_MODEL = 1024
DEPTH = 4
GRID_W = 64
HEADS = 4
DH = 96
DHP = 128
HW = HEADS * DH
HWP = HEADS * DHP
HY_W = 256
D_FF = 4 * D_MODEL
CHUNK = 128
N_BANDS = 16
FEAT_W = 1 + 2 * N_BANDS
HY_SHIFT = 0.05
HY_TARGET = 1e-2
HY_SHORT_DECAY_PCT = 0.3
HY_LONG_DECAY_PCT = 1.5
ROPE_BASE = 10000.0
EPS = 1e-6

MXU_DT = jnp.bfloat16
F32 = jnp.float32
VMEM_LIMIT = 56 * 1024 * 1024


def _dot(a, b):
    return jnp.dot(a.astype(MXU_DT), b.astype(MXU_DT), preferred_element_type=F32)


def _dot_nt(a, b):
    return lax.dot_general(a.astype(MXU_DT), b.astype(MXU_DT), (((1,), (1,)), ((), ())),
                           preferred_element_type=F32)


def _dot_tn(a, b):
    return lax.dot_general(a.astype(MXU_DT), b.astype(MXU_DT), (((0,), (0,)), ((), ())),
                           preferred_element_type=F32)


def _split2(x):
    hi = x.astype(MXU_DT)
    lo = (x - hi.astype(F32)).astype(MXU_DT)
    return hi, lo


def _dot_split(a_hi, a_lo, b):
    b_hi, b_lo = _split2(b)
    return (jnp.dot(a_hi, b_hi, preferred_element_type=F32)
            + jnp.dot(a_hi, b_lo, preferred_element_type=F32)
            + jnp.dot(a_lo, b_hi, preferred_element_type=F32))


def _dot3(a, b):
    a_hi, a_lo = _split2(a)
    return _dot_split(a_hi, a_lo, b)


def _dot_exact_lhs(t, x):
    x1 = x.astype(MXU_DT)
    r1 = x - x1.astype(F32)
    x2 = r1.astype(MXU_DT)
    x3 = (r1 - x2.astype(F32)).astype(MXU_DT)
    return (jnp.dot(t, x1, preferred_element_type=F32) + jnp.dot(t, x2, preferred_element_type=F32)
            + jnp.dot(t, x3, preferred_element_type=F32))


def _rms(x, n):
    return x * lax.rsqrt(jnp.sum(x * x, axis=-1, keepdims=True) * (1.0 / n) + EPS)


def _chunk_rows(c, size):
    if isinstance(c, int):
        return slice(c * size, (c + 1) * size)
    return pl.ds(pl.multiple_of(c * size, size), size)


def _log_sigmoid(x):
    return jnp.minimum(x, 0.0) - jnp.log1p(jnp.exp(-jnp.abs(x)))


def _resident(shape, index_map):
    return pl.BlockSpec(shape, index_map, pipeline_mode=pl.Buffered(1))


def _params(**kw):
    return pltpu.CompilerParams(vmem_limit_bytes=VMEM_LIMIT, **kw)


@functools.lru_cache(maxsize=None)
def _dft_tables(L):
    f = np.arange(L, dtype=np.int64)[:, None]
    s = np.arange(L, dtype=np.int64)[None, :]
    ang = np.pi * ((f * s) % (2 * L)).astype(np.float64) / L
    fr = np.cos(ang)
    fi = -np.sin(ang)
    fi[0, :] = np.where(np.arange(L) % 2 == 0, 1.0, -1.0)
    fwd = np.concatenate([fr, fi], axis=0)
    gr = np.cos(ang.T) * (2.0 / (2 * L))
    gr[:, 0] = 1.0 / (2 * L)
    gi = -np.sin(ang.T) * (2.0 / (2 * L))
    gi[:, 0] = np.where(np.arange(L) % 2 == 0, 1.0, -1.0) / (2 * L)
    inv = np.concatenate([gr, gi], axis=1)
    return fwd.astype(np.float32), inv.astype(np.float32)


@functools.lru_cache(maxsize=None)
def _filter_tables(L):
    tn = np.arange(L, dtype=np.float64) / L
    bands = np.linspace(1e-4, N_BANDS - 1, N_BANDS)
    ang = 2.0 * math.pi * tn[:, None] * bands[None, :]
    feat = np.zeros((L, 128), np.float32)
    feat[:, 0] = tn
    feat[:, 1:1 + N_BANDS] = np.cos(ang)
    feat[:, 1 + N_BANDS:FEAT_W] = np.sin(ang)
    deltas = np.abs(np.linspace(math.log(HY_TARGET) / HY_LONG_DECAY_PCT,
                                math.log(HY_TARGET) / HY_SHORT_DECAY_PCT, HY_W))
    window = (np.exp(-tn[:, None] * deltas[None, :]) + HY_SHIFT).astype(np.float32)
    return feat, window


@functools.lru_cache(maxsize=None)
def _rope_tables(L):
    rows = L // GRID_W
    row = np.repeat(np.arange(rows, dtype=np.float64), GRID_W)
    col = np.tile(np.arange(GRID_W, dtype=np.float64), rows)
    half = DH // 2
    n_freq = half // 2
    freqs = ROPE_BASE ** (-np.arange(n_freq, dtype=np.float64) / n_freq)
    ang = np.concatenate([row[:, None] * freqs, col[:, None] * freqs], axis=-1)
    cos = np.zeros((L, DHP), np.float32)
    sin = np.zeros((L, DHP), np.float32)
    cos[:, :half] = np.cos(ang)
    cos[:, 64:64 + half] = np.cos(ang)
    sin[:, :half] = -np.sin(ang)
    sin[:, 64:64 + half] = np.sin(ang)
    return cos, sin


@functools.lru_cache(maxsize=None)
def _tri_tables():
    i = np.arange(CHUNK)
    lower = (i[:, None] >= i[None, :]).astype(np.float32)
    upper = (i[:, None] <= i[None, :]).astype(np.float32)
    return lower, upper


def _pad_heads(w, axis=-1, rope=False):
    axis = axis % w.ndim
    parts = []

    def zeros(n):
        shp = list(w.shape)
        shp[axis] = n
        return jnp.zeros(shp, w.dtype)

    for h in range(HEADS):
        blk = lax.slice_in_dim(w, h * DH, (h + 1) * DH, axis=axis)
        if rope:
            half = DH // 2
            parts += [lax.slice_in_dim(blk, 0, half, axis=axis), zeros(64 - half),
                      lax.slice_in_dim(blk, half, DH, axis=axis), zeros(64 - half)]
        else:
            parts += [blk, zeros(DHP - DH)]
    return jnp.concatenate(parts, axis=axis)


def _unpad_heads_axis(x, axis, rope=False):
    if rope:
        half = DH // 2
        return jnp.concatenate([lax.slice_in_dim(x, 0, half, axis=axis),
                                lax.slice_in_dim(x, 64, 64 + half, axis=axis)], axis=axis)
    return lax.slice_in_dim(x, 0, DH, axis=axis)


def _mod_body(c_ref, w_ref, b_ref, o_ref):
    c = c_ref[...]
    s = c * (1.0 / (1.0 + jnp.exp(-c)))
    o_ref[...] = _dot(s, w_ref[...]) + b_ref[...]


def _mod_all(cvec8, w_mod, b_mod):
    tn = 1536
    nj = 6 * D_MODEL // tn
    return pl.pallas_call(
        _mod_body,
        grid=(DEPTH, nj),
        in_specs=[pl.BlockSpec((8, D_MODEL), lambda l, j: (0, 0)),
                  pl.BlockSpec((None, D_MODEL, tn), lambda l, j: (l, 0, j)),
                  pl.BlockSpec((None, 1, tn), lambda l, j: (l, 0, j))],
        out_specs=pl.BlockSpec((None, 8, tn), lambda l, j: (l, 0, j)),
        out_shape=jax.ShapeDtypeStruct((DEPTH, 8, 6 * D_MODEL), F32),
        compiler_params=_params(),
        name="mod_all",
    )(cvec8, w_mod, b_mod.reshape(DEPTH, 1, 6 * D_MODEL))


def _filt_body(feat_ref, win_ref, f_ref, w1_ref, b1_ref, w2_ref, b2_ref, w3_ref, b3_ref,
               fr_ref, hb_ref, p_ref, q_ref, p2_ref, *, L):
    fr = fr_ref[...]
    h = jnp.sin(fr * (_dot3(feat_ref[...], w1_ref[...]) + b1_ref[...]))
    h = jnp.sin(fr * (_dot3(h, w2_ref[...]) + b2_ref[...]))
    filt = _dot3(h, w3_ref[...]) + b3_ref[...]
    win = win_ref[...]
    row = lax.broadcasted_iota(jnp.int32, (L, 1), 0)
    taps = filt * jnp.concatenate([win, win, win, win], axis=1)
    lane = lax.broadcasted_iota(jnp.int32, (L, 4 * HY_W), 1)
    taps = jnp.where((row == 0) & (lane >= 2 * HY_W), 0.0, taps)
    spec = _dot(f_ref[...], taps)
    for o in range(2):
        a = spec[:, o * HY_W:(o + 1) * HY_W]
        b = spec[:, (2 + o) * HY_W:(3 + o) * HY_W]
        kr = a[:L] + b[:L]
        ki = a[L:] - b[L:]
        nyq = a[L:L + 1] + b[L:L + 1]
        bias = hb_ref[o:o + 1, :]
        p_ref[o] = kr + bias
        q_ref[o] = jnp.where(row == 0, 0.0, ki)
        p2_ref[o] = jnp.where(row == 0, nyq, kr) + bias


def _filter_spectra(L, fw1, fb1, fw2, fb2, fw3, fb3, ffr, hy_bias):
    feat, window = _filter_tables(L)
    fwd_table, _ = _dft_tables(L)
    const = lambda shape: _resident(shape, lambda l: (0,) * len(shape))
    per_layer = lambda shape: pl.BlockSpec((None,) + shape, lambda l: (l,) + (0,) * len(shape))
    out_spec = per_layer((2, L, HY_W))
    out_shape = jax.ShapeDtypeStruct((DEPTH, 2, L, HY_W), F32)
    return pl.pallas_call(
        functools.partial(_filt_body, L=L),
        grid=(DEPTH,),
        in_specs=[const((L, 128)), const((L, HY_W)), const((2 * L, L)),
                  per_layer((128, 128)), per_layer((1, 128)), per_layer((128, 128)), per_layer((1, 128)),
                  per_layer((128, 4 * HY_W)), per_layer((1, 4 * HY_W)), per_layer((1, 128)),
                  per_layer((2, HY_W))],
        out_specs=[out_spec, out_spec, out_spec],
        out_shape=[out_shape, out_shape, out_shape],
        compiler_params=_params(),
        name="hyena_filter_%d" % L,
    )(feat, window, fwd_table, fw1, fb1, fw2, fb2, fw3, fb3, ffr, hy_bias)


W_R0, W_H0, W_M0, W_G0, W_END = 0, 4 * HWP, 4 * HWP + 3 * HY_W, 8 * HWP + 3 * HY_W, 8 * HWP + 3 * HY_W + 128
IN_W = 8 * HW + 3 * HY_W + 16
FFN_SUB = 256
ROW_SUB = 256


def _prep_in_body(wt_ref, o_ref):
    kb = wt_ref.shape[1]
    half = DH // 2

    def zeros(n):
        return jnp.zeros((n, kb), F32)

    def head_tile(src, rope):
        if rope:
            return jnp.concatenate([wt_ref[src:src + half, :], zeros(64 - half),
                                    wt_ref[src + half:src + DH, :], zeros(64 - half)], axis=0)
        return jnp.concatenate([wt_ref[src:src + DH, :], zeros(DHP - DH)], axis=0)

    tiles = []
    for sec in range(4):
        tiles += [head_tile(sec * HW + h * DH, sec < 2) for h in range(HEADS)]
    tiles += [wt_ref[4 * HW + i * 128:4 * HW + (i + 1) * 128, :] for i in range(3 * HY_W // 128)]
    for sec in range(4):
        tiles += [head_tile(4 * HW + 3 * HY_W + sec * HW + h * DH, False) for h in range(HEADS)]
    g0 = 8 * HW + 3 * HY_W
    tiles.append(jnp.concatenate([wt_ref[g0:g0 + 16, :], zeros(112)], axis=0))
    for t, tile in enumerate(tiles):
        o_ref[:, t * 128:(t + 1) * 128] = tile.T.astype(o_ref.dtype)


def _prep_w_in(w_in_t):
    kb = 256
    return pl.pallas_call(
        _prep_in_body,
        grid=(DEPTH, D_MODEL // kb),
        in_specs=[pl.BlockSpec((None, IN_W, kb), lambda l, i: (l, 0, i))],
        out_specs=pl.BlockSpec((None, kb, W_END), lambda l, i: (l, i, 0)),
        out_shape=jax.ShapeDtypeStruct((DEPTH, D_MODEL, W_END), MXU_DT),
        compiler_params=_params(),
        name="prep_w_in",
    )(w_in_t)


def _in_body(x_ref, sc_ref, sh_ref, g_ref, w_ref, pr_ref, ph_ref, pm_ref, pg_ref):
    h = _rms(x_ref[...], D_MODEL) * g_ref[...]
    h = (h * (1.0 + sc_ref[...]) + sh_ref[...]).astype(MXU_DT)
    pr_ref[...] = jnp.dot(h, w_ref[:, W_R0:W_H0], preferred_element_type=F32)
    ph_ref[...] = jnp.dot(h, w_ref[:, W_H0:W_M0], preferred_element_type=F32)
    pm_ref[...] = jnp.dot(h, w_ref[:, W_M0:W_G0], preferred_element_type=F32)
    pg_ref[...] = jnp.dot(h, w_ref[:, W_G0:W_END], preferred_element_type=F32)


def _mod_spec(l, which, row_of_tile):
    return pl.BlockSpec((None, None, None, 1, D_MODEL), lambda i: (l, which, row_of_tile(i), 0, 0))


def _in_proj(x, mod, gain, w_in, l, row_of_tile, tm):
    n = x.shape[0]
    shapes = [4 * HWP, 3 * HY_W, 4 * HWP, 128]
    return pl.pallas_call(
        _in_body,
        grid=(n // tm,),
        in_specs=[pl.BlockSpec((tm, D_MODEL), lambda i: (i, 0)),
                  _mod_spec(l, 1, row_of_tile), _mod_spec(l, 0, row_of_tile),
                  pl.BlockSpec((None, 1, D_MODEL), lambda i: (l, 0, 0)),
                  _resident((None, D_MODEL, W_END), lambda i: (l, 0, 0))],
        out_specs=[pl.BlockSpec((tm, w), lambda i: (i, 0)) for w in shapes],
        out_shape=[jax.ShapeDtypeStruct((n, w), F32) for w in shapes],
        compiler_params=_params(),
        name="in_proj",
    )(x, mod, mod, gain, w_in)


def _ret_body(*refs, L, latent):
    if latent:
        (q_ref, k_ref, v_ref, gate_ref, dl_ref, gain_ref, s0_ref, cos_ref, sin_ref,
         o_ref, qs, ks, acc, sst) = refs
    else:
        (q_ref, k_ref, v_ref, gate_ref, dl_ref, gain_ref, _, o_ref, sfin_ref, qs, ks, acc, sst) = refs
    half = DH // 2
    C = CHUNK
    nchunk = L // C
    lg = _log_sigmoid(dl_ref[...])
    rel = (lax.broadcasted_iota(jnp.int32, (C, C), 0) - lax.broadcasted_iota(jnp.int32, (C, C), 1)).astype(F32)
    ri = lax.broadcasted_iota(jnp.int32, (C, 1), 0).astype(F32)

    for h in range(HEADS):
        cols = slice(h * DHP, (h + 1) * DHP)
        q = q_ref[:, cols]
        k = k_ref[:, cols]
        if latent:
            cos = cos_ref[...]
            sin = sin_ref[...]
            q = q * cos + pltpu.roll(q, 64, 1) * sin
            k = k * cos + pltpu.roll(k, 64, 1) * sin
        qs[:, cols] = q
        ks[:, cols] = k * (DH ** -0.5)

    acc[...] = jnp.zeros_like(acc)
    consts = []
    for h in range(HEADS):
        lgf = lg[h:h + 1, :]
        lgb = lg[HEADS + h:HEADS + h + 1, :]
        lgf1 = lgf[:, 0:1]
        lgb1 = lgb[:, 0:1]
        mask = (jnp.where(rel >= 0, jnp.exp(rel * lgf), 0.0)
                + jnp.where(rel <= 0, jnp.exp(-rel * lgb), 0.0))
        consts.append(dict(
            mask=mask,
            qdec_f=jnp.exp((ri + 1.0) * lgf1), qdec_b=jnp.exp((C - ri) * lgb1),
            kdec_f=jnp.exp((C - 1.0 - ri) * lgf1), kdec_b=jnp.exp(ri * lgb1),
            cdec_f=jnp.exp(C * lgf1), cdec_b=jnp.exp(C * lgb1)))
        for d in range(2):
            sst[d, h] = jnp.zeros((DHP, DHP), F32)
            if latent:
                sst[d, h, 0:half, 0:DH] = s0_ref[d, h, 0:half, :]
                sst[d, h, 64:64 + half, 0:DH] = s0_ref[d, h, half:DH, :]

    def step(j):
        rows_f = _chunk_rows(j, C)
        rows_b = _chunk_rows(nchunk - 1 - j, C)
        for h in range(HEADS):
            cols = slice(h * DHP, (h + 1) * DHP)
            cn = consts[h]
            q = qs[rows_f, cols]
            k = ks[rows_f, cols]
            v = v_ref[rows_f, cols]
            s_f = sst[0, h]
            sc = _dot_nt(q, k) * cn["mask"]
            acc[rows_f, cols] += _dot(sc, v) + _dot(q, s_f) * cn["qdec_f"]
            sst[0, h] = s_f * cn["cdec_f"] + _dot_tn(k * cn["kdec_f"], v)
            q = qs[rows_b, cols]
            k = ks[rows_b, cols]
            v = v_ref[rows_b, cols]
            s_b = sst[1, h]
            acc[rows_b, cols] += _dot(q, s_b) * cn["qdec_b"]
            sst[1, h] = s_b * cn["cdec_b"] + _dot_tn(k * cn["kdec_b"], v)

    if nchunk <= 2:
        for j in range(nchunk):
            step(j)
    else:
        pl.loop(0, nchunk)(step)

    if not latent:
        for d in range(2):
            for h in range(HEADS):
                sfin_ref[d, h, 0:half, :] = sst[d, h, 0:half, 0:DH]
                sfin_ref[d, h, half:DH, :] = sst[d, h, 64:64 + half, 0:DH]

    for h in range(HEADS):
        cols = slice(h * DHP, (h + 1) * DHP)
        g = gate_ref[:, cols]
        y = _rms(acc[:, cols], DH) * gain_ref[:, cols] * (g * (1.0 / (1.0 + jnp.exp(-g))))
        o_ref[:, cols] = y.astype(o_ref.dtype)


def _state_spec(l, *tail):
    shape = (None, None, 2, HEADS) + tail
    return pl.BlockSpec(shape, lambda b: (b, l) + (0,) * (len(shape) - 2))


def _retention(pr, dl, gain, l, B, L, latent, state):
    blk = lambda j: pl.BlockSpec((L, HWP), lambda b, j=j: (b, j))
    in_specs = [blk(0), blk(1), blk(2), blk(3),
                pl.BlockSpec((None, 8, 128), lambda b: (l, 0, 0)),
                pl.BlockSpec((None, 1, HWP), lambda b: (l, 0, 0))]
    args = [pr, pr, pr, pr, dl, gain]
    out_specs = [pl.BlockSpec((L, HWP), lambda b: (b, 0))]
    out_shape = [jax.ShapeDtypeStruct((B * L, HWP), MXU_DT)]
    aliases = {}
    if latent:
        cos, sin = _rope_tables(L)
        in_specs += [_state_spec(l, DH, DH),
                     pl.BlockSpec((L, DHP), lambda b: (0, 0)), pl.BlockSpec((L, DHP), lambda b: (0, 0))]
        args += [state, cos, sin]
    else:
        in_specs.append(pl.BlockSpec(memory_space=pl.ANY))
        args.append(state)
        aliases = {len(args) - 1: 1}
        out_specs.append(_state_spec(l, DH, DH))
        out_shape.append(jax.ShapeDtypeStruct(state.shape, F32))
    return pl.pallas_call(
        functools.partial(_ret_body, L=L, latent=latent),
        grid=(B,),
        in_specs=in_specs, out_specs=out_specs, out_shape=out_shape,
        input_output_aliases=aliases,
        scratch_shapes=[pltpu.VMEM((L, HWP), F32), pltpu.VMEM((L, HWP), F32), pltpu.VMEM((L, HWP), F32),
                        pltpu.VMEM((2, HEADS, DHP, DHP), F32)],
        compiler_params=_params(),
        name="retention_%d" % L,
    )(*args)


def _ml_body(*refs, L, latent):
    if latent:
        (q_ref, k_ref, v_ref, og_ref, gates_ref, gb_ref, gain_ref, tl_ref, tu_ref, c0_ref, n0_ref, m0_ref,
         o_ref, acc, bb_s, ub_s, pm_s, brow_s, ibrow_s, mc_s, mx_s, cn_s) = refs
    else:
        (q_ref, k_ref, v_ref, og_ref, gates_ref, gb_ref, gain_ref, tl_ref, tu_ref, _, _,
         o_ref, cfin_ref, nfin_ref, mfin_ref, acc, bb_s, ub_s, pm_s, brow_s, ibrow_s, mc_s, mx_s, cn_s) = refs
    C = CHUNK
    nchunk = L // C
    lane = lax.broadcasted_iota(jnp.int32, (C, DHP), 1)
    tri_r = lax.broadcasted_iota(jnp.int32, (C, C), 0)
    tri_c = lax.broadcasted_iota(jnp.int32, (C, C), 1)
    tl = tl_ref[...]
    tu = tu_ref[...]

    row_i = lax.broadcasted_iota(jnp.int32, (C, 128), 0)

    def cummax_rows(x, suffix):
        s = 1
        while s < C:
            if suffix:
                x = jnp.where(row_i < C - s, jnp.maximum(x, pltpu.roll(x, C - s, 0)), x)
            else:
                x = jnp.where(row_i >= s, jnp.maximum(x, pltpu.roll(x, s, 0)), x)
            s *= 2
        return x

    for c in range(nchunk):
        rows = slice(c * C, (c + 1) * C)
        g = gates_ref[rows, :] + gb_ref[...]
        lf = _log_sigmoid(g)
        pre = _dot_exact_lhs(tl, lf)
        suf = _dot_exact_lhs(tu, lf)
        bc = jnp.where(lane < 8, pre, suf)
        brow_s[c * 16:(c + 1) * 16, :] = bc.T[0:16, :]
        ibrow_s[c * 16:(c + 1) * 16, :] = g.T[0:16, :]
        for d in range(2):
            for h in range(HEADS):
                sr = d * HEADS + h
                b_b = jnp.broadcast_to(bc[:, d * 8 + 4 + h:d * 8 + 5 + h], (C, 128))
                u_b = jnp.broadcast_to(g[:, d * 8 + h:d * 8 + h + 1], (C, 128)) - b_b
                bb_s[sr, rows, :] = b_b
                ub_s[sr, rows, :] = u_b
                pm_s[sr, rows, :] = cummax_rows(u_b, suffix=(d == 1))

    for d in range(2):
        for h in range(HEADS):
            sr = d * HEADS + h
            m = m0_ref[sr:sr + 1, :] if latent else jnp.zeros((1, 128), F32)
            for j in range(nchunk):
                c = j if d == 0 else nchunk - 1 - j
                last = c * C + (C - 1 if d == 0 else 0)
                mx = jnp.maximum(m, pm_s[sr, last:last + 1, :])
                mc_s[c * 8 + sr:c * 8 + sr + 1, :] = m
                mx_s[c * 8 + sr:c * 8 + sr + 1, :] = mx
                m = bb_s[sr, last:last + 1, :] + mx
            if not latent:
                mfin_ref[sr:sr + 1, :] = m

    acc[...] = jnp.zeros_like(acc)
    cn_s[...] = jnp.zeros_like(cn_s)
    if latent:
        for d in range(2):
            for h in range(HEADS):
                cn_s[d, h, 0:DH, 0:DH] = c0_ref[d, h]
                n_row = jnp.concatenate([n0_ref[d, h:h + 1, :], jnp.zeros((1, DHP - DH), F32)], axis=1)
                cn_s[d, h, :, DHP:] = jnp.broadcast_to(n_row, (DHP, DHP)).T
    ones_blk = jnp.ones((C, DHP), MXU_DT)

    def step(j):
        for d in range(2):
            c = j if d == 0 else nchunk - 1 - j
            rows = _chunk_rows(c, C)
            brs = brow_s[_chunk_rows(c, 16), :]
            ibrs = ibrow_s[_chunk_rows(c, 16), :]
            mcs = mc_s[_chunk_rows(c, 8), :]
            mxs = mx_s[_chunk_rows(c, 8), :]
            causal = (tri_r >= tri_c) if d == 0 else (tri_r <= tri_c)
            for h in range(HEADS):
                cols = slice(h * DHP, (h + 1) * DHP)
                ic = d * 8 + h
                fc = d * 8 + 4 + h
                sr = d * HEADS + h
                m_c = mcs[sr:sr + 1, :]
                mx = mxs[sr:sr + 1, :]
                q = q_ref[rows, cols].astype(MXU_DT)
                k = k_ref[rows, cols] * (DH ** -0.5)
                v1 = jnp.concatenate([v_ref[rows, cols].astype(MXU_DT), ones_blk], axis=1)
                cn = cn_s[d, h]
                urow = ibrs[ic:ic + 1, :] - brs[fc:fc + 1, :]
                mb = jnp.maximum(m_c, pm_s[sr, rows, :])
                w_intra = jnp.exp(jnp.where(causal, urow - mb, -jnp.inf))
                w_inter = jnp.exp(m_c - mb)
                s = _dot_nt(q, k) * w_intra
                tot = (jnp.dot(s.astype(MXU_DT), v1, preferred_element_type=F32)
                       + jnp.concatenate([w_inter, w_inter], axis=1) * _dot(q, cn))
                den = tot[:, DHP:]
                floor = jnp.exp(-(bb_s[sr, rows, :] + mb))
                acc[rows, cols] += tot[:, :DHP] * (1.0 / jnp.maximum(jnp.abs(den), floor))
                kw = k * jnp.exp(ub_s[sr, rows, :] - mx)
                w_prev = jnp.exp(m_c - mx)
                cn_s[d, h] = jnp.concatenate([w_prev, w_prev], axis=1) * cn + _dot_tn(kw, v1)

    if nchunk <= 2:
        for j in range(nchunk):
            step(j)
    else:
        pl.loop(0, nchunk)(step)

    if not latent:
        for d in range(2):
            for h in range(HEADS):
                cfin_ref[d, h] = cn_s[d, h, 0:DH, 0:DH]
                nfin_ref[d, h:h + 1, :] = cn_s[d, h, :, DHP:].T[0:1, 0:DH]

    for h in range(HEADS):
        cols = slice(h * DHP, (h + 1) * DHP)
        g = og_ref[:, cols]
        y = _rms(acc[:, cols], DH) * gain_ref[:, cols] * (1.0 / (1.0 + jnp.exp(-g)))
        o_ref[:, cols] = y.astype(o_ref.dtype)


def _mlstm(pm, pg, gb, gain, l, B, L, latent, c_state, n_state, m0=None):
    tl, tu = _tri_tables()
    tl = tl.astype(MXU_DT)
    tu = tu.astype(MXU_DT)
    blk = lambda j: pl.BlockSpec((L, HWP), lambda b, j=j: (b, j))
    in_specs = [blk(0), blk(1), blk(2), blk(3),
                pl.BlockSpec((L, 128), lambda b: (b, 0)),
                pl.BlockSpec((None, 1, 128), lambda b: (l, 0, 0)),
                pl.BlockSpec((None, 1, HWP), lambda b: (l, 0, 0)),
                pl.BlockSpec((CHUNK, CHUNK), lambda b: (0, 0)),
                pl.BlockSpec((CHUNK, CHUNK), lambda b: (0, 0))]
    args = [pm, pm, pm, pm, pg, gb, gain, tl, tu]
    out_specs = [pl.BlockSpec((L, HWP), lambda b: (b, 0))]
    out_shape = [jax.ShapeDtypeStruct((B * L, HWP), MXU_DT)]
    aliases = {}
    if latent:
        in_specs += [_state_spec(l, DH, DH), _state_spec(l, DH),
                     pl.BlockSpec((None, None, 8, 128), lambda b: (b, l, 0, 0))]
        args += [c_state, n_state, m0]
    else:
        in_specs += [pl.BlockSpec(memory_space=pl.ANY), pl.BlockSpec(memory_space=pl.ANY)]
        args += [c_state, n_state]
        aliases = {len(args) - 2: 1, len(args) - 1: 2}
        out_specs += [_state_spec(l, DH, DH), _state_spec(l, DH),
                      pl.BlockSpec((None, 8, 128), lambda b: (b, 0, 0))]
        out_shape += [jax.ShapeDtypeStruct(c_state.shape, F32), jax.ShapeDtypeStruct(n_state.shape, F32),
                      jax.ShapeDtypeStruct((B, 8, 128), F32)]
    nchunk = L // CHUNK
    return pl.pallas_call(
        functools.partial(_ml_body, L=L, latent=latent),
        grid=(B,),
        in_specs=in_specs, out_specs=out_specs, out_shape=out_shape,
        input_output_aliases=aliases,
        scratch_shapes=[pltpu.VMEM((L, HWP), F32),
                        pltpu.VMEM((8, L, 128), F32), pltpu.VMEM((8, L, 128), F32), pltpu.VMEM((8, L, 128), F32),
                        pltpu.VMEM((nchunk * 16, 128), F32), pltpu.VMEM((nchunk * 16, 128), F32),
                        pltpu.VMEM((nchunk * 8, 128), F32), pltpu.VMEM((nchunk * 8, 128), F32),
                        pltpu.VMEM((2, HEADS, DHP, 2 * DHP), F32)],
        compiler_params=_params(),
        name="mlstm_%d" % L,
    )(*args)


def _shift_rows(x, seq_len, n_rows):
    pos = lax.broadcasted_iota(jnp.int32, (n_rows, 1), 0) & (seq_len - 1)
    prev = jnp.where(pos == 0, 0.0, pltpu.roll(x, 1, 0))
    nxt = jnp.where(pos == seq_len - 1, 0.0, pltpu.roll(x, n_rows - 1, 0))
    return prev, nxt


def _hy_body(p_ref, cw_ref, cb_ref, f_ref, g_ref, ps_ref, qs_ref, p2s_ref, o_ref, fh_ref, gh_ref, *, L):
    @pl.when(pl.program_id(0) == 0)
    def _():
        fh_ref[...] = f_ref[...].astype(MXU_DT)
        gh_ref[...] = g_ref[...].astype(MXU_DT)

    def conv(z, o):
        spec = _dot(fh_ref[...], z)
        xr = spec[:L]
        xi = spec[L:]
        p = ps_ref[o]
        q = qs_ref[o]
        yr = xr * p - xi * q
        yi = xr * q + xi * p2s_ref[o]
        return _dot(gh_ref[...], jnp.concatenate([yr, yi], axis=0))

    for i in range(p_ref.shape[0] // L):
        rows = slice(i * L, (i + 1) * L)
        x = p_ref[rows, :]
        prev, nxt = _shift_rows(x, L, L)
        hy = prev * cw_ref[0:1, :] + x * cw_ref[1:2, :] + nxt * cw_ref[2:3, :] + cb_ref[...]
        hv = hy[:, 0:HY_W]
        hx1 = hy[:, HY_W:2 * HY_W]
        hx2 = hy[:, 2 * HY_W:3 * HY_W]
        z = hx1 * conv(hv, 0)
        o_ref[rows, :] = (hx2 * conv(z, 1)).astype(o_ref.dtype)


def _hyena(ph, cw, cb, spectra, l, B, L):
    fwd_table, inv_table = _dft_tables(L)
    ps, qs, p2s = spectra
    const = lambda shape: _resident(shape, lambda b: (0,) * len(shape))
    spec_blk = _resident((None, 2, L, HY_W), lambda b: (l, 0, 0, 0))
    nb = min(B, max(1, 2048 // L))
    return pl.pallas_call(
        functools.partial(_hy_body, L=L),
        grid=(B // nb,),
        in_specs=[pl.BlockSpec((nb * L, 3 * HY_W), lambda b: (b, 0)),
                  pl.BlockSpec((None, 8, 3 * HY_W), lambda b: (l, 0, 0)),
                  pl.BlockSpec((None, 1, 3 * HY_W), lambda b: (l, 0, 0)),
                  const((2 * L, L)), const((L, 2 * L)),
                  spec_blk, spec_blk, spec_blk],
        out_specs=pl.BlockSpec((nb * L, HY_W), lambda b: (b, 0)),
        out_shape=jax.ShapeDtypeStruct((B * L, HY_W), MXU_DT),
        scratch_shapes=[pltpu.VMEM((2 * L, L), MXU_DT), pltpu.VMEM((L, 2 * L), MXU_DT)],
        compiler_params=_params(),
        name="hyena_%d" % L,
    )(ph, cw, cb, fwd_table, inv_table, ps, qs, p2s)


def _out_body(mr_ref, mh_ref, mm_ref, wr_ref, wh_ref, wm_ref, x_ref, g1_ref, sc2_ref, sh2_ref,
              gpost_ref, gpre_ref, x1_ref, h2_ref):
    for r in range(x_ref.shape[0] // ROW_SUB):
        rs = slice(r * ROW_SUB, (r + 1) * ROW_SUB)
        mix = (jnp.dot(mr_ref[rs, :], wr_ref[...], preferred_element_type=F32)
               + jnp.dot(mh_ref[rs, :], wh_ref[...], preferred_element_type=F32)
               + jnp.dot(mm_ref[rs, :], wm_ref[...], preferred_element_type=F32))
        x1 = x_ref[rs, :] + g1_ref[...] * (_rms(mix, D_MODEL) * gpost_ref[...])
        x1_ref[rs, :] = x1
        h2 = _rms(x1, D_MODEL) * gpre_ref[...] * (1.0 + sc2_ref[...]) + sh2_ref[...]
        h2_ref[rs, :] = h2.astype(h2_ref.dtype)


def _out_proj(mr, mh, mm, wr, wh, wm, x, mod, gpost, gpre, l, row_of_tile, tm):
    n = x.shape[0]
    tile = lambda w: pl.BlockSpec((tm, w), lambda i: (i, 0))
    layer = lambda r, c: pl.BlockSpec((None, r, c), lambda i: (l, 0, 0))
    return pl.pallas_call(
        _out_body,
        grid=(n // tm,),
        in_specs=[tile(HWP), tile(HY_W), tile(HWP),
                  layer(HWP, D_MODEL), layer(HY_W, D_MODEL), layer(HWP, D_MODEL),
                  tile(D_MODEL),
                  _mod_spec(l, 2, row_of_tile), _mod_spec(l, 4, row_of_tile), _mod_spec(l, 3, row_of_tile),
                  layer(1, D_MODEL), layer(1, D_MODEL)],
        out_specs=[tile(D_MODEL), tile(D_MODEL)],
        out_shape=[jax.ShapeDtypeStruct((n, D_MODEL), F32), jax.ShapeDtypeStruct((n, D_MODEL), MXU_DT)],
        compiler_params=_params(),
        name="out_proj",
    )(mr, mh, mm, wr, wh, wm, x, mod, mod, mod, gpost, gpre)


def _ffn_body(h_ref, wa_ref, wb_ref, cw_ref, cb_ref, wd_ref, x1_ref, g2_ref, gpost_ref, o_ref,
              *, seq_len, tm):
    j = pl.program_id(1)
    @pl.when(j == 0)
    def _():
        o_ref[...] = jnp.zeros_like(o_ref)

    h = h_ref[...]
    part = None
    for s in range(wa_ref.shape[1] // FFN_SUB):
        cs = slice(s * FFN_SUB, (s + 1) * FFN_SUB)
        a = jnp.dot(h, wa_ref[:, cs].astype(MXU_DT), preferred_element_type=F32)
        b = jnp.dot(h, wb_ref[:, cs].astype(MXU_DT), preferred_element_type=F32)
        prev, nxt = _shift_rows(a, seq_len, tm)
        a = prev * cw_ref[0:1, cs] + a * cw_ref[1:2, cs] + nxt * cw_ref[2:3, cs] + cb_ref[:, cs]
        gelu = 0.5 * a * (1.0 + jnp.tanh(math.sqrt(2.0 / math.pi) * (a + 0.044715 * (a * a * a))))
        p = jnp.dot((gelu * b).astype(MXU_DT), wd_ref[cs, :].astype(MXU_DT), preferred_element_type=F32)
        part = p if part is None else part + p
    o_ref[...] += part

    @pl.when(j == pl.num_programs(1) - 1)
    def _():
        o_ref[...] = x1_ref[...] + g2_ref[...] * (_rms(o_ref[...], D_MODEL) * gpost_ref[...])


def _conv_ffn(h2, w_up, cw, cb, w_down, x1, mod, gpost, l, row_of_tile, seq_len, tm, tf):
    n = h2.shape[0]
    nf = D_FF // tf
    return pl.pallas_call(
        functools.partial(_ffn_body, seq_len=seq_len, tm=tm),
        grid=(n // tm, nf),
        in_specs=[pl.BlockSpec((tm, D_MODEL), lambda i, j: (i, 0)),
                  pl.BlockSpec((None, D_MODEL, tf), lambda i, j: (l, 0, j)),
                  pl.BlockSpec((None, D_MODEL, tf), lambda i, j: (l, 0, j + nf)),
                  pl.BlockSpec((None, 8, tf), lambda i, j: (l, 0, j)),
                  pl.BlockSpec((None, 1, tf), lambda i, j: (l, 0, j)),
                  pl.BlockSpec((None, tf, D_MODEL), lambda i, j: (l, j, 0)),
                  pl.BlockSpec((tm, D_MODEL), lambda i, j: (i, 0)),
                  pl.BlockSpec((None, None, None, 1, D_MODEL), lambda i, j: (l, 5, row_of_tile(i), 0, 0)),
                  pl.BlockSpec((None, 1, D_MODEL), lambda i, j: (l, 0, 0))],
        out_specs=pl.BlockSpec((tm, D_MODEL), lambda i, j: (i, 0)),
        out_shape=jax.ShapeDtypeStruct((n, D_MODEL), F32),
        compiler_params=_params(),
        name="conv_ffn",
    )(h2, w_up, w_up, cw, cb, w_down, x1, mod, gpost)


def _pad_rows8(w):
    return jnp.concatenate([w, jnp.zeros((w.shape[0], 8 - w.shape[1], w.shape[2]), w.dtype)], axis=1)


def kernel(x_prompt, x_sample, c, state_ret, state_mlstm_c, state_mlstm_n, state_mlstm_m, c_ctx,
           norm_mix_pre, norm_mix_post, norm_ffn_pre, norm_ffn_post, w_mod, b_mod, w_in, w_out,
           ret_decay_logit, ret_norm_g, hy_conv_w, hy_conv_b, hy_f_w1, hy_f_b1, hy_f_w2, hy_f_b2,
           hy_f_w3, hy_f_b3, hy_sin_freq, hy_bias, ml_gate_bias, ml_norm_g,
           w_up, ffn_conv_w, ffn_conv_b, w_down):
    BP, LP, _ = x_prompt.shape
    BS, LS, _ = x_sample.shape

    w_in_p = _prep_w_in(jnp.swapaxes(w_in, 1, 2))
    w_out_r = _pad_heads(lax.slice_in_dim(w_out, 0, HW, axis=1), axis=1).astype(MXU_DT)
    w_out_h = lax.slice_in_dim(w_out, HW, HW + HY_W, axis=1).astype(MXU_DT)
    w_out_m = _pad_heads(lax.slice_in_dim(w_out, HW + HY_W, 2 * HW + HY_W, axis=1), axis=1).astype(MXU_DT)
    ret_gain = _pad_heads(ret_norm_g).reshape(DEPTH, 1, HWP)
    ml_gain = _pad_heads(ml_norm_g).reshape(DEPTH, 1, HWP)
    dl = jnp.broadcast_to(ret_decay_logit.reshape(DEPTH, 8, 1), (DEPTH, 8, 128))
    gate_bias = jnp.concatenate([ml_gate_bias.reshape(DEPTH, 1, 16), jnp.zeros((DEPTH, 1, 112), F32)], axis=2)
    hy_cw = _pad_rows8(hy_conv_w)
    hy_cb = hy_conv_b.reshape(DEPTH, 1, 3 * HY_W)
    ffn_cw = _pad_rows8(ffn_conv_w)
    ffn_cb = ffn_conv_b.reshape(DEPTH, 1, D_FF)
    g_mix_pre = norm_mix_pre.reshape(DEPTH, 1, D_MODEL)
    g_mix_post = norm_mix_post.reshape(DEPTH, 1, D_MODEL)
    g_ffn_pre = norm_ffn_pre.reshape(DEPTH, 1, D_MODEL)
    g_ffn_post = norm_ffn_post.reshape(DEPTH, 1, D_MODEL)
    pad2 = lambda a, r, cdim: jnp.pad(a, ((0, 0), (0, r - a.shape[1]), (0, cdim - a.shape[2])))
    fw1 = pad2(hy_f_w1, 128, 128)
    fb1 = pad2(hy_f_b1.reshape(DEPTH, 1, -1), 1, 128)
    fw2 = pad2(hy_f_w2, 128, 128)
    fb2 = pad2(hy_f_b2.reshape(DEPTH, 1, -1), 1, 128)
    fw3 = pad2(hy_f_w3, 128, 4 * HY_W)
    fb3 = hy_f_b3.reshape(DEPTH, 1, 4 * HY_W)
    ffr = pad2(hy_sin_freq.reshape(DEPTH, 1, -1), 1, 128)

    m0 = jnp.broadcast_to(state_mlstm_m.reshape(BS, DEPTH, 8, 1), (BS, DEPTH, 8, 128))
    new_ret = jnp.zeros((BP, DEPTH, 2, HEADS, DH, DH), F32)
    new_c = jnp.zeros((BP, DEPTH, 2, HEADS, DH, DH), F32)
    new_n = jnp.zeros((BP, DEPTH, 2, HEADS, DH), F32)

    cvec8 = jnp.concatenate([c_ctx.reshape(1, D_MODEL), c, jnp.zeros((8 - 1 - BS, D_MODEL), F32)], axis=0)
    mod = _mod_all(cvec8, w_mod, b_mod)
    mod = mod.reshape(DEPTH, 8, 6, 1, D_MODEL).transpose(0, 2, 1, 3, 4)
    spectra_p = _filter_spectra(LP, fw1, fb1, fw2, fb2, fw3, fb3, ffr, hy_bias)
    spectra_s = _filter_spectra(LS, fw1, fb1, fw2, fb2, fw3, fb3, ffr, hy_bias)

    xp = x_prompt.reshape(BP * LP, D_MODEL)
    xs = x_sample.reshape(BS * LS, D_MODEL)
    tm = 512
    ffn_tm = 1024
    ffn_tf = 1024
    row_ctx = lambda i: 0
    row_lat_tm = lambda i: 1 + i // (LS // tm)
    row_lat_ffn = lambda i: 1 + i // (LS // ffn_tm)
    new_m = []

    def mix_and_ffn(x, pr_ph, mr, mm, l, B, L, row_tm, row_ffn, spectra):
        mh = _hyena(pr_ph, hy_cw, hy_cb, spectra, l, B, L)
        x1, h2 = _out_proj(mr, mh, mm, w_out_r, w_out_h, w_out_m, x, mod, g_mix_post, g_ffn_pre, l, row_tm, tm)
        return _conv_ffn(h2, w_up, ffn_cw, ffn_cb, w_down, x1, mod, g_ffn_post, l, row_ffn, L, ffn_tm, ffn_tf)

    for l in range(DEPTH):
        pr, ph, pm, pg = _in_proj(xp, mod, g_mix_pre, w_in_p, l, row_ctx, tm)
        mr, new_ret = _retention(pr, dl, ret_gain, l, BP, LP, False, new_ret)
        mm, new_c, new_n, m_fin = _mlstm(pm, pg, gate_bias, ml_gain, l, BP, LP, False, new_c, new_n)
        new_m.append(m_fin)
        xp = mix_and_ffn(xp, ph, mr, mm, l, BP, LP, row_ctx, row_ctx, spectra_p)
        pr, ph, pm, pg = _in_proj(xs, mod, g_mix_pre, w_in_p, l, row_lat_tm, tm)
        (mr,) = _retention(pr, dl, ret_gain, l, BS, LS, True, state_ret)
        (mm,) = _mlstm(pm, pg, gate_bias, ml_gain, l, BS, LS, True, state_mlstm_c, state_mlstm_n, m0)
        xs = mix_and_ffn(xs, ph, mr, mm, l, BS, LS, row_lat_tm, row_lat_ffn, spectra_s)

    out_m = jnp.stack(new_m, axis=1)[..., 0].reshape(BP, DEPTH, 2, HEADS)
    return (xp.reshape(BP, LP, D_MODEL), xs.reshape(BS, LS, D_MODEL), new_ret, new_c, new_n, out_m)
```

```python
import functools
import math

import numpy as np
import jax
import jax.numpy as jnp
from jax import lax
from jax.experimental import pallas as pl
from jax.experimental.pallas import tpu as pltpu

D_MODEL = 1024
DEPTH = 4
GRID_W = 64
HEADS = 4
DH = 96
DHP = 128
HW = HEADS * DH
HWP = HEADS * DHP
HY_W = 256
D_FF = 4 * D_MODEL
CHUNK = 128
N_BANDS = 16
FEAT_W = 1 + 2 * N_BANDS
HY_SHIFT = 0.05
HY_TARGET = 1e-2
HY_SHORT_DECAY_PCT = 0.3
HY_LONG_DECAY_PCT = 1.5
ROPE_BASE = 10000.0
EPS = 1e-6

MXU_DT = jnp.bfloat16
F32 = jnp.float32
VMEM_LIMIT = 56 * 1024 * 1024


def _dot(a, b):
    return jnp.dot(a.astype(MXU_DT), b.astype(MXU_DT), preferred_element_type=F32)


def _dot_nt(a, b):
    return lax.dot_general(a.astype(MXU_DT), b.astype(MXU_DT), (((1,), (1,)), ((), ())),
                           preferred_element_type=F32)


def _dot_tn(a, b):
    return lax.dot_general(a.astype(MXU_DT), b.astype(MXU_DT), (((0,), (0,)), ((), ())),
                           preferred_element_type=F32)


def _split2(x):
    hi = x.astype(MXU_DT)
    lo = (x - hi.astype(F32)).astype(MXU_DT)
    return hi, lo


def _dot_split(a_hi, a_lo, b):
    b_hi, b_lo = _split2(b)
    return (jnp.dot(a_hi, b_hi, preferred_element_type=F32)
            + jnp.dot(a_hi, b_lo, preferred_element_type=F32)
            + jnp.dot(a_lo, b_hi, preferred_element_type=F32))


def _dot3(a, b):
    a_hi, a_lo = _split2(a)
    return _dot_split(a_hi, a_lo, b)


def _dot_exact_lhs(t, x):
    x1 = x.astype(MXU_DT)
    r1 = x - x1.astype(F32)
    x2 = r1.astype(MXU_DT)
    x3 = (r1 - x2.astype(F32)).astype(MXU_DT)
    return (jnp.dot(t, x1, preferred_element_type=F32) + jnp.dot(t, x2, preferred_element_type=F32)
            + jnp.dot(t, x3, preferred_element_type=F32))


def _rms(x, n):
    return x * lax.rsqrt(jnp.sum(x * x, axis=-1, keepdims=True) * (1.0 / n) + EPS)


def _chunk_rows(c, size, base=0):
    if isinstance(c, int):
        return slice(base + c * size, base + (c + 1) * size)
    return pl.ds(pl.multiple_of(base + c * size, size), size)


def _log_sigmoid(x):
    return jnp.minimum(x, 0.0) - jnp.log1p(jnp.exp(-jnp.abs(x)))


def _resident(shape, index_map):
    return pl.BlockSpec(shape, index_map, pipeline_mode=pl.Buffered(1))


def _params(**kw):
    return pltpu.CompilerParams(vmem_limit_bytes=VMEM_LIMIT, **kw)


@functools.lru_cache(maxsize=None)
def _dft_tables(L):
    f = np.arange(L, dtype=np.int64)[:, None]
    s = np.arange(L, dtype=np.int64)[None, :]
    ang = np.pi * ((f * s) % (2 * L)).astype(np.float64) / L
    fr = np.cos(ang)
    fi = -np.sin(ang)
    fi[0, :] = np.where(np.arange(L) % 2 == 0, 1.0, -1.0)
    fwd = np.concatenate([fr, fi], axis=0)
    gr = np.cos(ang.T) * (2.0 / (2 * L))
    gr[:, 0] = 1.0 / (2 * L)
    gi = -np.sin(ang.T) * (2.0 / (2 * L))
    gi[:, 0] = np.where(np.arange(L) % 2 == 0, 1.0, -1.0) / (2 * L)
    inv = np.concatenate([gr, gi], axis=1)
    return fwd.astype(np.float32), inv.astype(np.float32)


@functools.lru_cache(maxsize=None)
def _filter_tables(L):
    tn = np.arange(L, dtype=np.float64) / L
    bands = np.linspace(1e-4, N_BANDS - 1, N_BANDS)
    ang = 2.0 * math.pi * tn[:, None] * bands[None, :]
    feat = np.zeros((L, 128), np.float32)
    feat[:, 0] = tn
    feat[:, 1:1 + N_BANDS] = np.cos(ang)
    feat[:, 1 + N_BANDS:FEAT_W] = np.sin(ang)
    deltas = np.abs(np.linspace(math.log(HY_TARGET) / HY_LONG_DECAY_PCT,
                                math.log(HY_TARGET) / HY_SHORT_DECAY_PCT, HY_W))
    window = (np.exp(-tn[:, None] * deltas[None, :]) + HY_SHIFT).astype(np.float32)
    return feat, window


@functools.lru_cache(maxsize=None)
def _rope_tables(L):
    rows = L // GRID_W
    row = np.repeat(np.arange(rows, dtype=np.float64), GRID_W)
    col = np.tile(np.arange(GRID_W, dtype=np.float64), rows)
    half = DH // 2
    n_freq = half // 2
    freqs = ROPE_BASE ** (-np.arange(n_freq, dtype=np.float64) / n_freq)
    ang = np.concatenate([row[:, None] * freqs, col[:, None] * freqs], axis=-1)
    cos = np.zeros((L, DHP), np.float32)
    sin = np.zeros((L, DHP), np.float32)
    cos[:, :half] = np.cos(ang)
    cos[:, 64:64 + half] = np.cos(ang)
    sin[:, :half] = -np.sin(ang)
    sin[:, 64:64 + half] = np.sin(ang)
    return cos, sin


@functools.lru_cache(maxsize=None)
def _tri_tables():
    i = np.arange(CHUNK)
    lower = (i[:, None] >= i[None, :]).astype(np.float32)
    upper = (i[:, None] <= i[None, :]).astype(np.float32)
    return lower, upper


def _pad_heads(w, axis=-1, rope=False):
    axis = axis % w.ndim
    parts = []

    def zeros(n):
        shp = list(w.shape)
        shp[axis] = n
        return jnp.zeros(shp, w.dtype)

    for h in range(HEADS):
        blk = lax.slice_in_dim(w, h * DH, (h + 1) * DH, axis=axis)
        if rope:
            half = DH // 2
            parts += [lax.slice_in_dim(blk, 0, half, axis=axis), zeros(64 - half),
                      lax.slice_in_dim(blk, half, DH, axis=axis), zeros(64 - half)]
        else:
            parts += [blk, zeros(DHP - DH)]
    return jnp.concatenate(parts, axis=axis)


def _unpad_heads_axis(x, axis, rope=False):
    if rope:
        half = DH // 2
        return jnp.concatenate([lax.slice_in_dim(x, 0, half, axis=axis),
                                lax.slice_in_dim(x, 64, 64 + half, axis=axis)], axis=axis)
    return lax.slice_in_dim(x, 0, DH, axis=axis)


def _mod_body(c_ref, w_ref, b_ref, o_ref):
    c = c_ref[...]
    s = c * (1.0 / (1.0 + jnp.exp(-c)))
    o_ref[...] = _dot(s, w_ref[...]) + b_ref[...]


def _mod_all(cvec8, w_mod, b_mod):
    tn = 1536
    nj = 6 * D_MODEL // tn
    return pl.pallas_call(
        _mod_body,
        grid=(DEPTH, nj),
        in_specs=[pl.BlockSpec((8, D_MODEL), lambda l, j: (0, 0)),
                  pl.BlockSpec((None, D_MODEL, tn), lambda l, j: (l, 0, j)),
                  pl.BlockSpec((None, 1, tn), lambda l, j: (l, 0, j))],
        out_specs=pl.BlockSpec((None, 8, tn), lambda l, j: (l, 0, j)),
        out_shape=jax.ShapeDtypeStruct((DEPTH, 8, 6 * D_MODEL), F32),
        compiler_params=_params(),
        name="mod_all",
    )(cvec8, w_mod, b_mod.reshape(DEPTH, 1, 6 * D_MODEL))


def _filt_body(feat_ref, win_ref, f_ref, w1_ref, b1_ref, w2_ref, b2_ref, w3_ref, b3_ref,
               fr_ref, hb_ref, p_ref, q_ref, p2_ref, *, L):
    fr = fr_ref[...]
    h = jnp.sin(fr * (_dot3(feat_ref[...], w1_ref[...]) + b1_ref[...]))
    h = jnp.sin(fr * (_dot3(h, w2_ref[...]) + b2_ref[...]))
    filt = _dot3(h, w3_ref[...]) + b3_ref[...]
    win = win_ref[...]
    row = lax.broadcasted_iota(jnp.int32, (L, 1), 0)
    taps = filt * jnp.concatenate([win, win, win, win], axis=1)
    lane = lax.broadcasted_iota(jnp.int32, (L, 4 * HY_W), 1)
    taps = jnp.where((row == 0) & (lane >= 2 * HY_W), 0.0, taps)
    spec = _dot(f_ref[...], taps)
    for o in range(2):
        a = spec[:, o * HY_W:(o + 1) * HY_W]
        b = spec[:, (2 + o) * HY_W:(3 + o) * HY_W]
        kr = a[:L] + b[:L]
        ki = a[L:] - b[L:]
        nyq = a[L:L + 1] + b[L:L + 1]
        bias = hb_ref[o:o + 1, :]
        p_ref[o] = kr + bias
        q_ref[o] = jnp.where(row == 0, 0.0, ki)
        p2_ref[o] = jnp.where(row == 0, nyq, kr) + bias


def _filter_spectra(L, fw1, fb1, fw2, fb2, fw3, fb3, ffr, hy_bias):
    feat, window = _filter_tables(L)
    fwd_table, _ = _dft_tables(L)
    const = lambda shape: _resident(shape, lambda l: (0,) * len(shape))
    per_layer = lambda shape: pl.BlockSpec((None,) + shape, lambda l: (l,) + (0,) * len(shape))
    out_spec = per_layer((2, L, HY_W))
    out_shape = jax.ShapeDtypeStruct((DEPTH, 2, L, HY_W), F32)
    return pl.pallas_call(
        functools.partial(_filt_body, L=L),
        grid=(DEPTH,),
        in_specs=[const((L, 128)), const((L, HY_W)), const((2 * L, L)),
                  per_layer((128, 128)), per_layer((1, 128)), per_layer((128, 128)), per_layer((1, 128)),
                  per_layer((128, 4 * HY_W)), per_layer((1, 4 * HY_W)), per_layer((1, 128)),
                  per_layer((2, HY_W))],
        out_specs=[out_spec, out_spec, out_spec],
        out_shape=[out_shape, out_shape, out_shape],
        compiler_params=_params(),
        name="hyena_filter_%d" % L,
    )(feat, window, fwd_table, fw1, fb1, fw2, fb2, fw3, fb3, ffr, hy_bias)


W_R0, W_H0, W_M0, W_G0, W_END = 0, 4 * HWP, 4 * HWP + 3 * HY_W, 8 * HWP + 3 * HY_W, 8 * HWP + 3 * HY_W + 128
IN_W = 8 * HW + 3 * HY_W + 16
FFN_SUB = 256
ROW_SUB = 256
SCAN_UNROLL = 2


def _prep_in_body(wt_ref, o_ref):
    kb = wt_ref.shape[1]
    half = DH // 2

    def zeros(n):
        return jnp.zeros((n, kb), F32)

    def head_tile(src, rope):
        if rope:
            return jnp.concatenate([wt_ref[src:src + half, :], zeros(64 - half),
                                    wt_ref[src + half:src + DH, :], zeros(64 - half)], axis=0)
        return jnp.concatenate([wt_ref[src:src + DH, :], zeros(DHP - DH)], axis=0)

    tiles = []
    for sec in range(4):
        tiles += [head_tile(sec * HW + h * DH, sec < 2) for h in range(HEADS)]
    tiles += [wt_ref[4 * HW + i * 128:4 * HW + (i + 1) * 128, :] for i in range(3 * HY_W // 128)]
    for sec in range(4):
        tiles += [head_tile(4 * HW + 3 * HY_W + sec * HW + h * DH, False) for h in range(HEADS)]
    g0 = 8 * HW + 3 * HY_W
    tiles.append(jnp.concatenate([wt_ref[g0:g0 + 16, :], zeros(112)], axis=0))
    for t, tile in enumerate(tiles):
        o_ref[:, t * 128:(t + 1) * 128] = tile.T.astype(o_ref.dtype)


def _prep_w_in(w_in_t):
    kb = 256
    return pl.pallas_call(
        _prep_in_body,
        grid=(DEPTH, D_MODEL // kb),
        in_specs=[pl.BlockSpec((None, IN_W, kb), lambda l, i: (l, 0, i))],
        out_specs=pl.BlockSpec((None, kb, W_END), lambda l, i: (l, i, 0)),
        out_shape=jax.ShapeDtypeStruct((DEPTH, D_MODEL, W_END), MXU_DT),
        compiler_params=_params(),
        name="prep_w_in",
    )(w_in_t)


def _in_body(x_ref, sc_ref, sh_ref, g_ref, w_ref, pr_ref, ph_ref, pm_ref, pg_ref):
    h = _rms(x_ref[...], D_MODEL) * g_ref[...]
    h = (h * (1.0 + sc_ref[...]) + sh_ref[...]).astype(MXU_DT)
    pr_ref[...] = jnp.dot(h, w_ref[:, W_R0:W_H0], preferred_element_type=F32)
    ph_ref[...] = jnp.dot(h, w_ref[:, W_H0:W_M0], preferred_element_type=F32)
    pm_ref[...] = jnp.dot(h, w_ref[:, W_M0:W_G0], preferred_element_type=F32)
    pg_ref[...] = jnp.dot(h, w_ref[:, W_G0:W_END], preferred_element_type=F32)


def _mod_spec(l, which, row_of_tile):
    return pl.BlockSpec((None, None, None, 1, D_MODEL), lambda i: (l, which, row_of_tile(i), 0, 0))


def _in_proj(x, mod, gain, w_in, l, row_of_tile, tm):
    n = x.shape[0]
    shapes = [4 * HWP, 3 * HY_W, 4 * HWP, 128]
    return pl.pallas_call(
        _in_body,
        grid=(n // tm,),
        in_specs=[pl.BlockSpec((tm, D_MODEL), lambda i: (i, 0)),
                  _mod_spec(l, 1, row_of_tile), _mod_spec(l, 0, row_of_tile),
                  pl.BlockSpec((None, 1, D_MODEL), lambda i: (l, 0, 0)),
                  _resident((None, D_MODEL, W_END), lambda i: (l, 0, 0))],
        out_specs=[pl.BlockSpec((tm, w), lambda i: (i, 0)) for w in shapes],
        out_shape=[jax.ShapeDtypeStruct((n, w), F32) for w in shapes],
        compiler_params=_params(),
        name="in_proj",
    )(x, mod, mod, gain, w_in)


def _ret_body(*refs, L, latent, nb):
    if latent:
        (q_ref, k_ref, v_ref, gate_ref, dl_ref, gain_ref, s0_ref, cos_ref, sin_ref,
         o_ref, qs, ks, acc, sst) = refs
    else:
        (q_ref, k_ref, v_ref, gate_ref, dl_ref, gain_ref, _, o_ref, sfin_ref, qs, ks, acc, sst) = refs
    half = DH // 2
    C = CHUNK
    nchunk = L // C
    lg = _log_sigmoid(dl_ref[...])
    rel = (lax.broadcasted_iota(jnp.int32, (C, C), 0) - lax.broadcasted_iota(jnp.int32, (C, C), 1)).astype(F32)
    ri = lax.broadcasted_iota(jnp.int32, (C, 1), 0).astype(F32)

    for bi in range(nb):
        seq = slice(bi * L, (bi + 1) * L)
        for h in range(HEADS):
            cols = slice(h * DHP, (h + 1) * DHP)
            q = q_ref[seq, cols]
            k = k_ref[seq, cols]
            if latent:
                cos = cos_ref[...]
                sin = sin_ref[...]
                q = q * cos + pltpu.roll(q, 64, 1) * sin
                k = k * cos + pltpu.roll(k, 64, 1) * sin
            qs[seq, cols] = q
            ks[seq, cols] = k * (DH ** -0.5)
            for d in range(2):
                sst[bi, d, h] = jnp.zeros((DHP, DHP), F32)
                if latent:
                    sst[bi, d, h, 0:half, 0:DH] = s0_ref[bi, d, h, 0:half, :]
                    sst[bi, d, h, 64:64 + half, 0:DH] = s0_ref[bi, d, h, half:DH, :]

    acc[...] = jnp.zeros_like(acc)
    consts = []
    for h in range(HEADS):
        lgf = lg[h:h + 1, :]
        lgb = lg[HEADS + h:HEADS + h + 1, :]
        lgf1 = lgf[:, 0:1]
        lgb1 = lgb[:, 0:1]
        mask = (jnp.where(rel >= 0, jnp.exp(rel * lgf), 0.0)
                + jnp.where(rel <= 0, jnp.exp(-rel * lgb), 0.0))
        consts.append(dict(
            mask=mask,
            qdec_f=jnp.exp((ri + 1.0) * lgf1), qdec_b=jnp.exp((C - ri) * lgb1),
            kdec_f=jnp.exp((C - 1.0 - ri) * lgf1), kdec_b=jnp.exp(ri * lgb1),
            cdec_f=jnp.exp(C * lgf1), cdec_b=jnp.exp(C * lgb1)))

    def step(j):
        for bi in range(nb):
            rows_f = _chunk_rows(j, C, bi * L)
            rows_b = _chunk_rows(nchunk - 1 - j, C, bi * L)
            for h in range(HEADS):
                cols = slice(h * DHP, (h + 1) * DHP)
                cn = consts[h]
                q = qs[rows_f, cols]
                k = ks[rows_f, cols]
                v = v_ref[rows_f, cols]
                s_f = sst[bi, 0, h]
                sc = _dot_nt(q, k) * cn["mask"]
                acc[rows_f, cols] += _dot(sc, v) + _dot(q, s_f) * cn["qdec_f"]
                sst[bi, 0, h] = s_f * cn["cdec_f"] + _dot_tn(k * cn["kdec_f"], v)
                q = qs[rows_b, cols]
                k = ks[rows_b, cols]
                v = v_ref[rows_b, cols]
                s_b = sst[bi, 1, h]
                acc[rows_b, cols] += _dot(q, s_b) * cn["qdec_b"]
                sst[bi, 1, h] = s_b * cn["cdec_b"] + _dot_tn(k * cn["kdec_b"], v)

    if nchunk <= 2:
        for j in range(nchunk):
            step(j)
    else:
        pl.loop(0, nchunk, unroll=SCAN_UNROLL)(step)

    if not latent:
        for bi in range(nb):
            for d in range(2):
                for h in range(HEADS):
                    sfin_ref[bi, d, h, 0:half, :] = sst[bi, d, h, 0:half, 0:DH]
                    sfin_ref[bi, d, h, half:DH, :] = sst[bi, d, h, 64:64 + half, 0:DH]

    for h in range(HEADS):
        cols = slice(h * DHP, (h + 1) * DHP)
        g = gate_ref[:, cols]
        y = _rms(acc[:, cols], DH) * gain_ref[:, cols] * (g * (1.0 / (1.0 + jnp.exp(-g))))
        o_ref[:, cols] = y.astype(o_ref.dtype)


def _state_spec(l, nb, *tail):
    shape = (nb, None, 2, HEADS) + tail
    return pl.BlockSpec(shape, lambda b: (b, l) + (0,) * (len(shape) - 2))


def _seqs_per_step(B, L):
    return min(B, max(1, 512 // L))


def _retention(pr, dl, gain, l, B, L, latent, state):
    nb = _seqs_per_step(B, L)
    blk = lambda j: pl.BlockSpec((nb * L, HWP), lambda b, j=j: (b, j))
    in_specs = [blk(0), blk(1), blk(2), blk(3),
                pl.BlockSpec((None, 8, 128), lambda b: (l, 0, 0)),
                pl.BlockSpec((None, 1, HWP), lambda b: (l, 0, 0))]
    args = [pr, pr, pr, pr, dl, gain]
    out_specs = [pl.BlockSpec((nb * L, HWP), lambda b: (b, 0))]
    out_shape = [jax.ShapeDtypeStruct((B * L, HWP), MXU_DT)]
    aliases = {}
    if latent:
        cos, sin = _rope_tables(L)
        in_specs += [_state_spec(l, nb, DH, DH),
                     pl.BlockSpec((L, DHP), lambda b: (0, 0)), pl.BlockSpec((L, DHP), lambda b: (0, 0))]
        args += [state, cos, sin]
    else:
        in_specs.append(pl.BlockSpec(memory_space=pl.ANY))
        args.append(state)
        aliases = {len(args) - 1: 1}
        out_specs.append(_state_spec(l, nb, DH, DH))
        out_shape.append(jax.ShapeDtypeStruct(state.shape, F32))
    return pl.pallas_call(
        functools.partial(_ret_body, L=L, latent=latent, nb=nb),
        grid=(B // nb,),
        in_specs=in_specs, out_specs=out_specs, out_shape=out_shape,
        input_output_aliases=aliases,
        scratch_shapes=[pltpu.VMEM((nb * L, HWP), F32), pltpu.VMEM((nb * L, HWP), F32),
                        pltpu.VMEM((nb * L, HWP), F32), pltpu.VMEM((nb, 2, HEADS, DHP, DHP), F32)],
        compiler_params=_params(),
        name="retention_%d" % L,
    )(*args)


def _ml_body(*refs, L, latent, nb):
    if latent:
        (q_ref, k_ref, v_ref, og_ref, gates_ref, gb_ref, gain_ref, tl_ref, tu_ref, c0_ref, n0_ref, m0_ref,
         o_ref, acc, bb_s, ub_s, pm_s, brow_s, ibrow_s, mc_s, mx_s, cn_s) = refs
    else:
        (q_ref, k_ref, v_ref, og_ref, gates_ref, gb_ref, gain_ref, tl_ref, tu_ref, _, _,
         o_ref, cfin_ref, nfin_ref, mfin_ref, acc, bb_s, ub_s, pm_s, brow_s, ibrow_s, mc_s, mx_s, cn_s) = refs
    C = CHUNK
    nchunk = L // C
    lane = lax.broadcasted_iota(jnp.int32, (C, DHP), 1)
    tri_r = lax.broadcasted_iota(jnp.int32, (C, C), 0)
    tri_c = lax.broadcasted_iota(jnp.int32, (C, C), 1)
    tl = tl_ref[...]
    tu = tu_ref[...]

    row_i = lax.broadcasted_iota(jnp.int32, (C, 128), 0)

    def cummax_rows(x, suffix):
        s = 1
        while s < C:
            if suffix:
                x = jnp.where(row_i < C - s, jnp.maximum(x, pltpu.roll(x, C - s, 0)), x)
            else:
                x = jnp.where(row_i >= s, jnp.maximum(x, pltpu.roll(x, s, 0)), x)
            s *= 2
        return x

    for c in range(nb * nchunk):
        rows = slice(c * C, (c + 1) * C)
        g = gates_ref[rows, :] + gb_ref[...]
        lf = _log_sigmoid(g)
        pre = _dot_exact_lhs(tl, lf)
        suf = _dot_exact_lhs(tu, lf)
        bc = jnp.where(lane < 8, pre, suf)
        brow_s[c * 16:(c + 1) * 16, :] = bc.T[0:16, :]
        ibrow_s[c * 16:(c + 1) * 16, :] = g.T[0:16, :]
        for d in range(2):
            for h in range(HEADS):
                sr = d * HEADS + h
                b_b = jnp.broadcast_to(bc[:, d * 8 + 4 + h:d * 8 + 5 + h], (C, 128))
                u_b = jnp.broadcast_to(g[:, d * 8 + h:d * 8 + h + 1], (C, 128)) - b_b
                bb_s[sr, rows, :] = b_b
                ub_s[sr, rows, :] = u_b
                pm_s[sr, rows, :] = cummax_rows(u_b, suffix=(d == 1))

    for bi in range(nb):
        for d in range(2):
            for h in range(HEADS):
                sr = d * HEADS + h
                m = m0_ref[bi, sr:sr + 1, :] if latent else jnp.zeros((1, 128), F32)
                for j in range(nchunk):
                    c = bi * nchunk + (j if d == 0 else nchunk - 1 - j)
                    last = c * C + (C - 1 if d == 0 else 0)
                    mx = jnp.maximum(m, pm_s[sr, last:last + 1, :])
                    mc_s[c * 8 + sr:c * 8 + sr + 1, :] = m
                    mx_s[c * 8 + sr:c * 8 + sr + 1, :] = mx
                    m = bb_s[sr, last:last + 1, :] + mx
                if not latent:
                    mfin_ref[bi, sr:sr + 1, :] = m

    acc[...] = jnp.zeros_like(acc)
    cn_s[...] = jnp.zeros_like(cn_s)
    if latent:
        for bi in range(nb):
            for d in range(2):
                for h in range(HEADS):
                    cn_s[bi, d, h, 0:DH, 0:DH] = c0_ref[bi, d, h]
                    n_row = jnp.concatenate([n0_ref[bi, d, h:h + 1, :], jnp.zeros((1, DHP - DH), F32)], axis=1)
                    cn_s[bi, d, h, :, DHP:] = jnp.broadcast_to(n_row, (DHP, DHP)).T
    ones_blk = jnp.ones((C, DHP), MXU_DT)

    def chain(bi, d, h, c):
        rows = _chunk_rows(c, C)
        brs = brow_s[_chunk_rows(c, 16), :]
        ibrs = ibrow_s[_chunk_rows(c, 16), :]
        mcs = mc_s[_chunk_rows(c, 8), :]
        mxs = mx_s[_chunk_rows(c, 8), :]
        causal = (tri_r >= tri_c) if d == 0 else (tri_r <= tri_c)
        cols = slice(h * DHP, (h + 1) * DHP)
        ic = d * 8 + h
        fc = d * 8 + 4 + h
        sr = d * HEADS + h
        m_c = mcs[sr:sr + 1, :]
        mx = mxs[sr:sr + 1, :]
        q = q_ref[rows, cols].astype(MXU_DT)
        k = k_ref[rows, cols] * (DH ** -0.5)
        v1 = jnp.concatenate([v_ref[rows, cols].astype(MXU_DT), ones_blk], axis=1)
        cn = cn_s[bi, d, h]
        urow = ibrs[ic:ic + 1, :] - brs[fc:fc + 1, :]
        mb = jnp.maximum(m_c, pm_s[sr, rows, :])
        w_intra = jnp.exp(jnp.where(causal, urow - mb, -jnp.inf))
        w_inter = jnp.exp(m_c - mb)
        s = _dot_nt(q, k) * w_intra
        tot = (jnp.dot(s.astype(MXU_DT), v1, preferred_element_type=F32)
               + jnp.concatenate([w_inter, w_inter], axis=1) * _dot(q, cn))
        den = tot[:, DHP:]
        floor = jnp.exp(-(bb_s[sr, rows, :] + mb))
        acc[rows, cols] += tot[:, :DHP] * (1.0 / jnp.maximum(jnp.abs(den), floor))
        kw = k * jnp.exp(ub_s[sr, rows, :] - mx)
        w_prev = jnp.exp(m_c - mx)
        cn_s[bi, d, h] = jnp.concatenate([w_prev, w_prev], axis=1) * cn + _dot_tn(kw, v1)

    def step(j):
        for bi in range(nb):
            for d in range(2):
                c = bi * nchunk + (j if d == 0 else nchunk - 1 - j)
                for h in range(HEADS):
                    chain(bi, d, h, c)

    if nchunk <= 2:
        for j in range(nchunk):
            step(j)
    else:
        pl.loop(0, nchunk, unroll=SCAN_UNROLL)(step)

    if not latent:
        for bi in range(nb):
            for d in range(2):
                for h in range(HEADS):
                    cfin_ref[bi, d, h] = cn_s[bi, d, h, 0:DH, 0:DH]
                    nfin_ref[bi, d, h:h + 1, :] = cn_s[bi, d, h, :, DHP:].T[0:1, 0:DH]

    for h in range(HEADS):
        cols = slice(h * DHP, (h + 1) * DHP)
        g = og_ref[:, cols]
        y = _rms(acc[:, cols], DH) * gain_ref[:, cols] * (1.0 / (1.0 + jnp.exp(-g)))
        o_ref[:, cols] = y.astype(o_ref.dtype)


def _mlstm(pm, pg, gb, gain, l, B, L, latent, c_state, n_state, m0=None):
    tl, tu = _tri_tables()
    tl = tl.astype(MXU_DT)
    tu = tu.astype(MXU_DT)
    nb = _seqs_per_step(B, L)
    blk = lambda j: pl.BlockSpec((nb * L, HWP), lambda b, j=j: (b, j))
    in_specs = [blk(0), blk(1), blk(2), blk(3),
                pl.BlockSpec((nb * L, 128), lambda b: (b, 0)),
                pl.BlockSpec((None, 1, 128), lambda b: (l, 0, 0)),
                pl.BlockSpec((None, 1, HWP), lambda b: (l, 0, 0)),
                pl.BlockSpec((CHUNK, CHUNK), lambda b: (0, 0)),
                pl.BlockSpec((CHUNK, CHUNK), lambda b: (0, 0))]
    args = [pm, pm, pm, pm, pg, gb, gain, tl, tu]
    out_specs = [pl.BlockSpec((nb * L, HWP), lambda b: (b, 0))]
    out_shape = [jax.ShapeDtypeStruct((B * L, HWP), MXU_DT)]
    aliases = {}
    if latent:
        in_specs += [_state_spec(l, nb, DH, DH), _state_spec(l, nb, DH),
                     pl.BlockSpec((nb, None, 8, 128), lambda b: (b, l, 0, 0))]
        args += [c_state, n_state, m0]
    else:
        in_specs += [pl.BlockSpec(memory_space=pl.ANY), pl.BlockSpec(memory_space=pl.ANY)]
        args += [c_state, n_state]
        aliases = {len(args) - 2: 1, len(args) - 1: 2}
        out_specs += [_state_spec(l, nb, DH, DH), _state_spec(l, nb, DH),
                      pl.BlockSpec((nb, 8, 128), lambda b: (b, 0, 0))]
        out_shape += [jax.ShapeDtypeStruct(c_state.shape, F32), jax.ShapeDtypeStruct(n_state.shape, F32),
                      jax.ShapeDtypeStruct((B, 8, 128), F32)]
    nchunk = L // CHUNK
    return pl.pallas_call(
        functools.partial(_ml_body, L=L, latent=latent, nb=nb),
        grid=(B // nb,),
        in_specs=in_specs, out_specs=out_specs, out_shape=out_shape,
        input_output_aliases=aliases,
        scratch_shapes=[pltpu.VMEM((nb * L, HWP), F32),
                        pltpu.VMEM((8, nb * L, 128), F32), pltpu.VMEM((8, nb * L, 128), F32),
                        pltpu.VMEM((8, nb * L, 128), F32),
                        pltpu.VMEM((nb * nchunk * 16, 128), F32), pltpu.VMEM((nb * nchunk * 16, 128), F32),
                        pltpu.VMEM((nb * nchunk * 8, 128), F32), pltpu.VMEM((nb * nchunk * 8, 128), F32),
                        pltpu.VMEM((nb, 2, HEADS, DHP, 2 * DHP), F32)],
        compiler_params=_params(),
        name="mlstm_%d" % L,
    )(*args)


def _shift_rows(x, seq_len, n_rows):
    pos = lax.broadcasted_iota(jnp.int32, (n_rows, 1), 0) & (seq_len - 1)
    prev = jnp.where(pos == 0, 0.0, pltpu.roll(x, 1, 0))
    nxt = jnp.where(pos == seq_len - 1, 0.0, pltpu.roll(x, n_rows - 1, 0))
    return prev, nxt


def _hy_body(p_ref, cw_ref, cb_ref, f_ref, g_ref, ps_ref, qs_ref, p2s_ref, o_ref, fh_ref, gh_ref, *, L):
    @pl.when(pl.program_id(0) == 0)
    def _():
        fh_ref[...] = f_ref[...].astype(MXU_DT)
        gh_ref[...] = g_ref[...].astype(MXU_DT)

    def conv(z, o):
        spec = _dot(fh_ref[...], z)
        xr = spec[:L]
        xi = spec[L:]
        p = ps_ref[o]
        q = qs_ref[o]
        yr = xr * p - xi * q
        yi = xr * q + xi * p2s_ref[o]
        return _dot(gh_ref[...], jnp.concatenate([yr, yi], axis=0))

    for i in range(p_ref.shape[0] // L):
        rows = slice(i * L, (i + 1) * L)
        x = p_ref[rows, :]
        prev, nxt = _shift_rows(x, L, L)
        hy = prev * cw_ref[0:1, :] + x * cw_ref[1:2, :] + nxt * cw_ref[2:3, :] + cb_ref[...]
        hv = hy[:, 0:HY_W]
        hx1 = hy[:, HY_W:2 * HY_W]
        hx2 = hy[:, 2 * HY_W:3 * HY_W]
        z = hx1 * conv(hv, 0)
        o_ref[rows, :] = (hx2 * conv(z, 1)).astype(o_ref.dtype)


def _hyena(ph, cw, cb, spectra, l, B, L):
    fwd_table, inv_table = _dft_tables(L)
    ps, qs, p2s = spectra
    const = lambda shape: _resident(shape, lambda b: (0,) * len(shape))
    spec_blk = _resident((None, 2, L, HY_W), lambda b: (l, 0, 0, 0))
    nb = min(B, max(1, 2048 // L))
    return pl.pallas_call(
        functools.partial(_hy_body, L=L),
        grid=(B // nb,),
        in_specs=[pl.BlockSpec((nb * L, 3 * HY_W), lambda b: (b, 0)),
                  pl.BlockSpec((None, 8, 3 * HY_W), lambda b: (l, 0, 0)),
                  pl.BlockSpec((None, 1, 3 * HY_W), lambda b: (l, 0, 0)),
                  const((2 * L, L)), const((L, 2 * L)),
                  spec_blk, spec_blk, spec_blk],
        out_specs=pl.BlockSpec((nb * L, HY_W), lambda b: (b, 0)),
        out_shape=jax.ShapeDtypeStruct((B * L, HY_W), MXU_DT),
        scratch_shapes=[pltpu.VMEM((2 * L, L), MXU_DT), pltpu.VMEM((L, 2 * L), MXU_DT)],
        compiler_params=_params(),
        name="hyena_%d" % L,
    )(ph, cw, cb, fwd_table, inv_table, ps, qs, p2s)


def _out_body(mr_ref, mh_ref, mm_ref, wr_ref, wh_ref, wm_ref, x_ref, g1_ref, sc2_ref, sh2_ref,
              gpost_ref, gpre_ref, x1_ref, h2_ref):
    for r in range(x_ref.shape[0] // ROW_SUB):
        rs = slice(r * ROW_SUB, (r + 1) * ROW_SUB)
        mix = (jnp.dot(mr_ref[rs, :], wr_ref[...], preferred_element_type=F32)
               + jnp.dot(mh_ref[rs, :], wh_ref[...], preferred_element_type=F32)
               + jnp.dot(mm_ref[rs, :], wm_ref[...], preferred_element_type=F32))
        x1 = x_ref[rs, :] + g1_ref[...] * (_rms(mix, D_MODEL) * gpost_ref[...])
        x1_ref[rs, :] = x1
        h2 = _rms(x1, D_MODEL) * gpre_ref[...] * (1.0 + sc2_ref[...]) + sh2_ref[...]
        h2_ref[rs, :] = h2.astype(h2_ref.dtype)


def _out_proj(mr, mh, mm, wr, wh, wm, x, mod, gpost, gpre, l, row_of_tile, tm):
    n = x.shape[0]
    tile = lambda w: pl.BlockSpec((tm, w), lambda i: (i, 0))
    layer = lambda r, c: pl.BlockSpec((None, r, c), lambda i: (l, 0, 0))
    return pl.pallas_call(
        _out_body,
        grid=(n // tm,),
        in_specs=[tile(HWP), tile(HY_W), tile(HWP),
                  layer(HWP, D_MODEL), layer(HY_W, D_MODEL), layer(HWP, D_MODEL),
                  tile(D_MODEL),
                  _mod_spec(l, 2, row_of_tile), _mod_spec(l, 4, row_of_tile), _mod_spec(l, 3, row_of_tile),
                  layer(1, D_MODEL), layer(1, D_MODEL)],
        out_specs=[tile(D_MODEL), tile(D_MODEL)],
        out_shape=[jax.ShapeDtypeStruct((n, D_MODEL), F32), jax.ShapeDtypeStruct((n, D_MODEL), MXU_DT)],
        compiler_params=_params(),
        name="out_proj",
    )(mr, mh, mm, wr, wh, wm, x, mod, mod, mod, gpost, gpre)


def _ffn_body(h_ref, wa_ref, wb_ref, cw_ref, cb_ref, wd_ref, x1_ref, g2_ref, gpost_ref, o_ref,
              *, seq_len, tm):
    j = pl.program_id(1)
    @pl.when(j == 0)
    def _():
        o_ref[...] = jnp.zeros_like(o_ref)

    h = h_ref[...]
    part = None
    for s in range(wa_ref.shape[1] // FFN_SUB):
        cs = slice(s * FFN_SUB, (s + 1) * FFN_SUB)
        a = jnp.dot(h, wa_ref[:, cs].astype(MXU_DT), preferred_element_type=F32)
        b = jnp.dot(h, wb_ref[:, cs].astype(MXU_DT), preferred_element_type=F32)
        prev, nxt = _shift_rows(a, seq_len, tm)
        a = prev * cw_ref[0:1, cs] + a * cw_ref[1:2, cs] + nxt * cw_ref[2:3, cs] + cb_ref[:, cs]
        gelu = 0.5 * a * (1.0 + jnp.tanh(math.sqrt(2.0 / math.pi) * (a + 0.044715 * (a * a * a))))
        p = jnp.dot((gelu * b).astype(MXU_DT), wd_ref[cs, :].astype(MXU_DT), preferred_element_type=F32)
        part = p if part is None else part + p
    o_ref[...] += part

    @pl.when(j == pl.num_programs(1) - 1)
    def _():
        o_ref[...] = x1_ref[...] + g2_ref[...] * (_rms(o_ref[...], D_MODEL) * gpost_ref[...])


def _conv_ffn(h2, w_up, cw, cb, w_down, x1, mod, gpost, l, row_of_tile, seq_len, tm, tf):
    n = h2.shape[0]
    nf = D_FF // tf
    return pl.pallas_call(
        functools.partial(_ffn_body, seq_len=seq_len, tm=tm),
        grid=(n // tm, nf),
        in_specs=[pl.BlockSpec((tm, D_MODEL), lambda i, j: (i, 0)),
                  pl.BlockSpec((None, D_MODEL, tf), lambda i, j: (l, 0, j)),
                  pl.BlockSpec((None, D_MODEL, tf), lambda i, j: (l, 0, j + nf)),
                  pl.BlockSpec((None, 8, tf), lambda i, j: (l, 0, j)),
                  pl.BlockSpec((None, 1, tf), lambda i, j: (l, 0, j)),
                  pl.BlockSpec((None, tf, D_MODEL), lambda i, j: (l, j, 0)),
                  pl.BlockSpec((tm, D_MODEL), lambda i, j: (i, 0)),
                  pl.BlockSpec((None, None, None, 1, D_MODEL), lambda i, j: (l, 5, row_of_tile(i), 0, 0)),
                  pl.BlockSpec((None, 1, D_MODEL), lambda i, j: (l, 0, 0))],
        out_specs=pl.BlockSpec((tm, D_MODEL), lambda i, j: (i, 0)),
        out_shape=jax.ShapeDtypeStruct((n, D_MODEL), F32),
        compiler_params=_params(),
        name="conv_ffn",
    )(h2, w_up, w_up, cw, cb, w_down, x1, mod, gpost)


def _pad_rows8(w):
    return jnp.concatenate([w, jnp.zeros((w.shape[0], 8 - w.shape[1], w.shape[2]), w.dtype)], axis=1)


def kernel(x_prompt, x_sample, c, state_ret, state_mlstm_c, state_mlstm_n, state_mlstm_m, c_ctx,
           norm_mix_pre, norm_mix_post, norm_ffn_pre, norm_ffn_post, w_mod, b_mod, w_in, w_out,
           ret_decay_logit, ret_norm_g, hy_conv_w, hy_conv_b, hy_f_w1, hy_f_b1, hy_f_w2, hy_f_b2,
           hy_f_w3, hy_f_b3, hy_sin_freq, hy_bias, ml_gate_bias, ml_norm_g,
           w_up, ffn_conv_w, ffn_conv_b, w_down):
    BP, LP, _ = x_prompt.shape
    BS, LS, _ = x_sample.shape

    w_in_p = _prep_w_in(jnp.swapaxes(w_in, 1, 2))
    w_out_r = _pad_heads(lax.slice_in_dim(w_out, 0, HW, axis=1), axis=1).astype(MXU_DT)
    w_out_h = lax.slice_in_dim(w_out, HW, HW + HY_W, axis=1).astype(MXU_DT)
    w_out_m = _pad_heads(lax.slice_in_dim(w_out, HW + HY_W, 2 * HW + HY_W, axis=1), axis=1).astype(MXU_DT)
    ret_gain = _pad_heads(ret_norm_g).reshape(DEPTH, 1, HWP)
    ml_gain = _pad_heads(ml_norm_g).reshape(DEPTH, 1, HWP)
    dl = jnp.broadcast_to(ret_decay_logit.reshape(DEPTH, 8, 1), (DEPTH, 8, 128))
    gate_bias = jnp.concatenate([ml_gate_bias.reshape(DEPTH, 1, 16), jnp.zeros((DEPTH, 1, 112), F32)], axis=2)
    hy_cw = _pad_rows8(hy_conv_w)
    hy_cb = hy_conv_b.reshape(DEPTH, 1, 3 * HY_W)
    ffn_cw = _pad_rows8(ffn_conv_w)
    ffn_cb = ffn_conv_b.reshape(DEPTH, 1, D_FF)
    g_mix_pre = norm_mix_pre.reshape(DEPTH, 1, D_MODEL)
    g_mix_post = norm_mix_post.reshape(DEPTH, 1, D_MODEL)
    g_ffn_pre = norm_ffn_pre.reshape(DEPTH, 1, D_MODEL)
    g_ffn_post = norm_ffn_post.reshape(DEPTH, 1, D_MODEL)
    pad2 = lambda a, r, cdim: jnp.pad(a, ((0, 0), (0, r - a.shape[1]), (0, cdim - a.shape[2])))
    fw1 = pad2(hy_f_w1, 128, 128)
    fb1 = pad2(hy_f_b1.reshape(DEPTH, 1, -1), 1, 128)
    fw2 = pad2(hy_f_w2, 128, 128)
    fb2 = pad2(hy_f_b2.reshape(DEPTH, 1, -1), 1, 128)
    fw3 = pad2(hy_f_w3, 128, 4 * HY_W)
    fb3 = hy_f_b3.reshape(DEPTH, 1, 4 * HY_W)
    ffr = pad2(hy_sin_freq.reshape(DEPTH, 1, -1), 1, 128)

    m0 = jnp.broadcast_to(state_mlstm_m.reshape(BS, DEPTH, 8, 1), (BS, DEPTH, 8, 128))
    new_ret = jnp.zeros((BP, DEPTH, 2, HEADS, DH, DH), F32)
    new_c = jnp.zeros((BP, DEPTH, 2, HEADS, DH, DH), F32)
    new_n = jnp.zeros((BP, DEPTH, 2, HEADS, DH), F32)

    cvec8 = jnp.concatenate([c_ctx.reshape(1, D_MODEL), c, jnp.zeros((8 - 1 - BS, D_MODEL), F32)], axis=0)
    mod = _mod_all(cvec8, w_mod, b_mod)
    mod = mod.reshape(DEPTH, 8, 6, 1, D_MODEL).transpose(0, 2, 1, 3, 4)
    spectra_p = _filter_spectra(LP, fw1, fb1, fw2, fb2, fw3, fb3, ffr, hy_bias)
    spectra_s = _filter_spectra(LS, fw1, fb1, fw2, fb2, fw3, fb3, ffr, hy_bias)

    xp = x_prompt.reshape(BP * LP, D_MODEL)
    xs = x_sample.reshape(BS * LS, D_MODEL)
    tm = 512
    ffn_tm = 1024
    ffn_tf = 512
    row_ctx = lambda i: 0
    row_lat_tm = lambda i: 1 + i // (LS // tm)
    row_lat_ffn = lambda i: 1 + i // (LS // ffn_tm)
    new_m = []

    def mix_and_ffn(x, pr_ph, mr, mm, l, B, L, row_tm, row_ffn, spectra):
        mh = _hyena(pr_ph, hy_cw, hy_cb, spectra, l, B, L)
        x1, h2 = _out_proj(mr, mh, mm, w_out_r, w_out_h, w_out_m, x, mod, g_mix_post, g_ffn_pre, l, row_tm, tm)
        return _conv_ffn(h2, w_up, ffn_cw, ffn_cb, w_down, x1, mod, g_ffn_post, l, row_ffn, L, ffn_tm, ffn_tf)

    for l in range(DEPTH):
        pr, ph, pm, pg = _in_proj(xp, mod, g_mix_pre, w_in_p, l, row_ctx, tm)
        mr, new_ret = _retention(pr, dl, ret_gain, l, BP, LP, False, new_ret)
        mm, new_c, new_n, m_fin = _mlstm(pm, pg, gate_bias, ml_gain, l, BP, LP, False, new_c, new_n)
        new_m.append(m_fin)
        xp = mix_and_ffn(xp, ph, mr, mm, l, BP, LP, row_ctx, row_ctx, spectra_p)
        pr, ph, pm, pg = _in_proj(xs, mod, g_mix_pre, w_in_p, l, row_lat_tm, tm)
        (mr,) = _retention(pr, dl, ret_gain, l, BS, LS, True, state_ret)
        (mm,) = _mlstm(pm, pg, gate_bias, ml_gain, l, BS, LS, True, state_mlstm_c, state_mlstm_n, m0)
        xs = mix_and_ffn(xs, ph, mr, mm, l, BS, LS, row_lat_tm, row_lat_ffn, spectra_s)

    out_m = jnp.stack(new_m, axis=1)[..., 0].reshape(BP, DEPTH, 2, HEADS)
    return (xp.reshape(BP, LP, D_MODEL), xs.reshape(BS, LS, D_MODEL), new_ret, new_c, new_n, out_m)
```

```python
import functools
import math

import numpy as np
import jax
import jax.numpy as jnp
from jax import lax
from jax.experimental import pallas as pl
from jax.experimental.pallas import tpu as pltpu

D_MODEL = 1024
DEPTH = 4
GRID_W = 64
HEADS = 4
DH = 96
DHP = 128
HW = HEADS * DH
HWP = HEADS * DHP
HY_W = 256
D_FF = 4 * D_MODEL
CHUNK = 128
N_BANDS = 16
FEAT_W = 1 + 2 * N_BANDS
HY_SHIFT = 0.05
HY_TARGET = 1e-2
HY_SHORT_DECAY_PCT = 0.3
HY_LONG_DECAY_PCT = 1.5
ROPE_BASE = 10000.0
EPS = 1e-6

MXU_DT = jnp.bfloat16
F32 = jnp.float32
VMEM_LIMIT = 56 * 1024 * 1024


def _dot(a, b):
    return jnp.dot(a.astype(MXU_DT), b.astype(MXU_DT), preferred_element_type=F32)


def _dot_nt(a, b):
    return lax.dot_general(a.astype(MXU_DT), b.astype(MXU_DT), (((1,), (1,)), ((), ())),
                           preferred_element_type=F32)


def _dot_tn(a, b):
    return lax.dot_general(a.astype(MXU_DT), b.astype(MXU_DT), (((0,), (0,)), ((), ())),
                           preferred_element_type=F32)


def _split2(x):
    hi = x.astype(MXU_DT)
    lo = (x - hi.astype(F32)).astype(MXU_DT)
    return hi, lo


def _dot_split(a_hi, a_lo, b):
    b_hi, b_lo = _split2(b)
    return (jnp.dot(a_hi, b_hi, preferred_element_type=F32)
            + jnp.dot(a_hi, b_lo, preferred_element_type=F32)
            + jnp.dot(a_lo, b_hi, preferred_element_type=F32))


def _dot3(a, b):
    a_hi, a_lo = _split2(a)
    return _dot_split(a_hi, a_lo, b)


def _dot_exact_lhs(t, x):
    x1 = x.astype(MXU_DT)
    r1 = x - x1.astype(F32)
    x2 = r1.astype(MXU_DT)
    x3 = (r1 - x2.astype(F32)).astype(MXU_DT)
    return (jnp.dot(t, x1, preferred_element_type=F32) + jnp.dot(t, x2, preferred_element_type=F32)
            + jnp.dot(t, x3, preferred_element_type=F32))


def _rms(x, n):
    return x * lax.rsqrt(jnp.sum(x * x, axis=-1, keepdims=True) * (1.0 / n) + EPS)


def _chunk_rows(c, size, base=0):
    if isinstance(c, int):
        return slice(base + c * size, base + (c + 1) * size)
    return pl.ds(pl.multiple_of(base + c * size, size), size)


def _log_sigmoid(x):
    return jnp.minimum(x, 0.0) - jnp.log1p(jnp.exp(-jnp.abs(x)))


def _resident(shape, index_map):
    return pl.BlockSpec(shape, index_map, pipeline_mode=pl.Buffered(1))


def _params(**kw):
    return pltpu.CompilerParams(vmem_limit_bytes=VMEM_LIMIT, **kw)


@functools.lru_cache(maxsize=None)
def _dft_tables(L):
    f = np.arange(L, dtype=np.int64)[:, None]
    s = np.arange(L, dtype=np.int64)[None, :]
    ang = np.pi * ((f * s) % (2 * L)).astype(np.float64) / L
    fr = np.cos(ang)
    fi = -np.sin(ang)
    fi[0, :] = np.where(np.arange(L) % 2 == 0, 1.0, -1.0)
    fwd = np.concatenate([fr, fi], axis=0)
    gr = np.cos(ang.T) * (2.0 / (2 * L))
    gr[:, 0] = 1.0 / (2 * L)
    gi = -np.sin(ang.T) * (2.0 / (2 * L))
    gi[:, 0] = np.where(np.arange(L) % 2 == 0, 1.0, -1.0) / (2 * L)
    inv = np.concatenate([gr, gi], axis=1)
    return fwd.astype(np.float32), inv.astype(np.float32)


@functools.lru_cache(maxsize=None)
def _filter_tables(L):
    tn = np.arange(L, dtype=np.float64) / L
    bands = np.linspace(1e-4, N_BANDS - 1, N_BANDS)
    ang = 2.0 * math.pi * tn[:, None] * bands[None, :]
    feat = np.zeros((L, 128), np.float32)
    feat[:, 0] = tn
    feat[:, 1:1 + N_BANDS] = np.cos(ang)
    feat[:, 1 + N_BANDS:FEAT_W] = np.sin(ang)
    deltas = np.abs(np.linspace(math.log(HY_TARGET) / HY_LONG_DECAY_PCT,
                                math.log(HY_TARGET) / HY_SHORT_DECAY_PCT, HY_W))
    window = (np.exp(-tn[:, None] * deltas[None, :]) + HY_SHIFT).astype(np.float32)
    return feat, window


@functools.lru_cache(maxsize=None)
def _rope_tables(L):
    rows = L // GRID_W
    row = np.repeat(np.arange(rows, dtype=np.float64), GRID_W)
    col = np.tile(np.arange(GRID_W, dtype=np.float64), rows)
    half = DH // 2
    n_freq = half // 2
    freqs = ROPE_BASE ** (-np.arange(n_freq, dtype=np.float64) / n_freq)
    ang = np.concatenate([row[:, None] * freqs, col[:, None] * freqs], axis=-1)
    cos = np.zeros((L, DHP), np.float32)
    sin = np.zeros((L, DHP), np.float32)
    cos[:, :half] = np.cos(ang)
    cos[:, 64:64 + half] = np.cos(ang)
    sin[:, :half] = -np.sin(ang)
    sin[:, 64:64 + half] = np.sin(ang)
    return cos, sin


@functools.lru_cache(maxsize=None)
def _tri_tables():
    i = np.arange(CHUNK)
    lower = (i[:, None] >= i[None, :]).astype(np.float32)
    upper = (i[:, None] <= i[None, :]).astype(np.float32)
    return lower, upper


def _pad_heads(w, axis=-1, rope=False):
    axis = axis % w.ndim
    parts = []

    def zeros(n):
        shp = list(w.shape)
        shp[axis] = n
        return jnp.zeros(shp, w.dtype)

    for h in range(HEADS):
        blk = lax.slice_in_dim(w, h * DH, (h + 1) * DH, axis=axis)
        if rope:
            half = DH // 2
            parts += [lax.slice_in_dim(blk, 0, half, axis=axis), zeros(64 - half),
                      lax.slice_in_dim(blk, half, DH, axis=axis), zeros(64 - half)]
        else:
            parts += [blk, zeros(DHP - DH)]
    return jnp.concatenate(parts, axis=axis)


def _unpad_heads_axis(x, axis, rope=False):
    if rope:
        half = DH // 2
        return jnp.concatenate([lax.slice_in_dim(x, 0, half, axis=axis),
                                lax.slice_in_dim(x, 64, 64 + half, axis=axis)], axis=axis)
    return lax.slice_in_dim(x, 0, DH, axis=axis)


def _mod_body(c_ref, w_ref, b_ref, o_ref):
    c = c_ref[...]
    s = c * (1.0 / (1.0 + jnp.exp(-c)))
    o_ref[...] = _dot(s, w_ref[...]) + b_ref[...]


def _mod_all(cvec8, w_mod, b_mod):
    tn = 1536
    nj = 6 * D_MODEL // tn
    return pl.pallas_call(
        _mod_body,
        grid=(DEPTH, nj),
        in_specs=[pl.BlockSpec((8, D_MODEL), lambda l, j: (0, 0)),
                  pl.BlockSpec((None, D_MODEL, tn), lambda l, j: (l, 0, j)),
                  pl.BlockSpec((None, 1, tn), lambda l, j: (l, 0, j))],
        out_specs=pl.BlockSpec((None, 8, tn), lambda l, j: (l, 0, j)),
        out_shape=jax.ShapeDtypeStruct((DEPTH, 8, 6 * D_MODEL), F32),
        compiler_params=_params(),
        name="mod_all",
    )(cvec8, w_mod, b_mod.reshape(DEPTH, 1, 6 * D_MODEL))


def _filt_body(feat_ref, win_ref, f_ref, w1_ref, b1_ref, w2_ref, b2_ref, w3_ref, b3_ref,
               fr_ref, hb_ref, p_ref, q_ref, p2_ref, *, L):
    fr = fr_ref[...]
    h = jnp.sin(fr * (_dot3(feat_ref[...], w1_ref[...]) + b1_ref[...]))
    h = jnp.sin(fr * (_dot3(h, w2_ref[...]) + b2_ref[...]))
    filt = _dot3(h, w3_ref[...]) + b3_ref[...]
    win = win_ref[...]
    row = lax.broadcasted_iota(jnp.int32, (L, 1), 0)
    taps = filt * jnp.concatenate([win, win, win, win], axis=1)
    lane = lax.broadcasted_iota(jnp.int32, (L, 4 * HY_W), 1)
    taps = jnp.where((row == 0) & (lane >= 2 * HY_W), 0.0, taps)
    spec = _dot(f_ref[...], taps)
    for o in range(2):
        a = spec[:, o * HY_W:(o + 1) * HY_W]
        b = spec[:, (2 + o) * HY_W:(3 + o) * HY_W]
        kr = a[:L] + b[:L]
        ki = a[L:] - b[L:]
        nyq = a[L:L + 1] + b[L:L + 1]
        bias = hb_ref[o:o + 1, :]
        p_ref[o] = kr + bias
        q_ref[o] = jnp.where(row == 0, 0.0, ki)
        p2_ref[o] = jnp.where(row == 0, nyq, kr) + bias


def _filter_spectra(L, fw1, fb1, fw2, fb2, fw3, fb3, ffr, hy_bias):
    feat, window = _filter_tables(L)
    fwd_table, _ = _dft_tables(L)
    const = lambda shape: _resident(shape, lambda l: (0,) * len(shape))
    per_layer = lambda shape: pl.BlockSpec((None,) + shape, lambda l: (l,) + (0,) * len(shape))
    out_spec = per_layer((2, L, HY_W))
    out_shape = jax.ShapeDtypeStruct((DEPTH, 2, L, HY_W), F32)
    return pl.pallas_call(
        functools.partial(_filt_body, L=L),
        grid=(DEPTH,),
        in_specs=[const((L, 128)), const((L, HY_W)), const((2 * L, L)),
                  per_layer((128, 128)), per_layer((1, 128)), per_layer((128, 128)), per_layer((1, 128)),
                  per_layer((128, 4 * HY_W)), per_layer((1, 4 * HY_W)), per_layer((1, 128)),
                  per_layer((2, HY_W))],
        out_specs=[out_spec, out_spec, out_spec],
        out_shape=[out_shape, out_shape, out_shape],
        compiler_params=_params(),
        name="hyena_filter_%d" % L,
    )(feat, window, fwd_table, fw1, fb1, fw2, fb2, fw3, fb3, ffr, hy_bias)


W_R0, W_H0, W_M0, W_G0, W_END = 0, 4 * HWP, 4 * HWP + 3 * HY_W, 8 * HWP + 3 * HY_W, 8 * HWP + 3 * HY_W + 128
IN_W = 8 * HW + 3 * HY_W + 16
FFN_SUB = 256
ROW_SUB = 256
SCAN_UNROLL = 2


def _prep_in_body(wt_ref, o_ref):
    kb = wt_ref.shape[1]
    half = DH // 2

    def zeros(n):
        return jnp.zeros((n, kb), F32)

    def head_tile(src, rope):
        if rope:
            return jnp.concatenate([wt_ref[src:src + half, :], zeros(64 - half),
                                    wt_ref[src + half:src + DH, :], zeros(64 - half)], axis=0)
        return jnp.concatenate([wt_ref[src:src + DH, :], zeros(DHP - DH)], axis=0)

    tiles = []
    for sec in range(4):
        tiles += [head_tile(sec * HW + h * DH, sec < 2) for h in range(HEADS)]
    tiles += [wt_ref[4 * HW + i * 128:4 * HW + (i + 1) * 128, :] for i in range(3 * HY_W // 128)]
    for sec in range(4):
        tiles += [head_tile(4 * HW + 3 * HY_W + sec * HW + h * DH, False) for h in range(HEADS)]
    g0 = 8 * HW + 3 * HY_W
    tiles.append(jnp.concatenate([wt_ref[g0:g0 + 16, :], zeros(112)], axis=0))
    for t, tile in enumerate(tiles):
        o_ref[:, t * 128:(t + 1) * 128] = tile.T.astype(o_ref.dtype)


def _prep_w_in(w_in_t):
    kb = 256
    return pl.pallas_call(
        _prep_in_body,
        grid=(DEPTH, D_MODEL // kb),
        in_specs=[pl.BlockSpec((None, IN_W, kb), lambda l, i: (l, 0, i))],
        out_specs=pl.BlockSpec((None, kb, W_END), lambda l, i: (l, i, 0)),
        out_shape=jax.ShapeDtypeStruct((DEPTH, D_MODEL, W_END), MXU_DT),
        compiler_params=_params(),
        name="prep_w_in",
    )(w_in_t)


def _in_body(x_ref, sc_ref, sh_ref, g_ref, w_ref, pr_ref, ph_ref, pm_ref, pg_ref):
    h = _rms(x_ref[...], D_MODEL) * g_ref[...]
    h = (h * (1.0 + sc_ref[...]) + sh_ref[...]).astype(MXU_DT)
    pr_ref[...] = jnp.dot(h, w_ref[:, W_R0:W_H0], preferred_element_type=F32)
    ph_ref[...] = jnp.dot(h, w_ref[:, W_H0:W_M0], preferred_element_type=F32)
    pm_ref[...] = jnp.dot(h, w_ref[:, W_M0:W_G0], preferred_element_type=F32)
    pg_ref[...] = jnp.dot(h, w_ref[:, W_G0:W_END], preferred_element_type=F32)


def _mod_spec(l, which, row_of_tile):
    return pl.BlockSpec((None, None, None, 1, D_MODEL), lambda i: (l, which, row_of_tile(i), 0, 0))


def _in_proj(x, mod, gain, w_in, l, row_of_tile, tm):
    n = x.shape[0]
    shapes = [4 * HWP, 3 * HY_W, 4 * HWP, 128]
    return pl.pallas_call(
        _in_body,
        grid=(n // tm,),
        in_specs=[pl.BlockSpec((tm, D_MODEL), lambda i: (i, 0)),
                  _mod_spec(l, 1, row_of_tile), _mod_spec(l, 0, row_of_tile),
                  pl.BlockSpec((None, 1, D_MODEL), lambda i: (l, 0, 0)),
                  _resident((None, D_MODEL, W_END), lambda i: (l, 0, 0))],
        out_specs=[pl.BlockSpec((tm, w), lambda i: (i, 0)) for w in shapes],
        out_shape=[jax.ShapeDtypeStruct((n, w), F32) for w in shapes],
        compiler_params=_params(),
        name="in_proj",
    )(x, mod, mod, gain, w_in)


def _ret_body(*refs, L, latent, nb):
    if latent:
        (q_ref, k_ref, v_ref, gate_ref, dl_ref, gain_ref, s0_ref, cos_ref, sin_ref,
         o_ref, qs, ks, acc, sst) = refs
    else:
        (q_ref, k_ref, v_ref, gate_ref, dl_ref, gain_ref, _, o_ref, sfin_ref, qs, ks, acc, sst) = refs
    half = DH // 2
    C = CHUNK
    nchunk = L // C
    lg = _log_sigmoid(dl_ref[...])
    rel = (lax.broadcasted_iota(jnp.int32, (C, C), 0) - lax.broadcasted_iota(jnp.int32, (C, C), 1)).astype(F32)
    ri = lax.broadcasted_iota(jnp.int32, (C, 1), 0).astype(F32)

    for bi in range(nb):
        seq = slice(bi * L, (bi + 1) * L)
        for h in range(HEADS):
            cols = slice(h * DHP, (h + 1) * DHP)
            q = q_ref[seq, cols]
            k = k_ref[seq, cols]
            if latent:
                cos = cos_ref[...]
                sin = sin_ref[...]
                q = q * cos + pltpu.roll(q, 64, 1) * sin
                k = k * cos + pltpu.roll(k, 64, 1) * sin
            qs[seq, cols] = q
            ks[seq, cols] = k * (DH ** -0.5)
            for d in range(2):
                sst[bi, d, h] = jnp.zeros((DHP, DHP), F32)
                if latent:
                    sst[bi, d, h, 0:half, 0:DH] = s0_ref[bi, d, h, 0:half, :]
                    sst[bi, d, h, 64:64 + half, 0:DH] = s0_ref[bi, d, h, half:DH, :]

    acc[...] = jnp.zeros_like(acc)
    consts = []
    for h in range(HEADS):
        lgf = lg[h:h + 1, :]
        lgb = lg[HEADS + h:HEADS + h + 1, :]
        lgf1 = lgf[:, 0:1]
        lgb1 = lgb[:, 0:1]
        mask = (jnp.where(rel >= 0, jnp.exp(rel * lgf), 0.0)
                + jnp.where(rel <= 0, jnp.exp(-rel * lgb), 0.0))
        consts.append(dict(
            mask=mask,
            qdec_f=jnp.exp((ri + 1.0) * lgf1), qdec_b=jnp.exp((C - ri) * lgb1),
            kdec_f=jnp.exp((C - 1.0 - ri) * lgf1), kdec_b=jnp.exp(ri * lgb1),
            cdec_f=jnp.exp(C * lgf1), cdec_b=jnp.exp(C * lgb1)))

    def step(j):
        for bi in range(nb):
            rows_f = _chunk_rows(j, C, bi * L)
            rows_b = _chunk_rows(nchunk - 1 - j, C, bi * L)
            for h in range(HEADS):
                cols = slice(h * DHP, (h + 1) * DHP)
                cn = consts[h]
                q = qs[rows_f, cols]
                k = ks[rows_f, cols]
                v = v_ref[rows_f, cols]
                s_f = sst[bi, 0, h]
                sc = _dot_nt(q, k) * cn["mask"]
                acc[rows_f, cols] += _dot(sc, v) + _dot(q, s_f) * cn["qdec_f"]
                sst[bi, 0, h] = s_f * cn["cdec_f"] + _dot_tn(k * cn["kdec_f"], v)
                q = qs[rows_b, cols]
                k = ks[rows_b, cols]
                v = v_ref[rows_b, cols]
                s_b = sst[bi, 1, h]
                acc[rows_b, cols] += _dot(q, s_b) * cn["qdec_b"]
                sst[bi, 1, h] = s_b * cn["cdec_b"] + _dot_tn(k * cn["kdec_b"], v)

    if nchunk <= 2:
        for j in range(nchunk):
            step(j)
    else:
        pl.loop(0, nchunk, unroll=SCAN_UNROLL)(step)

    if not latent:
        for bi in range(nb):
            for d in range(2):
                for h in range(HEADS):
                    sfin_ref[bi, d, h, 0:half, :] = sst[bi, d, h, 0:half, 0:DH]
                    sfin_ref[bi, d, h, half:DH, :] = sst[bi, d, h, 64:64 + half, 0:DH]

    for h in range(HEADS):
        cols = slice(h * DHP, (h + 1) * DHP)
        g = gate_ref[:, cols]
        y = _rms(acc[:, cols], DH) * gain_ref[:, cols] * (g * (1.0 / (1.0 + jnp.exp(-g))))
        o_ref[:, cols] = y.astype(o_ref.dtype)


def _state_spec(l, nb, *tail):
    shape = (nb, None, 2, HEADS) + tail
    return pl.BlockSpec(shape, lambda b: (b, l) + (0,) * (len(shape) - 2))


def _seqs_per_step(B, L):
    return min(B, max(1, 512 // L))


def _retention(pr, dl, gain, l, B, L, latent, state):
    nb = _seqs_per_step(B, L)
    blk = lambda j: pl.BlockSpec((nb * L, HWP), lambda b, j=j: (b, j))
    in_specs = [blk(0), blk(1), blk(2), blk(3),
                pl.BlockSpec((None, 8, 128), lambda b: (l, 0, 0)),
                pl.BlockSpec((None, 1, HWP), lambda b: (l, 0, 0))]
    args = [pr, pr, pr, pr, dl, gain]
    out_specs = [pl.BlockSpec((nb * L, HWP), lambda b: (b, 0))]
    out_shape = [jax.ShapeDtypeStruct((B * L, HWP), MXU_DT)]
    aliases = {}
    if latent:
        cos, sin = _rope_tables(L)
        in_specs += [_state_spec(l, nb, DH, DH),
                     pl.BlockSpec((L, DHP), lambda b: (0, 0)), pl.BlockSpec((L, DHP), lambda b: (0, 0))]
        args += [state, cos, sin]
    else:
        in_specs.append(pl.BlockSpec(memory_space=pl.ANY))
        args.append(state)
        aliases = {len(args) - 1: 1}
        out_specs.append(_state_spec(l, nb, DH, DH))
        out_shape.append(jax.ShapeDtypeStruct(state.shape, F32))
    return pl.pallas_call(
        functools.partial(_ret_body, L=L, latent=latent, nb=nb),
        grid=(B // nb,),
        in_specs=in_specs, out_specs=out_specs, out_shape=out_shape,
        input_output_aliases=aliases,
        scratch_shapes=[pltpu.VMEM((nb * L, HWP), F32), pltpu.VMEM((nb * L, HWP), F32),
                        pltpu.VMEM((nb * L, HWP), F32), pltpu.VMEM((nb, 2, HEADS, DHP, DHP), F32)],
        compiler_params=_params(),
        name="retention_%d" % L,
    )(*args)


def _ml_body(*refs, L, latent, nb):
    if latent:
        (q_ref, k_ref, v_ref, og_ref, gates_ref, gb_ref, gain_ref, tl_ref, tu_ref, c0_ref, n0_ref, m0_ref,
         o_ref, acc, bb_s, ub_s, pm_s, brow_s, ibrow_s, mc_s, mx_s, cn_s) = refs
    else:
        (q_ref, k_ref, v_ref, og_ref, gates_ref, gb_ref, gain_ref, tl_ref, tu_ref, _, _,
         o_ref, cfin_ref, nfin_ref, mfin_ref, acc, bb_s, ub_s, pm_s, brow_s, ibrow_s, mc_s, mx_s, cn_s) = refs
    C = CHUNK
    nchunk = L // C
    lane = lax.broadcasted_iota(jnp.int32, (C, DHP), 1)
    tri_r = lax.broadcasted_iota(jnp.int32, (C, C), 0)
    tri_c = lax.broadcasted_iota(jnp.int32, (C, C), 1)
    tl = tl_ref[...]
    tu = tu_ref[...]

    row_i = lax.broadcasted_iota(jnp.int32, (C, 128), 0)

    def cummax_rows(x, suffix):
        s = 1
        while s < C:
            if suffix:
                x = jnp.where(row_i < C - s, jnp.maximum(x, pltpu.roll(x, C - s, 0)), x)
            else:
                x = jnp.where(row_i >= s, jnp.maximum(x, pltpu.roll(x, s, 0)), x)
            s *= 2
        return x

    for c in range(nb * nchunk):
        rows = slice(c * C, (c + 1) * C)
        g = gates_ref[rows, :] + gb_ref[...]
        lf = _log_sigmoid(g)
        pre = _dot_exact_lhs(tl, lf)
        suf = _dot_exact_lhs(tu, lf)
        bc = jnp.where(lane < 8, pre, suf)
        brow_s[c * 16:(c + 1) * 16, :] = bc.T[0:16, :]
        ibrow_s[c * 16:(c + 1) * 16, :] = g.T[0:16, :]
        for d in range(2):
            for h in range(HEADS):
                sr = d * HEADS + h
                b_b = jnp.broadcast_to(bc[:, d * 8 + 4 + h:d * 8 + 5 + h], (C, 128))
                u_b = jnp.broadcast_to(g[:, d * 8 + h:d * 8 + h + 1], (C, 128)) - b_b
                bb_s[sr, rows, :] = b_b
                ub_s[sr, rows, :] = u_b
                pm_s[sr, rows, :] = cummax_rows(u_b, suffix=(d == 1))

    for bi in range(nb):
        for d in range(2):
            for h in range(HEADS):
                sr = d * HEADS + h
                m = m0_ref[bi, sr:sr + 1, :] if latent else jnp.zeros((1, 128), F32)
                for j in range(nchunk):
                    c = bi * nchunk + (j if d == 0 else nchunk - 1 - j)
                    last = c * C + (C - 1 if d == 0 else 0)
                    mx = jnp.maximum(m, pm_s[sr, last:last + 1, :])
                    mc_s[c * 8 + sr:c * 8 + sr + 1, :] = m
                    mx_s[c * 8 + sr:c * 8 + sr + 1, :] = mx
                    m = bb_s[sr, last:last + 1, :] + mx
                if not latent:
                    mfin_ref[bi, sr:sr + 1, :] = m

    acc[...] = jnp.zeros_like(acc)
    cn_s[...] = jnp.zeros_like(cn_s)
    if latent:
        for bi in range(nb):
            for d in range(2):
                for h in range(HEADS):
                    cn_s[bi, d, h, 0:DH, 0:DH] = c0_ref[bi, d, h]
                    n_row = jnp.concatenate([n0_ref[bi, d, h:h + 1, :], jnp.zeros((1, DHP - DH), F32)], axis=1)
                    cn_s[bi, d, h, :, DHP:] = jnp.broadcast_to(n_row, (DHP, DHP)).T
    ones_blk = jnp.ones((C, DHP), MXU_DT)

    def chain(bi, d, h, c):
        rows = _chunk_rows(c, C)
        brs = brow_s[_chunk_rows(c, 16), :]
        ibrs = ibrow_s[_chunk_rows(c, 16), :]
        mcs = mc_s[_chunk_rows(c, 8), :]
        mxs = mx_s[_chunk_rows(c, 8), :]
        causal = (tri_r >= tri_c) if d == 0 else (tri_r <= tri_c)
        cols = slice(h * DHP, (h + 1) * DHP)
        ic = d * 8 + h
        fc = d * 8 + 4 + h
        sr = d * HEADS + h
        m_c = mcs[sr:sr + 1, :]
        mx = mxs[sr:sr + 1, :]
        q = q_ref[rows, cols].astype(MXU_DT)
        k = k_ref[rows, cols] * (DH ** -0.5)
        v1 = jnp.concatenate([v_ref[rows, cols].astype(MXU_DT), ones_blk], axis=1)
        cn = cn_s[bi, d, h]
        urow = ibrs[ic:ic + 1, :] - brs[fc:fc + 1, :]
        mb = jnp.maximum(m_c, pm_s[sr, rows, :])
        w_intra = jnp.exp(jnp.where(causal, urow - mb, -jnp.inf))
        w_inter = jnp.exp(m_c - mb)
        s = _dot_nt(q, k) * w_intra
        tot = (jnp.dot(s.astype(MXU_DT), v1, preferred_element_type=F32)
               + jnp.concatenate([w_inter, w_inter], axis=1) * _dot(q, cn))
        den = tot[:, DHP:]
        floor = jnp.exp(-(bb_s[sr, rows, :] + mb))
        acc[rows, cols] += tot[:, :DHP] * (1.0 / jnp.maximum(jnp.abs(den), floor))
        kw = k * jnp.exp(ub_s[sr, rows, :] - mx)
        w_prev = jnp.exp(m_c - mx)
        cn_s[bi, d, h] = jnp.concatenate([w_prev, w_prev], axis=1) * cn + _dot_tn(kw, v1)

    def step(j):
        for bi in range(nb):
            for d in range(2):
                c = bi * nchunk + (j if d == 0 else nchunk - 1 - j)
                for h in range(HEADS):
                    chain(bi, d, h, c)

    if nchunk <= 2:
        for j in range(nchunk):
            step(j)
    else:
        pl.loop(0, nchunk, unroll=SCAN_UNROLL)(step)

    if not latent:
        for bi in range(nb):
            for d in range(2):
                for h in range(HEADS):
                    cfin_ref[bi, d, h] = cn_s[bi, d, h, 0:DH, 0:DH]
                    nfin_ref[bi, d, h:h + 1, :] = cn_s[bi, d, h, :, DHP:].T[0:1, 0:DH]

    for h in range(HEADS):
        cols = slice(h * DHP, (h + 1) * DHP)
        g = og_ref[:, cols]
        y = _rms(acc[:, cols], DH) * gain_ref[:, cols] * (1.0 / (1.0 + jnp.exp(-g)))
        o_ref[:, cols] = y.astype(o_ref.dtype)


def _mlstm(pm, pg, gb, gain, l, B, L, latent, c_state, n_state, m0=None):
    tl, tu = _tri_tables()
    tl = tl.astype(MXU_DT)
    tu = tu.astype(MXU_DT)
    nb = _seqs_per_step(B, L)
    blk = lambda j: pl.BlockSpec((nb * L, HWP), lambda b, j=j: (b, j))
    in_specs = [blk(0), blk(1), blk(2), blk(3),
                pl.BlockSpec((nb * L, 128), lambda b: (b, 0)),
                pl.BlockSpec((None, 1, 128), lambda b: (l, 0, 0)),
                pl.BlockSpec((None, 1, HWP), lambda b: (l, 0, 0)),
                pl.BlockSpec((CHUNK, CHUNK), lambda b: (0, 0)),
                pl.BlockSpec((CHUNK, CHUNK), lambda b: (0, 0))]
    args = [pm, pm, pm, pm, pg, gb, gain, tl, tu]
    out_specs = [pl.BlockSpec((nb * L, HWP), lambda b: (b, 0))]
    out_shape = [jax.ShapeDtypeStruct((B * L, HWP), MXU_DT)]
    aliases = {}
    if latent:
        in_specs += [_state_spec(l, nb, DH, DH), _state_spec(l, nb, DH),
                     pl.BlockSpec((nb, None, 8, 128), lambda b: (b, l, 0, 0))]
        args += [c_state, n_state, m0]
    else:
        in_specs += [pl.BlockSpec(memory_space=pl.ANY), pl.BlockSpec(memory_space=pl.ANY)]
        args += [c_state, n_state]
        aliases = {len(args) - 2: 1, len(args) - 1: 2}
        out_specs += [_state_spec(l, nb, DH, DH), _state_spec(l, nb, DH),
                      pl.BlockSpec((nb, 8, 128), lambda b: (b, 0, 0))]
        out_shape += [jax.ShapeDtypeStruct(c_state.shape, F32), jax.ShapeDtypeStruct(n_state.shape, F32),
                      jax.ShapeDtypeStruct((B, 8, 128), F32)]
    nchunk = L // CHUNK
    return pl.pallas_call(
        functools.partial(_ml_body, L=L, latent=latent, nb=nb),
        grid=(B // nb,),
        in_specs=in_specs, out_specs=out_specs, out_shape=out_shape,
        input_output_aliases=aliases,
        scratch_shapes=[pltpu.VMEM((nb * L, HWP), F32),
                        pltpu.VMEM((8, nb * L, 128), F32), pltpu.VMEM((8, nb * L, 128), F32),
                        pltpu.VMEM((8, nb * L, 128), F32),
                        pltpu.VMEM((nb * nchunk * 16, 128), F32), pltpu.VMEM((nb * nchunk * 16, 128), F32),
                        pltpu.VMEM((nb * nchunk * 8, 128), F32), pltpu.VMEM((nb * nchunk * 8, 128), F32),
                        pltpu.VMEM((nb, 2, HEADS, DHP, 2 * DHP), F32)],
        compiler_params=_params(),
        name="mlstm_%d" % L,
    )(*args)


def _shift_rows(x, seq_len, n_rows):
    pos = lax.broadcasted_iota(jnp.int32, (n_rows, 1), 0) & (seq_len - 1)
    prev = jnp.where(pos == 0, 0.0, pltpu.roll(x, 1, 0))
    nxt = jnp.where(pos == seq_len - 1, 0.0, pltpu.roll(x, n_rows - 1, 0))
    return prev, nxt


def _hy_body(p_ref, cw_ref, cb_ref, f_ref, g_ref, ps_ref, qs_ref, p2s_ref, o_ref, fh_ref, gh_ref, *, L):
    @pl.when(pl.program_id(0) == 0)
    def _():
        fh_ref[...] = f_ref[...].astype(MXU_DT)
        gh_ref[...] = g_ref[...].astype(MXU_DT)

    def conv(z, o):
        spec = _dot(fh_ref[...], z)
        xr = spec[:L]
        xi = spec[L:]
        p = ps_ref[o]
        q = qs_ref[o]
        yr = xr * p - xi * q
        yi = xr * q + xi * p2s_ref[o]
        return _dot(gh_ref[...], jnp.concatenate([yr, yi], axis=0))

    for i in range(p_ref.shape[0] // L):
        rows = slice(i * L, (i + 1) * L)
        x = p_ref[rows, :]
        prev, nxt = _shift_rows(x, L, L)
        hy = prev * cw_ref[0:1, :] + x * cw_ref[1:2, :] + nxt * cw_ref[2:3, :] + cb_ref[...]
        hv = hy[:, 0:HY_W]
        hx1 = hy[:, HY_W:2 * HY_W]
        hx2 = hy[:, 2 * HY_W:3 * HY_W]
        z = hx1 * conv(hv, 0)
        o_ref[rows, :] = (hx2 * conv(z, 1)).astype(o_ref.dtype)


def _hyena(ph, cw, cb, spectra, l, B, L):
    fwd_table, inv_table = _dft_tables(L)
    ps, qs, p2s = spectra
    const = lambda shape: _resident(shape, lambda b: (0,) * len(shape))
    spec_blk = _resident((None, 2, L, HY_W), lambda b: (l, 0, 0, 0))
    nb = min(B, max(1, 2048 // L))
    return pl.pallas_call(
        functools.partial(_hy_body, L=L),
        grid=(B // nb,),
        in_specs=[pl.BlockSpec((nb * L, 3 * HY_W), lambda b: (b, 0)),
                  pl.BlockSpec((None, 8, 3 * HY_W), lambda b: (l, 0, 0)),
                  pl.BlockSpec((None, 1, 3 * HY_W), lambda b: (l, 0, 0)),
                  const((2 * L, L)), const((L, 2 * L)),
                  spec_blk, spec_blk, spec_blk],
        out_specs=pl.BlockSpec((nb * L, HY_W), lambda b: (b, 0)),
        out_shape=jax.ShapeDtypeStruct((B * L, HY_W), MXU_DT),
        scratch_shapes=[pltpu.VMEM((2 * L, L), MXU_DT), pltpu.VMEM((L, 2 * L), MXU_DT)],
        compiler_params=_params(),
        name="hyena_%d" % L,
    )(ph, cw, cb, fwd_table, inv_table, ps, qs, p2s)


def _out_body(mr_ref, mh_ref, mm_ref, wr_ref, wh_ref, wm_ref, x_ref, g1_ref, sc2_ref, sh2_ref,
              gpost_ref, gpre_ref, x1_ref, h2_ref):
    for r in range(x_ref.shape[0] // ROW_SUB):
        rs = slice(r * ROW_SUB, (r + 1) * ROW_SUB)
        mix = (jnp.dot(mr_ref[rs, :], wr_ref[...], preferred_element_type=F32)
               + jnp.dot(mh_ref[rs, :], wh_ref[...], preferred_element_type=F32)
               + jnp.dot(mm_ref[rs, :], wm_ref[...], preferred_element_type=F32))
        x1 = x_ref[rs, :] + g1_ref[...] * (_rms(mix, D_MODEL) * gpost_ref[...])
        x1_ref[rs, :] = x1
        h2 = _rms(x1, D_MODEL) * gpre_ref[...] * (1.0 + sc2_ref[...]) + sh2_ref[...]
        h2_ref[rs, :] = h2.astype(h2_ref.dtype)


def _ffn_body(mr_ref, mh_ref, mm_ref, wr_ref, wh_ref, wm_ref, x_ref, g1_ref, sc2_ref, sh2_ref, gmix_ref, gpre_ref,
              wa_ref, wb_ref, cw_ref, cb_ref, wd_ref, g2_ref, gpost_ref, o_ref, x1_ref, h_ref, *, seq_len, tm):
    j = pl.program_id(1)

    @pl.when(j == 0)
    def _():
        _out_body(mr_ref, mh_ref, mm_ref, wr_ref, wh_ref, wm_ref, x_ref, g1_ref, sc2_ref, sh2_ref,
                  gmix_ref, gpre_ref, x1_ref, h_ref)
        o_ref[...] = jnp.zeros_like(o_ref)

    h = h_ref[...]
    part = None
    for s in range(wa_ref.shape[1] // FFN_SUB):
        cs = slice(s * FFN_SUB, (s + 1) * FFN_SUB)
        a = jnp.dot(h, wa_ref[:, cs].astype(MXU_DT), preferred_element_type=F32)
        b = jnp.dot(h, wb_ref[:, cs].astype(MXU_DT), preferred_element_type=F32)
        prev, nxt = _shift_rows(a, seq_len, tm)
        a = prev * cw_ref[0:1, cs] + a * cw_ref[1:2, cs] + nxt * cw_ref[2:3, cs] + cb_ref[:, cs]
        gelu = 0.5 * a * (1.0 + jnp.tanh(math.sqrt(2.0 / math.pi) * (a + 0.044715 * (a * a * a))))
        p = jnp.dot((gelu * b).astype(MXU_DT), wd_ref[cs, :].astype(MXU_DT), preferred_element_type=F32)
        part = p if part is None else part + p
    o_ref[...] += part

    @pl.when(j == pl.num_programs(1) - 1)
    def _():
        o_ref[...] = x1_ref[...] + g2_ref[...] * (_rms(o_ref[...], D_MODEL) * gpost_ref[...])


def _mix_ffn(mr, mh, mm, wr, wh, wm, x, mod, gmix, gpre, w_up, cw, cb, w_down, gpost, l, row_of_tile,
             seq_len, tm, tf):
    n = x.shape[0]
    nf = D_FF // tf
    tile = lambda w: pl.BlockSpec((tm, w), lambda i, j: (i, 0))
    layer = lambda r, c: pl.BlockSpec((None, r, c), lambda i, j: (l, 0, 0))
    mod_row = lambda which: pl.BlockSpec((None, None, None, 1, D_MODEL),
                                         lambda i, j: (l, which, row_of_tile(i), 0, 0))
    return pl.pallas_call(
        functools.partial(_ffn_body, seq_len=seq_len, tm=tm),
        grid=(n // tm, nf),
        in_specs=[tile(HWP), tile(HY_W), tile(HWP),
                  layer(HWP, D_MODEL), layer(HY_W, D_MODEL), layer(HWP, D_MODEL),
                  tile(D_MODEL), mod_row(2), mod_row(4), mod_row(3),
                  layer(1, D_MODEL), layer(1, D_MODEL),
                  pl.BlockSpec((None, D_MODEL, tf), lambda i, j: (l, 0, j)),
                  pl.BlockSpec((None, D_MODEL, tf), lambda i, j: (l, 0, j + nf)),
                  pl.BlockSpec((None, 8, tf), lambda i, j: (l, 0, j)),
                  pl.BlockSpec((None, 1, tf), lambda i, j: (l, 0, j)),
                  pl.BlockSpec((None, tf, D_MODEL), lambda i, j: (l, j, 0)),
                  mod_row(5), layer(1, D_MODEL)],
        out_specs=pl.BlockSpec((tm, D_MODEL), lambda i, j: (i, 0)),
        out_shape=jax.ShapeDtypeStruct((n, D_MODEL), F32),
        scratch_shapes=[pltpu.VMEM((tm, D_MODEL), F32), pltpu.VMEM((tm, D_MODEL), MXU_DT)],
        compiler_params=_params(),
        name="mix_ffn",
    )(mr, mh, mm, wr, wh, wm, x, mod, mod, mod, gmix, gpre, w_up, w_up, cw, cb, w_down, mod, gpost)


def _pad_rows8(w):
    return jnp.concatenate([w, jnp.zeros((w.shape[0], 8 - w.shape[1], w.shape[2]), w.dtype)], axis=1)


def kernel(x_prompt, x_sample, c, state_ret, state_mlstm_c, state_mlstm_n, state_mlstm_m, c_ctx,
           norm_mix_pre, norm_mix_post, norm_ffn_pre, norm_ffn_post, w_mod, b_mod, w_in, w_out,
           ret_decay_logit, ret_norm_g, hy_conv_w, hy_conv_b, hy_f_w1, hy_f_b1, hy_f_w2, hy_f_b2,
           hy_f_w3, hy_f_b3, hy_sin_freq, hy_bias, ml_gate_bias, ml_norm_g,
           w_up, ffn_conv_w, ffn_conv_b, w_down):
    BP, LP, _ = x_prompt.shape
    BS, LS, _ = x_sample.shape

    w_in_p = _prep_w_in(jnp.swapaxes(w_in, 1, 2))
    w_out_r = _pad_heads(lax.slice_in_dim(w_out, 0, HW, axis=1), axis=1).astype(MXU_DT)
    w_out_h = lax.slice_in_dim(w_out, HW, HW + HY_W, axis=1).astype(MXU_DT)
    w_out_m = _pad_heads(lax.slice_in_dim(w_out, HW + HY_W, 2 * HW + HY_W, axis=1), axis=1).astype(MXU_DT)
    ret_gain = _pad_heads(ret_norm_g).reshape(DEPTH, 1, HWP)
    ml_gain = _pad_heads(ml_norm_g).reshape(DEPTH, 1, HWP)
    dl = jnp.broadcast_to(ret_decay_logit.reshape(DEPTH, 8, 1), (DEPTH, 8, 128))
    gate_bias = jnp.concatenate([ml_gate_bias.reshape(DEPTH, 1, 16), jnp.zeros((DEPTH, 1, 112), F32)], axis=2)
    hy_cw = _pad_rows8(hy_conv_w)
    hy_cb = hy_conv_b.reshape(DEPTH, 1, 3 * HY_W)
    ffn_cw = _pad_rows8(ffn_conv_w)
    ffn_cb = ffn_conv_b.reshape(DEPTH, 1, D_FF)
    g_mix_pre = norm_mix_pre.reshape(DEPTH, 1, D_MODEL)
    g_mix_post = norm_mix_post.reshape(DEPTH, 1, D_MODEL)
    g_ffn_pre = norm_ffn_pre.reshape(DEPTH, 1, D_MODEL)
    g_ffn_post = norm_ffn_post.reshape(DEPTH, 1, D_MODEL)
    pad2 = lambda a, r, cdim: jnp.pad(a, ((0, 0), (0, r - a.shape[1]), (0, cdim - a.shape[2])))
    fw1 = pad2(hy_f_w1, 128, 128)
    fb1 = pad2(hy_f_b1.reshape(DEPTH, 1, -1), 1, 128)
    fw2 = pad2(hy_f_w2, 128, 128)
    fb2 = pad2(hy_f_b2.reshape(DEPTH, 1, -1), 1, 128)
    fw3 = pad2(hy_f_w3, 128, 4 * HY_W)
    fb3 = hy_f_b3.reshape(DEPTH, 1, 4 * HY_W)
    ffr = pad2(hy_sin_freq.reshape(DEPTH, 1, -1), 1, 128)

    m0 = jnp.broadcast_to(state_mlstm_m.reshape(BS, DEPTH, 8, 1), (BS, DEPTH, 8, 128))
    new_ret = jnp.zeros((BP, DEPTH, 2, HEADS, DH, DH), F32)
    new_c = jnp.zeros((BP, DEPTH, 2, HEADS, DH, DH), F32)
    new_n = jnp.zeros((BP, DEPTH, 2, HEADS, DH), F32)

    cvec8 = jnp.concatenate([c_ctx.reshape(1, D_MODEL), c, jnp.zeros((8 - 1 - BS, D_MODEL), F32)], axis=0)
    mod = _mod_all(cvec8, w_mod, b_mod)
    mod = mod.reshape(DEPTH, 8, 6, 1, D_MODEL).transpose(0, 2, 1, 3, 4)
    spectra_p = _filter_spectra(LP, fw1, fb1, fw2, fb2, fw3, fb3, ffr, hy_bias)
    spectra_s = _filter_spectra(LS, fw1, fb1, fw2, fb2, fw3, fb3, ffr, hy_bias)

    xp = x_prompt.reshape(BP * LP, D_MODEL)
    xs = x_sample.reshape(BS * LS, D_MODEL)
    tm = 512
    ffn_tm = 1024
    ffn_tf = 512
    row_ctx = lambda i: 0
    row_lat_tm = lambda i: 1 + i // (LS // tm)
    row_lat_ffn = lambda i: 1 + i // (LS // ffn_tm)
    new_m = []

    def mix_and_ffn(x, pr_ph, mr, mm, l, B, L, row_tm, row_ffn, spectra):
        mh = _hyena(pr_ph, hy_cw, hy_cb, spectra, l, B, L)
        return _mix_ffn(mr, mh, mm, w_out_r, w_out_h, w_out_m, x, mod, g_mix_post, g_ffn_pre,
                        w_up, ffn_cw, ffn_cb, w_down, g_ffn_post, l, row_ffn, L, ffn_tm, ffn_tf)

    for l in range(DEPTH):
        pr, ph, pm, pg = _in_proj(xp, mod, g_mix_pre, w_in_p, l, row_ctx, tm)
        mr, new_ret = _retention(pr, dl, ret_gain, l, BP, LP, False, new_ret)
        mm, new_c, new_n, m_fin = _mlstm(pm, pg, gate_bias, ml_gain, l, BP, LP, False, new_c, new_n)
        new_m.append(m_fin)
        xp = mix_and_ffn(xp, ph, mr, mm, l, BP, LP, row_ctx, row_ctx, spectra_p)
        pr, ph, pm, pg = _in_proj(xs, mod, g_mix_pre, w_in_p, l, row_lat_tm, tm)
        (mr,) = _retention(pr, dl, ret_gain, l, BS, LS, True, state_ret)
        (mm,) = _mlstm(pm, pg, gate_bias, ml_gain, l, BS, LS, True, state_mlstm_c, state_mlstm_n, m0)
        xs = mix_and_ffn(xs, ph, mr, mm, l, BS, LS, row_lat_tm, row_lat_ffn, spectra_s)

    out_m = jnp.stack(new_m, axis=1)[..., 0].reshape(BP, DEPTH, 2, HEADS)
    return (xp.reshape(BP, LP, D_MODEL), xs.reshape(BS, LS, D_MODEL), new_ret, new_c, new_n, out_m)
```

```python
import functools
import math

import numpy as np
import jax
import jax.numpy as jnp
from jax import lax
from jax.experimental import pallas as pl
from jax.experimental.pallas import tpu as pltpu

D_MODEL = 1024
DEPTH = 4
GRID_W = 64
HEADS = 4
DH = 96
DHP = 128
HW = HEADS * DH
HWP = HEADS * DHP
HY_W = 256
D_FF = 4 * D_MODEL
CHUNK = 128
N_BANDS = 16
FEAT_W = 1 + 2 * N_BANDS
HY_SHIFT = 0.05
HY_TARGET = 1e-2
HY_SHORT_DECAY_PCT = 0.3
HY_LONG_DECAY_PCT = 1.5
ROPE_BASE = 10000.0
EPS = 1e-6

MXU_DT = jnp.bfloat16
F32 = jnp.float32
VMEM_LIMIT = 56 * 1024 * 1024


def _dot(a, b):
    return jnp.dot(a.astype(MXU_DT), b.astype(MXU_DT), preferred_element_type=F32)


def _dot_nt(a, b):
    return lax.dot_general(a.astype(MXU_DT), b.astype(MXU_DT), (((1,), (1,)), ((), ())),
                           preferred_element_type=F32)


def _dot_tn(a, b):
    return lax.dot_general(a.astype(MXU_DT), b.astype(MXU_DT), (((0,), (0,)), ((), ())),
                           preferred_element_type=F32)


def _split2(x):
    hi = x.astype(MXU_DT)
    lo = (x - hi.astype(F32)).astype(MXU_DT)
    return hi, lo


def _dot_split(a_hi, a_lo, b):
    b_hi, b_lo = _split2(b)
    return (jnp.dot(a_hi, b_hi, preferred_element_type=F32)
            + jnp.dot(a_hi, b_lo, preferred_element_type=F32)
            + jnp.dot(a_lo, b_hi, preferred_element_type=F32))


def _dot3(a, b):
    a_hi, a_lo = _split2(a)
    return _dot_split(a_hi, a_lo, b)


def _dot_exact_lhs(t, x):
    x1 = x.astype(MXU_DT)
    r1 = x - x1.astype(F32)
    x2 = r1.astype(MXU_DT)
    x3 = (r1 - x2.astype(F32)).astype(MXU_DT)
    return (jnp.dot(t, x1, preferred_element_type=F32) + jnp.dot(t, x2, preferred_element_type=F32)
            + jnp.dot(t, x3, preferred_element_type=F32))


def _rms(x, n):
    return x * lax.rsqrt(jnp.sum(x * x, axis=-1, keepdims=True) * (1.0 / n) + EPS)


def _chunk_rows(c, size, base=0):
    if isinstance(c, int):
        return slice(base + c * size, base + (c + 1) * size)
    return pl.ds(pl.multiple_of(base + c * size, size), size)


def _log_sigmoid(x):
    return jnp.minimum(x, 0.0) - jnp.log1p(jnp.exp(-jnp.abs(x)))


def _resident(shape, index_map):
    return pl.BlockSpec(shape, index_map, pipeline_mode=pl.Buffered(1))


def _params(**kw):
    return pltpu.CompilerParams(vmem_limit_bytes=VMEM_LIMIT, **kw)


@functools.lru_cache(maxsize=None)
def _dft_tables(L):
    f = np.arange(L, dtype=np.int64)[:, None]
    s = np.arange(L, dtype=np.int64)[None, :]
    ang = np.pi * ((f * s) % (2 * L)).astype(np.float64) / L
    fr = np.cos(ang)
    fi = -np.sin(ang)
    fi[0, :] = np.where(np.arange(L) % 2 == 0, 1.0, -1.0)
    fwd = np.concatenate([fr, fi], axis=0)
    gr = np.cos(ang.T) * (2.0 / (2 * L))
    gr[:, 0] = 1.0 / (2 * L)
    gi = -np.sin(ang.T) * (2.0 / (2 * L))
    gi[:, 0] = np.where(np.arange(L) % 2 == 0, 1.0, -1.0) / (2 * L)
    inv = np.concatenate([gr, gi], axis=1)
    return fwd.astype(np.float32), inv.astype(np.float32)


@functools.lru_cache(maxsize=None)
def _filter_tables(L):
    tn = np.arange(L, dtype=np.float64) / L
    bands = np.linspace(1e-4, N_BANDS - 1, N_BANDS)
    ang = 2.0 * math.pi * tn[:, None] * bands[None, :]
    feat = np.zeros((L, 128), np.float32)
    feat[:, 0] = tn
    feat[:, 1:1 + N_BANDS] = np.cos(ang)
    feat[:, 1 + N_BANDS:FEAT_W] = np.sin(ang)
    deltas = np.abs(np.linspace(math.log(HY_TARGET) / HY_LONG_DECAY_PCT,
                                math.log(HY_TARGET) / HY_SHORT_DECAY_PCT, HY_W))
    window = (np.exp(-tn[:, None] * deltas[None, :]) + HY_SHIFT).astype(np.float32)
    return feat, window


@functools.lru_cache(maxsize=None)
def _rope_tables(L):
    rows = L // GRID_W
    row = np.repeat(np.arange(rows, dtype=np.float64), GRID_W)
    col = np.tile(np.arange(GRID_W, dtype=np.float64), rows)
    half = DH // 2
    n_freq = half // 2
    freqs = ROPE_BASE ** (-np.arange(n_freq, dtype=np.float64) / n_freq)
    ang = np.concatenate([row[:, None] * freqs, col[:, None] * freqs], axis=-1)
    cos = np.zeros((L, DHP), np.float32)
    sin = np.zeros((L, DHP), np.float32)
    cos[:, :half] = np.cos(ang)
    cos[:, 64:64 + half] = np.cos(ang)
    sin[:, :half] = -np.sin(ang)
    sin[:, 64:64 + half] = np.sin(ang)
    return cos, sin


@functools.lru_cache(maxsize=None)
def _tri_tables():
    i = np.arange(CHUNK)
    lower = (i[:, None] >= i[None, :]).astype(np.float32)
    upper = (i[:, None] <= i[None, :]).astype(np.float32)
    return lower, upper


def _pad_heads(w, axis=-1, rope=False):
    axis = axis % w.ndim
    parts = []

    def zeros(n):
        shp = list(w.shape)
        shp[axis] = n
        return jnp.zeros(shp, w.dtype)

    for h in range(HEADS):
        blk = lax.slice_in_dim(w, h * DH, (h + 1) * DH, axis=axis)
        if rope:
            half = DH // 2
            parts += [lax.slice_in_dim(blk, 0, half, axis=axis), zeros(64 - half),
                      lax.slice_in_dim(blk, half, DH, axis=axis), zeros(64 - half)]
        else:
            parts += [blk, zeros(DHP - DH)]
    return jnp.concatenate(parts, axis=axis)


def _unpad_heads_axis(x, axis, rope=False):
    if rope:
        half = DH // 2
        return jnp.concatenate([lax.slice_in_dim(x, 0, half, axis=axis),
                                lax.slice_in_dim(x, 64, 64 + half, axis=axis)], axis=axis)
    return lax.slice_in_dim(x, 0, DH, axis=axis)


def _mod_body(c_ref, w_ref, b_ref, o_ref):
    c = c_ref[...]
    s = c * (1.0 / (1.0 + jnp.exp(-c)))
    o_ref[...] = _dot(s, w_ref[...]) + b_ref[...]


def _mod_all(cvec8, w_mod, b_mod):
    tn = 1536
    nj = 6 * D_MODEL // tn
    return pl.pallas_call(
        _mod_body,
        grid=(DEPTH, nj),
        in_specs=[pl.BlockSpec((8, D_MODEL), lambda l, j: (0, 0)),
                  pl.BlockSpec((None, D_MODEL, tn), lambda l, j: (l, 0, j)),
                  pl.BlockSpec((None, 1, tn), lambda l, j: (l, 0, j))],
        out_specs=pl.BlockSpec((None, 8, tn), lambda l, j: (l, 0, j)),
        out_shape=jax.ShapeDtypeStruct((DEPTH, 8, 6 * D_MODEL), F32),
        compiler_params=_params(),
        name="mod_all",
    )(cvec8, w_mod, b_mod.reshape(DEPTH, 1, 6 * D_MODEL))


def _filt_body(feat_ref, win_ref, f_ref, w1_ref, b1_ref, w2_ref, b2_ref, w3_ref, b3_ref,
               fr_ref, hb_ref, p_ref, q_ref, p2_ref, *, L):
    fr = fr_ref[...]
    h = jnp.sin(fr * (_dot3(feat_ref[...], w1_ref[...]) + b1_ref[...]))
    h = jnp.sin(fr * (_dot3(h, w2_ref[...]) + b2_ref[...]))
    filt = _dot3(h, w3_ref[...]) + b3_ref[...]
    win = win_ref[...]
    row = lax.broadcasted_iota(jnp.int32, (L, 1), 0)
    taps = filt * jnp.concatenate([win, win, win, win], axis=1)
    lane = lax.broadcasted_iota(jnp.int32, (L, 4 * HY_W), 1)
    taps = jnp.where((row == 0) & (lane >= 2 * HY_W), 0.0, taps)
    spec = _dot(f_ref[...], taps)
    for o in range(2):
        a = spec[:, o * HY_W:(o + 1) * HY_W]
        b = spec[:, (2 + o) * HY_W:(3 + o) * HY_W]
        kr = a[:L] + b[:L]
        ki = a[L:] - b[L:]
        nyq = a[L:L + 1] + b[L:L + 1]
        bias = hb_ref[o:o + 1, :]
        p_ref[o] = kr + bias
        q_ref[o] = jnp.where(row == 0, 0.0, ki)
        p2_ref[o] = jnp.where(row == 0, nyq, kr) + bias


def _filter_spectra(L, fw1, fb1, fw2, fb2, fw3, fb3, ffr, hy_bias):
    feat, window = _filter_tables(L)
    fwd_table, _ = _dft_tables(L)
    const = lambda shape: _resident(shape, lambda l: (0,) * len(shape))
    per_layer = lambda shape: pl.BlockSpec((None,) + shape, lambda l: (l,) + (0,) * len(shape))
    out_spec = per_layer((2, L, HY_W))
    out_shape = jax.ShapeDtypeStruct((DEPTH, 2, L, HY_W), F32)
    return pl.pallas_call(
        functools.partial(_filt_body, L=L),
        grid=(DEPTH,),
        in_specs=[const((L, 128)), const((L, HY_W)), const((2 * L, L)),
                  per_layer((128, 128)), per_layer((1, 128)), per_layer((128, 128)), per_layer((1, 128)),
                  per_layer((128, 4 * HY_W)), per_layer((1, 4 * HY_W)), per_layer((1, 128)),
                  per_layer((2, HY_W))],
        out_specs=[out_spec, out_spec, out_spec],
        out_shape=[out_shape, out_shape, out_shape],
        compiler_params=_params(),
        name="hyena_filter_%d" % L,
    )(feat, window, fwd_table, fw1, fb1, fw2, fb2, fw3, fb3, ffr, hy_bias)


W_R0, W_H0, W_M0, W_G0, W_END = 0, 4 * HWP, 4 * HWP + 3 * HY_W, 8 * HWP + 3 * HY_W, 8 * HWP + 3 * HY_W + 128
IN_W = 8 * HW + 3 * HY_W + 16
FFN_SUB = 256
ROW_SUB = 256
SCAN_UNROLL = 4


def _prep_in_body(wt_ref, o_ref):
    kb = wt_ref.shape[1]
    half = DH // 2

    def zeros(n):
        return jnp.zeros((n, kb), F32)

    def head_tile(src, rope):
        if rope:
            return jnp.concatenate([wt_ref[src:src + half, :], zeros(64 - half),
                                    wt_ref[src + half:src + DH, :], zeros(64 - half)], axis=0)
        return jnp.concatenate([wt_ref[src:src + DH, :], zeros(DHP - DH)], axis=0)

    tiles = []
    for sec in range(4):
        tiles += [head_tile(sec * HW + h * DH, sec < 2) for h in range(HEADS)]
    tiles += [wt_ref[4 * HW + i * 128:4 * HW + (i + 1) * 128, :] for i in range(3 * HY_W // 128)]
    for sec in range(4):
        tiles += [head_tile(4 * HW + 3 * HY_W + sec * HW + h * DH, False) for h in range(HEADS)]
    g0 = 8 * HW + 3 * HY_W
    tiles.append(jnp.concatenate([wt_ref[g0:g0 + 16, :], zeros(112)], axis=0))
    for t, tile in enumerate(tiles):
        o_ref[:, t * 128:(t + 1) * 128] = tile.T.astype(o_ref.dtype)


def _prep_w_in(w_in_t):
    kb = 256
    return pl.pallas_call(
        _prep_in_body,
        grid=(DEPTH, D_MODEL // kb),
        in_specs=[pl.BlockSpec((None, IN_W, kb), lambda l, i: (l, 0, i))],
        out_specs=pl.BlockSpec((None, kb, W_END), lambda l, i: (l, i, 0)),
        out_shape=jax.ShapeDtypeStruct((DEPTH, D_MODEL, W_END), MXU_DT),
        compiler_params=_params(),
        name="prep_w_in",
    )(w_in_t)


def _in_body(x_ref, sc_ref, sh_ref, g_ref, w_ref, pr_ref, ph_ref, pm_ref, pg_ref):
    h = _rms(x_ref[...], D_MODEL) * g_ref[...]
    h = (h * (1.0 + sc_ref[...]) + sh_ref[...]).astype(MXU_DT)
    pr_ref[...] = jnp.dot(h, w_ref[:, W_R0:W_H0], preferred_element_type=F32)
    ph_ref[...] = jnp.dot(h, w_ref[:, W_H0:W_M0], preferred_element_type=F32)
    pm_ref[...] = jnp.dot(h, w_ref[:, W_M0:W_G0], preferred_element_type=F32)
    pg_ref[...] = jnp.dot(h, w_ref[:, W_G0:W_END], preferred_element_type=F32)


def _mod_spec(l, which, row_of_tile):
    return pl.BlockSpec((None, None, None, 1, D_MODEL), lambda i: (l, which, row_of_tile(i), 0, 0))


def _in_proj(x, mod, gain, w_in, l, row_of_tile, tm):
    n = x.shape[0]
    shapes = [4 * HWP, 3 * HY_W, 4 * HWP, 128]
    return pl.pallas_call(
        _in_body,
        grid=(n // tm,),
        in_specs=[pl.BlockSpec((tm, D_MODEL), lambda i: (i, 0)),
                  _mod_spec(l, 1, row_of_tile), _mod_spec(l, 0, row_of_tile),
                  pl.BlockSpec((None, 1, D_MODEL), lambda i: (l, 0, 0)),
                  _resident((None, D_MODEL, W_END), lambda i: (l, 0, 0))],
        out_specs=[pl.BlockSpec((tm, w), lambda i: (i, 0)) for w in shapes],
        out_shape=[jax.ShapeDtypeStruct((n, w), F32) for w in shapes],
        compiler_params=_params(),
        name="in_proj",
    )(x, mod, mod, gain, w_in)


def _ret_body(*refs, L, latent, nb):
    if latent:
        (q_ref, k_ref, v_ref, gate_ref, dl_ref, gain_ref, s0_ref, cos_ref, sin_ref,
         o_ref, qs, ks, acc, sst) = refs
    else:
        (q_ref, k_ref, v_ref, gate_ref, dl_ref, gain_ref, _, o_ref, sfin_ref, qs, ks, acc, sst) = refs
    half = DH // 2
    C = CHUNK
    nchunk = L // C
    lg = _log_sigmoid(dl_ref[...])
    rel = (lax.broadcasted_iota(jnp.int32, (C, C), 0) - lax.broadcasted_iota(jnp.int32, (C, C), 1)).astype(F32)
    ri = lax.broadcasted_iota(jnp.int32, (C, 1), 0).astype(F32)

    for bi in range(nb):
        seq = slice(bi * L, (bi + 1) * L)
        for h in range(HEADS):
            cols = slice(h * DHP, (h + 1) * DHP)
            q = q_ref[seq, cols]
            k = k_ref[seq, cols]
            if latent:
                cos = cos_ref[...]
                sin = sin_ref[...]
                q = q * cos + pltpu.roll(q, 64, 1) * sin
                k = k * cos + pltpu.roll(k, 64, 1) * sin
            qs[seq, cols] = q
            ks[seq, cols] = k * (DH ** -0.5)
            for d in range(2):
                sst[bi, d, h] = jnp.zeros((DHP, DHP), F32)
                if latent:
                    sst[bi, d, h, 0:half, 0:DH] = s0_ref[bi, d, h, 0:half, :]
                    sst[bi, d, h, 64:64 + half, 0:DH] = s0_ref[bi, d, h, half:DH, :]

    acc[...] = jnp.zeros_like(acc)
    consts = []
    for h in range(HEADS):
        lgf = lg[h:h + 1, :]
        lgb = lg[HEADS + h:HEADS + h + 1, :]
        lgf1 = lgf[:, 0:1]
        lgb1 = lgb[:, 0:1]
        mask = (jnp.where(rel >= 0, jnp.exp(rel * lgf), 0.0)
                + jnp.where(rel <= 0, jnp.exp(-rel * lgb), 0.0))
        consts.append(dict(
            mask=mask,
            qdec_f=jnp.exp((ri + 1.0) * lgf1), qdec_b=jnp.exp((C - ri) * lgb1),
            kdec_f=jnp.exp((C - 1.0 - ri) * lgf1), kdec_b=jnp.exp(ri * lgb1),
            cdec_f=jnp.exp(C * lgf1), cdec_b=jnp.exp(C * lgb1)))

    def step(j):
        for bi in range(nb):
            rows_f = _chunk_rows(j, C, bi * L)
            rows_b = _chunk_rows(nchunk - 1 - j, C, bi * L)
            for h in range(HEADS):
                cols = slice(h * DHP, (h + 1) * DHP)
                cn = consts[h]
                q = qs[rows_f, cols]
                k = ks[rows_f, cols]
                v = v_ref[rows_f, cols]
                s_f = sst[bi, 0, h]
                sc = _dot_nt(q, k) * cn["mask"]
                acc[rows_f, cols] += _dot(sc, v) + _dot(q, s_f) * cn["qdec_f"]
                sst[bi, 0, h] = s_f * cn["cdec_f"] + _dot_tn(k * cn["kdec_f"], v)
                q = qs[rows_b, cols]
                k = ks[rows_b, cols]
                v = v_ref[rows_b, cols]
                s_b = sst[bi, 1, h]
                acc[rows_b, cols] += _dot(q, s_b) * cn["qdec_b"]
                sst[bi, 1, h] = s_b * cn["cdec_b"] + _dot_tn(k * cn["kdec_b"], v)

    if nchunk <= 2:
        for j in range(nchunk):
            step(j)
    else:
        pl.loop(0, nchunk, unroll=SCAN_UNROLL)(step)

    if not latent:
        for bi in range(nb):
            for d in range(2):
                for h in range(HEADS):
                    sfin_ref[bi, d, h, 0:half, :] = sst[bi, d, h, 0:half, 0:DH]
                    sfin_ref[bi, d, h, half:DH, :] = sst[bi, d, h, 64:64 + half, 0:DH]

    for h in range(HEADS):
        cols = slice(h * DHP, (h + 1) * DHP)
        g = gate_ref[:, cols]
        y = _rms(acc[:, cols], DH) * gain_ref[:, cols] * (g * (1.0 / (1.0 + jnp.exp(-g))))
        o_ref[:, cols] = y.astype(o_ref.dtype)


def _state_spec(l, nb, *tail):
    shape = (nb, None, 2, HEADS) + tail
    return pl.BlockSpec(shape, lambda b: (b, l) + (0,) * (len(shape) - 2))


def _seqs_per_step(B, L):
    return min(B, max(1, 1024 // L))


def _retention(pr, dl, gain, l, B, L, latent, state):
    nb = _seqs_per_step(B, L)
    blk = lambda j: pl.BlockSpec((nb * L, HWP), lambda b, j=j: (b, j))
    in_specs = [blk(0), blk(1), blk(2), blk(3),
                pl.BlockSpec((None, 8, 128), lambda b: (l, 0, 0)),
                pl.BlockSpec((None, 1, HWP), lambda b: (l, 0, 0))]
    args = [pr, pr, pr, pr, dl, gain]
    out_specs = [pl.BlockSpec((nb * L, HWP), lambda b: (b, 0))]
    out_shape = [jax.ShapeDtypeStruct((B * L, HWP), MXU_DT)]
    aliases = {}
    if latent:
        cos, sin = _rope_tables(L)
        in_specs += [_state_spec(l, nb, DH, DH),
                     pl.BlockSpec((L, DHP), lambda b: (0, 0)), pl.BlockSpec((L, DHP), lambda b: (0, 0))]
        args += [state, cos, sin]
    else:
        in_specs.append(pl.BlockSpec(memory_space=pl.ANY))
        args.append(state)
        aliases = {len(args) - 1: 1}
        out_specs.append(_state_spec(l, nb, DH, DH))
        out_shape.append(jax.ShapeDtypeStruct(state.shape, F32))
    return pl.pallas_call(
        functools.partial(_ret_body, L=L, latent=latent, nb=nb),
        grid=(B // nb,),
        in_specs=in_specs, out_specs=out_specs, out_shape=out_shape,
        input_output_aliases=aliases,
        scratch_shapes=[pltpu.VMEM((nb * L, HWP), F32), pltpu.VMEM((nb * L, HWP), F32),
                        pltpu.VMEM((nb * L, HWP), F32), pltpu.VMEM((nb, 2, HEADS, DHP, DHP), F32)],
        compiler_params=_params(),
        name="retention_%d" % L,
    )(*args)


def _ml_body(*refs, L, latent, nb):
    if latent:
        (q_ref, k_ref, v_ref, og_ref, gates_ref, gb_ref, gain_ref, tl_ref, tu_ref, c0_ref, n0_ref, m0_ref,
         o_ref, acc, bb_s, ub_s, pm_s, brow_s, ibrow_s, mc_s, mx_s, cn_s) = refs
    else:
        (q_ref, k_ref, v_ref, og_ref, gates_ref, gb_ref, gain_ref, tl_ref, tu_ref, _, _,
         o_ref, cfin_ref, nfin_ref, mfin_ref, acc, bb_s, ub_s, pm_s, brow_s, ibrow_s, mc_s, mx_s, cn_s) = refs
    C = CHUNK
    nchunk = L // C
    lane = lax.broadcasted_iota(jnp.int32, (C, DHP), 1)
    tri_r = lax.broadcasted_iota(jnp.int32, (C, C), 0)
    tri_c = lax.broadcasted_iota(jnp.int32, (C, C), 1)
    tl = tl_ref[...]
    tu = tu_ref[...]

    row_i = lax.broadcasted_iota(jnp.int32, (C, 128), 0)

    def cummax_rows(x, suffix):
        s = 1
        while s < C:
            if suffix:
                x = jnp.where(row_i < C - s, jnp.maximum(x, pltpu.roll(x, C - s, 0)), x)
            else:
                x = jnp.where(row_i >= s, jnp.maximum(x, pltpu.roll(x, s, 0)), x)
            s *= 2
        return x

    for c in range(nb * nchunk):
        rows = slice(c * C, (c + 1) * C)
        g = gates_ref[rows, :] + gb_ref[...]
        lf = _log_sigmoid(g)
        pre = _dot_exact_lhs(tl, lf)
        suf = _dot_exact_lhs(tu, lf)
        bc = jnp.where(lane < 8, pre, suf)
        brow_s[c * 16:(c + 1) * 16, :] = bc.T[0:16, :]
        ibrow_s[c * 16:(c + 1) * 16, :] = g.T[0:16, :]
        for d in range(2):
            for h in range(HEADS):
                sr = d * HEADS + h
                b_b = jnp.broadcast_to(bc[:, d * 8 + 4 + h:d * 8 + 5 + h], (C, 128))
                u_b = jnp.broadcast_to(g[:, d * 8 + h:d * 8 + h + 1], (C, 128)) - b_b
                bb_s[sr, rows, :] = b_b
                ub_s[sr, rows, :] = u_b
                pm_s[sr, rows, :] = cummax_rows(u_b, suffix=(d == 1))

    for bi in range(nb):
        for d in range(2):
            for h in range(HEADS):
                sr = d * HEADS + h
                m = m0_ref[bi, sr:sr + 1, :] if latent else jnp.zeros((1, 128), F32)
                for j in range(nchunk):
                    c = bi * nchunk + (j if d == 0 else nchunk - 1 - j)
                    last = c * C + (C - 1 if d == 0 else 0)
                    mx = jnp.maximum(m, pm_s[sr, last:last + 1, :])
                    mc_s[c * 8 + sr:c * 8 + sr + 1, :] = m
                    mx_s[c * 8 + sr:c * 8 + sr + 1, :] = mx
                    m = bb_s[sr, last:last + 1, :] + mx
                if not latent:
                    mfin_ref[bi, sr:sr + 1, :] = m

    acc[...] = jnp.zeros_like(acc)
    cn_s[...] = jnp.zeros_like(cn_s)
    if latent:
        for bi in range(nb):
            for d in range(2):
                for h in range(HEADS):
                    cn_s[bi, d, h, 0:DH, 0:DH] = c0_ref[bi, d, h]
                    n_row = jnp.concatenate([n0_ref[bi, d, h:h + 1, :], jnp.zeros((1, DHP - DH), F32)], axis=1)
                    cn_s[bi, d, h, :, DHP:] = jnp.broadcast_to(n_row, (DHP, DHP)).T
    ones_blk = jnp.ones((C, DHP), MXU_DT)

    def chain(bi, d, h, c):
        rows = _chunk_rows(c, C)
        brs = brow_s[_chunk_rows(c, 16), :]
        ibrs = ibrow_s[_chunk_rows(c, 16), :]
        mcs = mc_s[_chunk_rows(c, 8), :]
        mxs = mx_s[_chunk_rows(c, 8), :]
        causal = (tri_r >= tri_c) if d == 0 else (tri_r <= tri_c)
        cols = slice(h * DHP, (h + 1) * DHP)
        ic = d * 8 + h
        fc = d * 8 + 4 + h
        sr = d * HEADS + h
        m_c = mcs[sr:sr + 1, :]
        mx = mxs[sr:sr + 1, :]
        q = q_ref[rows, cols].astype(MXU_DT)
        k = k_ref[rows, cols] * (DH ** -0.5)
        v1 = jnp.concatenate([v_ref[rows, cols].astype(MXU_DT), ones_blk], axis=1)
        cn = cn_s[bi, d, h]
        urow = ibrs[ic:ic + 1, :] - brs[fc:fc + 1, :]
        mb = jnp.maximum(m_c, pm_s[sr, rows, :])
        w_intra = jnp.exp(jnp.where(causal, urow - mb, -jnp.inf))
        w_inter = jnp.exp(m_c - mb)
        s = _dot_nt(q, k) * w_intra
        tot = (jnp.dot(s.astype(MXU_DT), v1, preferred_element_type=F32)
               + jnp.concatenate([w_inter, w_inter], axis=1) * _dot(q, cn))
        den = tot[:, DHP:]
        floor = jnp.exp(-(bb_s[sr, rows, :] + mb))
        acc[rows, cols] += tot[:, :DHP] * (1.0 / jnp.maximum(jnp.abs(den), floor))
        kw = k * jnp.exp(ub_s[sr, rows, :] - mx)
        w_prev = jnp.exp(m_c - mx)
        cn_s[bi, d, h] = jnp.concatenate([w_prev, w_prev], axis=1) * cn + _dot_tn(kw, v1)

    def step(j):
        for bi in range(nb):
            for d in range(2):
                c = bi * nchunk + (j if d == 0 else nchunk - 1 - j)
                for h in range(HEADS):
                    chain(bi, d, h, c)

    if nchunk <= 2:
        for j in range(nchunk):
            step(j)
    else:
        pl.loop(0, nchunk, unroll=SCAN_UNROLL)(step)

    if not latent:
        for bi in range(nb):
            for d in range(2):
                for h in range(HEADS):
                    cfin_ref[bi, d, h] = cn_s[bi, d, h, 0:DH, 0:DH]
                    nfin_ref[bi, d, h:h + 1, :] = cn_s[bi, d, h, :, DHP:].T[0:1, 0:DH]

    for h in range(HEADS):
        cols = slice(h * DHP, (h + 1) * DHP)
        g = og_ref[:, cols]
        y = _rms(acc[:, cols], DH) * gain_ref[:, cols] * (1.0 / (1.0 + jnp.exp(-g)))
        o_ref[:, cols] = y.astype(o_ref.dtype)


def _mlstm(pm, pg, gb, gain, l, B, L, latent, c_state, n_state, m0=None):
    tl, tu = _tri_tables()
    tl = tl.astype(MXU_DT)
    tu = tu.astype(MXU_DT)
    nb = _seqs_per_step(B, L)
    blk = lambda j: pl.BlockSpec((nb * L, HWP), lambda b, j=j: (b, j))
    in_specs = [blk(0), blk(1), blk(2), blk(3),
                pl.BlockSpec((nb * L, 128), lambda b: (b, 0)),
                pl.BlockSpec((None, 1, 128), lambda b: (l, 0, 0)),
                pl.BlockSpec((None, 1, HWP), lambda b: (l, 0, 0)),
                pl.BlockSpec((CHUNK, CHUNK), lambda b: (0, 0)),
                pl.BlockSpec((CHUNK, CHUNK), lambda b: (0, 0))]
    args = [pm, pm, pm, pm, pg, gb, gain, tl, tu]
    out_specs = [pl.BlockSpec((nb * L, HWP), lambda b: (b, 0))]
    out_shape = [jax.ShapeDtypeStruct((B * L, HWP), MXU_DT)]
    aliases = {}
    if latent:
        in_specs += [_state_spec(l, nb, DH, DH), _state_spec(l, nb, DH),
                     pl.BlockSpec((nb, None, 8, 128), lambda b: (b, l, 0, 0))]
        args += [c_state, n_state, m0]
    else:
        in_specs += [pl.BlockSpec(memory_space=pl.ANY), pl.BlockSpec(memory_space=pl.ANY)]
        args += [c_state, n_state]
        aliases = {len(args) - 2: 1, len(args) - 1: 2}
        out_specs += [_state_spec(l, nb, DH, DH), _state_spec(l, nb, DH),
                      pl.BlockSpec((nb, 8, 128), lambda b: (b, 0, 0))]
        out_shape += [jax.ShapeDtypeStruct(c_state.shape, F32), jax.ShapeDtypeStruct(n_state.shape, F32),
                      jax.ShapeDtypeStruct((B, 8, 128), F32)]
    nchunk = L // CHUNK
    return pl.pallas_call(
        functools.partial(_ml_body, L=L, latent=latent, nb=nb),
        grid=(B // nb,),
        in_specs=in_specs, out_specs=out_specs, out_shape=out_shape,
        input_output_aliases=aliases,
        scratch_shapes=[pltpu.VMEM((nb * L, HWP), F32),
                        pltpu.VMEM((8, nb * L, 128), F32), pltpu.VMEM((8, nb * L, 128), F32),
                        pltpu.VMEM((8, nb * L, 128), F32),
                        pltpu.VMEM((nb * nchunk * 16, 128), F32), pltpu.VMEM((nb * nchunk * 16, 128), F32),
                        pltpu.VMEM((nb * nchunk * 8, 128), F32), pltpu.VMEM((nb * nchunk * 8, 128), F32),
                        pltpu.VMEM((nb, 2, HEADS, DHP, 2 * DHP), F32)],
        compiler_params=_params(),
        name="mlstm_%d" % L,
    )(*args)


def _shift_rows(x, seq_len, n_rows):
    pos = lax.broadcasted_iota(jnp.int32, (n_rows, 1), 0) & (seq_len - 1)
    prev = jnp.where(pos == 0, 0.0, pltpu.roll(x, 1, 0))
    nxt = jnp.where(pos == seq_len - 1, 0.0, pltpu.roll(x, n_rows - 1, 0))
    return prev, nxt


def _hy_body(p_ref, cw_ref, cb_ref, f_ref, g_ref, ps_ref, qs_ref, p2s_ref, o_ref, fh_ref, gh_ref, *, L):
    @pl.when(pl.program_id(0) == 0)
    def _():
        fh_ref[...] = f_ref[...].astype(MXU_DT)
        gh_ref[...] = g_ref[...].astype(MXU_DT)

    def conv(z, o):
        spec = _dot(fh_ref[...], z)
        xr = spec[:L]
        xi = spec[L:]
        p = ps_ref[o]
        q = qs_ref[o]
        yr = xr * p - xi * q
        yi = xr * q + xi * p2s_ref[o]
        return _dot(gh_ref[...], jnp.concatenate([yr, yi], axis=0))

    for i in range(p_ref.shape[0] // L):
        rows = slice(i * L, (i + 1) * L)
        x = p_ref[rows, :]
        prev, nxt = _shift_rows(x, L, L)
        hy = prev * cw_ref[0:1, :] + x * cw_ref[1:2, :] + nxt * cw_ref[2:3, :] + cb_ref[...]
        hv = hy[:, 0:HY_W]
        hx1 = hy[:, HY_W:2 * HY_W]
        hx2 = hy[:, 2 * HY_W:3 * HY_W]
        z = hx1 * conv(hv, 0)
        o_ref[rows, :] = (hx2 * conv(z, 1)).astype(o_ref.dtype)


def _hyena(ph, cw, cb, spectra, l, B, L):
    fwd_table, inv_table = _dft_tables(L)
    ps, qs, p2s = spectra
    const = lambda shape: _resident(shape, lambda b: (0,) * len(shape))
    spec_blk = _resident((None, 2, L, HY_W), lambda b: (l, 0, 0, 0))
    nb = min(B, max(1, 2048 // L))
    return pl.pallas_call(
        functools.partial(_hy_body, L=L),
        grid=(B // nb,),
        in_specs=[pl.BlockSpec((nb * L, 3 * HY_W), lambda b: (b, 0)),
                  pl.BlockSpec((None, 8, 3 * HY_W), lambda b: (l, 0, 0)),
                  pl.BlockSpec((None, 1, 3 * HY_W), lambda b: (l, 0, 0)),
                  const((2 * L, L)), const((L, 2 * L)),
                  spec_blk, spec_blk, spec_blk],
        out_specs=pl.BlockSpec((nb * L, HY_W), lambda b: (b, 0)),
        out_shape=jax.ShapeDtypeStruct((B * L, HY_W), MXU_DT),
        scratch_shapes=[pltpu.VMEM((2 * L, L), MXU_DT), pltpu.VMEM((L, 2 * L), MXU_DT)],
        compiler_params=_params(),
        name="hyena_%d" % L,
    )(ph, cw, cb, fwd_table, inv_table, ps, qs, p2s)


def _out_body(mr_ref, mh_ref, mm_ref, wr_ref, wh_ref, wm_ref, x_ref, g1_ref, sc2_ref, sh2_ref,
              gpost_ref, gpre_ref, x1_ref, h2_ref):
    for r in range(x_ref.shape[0] // ROW_SUB):
        rs = slice(r * ROW_SUB, (r + 1) * ROW_SUB)
        mix = (jnp.dot(mr_ref[rs, :], wr_ref[...], preferred_element_type=F32)
               + jnp.dot(mh_ref[rs, :], wh_ref[...], preferred_element_type=F32)
               + jnp.dot(mm_ref[rs, :], wm_ref[...], preferred_element_type=F32))
        x1 = x_ref[rs, :] + g1_ref[...] * (_rms(mix, D_MODEL) * gpost_ref[...])
        x1_ref[rs, :] = x1
        h2 = _rms(x1, D_MODEL) * gpre_ref[...] * (1.0 + sc2_ref[...]) + sh2_ref[...]
        h2_ref[rs, :] = h2.astype(h2_ref.dtype)


def _ffn_body(mr_ref, mh_ref, mm_ref, wr_ref, wh_ref, wm_ref, x_ref, g1_ref, sc2_ref, sh2_ref, gmix_ref, gpre_ref,
              wa_ref, wb_ref, cw_ref, cb_ref, wd_ref, g2_ref, gpost_ref, o_ref, x1_ref, h_ref, *, seq_len, tm):
    j = pl.program_id(1)

    @pl.when(j == 0)
    def _():
        _out_body(mr_ref, mh_ref, mm_ref, wr_ref, wh_ref, wm_ref, x_ref, g1_ref, sc2_ref, sh2_ref,
                  gmix_ref, gpre_ref, x1_ref, h_ref)
        o_ref[...] = jnp.zeros_like(o_ref)

    h = h_ref[...]
    part = None
    for s in range(wa_ref.shape[1] // FFN_SUB):
        cs = slice(s * FFN_SUB, (s + 1) * FFN_SUB)
        a = jnp.dot(h, wa_ref[:, cs].astype(MXU_DT), preferred_element_type=F32)
        b = jnp.dot(h, wb_ref[:, cs].astype(MXU_DT), preferred_element_type=F32)
        prev, nxt = _shift_rows(a, seq_len, tm)
        a = prev * cw_ref[0:1, cs] + a * cw_ref[1:2, cs] + nxt * cw_ref[2:3, cs] + cb_ref[:, cs]
        gelu = 0.5 * a * (1.0 + jnp.tanh(math.sqrt(2.0 / math.pi) * (a + 0.044715 * (a * a * a))))
        p = jnp.dot((gelu * b).astype(MXU_DT), wd_ref[cs, :].astype(MXU_DT), preferred_element_type=F32)
        part = p if part is None else part + p
    o_ref[...] += part

    @pl.when(j == pl.num_programs(1) - 1)
    def _():
        o_ref[...] = x1_ref[...] + g2_ref[...] * (_rms(o_ref[...], D_MODEL) * gpost_ref[...])


def _mix_ffn(mr, mh, mm, wr, wh, wm, x, mod, gmix, gpre, w_up, cw, cb, w_down, gpost, l, row_of_tile,
             seq_len, tm, tf):
    n = x.shape[0]
    nf = D_FF // tf
    tile = lambda w: pl.BlockSpec((tm, w), lambda i, j: (i, 0))
    layer = lambda r, c: pl.BlockSpec((None, r, c), lambda i, j: (l, 0, 0))
    mod_row = lambda which: pl.BlockSpec((None, None, None, 1, D_MODEL),
                                         lambda i, j: (l, which, row_of_tile(i), 0, 0))
    return pl.pallas_call(
        functools.partial(_ffn_body, seq_len=seq_len, tm=tm),
        grid=(n // tm, nf),
        in_specs=[tile(HWP), tile(HY_W), tile(HWP),
                  layer(HWP, D_MODEL), layer(HY_W, D_MODEL), layer(HWP, D_MODEL),
                  tile(D_MODEL), mod_row(2), mod_row(4), mod_row(3),
                  layer(1, D_MODEL), layer(1, D_MODEL),
                  pl.BlockSpec((None, D_MODEL, tf), lambda i, j: (l, 0, j)),
                  pl.BlockSpec((None, D_MODEL, tf), lambda i, j: (l, 0, j + nf)),
                  pl.BlockSpec((None, 8, tf), lambda i, j: (l, 0, j)),
                  pl.BlockSpec((None, 1, tf), lambda i, j: (l, 0, j)),
                  pl.BlockSpec((None, tf, D_MODEL), lambda i, j: (l, j, 0)),
                  mod_row(5), layer(1, D_MODEL)],
        out_specs=pl.BlockSpec((tm, D_MODEL), lambda i, j: (i, 0)),
        out_shape=jax.ShapeDtypeStruct((n, D_MODEL), F32),
        scratch_shapes=[pltpu.VMEM((tm, D_MODEL), F32), pltpu.VMEM((tm, D_MODEL), MXU_DT)],
        compiler_params=_params(),
        name="mix_ffn",
    )(mr, mh, mm, wr, wh, wm, x, mod, mod, mod, gmix, gpre, w_up, w_up, cw, cb, w_down, mod, gpost)


def _pad_rows8(w):
    return jnp.concatenate([w, jnp.zeros((w.shape[0], 8 - w.shape[1], w.shape[2]), w.dtype)], axis=1)


def kernel(x_prompt, x_sample, c, state_ret, state_mlstm_c, state_mlstm_n, state_mlstm_m, c_ctx,
           norm_mix_pre, norm_mix_post, norm_ffn_pre, norm_ffn_post, w_mod, b_mod, w_in, w_out,
           ret_decay_logit, ret_norm_g, hy_conv_w, hy_conv_b, hy_f_w1, hy_f_b1, hy_f_w2, hy_f_b2,
           hy_f_w3, hy_f_b3, hy_sin_freq, hy_bias, ml_gate_bias, ml_norm_g,
           w_up, ffn_conv_w, ffn_conv_b, w_down):
    BP, LP, _ = x_prompt.shape
    BS, LS, _ = x_sample.shape

    w_in_p = _prep_w_in(jnp.swapaxes(w_in, 1, 2))
    w_out_r = _pad_heads(lax.slice_in_dim(w_out, 0, HW, axis=1), axis=1).astype(MXU_DT)
    w_out_h = lax.slice_in_dim(w_out, HW, HW + HY_W, axis=1).astype(MXU_DT)
    w_out_m = _pad_heads(lax.slice_in_dim(w_out, HW + HY_W, 2 * HW + HY_W, axis=1), axis=1).astype(MXU_DT)
    ret_gain = _pad_heads(ret_norm_g).reshape(DEPTH, 1, HWP)
    ml_gain = _pad_heads(ml_norm_g).reshape(DEPTH, 1, HWP)
    dl = jnp.broadcast_to(ret_decay_logit.reshape(DEPTH, 8, 1), (DEPTH, 8, 128))
    gate_bias = jnp.concatenate([ml_gate_bias.reshape(DEPTH, 1, 16), jnp.zeros((DEPTH, 1, 112), F32)], axis=2)
    hy_cw = _pad_rows8(hy_conv_w)
    hy_cb = hy_conv_b.reshape(DEPTH, 1, 3 * HY_W)
    ffn_cw = _pad_rows8(ffn_conv_w)
    ffn_cb = ffn_conv_b.reshape(DEPTH, 1, D_FF)
    g_mix_pre = norm_mix_pre.reshape(DEPTH, 1, D_MODEL)
    g_mix_post = norm_mix_post.reshape(DEPTH, 1, D_MODEL)
    g_ffn_pre = norm_ffn_pre.reshape(DEPTH, 1, D_MODEL)
    g_ffn_post = norm_ffn_post.reshape(DEPTH, 1, D_MODEL)
    pad2 = lambda a, r, cdim: jnp.pad(a, ((0, 0), (0, r - a.shape[1]), (0, cdim - a.shape[2])))
    fw1 = pad2(hy_f_w1, 128, 128)
    fb1 = pad2(hy_f_b1.reshape(DEPTH, 1, -1), 1, 128)
    fw2 = pad2(hy_f_w2, 128, 128)
    fb2 = pad2(hy_f_b2.reshape(DEPTH, 1, -1), 1, 128)
    fw3 = pad2(hy_f_w3, 128, 4 * HY_W)
    fb3 = hy_f_b3.reshape(DEPTH, 1, 4 * HY_W)
    ffr = pad2(hy_sin_freq.reshape(DEPTH, 1, -1), 1, 128)

    m0 = jnp.broadcast_to(state_mlstm_m.reshape(BS, DEPTH, 8, 1), (BS, DEPTH, 8, 128))
    new_ret = jnp.zeros((BP, DEPTH, 2, HEADS, DH, DH), F32)
    new_c = jnp.zeros((BP, DEPTH, 2, HEADS, DH, DH), F32)
    new_n = jnp.zeros((BP, DEPTH, 2, HEADS, DH), F32)

    cvec8 = jnp.concatenate([c_ctx.reshape(1, D_MODEL), c, jnp.zeros((8 - 1 - BS, D_MODEL), F32)], axis=0)
    mod = _mod_all(cvec8, w_mod, b_mod)
    mod = mod.reshape(DEPTH, 8, 6, 1, D_MODEL).transpose(0, 2, 1, 3, 4)
    spectra_p = _filter_spectra(LP, fw1, fb1, fw2, fb2, fw3, fb3, ffr, hy_bias)
    spectra_s = _filter_spectra(LS, fw1, fb1, fw2, fb2, fw3, fb3, ffr, hy_bias)

    xp = x_prompt.reshape(BP * LP, D_MODEL)
    xs = x_sample.reshape(BS * LS, D_MODEL)
    tm = 512
    ffn_tm = 1024
    ffn_tf = 512
    row_ctx = lambda i: 0
    row_lat_tm = lambda i: 1 + i // (LS // tm)
    row_lat_ffn = lambda i: 1 + i // (LS // ffn_tm)
    new_m = []

    def mix_and_ffn(x, pr_ph, mr, mm, l, B, L, row_tm, row_ffn, spectra):
        mh = _hyena(pr_ph, hy_cw, hy_cb, spectra, l, B, L)
        return _mix_ffn(mr, mh, mm, w_out_r, w_out_h, w_out_m, x, mod, g_mix_post, g_ffn_pre,
                        w_up, ffn_cw, ffn_cb, w_down, g_ffn_post, l, row_ffn, L, ffn_tm, ffn_tf)

    for l in range(DEPTH):
        pr, ph, pm, pg = _in_proj(xp, mod, g_mix_pre, w_in_p, l, row_ctx, tm)
        mr, new_ret = _retention(pr, dl, ret_gain, l, BP, LP, False, new_ret)
        mm, new_c, new_n, m_fin = _mlstm(pm, pg, gate_bias, ml_gain, l, BP, LP, False, new_c, new_n)
        new_m.append(m_fin)
        xp = mix_and_ffn(xp, ph, mr, mm, l, BP, LP, row_ctx, row_ctx, spectra_p)
        pr, ph, pm, pg = _in_proj(xs, mod, g_mix_pre, w_in_p, l, row_lat_tm, tm)
        (mr,) = _retention(pr, dl, ret_gain, l, BS, LS, True, state_ret)
        (mm,) = _mlstm(pm, pg, gate_bias, ml_gain, l, BS, LS, True, state_mlstm_c, state_mlstm_n, m0)
        xs = mix_and_ffn(xs, ph, mr, mm, l, BS, LS, row_lat_tm, row_lat_ffn, spectra_s)

    out_m = jnp.stack(new_m, axis=1)[..., 0].reshape(BP, DEPTH, 2, HEADS)
    return (xp.reshape(BP, LP, D_MODEL), xs.reshape(BS, LS, D_MODEL), new_ret, new_c, new_n, out_m)
```

```python
import functools
import math

import numpy as np
import jax
import jax.numpy as jnp
from jax import lax
from jax.experimental import pallas as pl
from jax.experimental.pallas import tpu as pltpu

D_MODEL = 1024
DEPTH = 4
GRID_W = 64
HEADS = 4
DH = 96
DHP = 128
HW = HEADS * DH
HWP = HEADS * DHP
HY_W = 256
D_FF = 4 * D_MODEL
CHUNK = 128
N_BANDS = 16
FEAT_W = 1 + 2 * N_BANDS
HY_SHIFT = 0.05
HY_TARGET = 1e-2
HY_SHORT_DECAY_PCT = 0.3
HY_LONG_DECAY_PCT = 1.5
ROPE_BASE = 10000.0
EPS = 1e-6

MXU_DT = jnp.bfloat16
F32 = jnp.float32
VMEM_LIMIT = 56 * 1024 * 1024


def _dot(a, b):
    return jnp.dot(a.astype(MXU_DT), b.astype(MXU_DT), preferred_element_type=F32)


def _dot_nt(a, b):
    return lax.dot_general(a.astype(MXU_DT), b.astype(MXU_DT), (((1,), (1,)), ((), ())),
                           preferred_element_type=F32)


def _dot_tn(a, b):
    return lax.dot_general(a.astype(MXU_DT), b.astype(MXU_DT), (((0,), (0,)), ((), ())),
                           preferred_element_type=F32)


def _split2(x):
    hi = x.astype(MXU_DT)
    lo = (x - hi.astype(F32)).astype(MXU_DT)
    return hi, lo


def _dot_split(a_hi, a_lo, b):
    b_hi, b_lo = _split2(b)
    return (jnp.dot(a_hi, b_hi, preferred_element_type=F32)
            + jnp.dot(a_hi, b_lo, preferred_element_type=F32)
            + jnp.dot(a_lo, b_hi, preferred_element_type=F32))


def _dot3(a, b):
    a_hi, a_lo = _split2(a)
    return _dot_split(a_hi, a_lo, b)


def _dot_exact_lhs(t, x):
    x1 = x.astype(MXU_DT)
    r1 = x - x1.astype(F32)
    x2 = r1.astype(MXU_DT)
    x3 = (r1 - x2.astype(F32)).astype(MXU_DT)
    return (jnp.dot(t, x1, preferred_element_type=F32) + jnp.dot(t, x2, preferred_element_type=F32)
            + jnp.dot(t, x3, preferred_element_type=F32))


def _rms(x, n):
    return x * lax.rsqrt(jnp.sum(x * x, axis=-1, keepdims=True) * (1.0 / n) + EPS)


def _chunk_rows(c, size, base=0):
    if isinstance(c, int):
        return slice(base + c * size, base + (c + 1) * size)
    return pl.ds(pl.multiple_of(base + c * size, size), size)


def _log_sigmoid(x):
    return jnp.minimum(x, 0.0) - jnp.log1p(jnp.exp(-jnp.abs(x)))


def _resident(shape, index_map):
    return pl.BlockSpec(shape, index_map, pipeline_mode=pl.Buffered(1))


def _params(**kw):
    return pltpu.CompilerParams(vmem_limit_bytes=VMEM_LIMIT, **kw)


@functools.lru_cache(maxsize=None)
def _dft_tables(L):
    f = np.arange(L, dtype=np.int64)[:, None]
    s = np.arange(L, dtype=np.int64)[None, :]
    ang = np.pi * ((f * s) % (2 * L)).astype(np.float64) / L
    fr = np.cos(ang)
    fi = -np.sin(ang)
    fi[0, :] = np.where(np.arange(L) % 2 == 0, 1.0, -1.0)
    fwd = np.concatenate([fr, fi], axis=0)
    gr = np.cos(ang.T) * (2.0 / (2 * L))
    gr[:, 0] = 1.0 / (2 * L)
    gi = -np.sin(ang.T) * (2.0 / (2 * L))
    gi[:, 0] = np.where(np.arange(L) % 2 == 0, 1.0, -1.0) / (2 * L)
    inv = np.concatenate([gr, gi], axis=1)
    return fwd.astype(np.float32), inv.astype(np.float32)


@functools.lru_cache(maxsize=None)
def _filter_tables(L):
    tn = np.arange(L, dtype=np.float64) / L
    bands = np.linspace(1e-4, N_BANDS - 1, N_BANDS)
    ang = 2.0 * math.pi * tn[:, None] * bands[None, :]
    feat = np.zeros((L, 128), np.float32)
    feat[:, 0] = tn
    feat[:, 1:1 + N_BANDS] = np.cos(ang)
    feat[:, 1 + N_BANDS:FEAT_W] = np.sin(ang)
    deltas = np.abs(np.linspace(math.log(HY_TARGET) / HY_LONG_DECAY_PCT,
                                math.log(HY_TARGET) / HY_SHORT_DECAY_PCT, HY_W))
    window = (np.exp(-tn[:, None] * deltas[None, :]) + HY_SHIFT).astype(np.float32)
    return feat, window


@functools.lru_cache(maxsize=None)
def _rope_tables(L):
    rows = L // GRID_W
    row = np.repeat(np.arange(rows, dtype=np.float64), GRID_W)
    col = np.tile(np.arange(GRID_W, dtype=np.float64), rows)
    half = DH // 2
    n_freq = half // 2
    freqs = ROPE_BASE ** (-np.arange(n_freq, dtype=np.float64) / n_freq)
    ang = np.concatenate([row[:, None] * freqs, col[:, None] * freqs], axis=-1)
    cos = np.zeros((L, DHP), np.float32)
    sin = np.zeros((L, DHP), np.float32)
    cos[:, :half] = np.cos(ang)
    cos[:, 64:64 + half] = np.cos(ang)
    sin[:, :half] = -np.sin(ang)
    sin[:, 64:64 + half] = np.sin(ang)
    return cos, sin


@functools.lru_cache(maxsize=None)
def _tri_tables():
    i = np.arange(CHUNK)
    lower = (i[:, None] >= i[None, :]).astype(np.float32)
    upper = (i[:, None] <= i[None, :]).astype(np.float32)
    return lower, upper


def _pad_heads(w, axis=-1, rope=False):
    axis = axis % w.ndim
    parts = []

    def zeros(n):
        shp = list(w.shape)
        shp[axis] = n
        return jnp.zeros(shp, w.dtype)

    for h in range(HEADS):
        blk = lax.slice_in_dim(w, h * DH, (h + 1) * DH, axis=axis)
        if rope:
            half = DH // 2
            parts += [lax.slice_in_dim(blk, 0, half, axis=axis), zeros(64 - half),
                      lax.slice_in_dim(blk, half, DH, axis=axis), zeros(64 - half)]
        else:
            parts += [blk, zeros(DHP - DH)]
    return jnp.concatenate(parts, axis=axis)


def _unpad_heads_axis(x, axis, rope=False):
    if rope:
        half = DH // 2
        return jnp.concatenate([lax.slice_in_dim(x, 0, half, axis=axis),
                                lax.slice_in_dim(x, 64, 64 + half, axis=axis)], axis=axis)
    return lax.slice_in_dim(x, 0, DH, axis=axis)


def _mod_body(c_ref, w_ref, b_ref, o_ref):
    c = c_ref[...]
    s = c * (1.0 / (1.0 + jnp.exp(-c)))
    o_ref[...] = _dot(s, w_ref[...]) + b_ref[...]


def _mod_all(cvec8, w_mod, b_mod):
    tn = 1536
    nj = 6 * D_MODEL // tn
    return pl.pallas_call(
        _mod_body,
        grid=(DEPTH, nj),
        in_specs=[pl.BlockSpec((8, D_MODEL), lambda l, j: (0, 0)),
                  pl.BlockSpec((None, D_MODEL, tn), lambda l, j: (l, 0, j)),
                  pl.BlockSpec((None, 1, tn), lambda l, j: (l, 0, j))],
        out_specs=pl.BlockSpec((None, 8, tn), lambda l, j: (l, 0, j)),
        out_shape=jax.ShapeDtypeStruct((DEPTH, 8, 6 * D_MODEL), F32),
        compiler_params=_params(),
        name="mod_all",
    )(cvec8, w_mod, b_mod.reshape(DEPTH, 1, 6 * D_MODEL))


def _filt_body(feat_ref, win_ref, f_ref, w1_ref, b1_ref, w2_ref, b2_ref, w3_ref, b3_ref,
               fr_ref, hb_ref, p_ref, q_ref, p2_ref, *, L):
    fr = fr_ref[...]
    h = jnp.sin(fr * (_dot3(feat_ref[...], w1_ref[...]) + b1_ref[...]))
    h = jnp.sin(fr * (_dot3(h, w2_ref[...]) + b2_ref[...]))
    filt = _dot3(h, w3_ref[...]) + b3_ref[...]
    win = win_ref[...]
    row = lax.broadcasted_iota(jnp.int32, (L, 1), 0)
    taps = filt * jnp.concatenate([win, win, win, win], axis=1)
    lane = lax.broadcasted_iota(jnp.int32, (L, 4 * HY_W), 1)
    taps = jnp.where((row == 0) & (lane >= 2 * HY_W), 0.0, taps)
    spec = _dot(f_ref[...], taps)
    for o in range(2):
        a = spec[:, o * HY_W:(o + 1) * HY_W]
        b = spec[:, (2 + o) * HY_W:(3 + o) * HY_W]
        kr = a[:L] + b[:L]
        ki = a[L:] - b[L:]
        nyq = a[L:L + 1] + b[L:L + 1]
        bias = hb_ref[o:o + 1, :]
        p_ref[o] = kr + bias
        q_ref[o] = jnp.where(row == 0, 0.0, ki)
        p2_ref[o] = jnp.where(row == 0, nyq, kr) + bias


def _filter_spectra(L, fw1, fb1, fw2, fb2, fw3, fb3, ffr, hy_bias):
    feat, window = _filter_tables(L)
    fwd_table, _ = _dft_tables(L)
    const = lambda shape: _resident(shape, lambda l: (0,) * len(shape))
    per_layer = lambda shape: pl.BlockSpec((None,) + shape, lambda l: (l,) + (0,) * len(shape))
    out_spec = per_layer((2, L, HY_W))
    out_shape = jax.ShapeDtypeStruct((DEPTH, 2, L, HY_W), F32)
    return pl.pallas_call(
        functools.partial(_filt_body, L=L),
        grid=(DEPTH,),
        in_specs=[const((L, 128)), const((L, HY_W)), const((2 * L, L)),
                  per_layer((128, 128)), per_layer((1, 128)), per_layer((128, 128)), per_layer((1, 128)),
                  per_layer((128, 4 * HY_W)), per_layer((1, 4 * HY_W)), per_layer((1, 128)),
                  per_layer((2, HY_W))],
        out_specs=[out_spec, out_spec, out_spec],
        out_shape=[out_shape, out_shape, out_shape],
        compiler_params=_params(),
        name="hyena_filter_%d" % L,
    )(feat, window, fwd_table, fw1, fb1, fw2, fb2, fw3, fb3, ffr, hy_bias)


W_R0, W_H0, W_M0, W_G0, W_END = 0, 4 * HWP, 4 * HWP + 3 * HY_W, 8 * HWP + 3 * HY_W, 8 * HWP + 3 * HY_W + 128
IN_W = 8 * HW + 3 * HY_W + 16
FFN_SUB = 256
ROW_SUB = 256
SCAN_UNROLL = 4


def _prep_in_body(wt_ref, o_ref):
    kb = wt_ref.shape[1]
    half = DH // 2

    def zeros(n):
        return jnp.zeros((n, kb), F32)

    def head_tile(src, rope):
        if rope:
            return jnp.concatenate([wt_ref[src:src + half, :], zeros(64 - half),
                                    wt_ref[src + half:src + DH, :], zeros(64 - half)], axis=0)
        return jnp.concatenate([wt_ref[src:src + DH, :], zeros(DHP - DH)], axis=0)

    tiles = []
    for sec in range(4):
        tiles += [head_tile(sec * HW + h * DH, sec < 2) for h in range(HEADS)]
    tiles += [wt_ref[4 * HW + i * 128:4 * HW + (i + 1) * 128, :] for i in range(3 * HY_W // 128)]
    for sec in range(4):
        tiles += [head_tile(4 * HW + 3 * HY_W + sec * HW + h * DH, False) for h in range(HEADS)]
    g0 = 8 * HW + 3 * HY_W
    tiles.append(jnp.concatenate([wt_ref[g0:g0 + 16, :], zeros(112)], axis=0))
    for t, tile in enumerate(tiles):
        o_ref[:, t * 128:(t + 1) * 128] = tile.T.astype(o_ref.dtype)


def _prep_w_in(w_in_t):
    kb = 256
    return pl.pallas_call(
        _prep_in_body,
        grid=(DEPTH, D_MODEL // kb),
        in_specs=[pl.BlockSpec((None, IN_W, kb), lambda l, i: (l, 0, i))],
        out_specs=pl.BlockSpec((None, kb, W_END), lambda l, i: (l, i, 0)),
        out_shape=jax.ShapeDtypeStruct((DEPTH, D_MODEL, W_END), MXU_DT),
        compiler_params=_params(),
        name="prep_w_in",
    )(w_in_t)


def _in_body(xa_ref, xb_ref, sc_ref, sh_ref, g_ref, w_ref, pr_ref, ph_ref, pm_ref, pg_ref, *, tiles_a):
    x = jnp.where(pl.program_id(0) < tiles_a, xa_ref[...], xb_ref[...])
    h = _rms(x, D_MODEL) * g_ref[...]
    h = (h * (1.0 + sc_ref[...]) + sh_ref[...]).astype(MXU_DT)
    pr_ref[...] = jnp.dot(h, w_ref[:, W_R0:W_H0], preferred_element_type=F32)
    ph_ref[...] = jnp.dot(h, w_ref[:, W_H0:W_M0], preferred_element_type=F32)
    pm_ref[...] = jnp.dot(h, w_ref[:, W_M0:W_G0], preferred_element_type=F32)
    pg_ref[...] = jnp.dot(h, w_ref[:, W_G0:W_END], preferred_element_type=F32)


def _mod_spec(l, which, row_of_tile):
    return pl.BlockSpec((None, None, None, 1, D_MODEL), lambda i: (l, which, row_of_tile(i), 0, 0))


def _in_proj(xa, xb, mod, gain, w_in, l, row_of_tile, tm):
    tiles_a = xa.shape[0] // tm
    n = xa.shape[0] + xb.shape[0]
    shapes = [4 * HWP, 3 * HY_W, 4 * HWP, 128]
    return pl.pallas_call(
        functools.partial(_in_body, tiles_a=tiles_a),
        grid=(n // tm,),
        in_specs=[pl.BlockSpec((tm, D_MODEL), lambda i: (jnp.minimum(i, tiles_a - 1), 0)),
                  pl.BlockSpec((tm, D_MODEL), lambda i: (jnp.maximum(i - tiles_a, 0), 0)),
                  _mod_spec(l, 1, row_of_tile), _mod_spec(l, 0, row_of_tile),
                  pl.BlockSpec((None, 1, D_MODEL), lambda i: (l, 0, 0)),
                  _resident((None, D_MODEL, W_END), lambda i: (l, 0, 0))],
        out_specs=[pl.BlockSpec((tm, w), lambda i: (i, 0)) for w in shapes],
        out_shape=[jax.ShapeDtypeStruct((n, w), F32) for w in shapes],
        compiler_params=_params(),
        name="in_proj",
    )(xa, xb, mod, mod, gain, w_in)


def _ret_body(*refs, L, latent, nb):
    if latent:
        (q_ref, k_ref, v_ref, gate_ref, dl_ref, gain_ref, s0_ref, cos_ref, sin_ref,
         o_ref, qs, ks, acc, sst) = refs
    else:
        (q_ref, k_ref, v_ref, gate_ref, dl_ref, gain_ref, _, o_ref, sfin_ref, qs, ks, acc, sst) = refs
    half = DH // 2
    C = CHUNK
    nchunk = L // C
    lg = _log_sigmoid(dl_ref[...])
    rel = (lax.broadcasted_iota(jnp.int32, (C, C), 0) - lax.broadcasted_iota(jnp.int32, (C, C), 1)).astype(F32)
    ri = lax.broadcasted_iota(jnp.int32, (C, 1), 0).astype(F32)

    for bi in range(nb):
        seq = slice(bi * L, (bi + 1) * L)
        for h in range(HEADS):
            cols = slice(h * DHP, (h + 1) * DHP)
            q = q_ref[seq, cols]
            k = k_ref[seq, cols]
            if latent:
                cos = cos_ref[...]
                sin = sin_ref[...]
                q = q * cos + pltpu.roll(q, 64, 1) * sin
                k = k * cos + pltpu.roll(k, 64, 1) * sin
            qs[seq, cols] = q
            ks[seq, cols] = k * (DH ** -0.5)
            for d in range(2):
                sst[bi, d, h] = jnp.zeros((DHP, DHP), F32)
                if latent:
                    sst[bi, d, h, 0:half, 0:DH] = s0_ref[bi, d, h, 0:half, :]
                    sst[bi, d, h, 64:64 + half, 0:DH] = s0_ref[bi, d, h, half:DH, :]

    acc[...] = jnp.zeros_like(acc)
    consts = []
    for h in range(HEADS):
        lgf = lg[h:h + 1, :]
        lgb = lg[HEADS + h:HEADS + h + 1, :]
        lgf1 = lgf[:, 0:1]
        lgb1 = lgb[:, 0:1]
        mask = (jnp.where(rel >= 0, jnp.exp(rel * lgf), 0.0)
                + jnp.where(rel <= 0, jnp.exp(-rel * lgb), 0.0))
        consts.append(dict(
            mask=mask,
            qdec_f=jnp.exp((ri + 1.0) * lgf1), qdec_b=jnp.exp((C - ri) * lgb1),
            kdec_f=jnp.exp((C - 1.0 - ri) * lgf1), kdec_b=jnp.exp(ri * lgb1),
            cdec_f=jnp.exp(C * lgf1), cdec_b=jnp.exp(C * lgb1)))

    def step(j):
        for bi in range(nb):
            rows_f = _chunk_rows(j, C, bi * L)
            rows_b = _chunk_rows(nchunk - 1 - j, C, bi * L)
            for h in range(HEADS):
                cols = slice(h * DHP, (h + 1) * DHP)
                cn = consts[h]
                q = qs[rows_f, cols]
                k = ks[rows_f, cols]
                v = v_ref[rows_f, cols]
                s_f = sst[bi, 0, h]
                sc = _dot_nt(q, k) * cn["mask"]
                acc[rows_f, cols] += _dot(sc, v) + _dot(q, s_f) * cn["qdec_f"]
                sst[bi, 0, h] = s_f * cn["cdec_f"] + _dot_tn(k * cn["kdec_f"], v)
                q = qs[rows_b, cols]
                k = ks[rows_b, cols]
                v = v_ref[rows_b, cols]
                s_b = sst[bi, 1, h]
                acc[rows_b, cols] += _dot(q, s_b) * cn["qdec_b"]
                sst[bi, 1, h] = s_b * cn["cdec_b"] + _dot_tn(k * cn["kdec_b"], v)

    if nchunk <= 2:
        for j in range(nchunk):
            step(j)
    else:
        pl.loop(0, nchunk, unroll=SCAN_UNROLL)(step)

    if not latent:
        for bi in range(nb):
            for d in range(2):
                for h in range(HEADS):
                    sfin_ref[bi, d, h, 0:half, :] = sst[bi, d, h, 0:half, 0:DH]
                    sfin_ref[bi, d, h, half:DH, :] = sst[bi, d, h, 64:64 + half, 0:DH]

    for h in range(HEADS):
        cols = slice(h * DHP, (h + 1) * DHP)
        g = gate_ref[:, cols]
        y = _rms(acc[:, cols], DH) * gain_ref[:, cols] * (g * (1.0 / (1.0 + jnp.exp(-g))))
        o_ref[:, cols] = y.astype(o_ref.dtype)


def _state_spec(l, nb, *tail):
    shape = (nb, None, 2, HEADS) + tail
    return pl.BlockSpec(shape, lambda b: (b, l) + (0,) * (len(shape) - 2))


def _seqs_per_step(B, L):
    return min(B, max(1, 512 // L))


def _retention(pr, dl, gain, l, B, L, latent, state, row0):
    nb = _seqs_per_step(B, L)
    off = row0 // (nb * L)
    blk = lambda j: pl.BlockSpec((nb * L, HWP), lambda b, j=j: (b + off, j))
    in_specs = [blk(0), blk(1), blk(2), blk(3),
                pl.BlockSpec((None, 8, 128), lambda b: (l, 0, 0)),
                pl.BlockSpec((None, 1, HWP), lambda b: (l, 0, 0))]
    args = [pr, pr, pr, pr, dl, gain]
    out_specs = [pl.BlockSpec((nb * L, HWP), lambda b: (b, 0))]
    out_shape = [jax.ShapeDtypeStruct((B * L, HWP), MXU_DT)]
    aliases = {}
    if latent:
        cos, sin = _rope_tables(L)
        in_specs += [_state_spec(l, nb, DH, DH),
                     pl.BlockSpec((L, DHP), lambda b: (0, 0)), pl.BlockSpec((L, DHP), lambda b: (0, 0))]
        args += [state, cos, sin]
    else:
        in_specs.append(pl.BlockSpec(memory_space=pl.ANY))
        args.append(state)
        aliases = {len(args) - 1: 1}
        out_specs.append(_state_spec(l, nb, DH, DH))
        out_shape.append(jax.ShapeDtypeStruct(state.shape, F32))
    return pl.pallas_call(
        functools.partial(_ret_body, L=L, latent=latent, nb=nb),
        grid=(B // nb,),
        in_specs=in_specs, out_specs=out_specs, out_shape=out_shape,
        input_output_aliases=aliases,
        scratch_shapes=[pltpu.VMEM((nb * L, HWP), F32), pltpu.VMEM((nb * L, HWP), F32),
                        pltpu.VMEM((nb * L, HWP), F32), pltpu.VMEM((nb, 2, HEADS, DHP, DHP), F32)],
        compiler_params=_params(),
        name="retention_%d" % L,
    )(*args)


def _ml_body(*refs, L, latent, nb):
    if latent:
        (q_ref, k_ref, v_ref, og_ref, gates_ref, gb_ref, gain_ref, tl_ref, tu_ref, c0_ref, n0_ref, m0_ref,
         o_ref, acc, bb_s, ub_s, pm_s, brow_s, ibrow_s, mc_s, mx_s, cn_s) = refs
    else:
        (q_ref, k_ref, v_ref, og_ref, gates_ref, gb_ref, gain_ref, tl_ref, tu_ref, _, _,
         o_ref, cfin_ref, nfin_ref, mfin_ref, acc, bb_s, ub_s, pm_s, brow_s, ibrow_s, mc_s, mx_s, cn_s) = refs
    C = CHUNK
    nchunk = L // C
    lane = lax.broadcasted_iota(jnp.int32, (C, DHP), 1)
    tri_r = lax.broadcasted_iota(jnp.int32, (C, C), 0)
    tri_c = lax.broadcasted_iota(jnp.int32, (C, C), 1)
    tl = tl_ref[...]
    tu = tu_ref[...]

    row_i = lax.broadcasted_iota(jnp.int32, (C, 128), 0)

    def cummax_rows(x, suffix):
        s = 1
        while s < C:
            if suffix:
                x = jnp.where(row_i < C - s, jnp.maximum(x, pltpu.roll(x, C - s, 0)), x)
            else:
                x = jnp.where(row_i >= s, jnp.maximum(x, pltpu.roll(x, s, 0)), x)
            s *= 2
        return x

    for c in range(nb * nchunk):
        rows = slice(c * C, (c + 1) * C)
        g = gates_ref[rows, :] + gb_ref[...]
        lf = _log_sigmoid(g)
        pre = _dot_exact_lhs(tl, lf)
        suf = _dot_exact_lhs(tu, lf)
        bc = jnp.where(lane < 8, pre, suf)
        brow_s[c * 16:(c + 1) * 16, :] = bc.T[0:16, :]
        ibrow_s[c * 16:(c + 1) * 16, :] = g.T[0:16, :]
        for d in range(2):
            for h in range(HEADS):
                sr = d * HEADS + h
                b_b = jnp.broadcast_to(bc[:, d * 8 + 4 + h:d * 8 + 5 + h], (C, 128))
                u_b = jnp.broadcast_to(g[:, d * 8 + h:d * 8 + h + 1], (C, 128)) - b_b
                bb_s[sr, rows, :] = b_b
                ub_s[sr, rows, :] = u_b
                pm_s[sr, rows, :] = cummax_rows(u_b, suffix=(d == 1))

    for bi in range(nb):
        for d in range(2):
            for h in range(HEADS):
                sr = d * HEADS + h
                m = m0_ref[bi, sr:sr + 1, :] if latent else jnp.zeros((1, 128), F32)
                for j in range(nchunk):
                    c = bi * nchunk + (j if d == 0 else nchunk - 1 - j)
                    last = c * C + (C - 1 if d == 0 else 0)
                    mx = jnp.maximum(m, pm_s[sr, last:last + 1, :])
                    mc_s[c * 8 + sr:c * 8 + sr + 1, :] = m
                    mx_s[c * 8 + sr:c * 8 + sr + 1, :] = mx
                    m = bb_s[sr, last:last + 1, :] + mx
                if not latent:
                    mfin_ref[bi, sr:sr + 1, :] = m

    acc[...] = jnp.zeros_like(acc)
    cn_s[...] = jnp.zeros_like(cn_s)
    if latent:
        for bi in range(nb):
            for d in range(2):
                for h in range(HEADS):
                    cn_s[bi, d, h, 0:DH, 0:DH] = c0_ref[bi, d, h]
                    n_row = jnp.concatenate([n0_ref[bi, d, h:h + 1, :], jnp.zeros((1, DHP - DH), F32)], axis=1)
                    cn_s[bi, d, h, :, DHP:] = jnp.broadcast_to(n_row, (DHP, DHP)).T
    ones_blk = jnp.ones((C, DHP), MXU_DT)

    def chain(bi, d, h, c):
        rows = _chunk_rows(c, C)
        brs = brow_s[_chunk_rows(c, 16), :]
        ibrs = ibrow_s[_chunk_rows(c, 16), :]
        mcs = mc_s[_chunk_rows(c, 8), :]
        mxs = mx_s[_chunk_rows(c, 8), :]
        causal = (tri_r >= tri_c) if d == 0 else (tri_r <= tri_c)
        cols = slice(h * DHP, (h + 1) * DHP)
        ic = d * 8 + h
        fc = d * 8 + 4 + h
        sr = d * HEADS + h
        m_c = mcs[sr:sr + 1, :]
        mx = mxs[sr:sr + 1, :]
        q = q_ref[rows, cols].astype(MXU_DT)
        k = k_ref[rows, cols] * (DH ** -0.5)
        v1 = jnp.concatenate([v_ref[rows, cols].astype(MXU_DT), ones_blk], axis=1)
        cn = cn_s[bi, d, h]
        urow = ibrs[ic:ic + 1, :] - brs[fc:fc + 1, :]
        mb = jnp.maximum(m_c, pm_s[sr, rows, :])
        w_intra = jnp.exp(jnp.where(causal, urow - mb, -jnp.inf))
        w_inter = jnp.exp(m_c - mb)
        s = _dot_nt(q, k) * w_intra
        tot = (jnp.dot(s.astype(MXU_DT), v1, preferred_element_type=F32)
               + jnp.concatenate([w_inter, w_inter], axis=1) * _dot(q, cn))
        den = tot[:, DHP:]
        floor = jnp.exp(-(bb_s[sr, rows, :] + mb))
        acc[rows, cols] += tot[:, :DHP] * (1.0 / jnp.maximum(jnp.abs(den), floor))
        kw = k * jnp.exp(ub_s[sr, rows, :] - mx)
        w_prev = jnp.exp(m_c - mx)
        cn_s[bi, d, h] = jnp.concatenate([w_prev, w_prev], axis=1) * cn + _dot_tn(kw, v1)

    def step(j):
        for bi in range(nb):
            for d in range(2):
                c = bi * nchunk + (j if d == 0 else nchunk - 1 - j)
                for h in range(HEADS):
                    chain(bi, d, h, c)

    if nchunk <= 2:
        for j in range(nchunk):
            step(j)
    else:
        pl.loop(0, nchunk, unroll=SCAN_UNROLL)(step)

    if not latent:
        for bi in range(nb):
            for d in range(2):
                for h in range(HEADS):
                    cfin_ref[bi, d, h] = cn_s[bi, d, h, 0:DH, 0:DH]
                    nfin_ref[bi, d, h:h + 1, :] = cn_s[bi, d, h, :, DHP:].T[0:1, 0:DH]

    for h in range(HEADS):
        cols = slice(h * DHP, (h + 1) * DHP)
        g = og_ref[:, cols]
        y = _rms(acc[:, cols], DH) * gain_ref[:, cols] * (1.0 / (1.0 + jnp.exp(-g)))
        o_ref[:, cols] = y.astype(o_ref.dtype)


def _mlstm(pm, pg, gb, gain, l, B, L, latent, c_state, n_state, row0, m0=None):
    tl, tu = _tri_tables()
    tl = tl.astype(MXU_DT)
    tu = tu.astype(MXU_DT)
    nb = _seqs_per_step(B, L)
    off = row0 // (nb * L)
    blk = lambda j: pl.BlockSpec((nb * L, HWP), lambda b, j=j: (b + off, j))
    in_specs = [blk(0), blk(1), blk(2), blk(3),
                pl.BlockSpec((nb * L, 128), lambda b: (b + off, 0)),
                pl.BlockSpec((None, 1, 128), lambda b: (l, 0, 0)),
                pl.BlockSpec((None, 1, HWP), lambda b: (l, 0, 0)),
                pl.BlockSpec((CHUNK, CHUNK), lambda b: (0, 0)),
                pl.BlockSpec((CHUNK, CHUNK), lambda b: (0, 0))]
    args = [pm, pm, pm, pm, pg, gb, gain, tl, tu]
    out_specs = [pl.BlockSpec((nb * L, HWP), lambda b: (b, 0))]
    out_shape = [jax.ShapeDtypeStruct((B * L, HWP), MXU_DT)]
    aliases = {}
    if latent:
        in_specs += [_state_spec(l, nb, DH, DH), _state_spec(l, nb, DH),
                     pl.BlockSpec((nb, None, 8, 128), lambda b: (b, l, 0, 0))]
        args += [c_state, n_state, m0]
    else:
        in_specs += [pl.BlockSpec(memory_space=pl.ANY), pl.BlockSpec(memory_space=pl.ANY)]
        args += [c_state, n_state]
        aliases = {len(args) - 2: 1, len(args) - 1: 2}
        out_specs += [_state_spec(l, nb, DH, DH), _state_spec(l, nb, DH),
                      pl.BlockSpec((nb, 8, 128), lambda b: (b, 0, 0))]
        out_shape += [jax.ShapeDtypeStruct(c_state.shape, F32), jax.ShapeDtypeStruct(n_state.shape, F32),
                      jax.ShapeDtypeStruct((B, 8, 128), F32)]
    nchunk = L // CHUNK
    return pl.pallas_call(
        functools.partial(_ml_body, L=L, latent=latent, nb=nb),
        grid=(B // nb,),
        in_specs=in_specs, out_specs=out_specs, out_shape=out_shape,
        input_output_aliases=aliases,
        scratch_shapes=[pltpu.VMEM((nb * L, HWP), F32),
                        pltpu.VMEM((8, nb * L, 128), F32), pltpu.VMEM((8, nb * L, 128), F32),
                        pltpu.VMEM((8, nb * L, 128), F32),
                        pltpu.VMEM((nb * nchunk * 16, 128), F32), pltpu.VMEM((nb * nchunk * 16, 128), F32),
                        pltpu.VMEM((nb * nchunk * 8, 128), F32), pltpu.VMEM((nb * nchunk * 8, 128), F32),
                        pltpu.VMEM((nb, 2, HEADS, DHP, 2 * DHP), F32)],
        compiler_params=_params(),
        name="mlstm_%d" % L,
    )(*args)


def _shift_rows(x, seq_len, n_rows):
    pos = lax.broadcasted_iota(jnp.int32, (n_rows, 1), 0) & (seq_len - 1)
    prev = jnp.where(pos == 0, 0.0, pltpu.roll(x, 1, 0))
    nxt = jnp.where(pos == seq_len - 1, 0.0, pltpu.roll(x, n_rows - 1, 0))
    return prev, nxt


def _hy_body(p_ref, cw_ref, cb_ref, f_ref, g_ref, ps_ref, qs_ref, p2s_ref, o_ref, fh_ref, gh_ref, *, L):
    @pl.when(pl.program_id(0) == 0)
    def _():
        fh_ref[...] = f_ref[...].astype(MXU_DT)
        gh_ref[...] = g_ref[...].astype(MXU_DT)

    def conv(z, o):
        spec = _dot(fh_ref[...], z)
        xr = spec[:L]
        xi = spec[L:]
        p = ps_ref[o]
        q = qs_ref[o]
        yr = xr * p - xi * q
        yi = xr * q + xi * p2s_ref[o]
        return _dot(gh_ref[...], jnp.concatenate([yr, yi], axis=0))

    for i in range(p_ref.shape[0] // L):
        rows = slice(i * L, (i + 1) * L)
        x = p_ref[rows, :]
        prev, nxt = _shift_rows(x, L, L)
        hy = prev * cw_ref[0:1, :] + x * cw_ref[1:2, :] + nxt * cw_ref[2:3, :] + cb_ref[...]
        hv = hy[:, 0:HY_W]
        hx1 = hy[:, HY_W:2 * HY_W]
        hx2 = hy[:, 2 * HY_W:3 * HY_W]
        z = hx1 * conv(hv, 0)
        o_ref[rows, :] = (hx2 * conv(z, 1)).astype(o_ref.dtype)


def _hyena(ph, cw, cb, spectra, l, B, L, row0):
    fwd_table, inv_table = _dft_tables(L)
    ps, qs, p2s = spectra
    const = lambda shape: _resident(shape, lambda b: (0,) * len(shape))
    spec_blk = _resident((None, 2, L, HY_W), lambda b: (l, 0, 0, 0))
    nb = min(B, max(1, 2048 // L))
    off = row0 // (nb * L)
    return pl.pallas_call(
        functools.partial(_hy_body, L=L),
        grid=(B // nb,),
        in_specs=[pl.BlockSpec((nb * L, 3 * HY_W), lambda b: (b + off, 0)),
                  pl.BlockSpec((None, 8, 3 * HY_W), lambda b: (l, 0, 0)),
                  pl.BlockSpec((None, 1, 3 * HY_W), lambda b: (l, 0, 0)),
                  const((2 * L, L)), const((L, 2 * L)),
                  spec_blk, spec_blk, spec_blk],
        out_specs=pl.BlockSpec((nb * L, HY_W), lambda b: (b, 0)),
        out_shape=jax.ShapeDtypeStruct((B * L, HY_W), MXU_DT),
        scratch_shapes=[pltpu.VMEM((2 * L, L), MXU_DT), pltpu.VMEM((L, 2 * L), MXU_DT)],
        compiler_params=_params(),
        name="hyena_%d" % L,
    )(ph, cw, cb, fwd_table, inv_table, ps, qs, p2s)


def _out_body(mr_ref, mh_ref, mm_ref, wr_ref, wh_ref, wm_ref, x_ref, g1_ref, sc2_ref, sh2_ref,
              gpost_ref, gpre_ref, x1_ref, h2_ref):
    for r in range(x_ref.shape[0] // ROW_SUB):
        rs = slice(r * ROW_SUB, (r + 1) * ROW_SUB)
        mix = (jnp.dot(mr_ref[rs, :], wr_ref[...], preferred_element_type=F32)
               + jnp.dot(mh_ref[rs, :], wh_ref[...], preferred_element_type=F32)
               + jnp.dot(mm_ref[rs, :], wm_ref[...], preferred_element_type=F32))
        x1 = x_ref[rs, :] + g1_ref[...] * (_rms(mix, D_MODEL) * gpost_ref[...])
        x1_ref[rs, :] = x1
        h2 = _rms(x1, D_MODEL) * gpre_ref[...] * (1.0 + sc2_ref[...]) + sh2_ref[...]
        h2_ref[rs, :] = h2.astype(h2_ref.dtype)


def _ffn_body(mr_ref, mh_ref, mm_ref, wr_ref, wh_ref, wm_ref, x_ref, g1_ref, sc2_ref, sh2_ref, gmix_ref, gpre_ref,
              wa_ref, wb_ref, cw_ref, cb_ref, wd_ref, g2_ref, gpost_ref, o_ref, x1_ref, h_ref, *, seq_len, tm):
    j = pl.program_id(1)

    @pl.when(j == 0)
    def _():
        _out_body(mr_ref, mh_ref, mm_ref, wr_ref, wh_ref, wm_ref, x_ref, g1_ref, sc2_ref, sh2_ref,
                  gmix_ref, gpre_ref, x1_ref, h_ref)
        o_ref[...] = jnp.zeros_like(o_ref)

    h = h_ref[...]
    part = None
    for s in range(wa_ref.shape[1] // FFN_SUB):
        cs = slice(s * FFN_SUB, (s + 1) * FFN_SUB)
        a = jnp.dot(h, wa_ref[:, cs].astype(MXU_DT), preferred_element_type=F32)
        b = jnp.dot(h, wb_ref[:, cs].astype(MXU_DT), preferred_element_type=F32)
        prev, nxt = _shift_rows(a, seq_len, tm)
        a = prev * cw_ref[0:1, cs] + a * cw_ref[1:2, cs] + nxt * cw_ref[2:3, cs] + cb_ref[:, cs]
        gelu = 0.5 * a * (1.0 + jnp.tanh(math.sqrt(2.0 / math.pi) * (a + 0.044715 * (a * a * a))))
        p = jnp.dot((gelu * b).astype(MXU_DT), wd_ref[cs, :].astype(MXU_DT), preferred_element_type=F32)
        part = p if part is None else part + p
    o_ref[...] += part

    @pl.when(j == pl.num_programs(1) - 1)
    def _():
        o_ref[...] = x1_ref[...] + g2_ref[...] * (_rms(o_ref[...], D_MODEL) * gpost_ref[...])


def _mix_ffn(mr, mh, mm, wr, wh, wm, x, mod, gmix, gpre, w_up, cw, cb, w_down, gpost, l, row_of_tile,
             seq_len, tm, tf):
    n = x.shape[0]
    nf = D_FF // tf
    tile = lambda w: pl.BlockSpec((tm, w), lambda i, j: (i, 0))
    layer = lambda r, c: pl.BlockSpec((None, r, c), lambda i, j: (l, 0, 0))
    mod_row = lambda which: pl.BlockSpec((None, None, None, 1, D_MODEL),
                                         lambda i, j: (l, which, row_of_tile(i), 0, 0))
    return pl.pallas_call(
        functools.partial(_ffn_body, seq_len=seq_len, tm=tm),
        grid=(n // tm, nf),
        in_specs=[tile(HWP), tile(HY_W), tile(HWP),
                  layer(HWP, D_MODEL), layer(HY_W, D_MODEL), layer(HWP, D_MODEL),
                  tile(D_MODEL), mod_row(2), mod_row(4), mod_row(3),
                  layer(1, D_MODEL), layer(1, D_MODEL),
                  pl.BlockSpec((None, D_MODEL, tf), lambda i, j: (l, 0, j)),
                  pl.BlockSpec((None, D_MODEL, tf), lambda i, j: (l, 0, j + nf)),
                  pl.BlockSpec((None, 8, tf), lambda i, j: (l, 0, j)),
                  pl.BlockSpec((None, 1, tf), lambda i, j: (l, 0, j)),
                  pl.BlockSpec((None, tf, D_MODEL), lambda i, j: (l, j, 0)),
                  mod_row(5), layer(1, D_MODEL)],
        out_specs=pl.BlockSpec((tm, D_MODEL), lambda i, j: (i, 0)),
        out_shape=jax.ShapeDtypeStruct((n, D_MODEL), F32),
        scratch_shapes=[pltpu.VMEM((tm, D_MODEL), F32), pltpu.VMEM((tm, D_MODEL), MXU_DT)],
        compiler_params=_params(),
        name="mix_ffn",
    )(mr, mh, mm, wr, wh, wm, x, mod, mod, mod, gmix, gpre, w_up, w_up, cw, cb, w_down, mod, gpost)


def _pad_rows8(w):
    return jnp.concatenate([w, jnp.zeros((w.shape[0], 8 - w.shape[1], w.shape[2]), w.dtype)], axis=1)


def kernel(x_prompt, x_sample, c, state_ret, state_mlstm_c, state_mlstm_n, state_mlstm_m, c_ctx,
           norm_mix_pre, norm_mix_post, norm_ffn_pre, norm_ffn_post, w_mod, b_mod, w_in, w_out,
           ret_decay_logit, ret_norm_g, hy_conv_w, hy_conv_b, hy_f_w1, hy_f_b1, hy_f_w2, hy_f_b2,
           hy_f_w3, hy_f_b3, hy_sin_freq, hy_bias, ml_gate_bias, ml_norm_g,
           w_up, ffn_conv_w, ffn_conv_b, w_down):
    BP, LP, _ = x_prompt.shape
    BS, LS, _ = x_sample.shape

    w_in_p = _prep_w_in(jnp.swapaxes(w_in, 1, 2))
    w_out_r = _pad_heads(lax.slice_in_dim(w_out, 0, HW, axis=1), axis=1).astype(MXU_DT)
    w_out_h = lax.slice_in_dim(w_out, HW, HW + HY_W, axis=1).astype(MXU_DT)
    w_out_m = _pad_heads(lax.slice_in_dim(w_out, HW + HY_W, 2 * HW + HY_W, axis=1), axis=1).astype(MXU_DT)
    ret_gain = _pad_heads(ret_norm_g).reshape(DEPTH, 1, HWP)
    ml_gain = _pad_heads(ml_norm_g).reshape(DEPTH, 1, HWP)
    dl = jnp.broadcast_to(ret_decay_logit.reshape(DEPTH, 8, 1), (DEPTH, 8, 128))
    gate_bias = jnp.concatenate([ml_gate_bias.reshape(DEPTH, 1, 16), jnp.zeros((DEPTH, 1, 112), F32)], axis=2)
    hy_cw = _pad_rows8(hy_conv_w)
    hy_cb = hy_conv_b.reshape(DEPTH, 1, 3 * HY_W)
    ffn_cw = _pad_rows8(ffn_conv_w)
    ffn_cb = ffn_conv_b.reshape(DEPTH, 1, D_FF)
    g_mix_pre = norm_mix_pre.reshape(DEPTH, 1, D_MODEL)
    g_mix_post = norm_mix_post.reshape(DEPTH, 1, D_MODEL)
    g_ffn_pre = norm_ffn_pre.reshape(DEPTH, 1, D_MODEL)
    g_ffn_post = norm_ffn_post.reshape(DEPTH, 1, D_MODEL)
    pad2 = lambda a, r, cdim: jnp.pad(a, ((0, 0), (0, r - a.shape[1]), (0, cdim - a.shape[2])))
    fw1 = pad2(hy_f_w1, 128, 128)
    fb1 = pad2(hy_f_b1.reshape(DEPTH, 1, -1), 1, 128)
    fw2 = pad2(hy_f_w2, 128, 128)
    fb2 = pad2(hy_f_b2.reshape(DEPTH, 1, -1), 1, 128)
    fw3 = pad2(hy_f_w3, 128, 4 * HY_W)
    fb3 = hy_f_b3.reshape(DEPTH, 1, 4 * HY_W)
    ffr = pad2(hy_sin_freq.reshape(DEPTH, 1, -1), 1, 128)

    m0 = jnp.broadcast_to(state_mlstm_m.reshape(BS, DEPTH, 8, 1), (BS, DEPTH, 8, 128))
    new_ret = jnp.zeros((BP, DEPTH, 2, HEADS, DH, DH), F32)
    new_c = jnp.zeros((BP, DEPTH, 2, HEADS, DH, DH), F32)
    new_n = jnp.zeros((BP, DEPTH, 2, HEADS, DH), F32)

    cvec8 = jnp.concatenate([c_ctx.reshape(1, D_MODEL), c, jnp.zeros((8 - 1 - BS, D_MODEL), F32)], axis=0)
    mod = _mod_all(cvec8, w_mod, b_mod)
    mod = mod.reshape(DEPTH, 8, 6, 1, D_MODEL).transpose(0, 2, 1, 3, 4)
    spectra_p = _filter_spectra(LP, fw1, fb1, fw2, fb2, fw3, fb3, ffr, hy_bias)
    spectra_s = _filter_spectra(LS, fw1, fb1, fw2, fb2, fw3, fb3, ffr, hy_bias)

    xp = x_prompt.reshape(BP * LP, D_MODEL)
    xs = x_sample.reshape(BS * LS, D_MODEL)
    tm = 512
    ffn_tm = 1024
    ffn_tf = 512
    n_ctx = BP * LP
    row_ctx = lambda i: 0
    row_both = lambda i: jnp.where(i < n_ctx // tm, 0, 1 + jnp.maximum(i - n_ctx // tm, 0) // (LS // tm))
    row_lat_ffn = lambda i: 1 + i // (LS // ffn_tm)
    new_m = []

    def mix_and_ffn(x, ph, mr, mm, l, B, L, row0, row_ffn, spectra):
        mh = _hyena(ph, hy_cw, hy_cb, spectra, l, B, L, row0)
        return _mix_ffn(mr, mh, mm, w_out_r, w_out_h, w_out_m, x, mod, g_mix_post, g_ffn_pre,
                        w_up, ffn_cw, ffn_cb, w_down, g_ffn_post, l, row_ffn, L, ffn_tm, ffn_tf)

    for l in range(DEPTH):
        pr, ph, pm, pg = _in_proj(xp, xs, mod, g_mix_pre, w_in_p, l, row_both, tm)
        mr, new_ret = _retention(pr, dl, ret_gain, l, BP, LP, False, new_ret, 0)
        mm, new_c, new_n, m_fin = _mlstm(pm, pg, gate_bias, ml_gain, l, BP, LP, False, new_c, new_n, 0)
        new_m.append(m_fin)
        xp = mix_and_ffn(xp, ph, mr, mm, l, BP, LP, 0, row_ctx, spectra_p)
        (mr,) = _retention(pr, dl, ret_gain, l, BS, LS, True, state_ret, n_ctx)
        (mm,) = _mlstm(pm, pg, gate_bias, ml_gain, l, BS, LS, True, state_mlstm_c, state_mlstm_n, n_ctx, m0)
        xs = mix_and_ffn(xs, ph, mr, mm, l, BS, LS, n_ctx, row_lat_ffn, spectra_s)

    out_m = jnp.stack(new_m, axis=1)[..., 0].reshape(BP, DEPTH, 2, HEADS)
    return (xp.reshape(BP, LP, D_MODEL), xs.reshape(BS, LS, D_MODEL), new_ret, new_c, new_n, out_m)
```

```python
import functools
import math

import numpy as np
import jax
import jax.numpy as jnp
from jax import lax
from jax.experimental import pallas as pl
from jax.experimental.pallas import tpu as pltpu

D_MODEL = 1024
DEPTH = 4
GRID_W = 64
HEADS = 4
DH = 96
DHP = 128
HW = HEADS * DH
HWP = HEADS * DHP
HY_W = 256
D_FF = 4 * D_MODEL
CHUNK = 128
N_BANDS = 16
FEAT_W = 1 + 2 * N_BANDS
HY_SHIFT = 0.05
HY_TARGET = 1e-2
HY_SHORT_DECAY_PCT = 0.3
HY_LONG_DECAY_PCT = 1.5
ROPE_BASE = 10000.0
EPS = 1e-6

MXU_DT = jnp.bfloat16
F32 = jnp.float32
VMEM_LIMIT = 56 * 1024 * 1024


def _dot(a, b):
    return jnp.dot(a.astype(MXU_DT), b.astype(MXU_DT), preferred_element_type=F32)


def _dot_nt(a, b):
    return lax.dot_general(a.astype(MXU_DT), b.astype(MXU_DT), (((1,), (1,)), ((), ())),
                           preferred_element_type=F32)


def _dot_tn(a, b):
    return lax.dot_general(a.astype(MXU_DT), b.astype(MXU_DT), (((0,), (0,)), ((), ())),
                           preferred_element_type=F32)


def _split2(x):
    hi = x.astype(MXU_DT)
    lo = (x - hi.astype(F32)).astype(MXU_DT)
    return hi, lo


def _dot_split(a_hi, a_lo, b):
    b_hi, b_lo = _split2(b)
    return (jnp.dot(a_hi, b_hi, preferred_element_type=F32)
            + jnp.dot(a_hi, b_lo, preferred_element_type=F32)
            + jnp.dot(a_lo, b_hi, preferred_element_type=F32))


def _dot3(a, b):
    a_hi, a_lo = _split2(a)
    return _dot_split(a_hi, a_lo, b)


def _dot_exact_lhs(t, x):
    x1 = x.astype(MXU_DT)
    r1 = x - x1.astype(F32)
    x2 = r1.astype(MXU_DT)
    x3 = (r1 - x2.astype(F32)).astype(MXU_DT)
    return (jnp.dot(t, x1, preferred_element_type=F32) + jnp.dot(t, x2, preferred_element_type=F32)
            + jnp.dot(t, x3, preferred_element_type=F32))


def _rms(x, n):
    return x * lax.rsqrt(jnp.sum(x * x, axis=-1, keepdims=True) * (1.0 / n) + EPS)


def _chunk_rows(c, size, base=0):
    if isinstance(c, int):
        return slice(base + c * size, base + (c + 1) * size)
    return pl.ds(pl.multiple_of(base + c * size, size), size)


def _log_sigmoid(x):
    return jnp.minimum(x, 0.0) - jnp.log1p(jnp.exp(-jnp.abs(x)))


def _resident(shape, index_map):
    return pl.BlockSpec(shape, index_map, pipeline_mode=pl.Buffered(1))


def _params(**kw):
    return pltpu.CompilerParams(vmem_limit_bytes=VMEM_LIMIT, **kw)


@functools.lru_cache(maxsize=None)
def _dft_tables(L):
    f = np.arange(L, dtype=np.int64)[:, None]
    s = np.arange(L, dtype=np.int64)[None, :]
    ang = np.pi * ((f * s) % (2 * L)).astype(np.float64) / L
    fr = np.cos(ang)
    fi = -np.sin(ang)
    fi[0, :] = np.where(np.arange(L) % 2 == 0, 1.0, -1.0)
    fwd = np.concatenate([fr, fi], axis=0)
    gr = np.cos(ang.T) * (2.0 / (2 * L))
    gr[:, 0] = 1.0 / (2 * L)
    gi = -np.sin(ang.T) * (2.0 / (2 * L))
    gi[:, 0] = np.where(np.arange(L) % 2 == 0, 1.0, -1.0) / (2 * L)
    inv = np.concatenate([gr, gi], axis=1)
    return fwd.astype(np.float32), inv.astype(np.float32)


@functools.lru_cache(maxsize=None)
def _filter_tables(L):
    tn = np.arange(L, dtype=np.float64) / L
    bands = np.linspace(1e-4, N_BANDS - 1, N_BANDS)
    ang = 2.0 * math.pi * tn[:, None] * bands[None, :]
    feat = np.zeros((L, 128), np.float32)
    feat[:, 0] = tn
    feat[:, 1:1 + N_BANDS] = np.cos(ang)
    feat[:, 1 + N_BANDS:FEAT_W] = np.sin(ang)
    deltas = np.abs(np.linspace(math.log(HY_TARGET) / HY_LONG_DECAY_PCT,
                                math.log(HY_TARGET) / HY_SHORT_DECAY_PCT, HY_W))
    window = (np.exp(-tn[:, None] * deltas[None, :]) + HY_SHIFT).astype(np.float32)
    return feat, window


@functools.lru_cache(maxsize=None)
def _rope_tables(L):
    rows = L // GRID_W
    row = np.repeat(np.arange(rows, dtype=np.float64), GRID_W)
    col = np.tile(np.arange(GRID_W, dtype=np.float64), rows)
    half = DH // 2
    n_freq = half // 2
    freqs = ROPE_BASE ** (-np.arange(n_freq, dtype=np.float64) / n_freq)
    ang = np.concatenate([row[:, None] * freqs, col[:, None] * freqs], axis=-1)
    cos = np.zeros((L, DHP), np.float32)
    sin = np.zeros((L, DHP), np.float32)
    cos[:, :half] = np.cos(ang)
    cos[:, 64:64 + half] = np.cos(ang)
    sin[:, :half] = -np.sin(ang)
    sin[:, 64:64 + half] = np.sin(ang)
    return cos, sin


@functools.lru_cache(maxsize=None)
def _tri_tables():
    i = np.arange(CHUNK)
    lower = (i[:, None] >= i[None, :]).astype(np.float32)
    upper = (i[:, None] <= i[None, :]).astype(np.float32)
    return lower, upper


def _pad_heads(w, axis=-1, rope=False):
    axis = axis % w.ndim
    parts = []

    def zeros(n):
        shp = list(w.shape)
        shp[axis] = n
        return jnp.zeros(shp, w.dtype)

    for h in range(HEADS):
        blk = lax.slice_in_dim(w, h * DH, (h + 1) * DH, axis=axis)
        if rope:
            half = DH // 2
            parts += [lax.slice_in_dim(blk, 0, half, axis=axis), zeros(64 - half),
                      lax.slice_in_dim(blk, half, DH, axis=axis), zeros(64 - half)]
        else:
            parts += [blk, zeros(DHP - DH)]
    return jnp.concatenate(parts, axis=axis)


def _unpad_heads_axis(x, axis, rope=False):
    if rope:
        half = DH // 2
        return jnp.concatenate([lax.slice_in_dim(x, 0, half, axis=axis),
                                lax.slice_in_dim(x, 64, 64 + half, axis=axis)], axis=axis)
    return lax.slice_in_dim(x, 0, DH, axis=axis)


def _mod_body(c_ref, w_ref, b_ref, o_ref):
    c = c_ref[...]
    s = c * (1.0 / (1.0 + jnp.exp(-c)))
    o_ref[...] = _dot(s, w_ref[...]) + b_ref[...]


def _mod_all(cvec8, w_mod, b_mod):
    tn = 1536
    nj = 6 * D_MODEL // tn
    return pl.pallas_call(
        _mod_body,
        grid=(DEPTH, nj),
        in_specs=[pl.BlockSpec((8, D_MODEL), lambda l, j: (0, 0)),
                  pl.BlockSpec((None, D_MODEL, tn), lambda l, j: (l, 0, j)),
                  pl.BlockSpec((None, 1, tn), lambda l, j: (l, 0, j))],
        out_specs=pl.BlockSpec((None, 8, tn), lambda l, j: (l, 0, j)),
        out_shape=jax.ShapeDtypeStruct((DEPTH, 8, 6 * D_MODEL), F32),
        compiler_params=_params(),
        name="mod_all",
    )(cvec8, w_mod, b_mod.reshape(DEPTH, 1, 6 * D_MODEL))


def _filt_body(feat_ref, win_ref, f_ref, w1_ref, b1_ref, w2_ref, b2_ref, w3_ref, b3_ref,
               fr_ref, hb_ref, p_ref, q_ref, p2_ref, *, L):
    fr = fr_ref[...]
    h = jnp.sin(fr * (_dot3(feat_ref[...], w1_ref[...]) + b1_ref[...]))
    h = jnp.sin(fr * (_dot3(h, w2_ref[...]) + b2_ref[...]))
    filt = _dot3(h, w3_ref[...]) + b3_ref[...]
    win = win_ref[...]
    row = lax.broadcasted_iota(jnp.int32, (L, 1), 0)
    taps = filt * jnp.concatenate([win, win, win, win], axis=1)
    lane = lax.broadcasted_iota(jnp.int32, (L, 4 * HY_W), 1)
    taps = jnp.where((row == 0) & (lane >= 2 * HY_W), 0.0, taps)
    spec = _dot(f_ref[...], taps)
    for o in range(2):
        a = spec[:, o * HY_W:(o + 1) * HY_W]
        b = spec[:, (2 + o) * HY_W:(3 + o) * HY_W]
        kr = a[:L] + b[:L]
        ki = a[L:] - b[L:]
        nyq = a[L:L + 1] + b[L:L + 1]
        bias = hb_ref[o:o + 1, :]
        p_ref[o] = kr + bias
        q_ref[o] = jnp.where(row == 0, 0.0, ki)
        p2_ref[o] = jnp.where(row == 0, nyq, kr) + bias


def _filter_spectra(L, fw1, fb1, fw2, fb2, fw3, fb3, ffr, hy_bias):
    feat, window = _filter_tables(L)
    fwd_table, _ = _dft_tables(L)
    const = lambda shape: _resident(shape, lambda l: (0,) * len(shape))
    per_layer = lambda shape: pl.BlockSpec((None,) + shape, lambda l: (l,) + (0,) * len(shape))
    out_spec = per_layer((2, L, HY_W))
    out_shape = jax.ShapeDtypeStruct((DEPTH, 2, L, HY_W), F32)
    return pl.pallas_call(
        functools.partial(_filt_body, L=L),
        grid=(DEPTH,),
        in_specs=[const((L, 128)), const((L, HY_W)), const((2 * L, L)),
                  per_layer((128, 128)), per_layer((1, 128)), per_layer((128, 128)), per_layer((1, 128)),
                  per_layer((128, 4 * HY_W)), per_layer((1, 4 * HY_W)), per_layer((1, 128)),
                  per_layer((2, HY_W))],
        out_specs=[out_spec, out_spec, out_spec],
        out_shape=[out_shape, out_shape, out_shape],
        compiler_params=_params(),
        name="hyena_filter_%d" % L,
    )(feat, window, fwd_table, fw1, fb1, fw2, fb2, fw3, fb3, ffr, hy_bias)


SEC_W = 3 * HWP + HW
W_R0 = 0
W_H0 = W_R0 + SEC_W
W_M0 = W_H0 + 3 * HY_W
W_G0 = W_M0 + SEC_W
W_END = W_G0 + 128
IN_W = 8 * HW + 3 * HY_W + 16
FFN_SUB = 256
ROW_SUB = 256
SCAN_UNROLL = 4


def _prep_in_body(wt_ref, o_ref):
    kb = wt_ref.shape[1]
    half = DH // 2

    def zeros(n):
        return jnp.zeros((n, kb), F32)

    def head_tile(src, rope):
        if rope:
            return jnp.concatenate([wt_ref[src:src + half, :], zeros(64 - half),
                                    wt_ref[src + half:src + DH, :], zeros(64 - half)], axis=0)
        return jnp.concatenate([wt_ref[src:src + DH, :], zeros(DHP - DH)], axis=0)

    def dense_tiles(src, width):
        return [wt_ref[src + i * 128:src + (i + 1) * 128, :] for i in range(width // 128)]

    tiles = []
    for sec in range(3):
        tiles += [head_tile(sec * HW + h * DH, sec < 2) for h in range(HEADS)]
    tiles += dense_tiles(3 * HW, HW)
    tiles += dense_tiles(4 * HW, 3 * HY_W)
    m0 = 4 * HW + 3 * HY_W
    for sec in range(3):
        tiles += [head_tile(m0 + sec * HW + h * DH, False) for h in range(HEADS)]
    tiles += dense_tiles(m0 + 3 * HW, HW)
    g0 = 8 * HW + 3 * HY_W
    tiles.append(jnp.concatenate([wt_ref[g0:g0 + 16, :], zeros(112)], axis=0))
    for t, tile in enumerate(tiles):
        o_ref[:, t * 128:(t + 1) * 128] = tile.T.astype(o_ref.dtype)


def _prep_w_in(w_in_t):
    kb = 256
    return pl.pallas_call(
        _prep_in_body,
        grid=(DEPTH, D_MODEL // kb),
        in_specs=[pl.BlockSpec((None, IN_W, kb), lambda l, i: (l, 0, i))],
        out_specs=pl.BlockSpec((None, kb, W_END), lambda l, i: (l, i, 0)),
        out_shape=jax.ShapeDtypeStruct((DEPTH, D_MODEL, W_END), MXU_DT),
        compiler_params=_params(),
        name="prep_w_in",
    )(w_in_t)


def _in_body(xa_ref, xb_ref, sc_ref, sh_ref, g_ref, w_ref, pr_ref, ph_ref, pm_ref, pg_ref, *, tiles_a):
    x = jnp.where(pl.program_id(0) < tiles_a, xa_ref[...], xb_ref[...])
    h = _rms(x, D_MODEL) * g_ref[...]
    h = (h * (1.0 + sc_ref[...]) + sh_ref[...]).astype(MXU_DT)
    proj = jnp.dot(h, w_ref[...], preferred_element_type=F32)
    pr_ref[...] = proj[:, W_R0:W_H0]
    ph_ref[...] = proj[:, W_H0:W_M0]
    pm_ref[...] = proj[:, W_M0:W_G0]
    pg_ref[...] = proj[:, W_G0:W_END]


def _mod_spec(l, which, row_of_tile):
    return pl.BlockSpec((None, None, None, 1, D_MODEL), lambda i: (l, which, row_of_tile(i), 0, 0))


def _in_proj(xa, xb, mod, gain, w_in, l, row_of_tile, tm):
    tiles_a = xa.shape[0] // tm
    n = xa.shape[0] + xb.shape[0]
    shapes = [SEC_W, 3 * HY_W, SEC_W, 128]
    return pl.pallas_call(
        functools.partial(_in_body, tiles_a=tiles_a),
        grid=(n // tm,),
        in_specs=[pl.BlockSpec((tm, D_MODEL), lambda i: (jnp.minimum(i, tiles_a - 1), 0)),
                  pl.BlockSpec((tm, D_MODEL), lambda i: (jnp.maximum(i - tiles_a, 0), 0)),
                  _mod_spec(l, 1, row_of_tile), _mod_spec(l, 0, row_of_tile),
                  pl.BlockSpec((None, 1, D_MODEL), lambda i: (l, 0, 0)),
                  _resident((None, D_MODEL, W_END), lambda i: (l, 0, 0))],
        out_specs=[pl.BlockSpec((tm, w), lambda i: (i, 0)) for w in shapes],
        out_shape=[jax.ShapeDtypeStruct((n, w), F32) for w in shapes],
        compiler_params=_params(),
        name="in_proj",
    )(xa, xb, mod, mod, gain, w_in)


def _ret_body(*refs, L, latent, nb):
    if latent:
        (q_ref, k_ref, v_ref, gate_ref, dl_ref, gain_ref, s0_ref, cos_ref, sin_ref,
         o_ref, qs, ks, acc, sst) = refs
    else:
        (q_ref, k_ref, v_ref, gate_ref, dl_ref, gain_ref, _, o_ref, sfin_ref, qs, ks, acc, sst) = refs
    half = DH // 2
    C = CHUNK
    nchunk = L // C
    lg = _log_sigmoid(dl_ref[...])
    rel = (lax.broadcasted_iota(jnp.int32, (C, C), 0) - lax.broadcasted_iota(jnp.int32, (C, C), 1)).astype(F32)
    ri = lax.broadcasted_iota(jnp.int32, (C, 1), 0).astype(F32)

    for bi in range(nb):
        seq = slice(bi * L, (bi + 1) * L)
        for h in range(HEADS):
            cols = slice(h * DHP, (h + 1) * DHP)
            q = q_ref[seq, cols]
            k = k_ref[seq, cols]
            if latent:
                cos = cos_ref[...]
                sin = sin_ref[...]
                q = q * cos + pltpu.roll(q, 64, 1) * sin
                k = k * cos + pltpu.roll(k, 64, 1) * sin
            qs[seq, cols] = q
            ks[seq, cols] = k * (DH ** -0.5)
            for d in range(2):
                sst[bi, d, h] = jnp.zeros((DHP, DHP), F32)
                if latent:
                    sst[bi, d, h, 0:half, 0:DH] = s0_ref[bi, d, h, 0:half, :]
                    sst[bi, d, h, 64:64 + half, 0:DH] = s0_ref[bi, d, h, half:DH, :]

    acc[...] = jnp.zeros_like(acc)
    consts = []
    for h in range(HEADS):
        lgf = lg[h:h + 1, :]
        lgb = lg[HEADS + h:HEADS + h + 1, :]
        lgf1 = lgf[:, 0:1]
        lgb1 = lgb[:, 0:1]
        mask = (jnp.where(rel >= 0, jnp.exp(rel * lgf), 0.0)
                + jnp.where(rel <= 0, jnp.exp(-rel * lgb), 0.0))
        consts.append(dict(
            mask=mask,
            qdec_f=jnp.exp((ri + 1.0) * lgf1), qdec_b=jnp.exp((C - ri) * lgb1),
            kdec_f=jnp.exp((C - 1.0 - ri) * lgf1), kdec_b=jnp.exp(ri * lgb1),
            cdec_f=jnp.exp(C * lgf1), cdec_b=jnp.exp(C * lgb1)))

    def step(j):
        for bi in range(nb):
            rows_f = _chunk_rows(j, C, bi * L)
            rows_b = _chunk_rows(nchunk - 1 - j, C, bi * L)
            for h in range(HEADS):
                cols = slice(h * DHP, (h + 1) * DHP)
                cn = consts[h]
                q = qs[rows_f, cols]
                k = ks[rows_f, cols]
                v = v_ref[rows_f, cols]
                s_f = sst[bi, 0, h]
                sc = _dot_nt(q, k) * cn["mask"]
                acc[rows_f, cols] += _dot(sc, v) + _dot(q, s_f) * cn["qdec_f"]
                sst[bi, 0, h] = s_f * cn["cdec_f"] + _dot_tn(k * cn["kdec_f"], v)
                q = qs[rows_b, cols]
                k = ks[rows_b, cols]
                v = v_ref[rows_b, cols]
                s_b = sst[bi, 1, h]
                acc[rows_b, cols] += _dot(q, s_b) * cn["qdec_b"]
                sst[bi, 1, h] = s_b * cn["cdec_b"] + _dot_tn(k * cn["kdec_b"], v)

    if nchunk <= 2:
        for j in range(nchunk):
            step(j)
    else:
        pl.loop(0, nchunk, unroll=SCAN_UNROLL)(step)

    if not latent:
        for bi in range(nb):
            for d in range(2):
                for h in range(HEADS):
                    sfin_ref[bi, d, h, 0:half, :] = sst[bi, d, h, 0:half, 0:DH]
                    sfin_ref[bi, d, h, half:DH, :] = sst[bi, d, h, 64:64 + half, 0:DH]

    y = jnp.concatenate([(_rms(acc[:, h * DHP:(h + 1) * DHP], DH) * gain_ref[:, h * DHP:(h + 1) * DHP])[:, :DH]
                         for h in range(HEADS)], axis=1)
    g = gate_ref[...]
    o_ref[...] = (y * (g * (1.0 / (1.0 + jnp.exp(-g))))).astype(o_ref.dtype)


def _state_spec(l, nb, *tail):
    shape = (nb, None, 2, HEADS) + tail
    return pl.BlockSpec(shape, lambda b: (b, l) + (0,) * (len(shape) - 2))


def _seqs_per_step(B, L):
    return min(B, max(1, 512 // L))


def _retention(pr, dl, gain, l, B, L, latent, state, row0):
    nb = _seqs_per_step(B, L)
    off = row0 // (nb * L)
    blk = lambda j: pl.BlockSpec((nb * L, HWP), lambda b, j=j: (b + off, j))
    gate_blk = pl.BlockSpec((nb * L, HW), lambda b: (b + off, 3 * HWP // HW))
    in_specs = [blk(0), blk(1), blk(2), gate_blk,
                pl.BlockSpec((None, 8, 128), lambda b: (l, 0, 0)),
                pl.BlockSpec((None, 1, HWP), lambda b: (l, 0, 0))]
    args = [pr, pr, pr, pr, dl, gain]
    out_specs = [pl.BlockSpec((nb * L, HW), lambda b: (b, 0))]
    out_shape = [jax.ShapeDtypeStruct((B * L, HW), MXU_DT)]
    aliases = {}
    if latent:
        cos, sin = _rope_tables(L)
        in_specs += [_state_spec(l, nb, DH, DH),
                     pl.BlockSpec((L, DHP), lambda b: (0, 0)), pl.BlockSpec((L, DHP), lambda b: (0, 0))]
        args += [state, cos, sin]
    else:
        in_specs.append(pl.BlockSpec(memory_space=pl.ANY))
        args.append(state)
        aliases = {len(args) - 1: 1}
        out_specs.append(_state_spec(l, nb, DH, DH))
        out_shape.append(jax.ShapeDtypeStruct(state.shape, F32))
    return pl.pallas_call(
        functools.partial(_ret_body, L=L, latent=latent, nb=nb),
        grid=(B // nb,),
        in_specs=in_specs, out_specs=out_specs, out_shape=out_shape,
        input_output_aliases=aliases,
        scratch_shapes=[pltpu.VMEM((nb * L, HWP), F32), pltpu.VMEM((nb * L, HWP), F32),
                        pltpu.VMEM((nb * L, HWP), F32), pltpu.VMEM((nb, 2, HEADS, DHP, DHP), F32)],
        compiler_params=_params(),
        name="retention_%d" % L,
    )(*args)


def _ml_body(*refs, L, latent, nb):
    if latent:
        (q_ref, k_ref, v_ref, og_ref, gates_ref, gb_ref, gain_ref, tl_ref, tu_ref, c0_ref, n0_ref, m0_ref,
         o_ref, acc, bb_s, ub_s, pm_s, brow_s, ibrow_s, mc_s, mx_s, cn_s) = refs
    else:
        (q_ref, k_ref, v_ref, og_ref, gates_ref, gb_ref, gain_ref, tl_ref, tu_ref, _, _,
         o_ref, cfin_ref, nfin_ref, mfin_ref, acc, bb_s, ub_s, pm_s, brow_s, ibrow_s, mc_s, mx_s, cn_s) = refs
    C = CHUNK
    nchunk = L // C
    lane = lax.broadcasted_iota(jnp.int32, (C, DHP), 1)
    tri_r = lax.broadcasted_iota(jnp.int32, (C, C), 0)
    tri_c = lax.broadcasted_iota(jnp.int32, (C, C), 1)
    tl = tl_ref[...]
    tu = tu_ref[...]

    row_i = lax.broadcasted_iota(jnp.int32, (C, 128), 0)

    def cummax_rows(x, suffix):
        s = 1
        while s < C:
            if suffix:
                x = jnp.where(row_i < C - s, jnp.maximum(x, pltpu.roll(x, C - s, 0)), x)
            else:
                x = jnp.where(row_i >= s, jnp.maximum(x, pltpu.roll(x, s, 0)), x)
            s *= 2
        return x

    for c in range(nb * nchunk):
        rows = slice(c * C, (c + 1) * C)
        g = gates_ref[rows, :] + gb_ref[...]
        lf = _log_sigmoid(g)
        pre = _dot_exact_lhs(tl, lf)
        suf = _dot_exact_lhs(tu, lf)
        bc = jnp.where(lane < 8, pre, suf)
        brow_s[c * 16:(c + 1) * 16, :] = bc.T[0:16, :]
        ibrow_s[c * 16:(c + 1) * 16, :] = g.T[0:16, :]
        for d in range(2):
            for h in range(HEADS):
                sr = d * HEADS + h
                b_b = jnp.broadcast_to(bc[:, d * 8 + 4 + h:d * 8 + 5 + h], (C, 128))
                u_b = jnp.broadcast_to(g[:, d * 8 + h:d * 8 + h + 1], (C, 128)) - b_b
                bb_s[sr, rows, :] = b_b
                ub_s[sr, rows, :] = u_b
                pm_s[sr, rows, :] = cummax_rows(u_b, suffix=(d == 1))

    for bi in range(nb):
        for d in range(2):
            for h in range(HEADS):
                sr = d * HEADS + h
                m = m0_ref[bi, sr:sr + 1, :] if latent else jnp.zeros((1, 128), F32)
                for j in range(nchunk):
                    c = bi * nchunk + (j if d == 0 else nchunk - 1 - j)
                    last = c * C + (C - 1 if d == 0 else 0)
                    mx = jnp.maximum(m, pm_s[sr, last:last + 1, :])
                    mc_s[c * 8 + sr:c * 8 + sr + 1, :] = m
                    mx_s[c * 8 + sr:c * 8 + sr + 1, :] = mx
                    m = bb_s[sr, last:last + 1, :] + mx
                if not latent:
                    mfin_ref[bi, sr:sr + 1, :] = m

    acc[...] = jnp.zeros_like(acc)
    cn_s[...] = jnp.zeros_like(cn_s)
    if latent:
        for bi in range(nb):
            for d in range(2):
                for h in range(HEADS):
                    cn_s[bi, d, h, 0:DH, 0:DH] = c0_ref[bi, d, h]
                    n_row = jnp.concatenate([n0_ref[bi, d, h:h + 1, :], jnp.zeros((1, DHP - DH), F32)], axis=1)
                    cn_s[bi, d, h, :, DHP:] = jnp.broadcast_to(n_row, (DHP, DHP)).T
    ones_blk = jnp.ones((C, DHP), MXU_DT)

    def chain(bi, d, h, c):
        rows = _chunk_rows(c, C)
        brs = brow_s[_chunk_rows(c, 16), :]
        ibrs = ibrow_s[_chunk_rows(c, 16), :]
        mcs = mc_s[_chunk_rows(c, 8), :]
        mxs = mx_s[_chunk_rows(c, 8), :]
        causal = (tri_r >= tri_c) if d == 0 else (tri_r <= tri_c)
        cols = slice(h * DHP, (h + 1) * DHP)
        ic = d * 8 + h
        fc = d * 8 + 4 + h
        sr = d * HEADS + h
        m_c = mcs[sr:sr + 1, :]
        mx = mxs[sr:sr + 1, :]
        q = q_ref[rows, cols].astype(MXU_DT)
        k = k_ref[rows, cols] * (DH ** -0.5)
        v1 = jnp.concatenate([v_ref[rows, cols].astype(MXU_DT), ones_blk], axis=1)
        cn = cn_s[bi, d, h]
        urow = ibrs[ic:ic + 1, :] - brs[fc:fc + 1, :]
        mb = jnp.maximum(m_c, pm_s[sr, rows, :])
        w_intra = jnp.exp(jnp.where(causal, urow - mb, -jnp.inf))
        w_inter = jnp.exp(m_c - mb)
        s = _dot_nt(q, k) * w_intra
        tot = (jnp.dot(s.astype(MXU_DT), v1, preferred_element_type=F32)
               + jnp.concatenate([w_inter, w_inter], axis=1) * _dot(q, cn))
        den = tot[:, DHP:]
        floor = jnp.exp(-(bb_s[sr, rows, :] + mb))
        acc[rows, cols] += tot[:, :DHP] * (1.0 / jnp.maximum(jnp.abs(den), floor))
        kw = k * jnp.exp(ub_s[sr, rows, :] - mx)
        w_prev = jnp.exp(m_c - mx)
        cn_s[bi, d, h] = jnp.concatenate([w_prev, w_prev], axis=1) * cn + _dot_tn(kw, v1)

    def step(j):
        for bi in range(nb):
            for d in range(2):
                c = bi * nchunk + (j if d == 0 else nchunk - 1 - j)
                for h in range(HEADS):
                    chain(bi, d, h, c)

    if nchunk <= 2:
        for j in range(nchunk):
            step(j)
    else:
        pl.loop(0, nchunk, unroll=SCAN_UNROLL)(step)

    if not latent:
        for bi in range(nb):
            for d in range(2):
                for h in range(HEADS):
                    cfin_ref[bi, d, h] = cn_s[bi, d, h, 0:DH, 0:DH]
                    nfin_ref[bi, d, h:h + 1, :] = cn_s[bi, d, h, :, DHP:].T[0:1, 0:DH]

    y = jnp.concatenate([(_rms(acc[:, h * DHP:(h + 1) * DHP], DH) * gain_ref[:, h * DHP:(h + 1) * DHP])[:, :DH]
                         for h in range(HEADS)], axis=1)
    o_ref[...] = (y * (1.0 / (1.0 + jnp.exp(-og_ref[...])))).astype(o_ref.dtype)


def _mlstm(pm, pg, gb, gain, l, B, L, latent, c_state, n_state, row0, m0=None):
    tl, tu = _tri_tables()
    tl = tl.astype(MXU_DT)
    tu = tu.astype(MXU_DT)
    nb = _seqs_per_step(B, L)
    off = row0 // (nb * L)
    blk = lambda j: pl.BlockSpec((nb * L, HWP), lambda b, j=j: (b + off, j))
    gate_blk = pl.BlockSpec((nb * L, HW), lambda b: (b + off, 3 * HWP // HW))
    in_specs = [blk(0), blk(1), blk(2), gate_blk,
                pl.BlockSpec((nb * L, 128), lambda b: (b + off, 0)),
                pl.BlockSpec((None, 1, 128), lambda b: (l, 0, 0)),
                pl.BlockSpec((None, 1, HWP), lambda b: (l, 0, 0)),
                pl.BlockSpec((CHUNK, CHUNK), lambda b: (0, 0)),
                pl.BlockSpec((CHUNK, CHUNK), lambda b: (0, 0))]
    args = [pm, pm, pm, pm, pg, gb, gain, tl, tu]
    out_specs = [pl.BlockSpec((nb * L, HW), lambda b: (b, 0))]
    out_shape = [jax.ShapeDtypeStruct((B * L, HW), MXU_DT)]
    aliases = {}
    if latent:
        in_specs += [_state_spec(l, nb, DH, DH), _state_spec(l, nb, DH),
                     pl.BlockSpec((nb, None, 8, 128), lambda b: (b, l, 0, 0))]
        args += [c_state, n_state, m0]
    else:
        in_specs += [pl.BlockSpec(memory_space=pl.ANY), pl.BlockSpec(memory_space=pl.ANY)]
        args += [c_state, n_state]
        aliases = {len(args) - 2: 1, len(args) - 1: 2}
        out_specs += [_state_spec(l, nb, DH, DH), _state_spec(l, nb, DH),
                      pl.BlockSpec((nb, 8, 128), lambda b: (b, 0, 0))]
        out_shape += [jax.ShapeDtypeStruct(c_state.shape, F32), jax.ShapeDtypeStruct(n_state.shape, F32),
                      jax.ShapeDtypeStruct((B, 8, 128), F32)]
    nchunk = L // CHUNK
    return pl.pallas_call(
        functools.partial(_ml_body, L=L, latent=latent, nb=nb),
        grid=(B // nb,),
        in_specs=in_specs, out_specs=out_specs, out_shape=out_shape,
        input_output_aliases=aliases,
        scratch_shapes=[pltpu.VMEM((nb * L, HWP), F32),
                        pltpu.VMEM((8, nb * L, 128), F32), pltpu.VMEM((8, nb * L, 128), F32),
                        pltpu.VMEM((8, nb * L, 128), F32),
                        pltpu.VMEM((nb * nchunk * 16, 128), F32), pltpu.VMEM((nb * nchunk * 16, 128), F32),
                        pltpu.VMEM((nb * nchunk * 8, 128), F32), pltpu.VMEM((nb * nchunk * 8, 128), F32),
                        pltpu.VMEM((nb, 2, HEADS, DHP, 2 * DHP), F32)],
        compiler_params=_params(),
        name="mlstm_%d" % L,
    )(*args)


def _shift_rows(x, seq_len, n_rows):
    pos = lax.broadcasted_iota(jnp.int32, (n_rows, 1), 0) & (seq_len - 1)
    prev = jnp.where(pos == 0, 0.0, pltpu.roll(x, 1, 0))
    nxt = jnp.where(pos == seq_len - 1, 0.0, pltpu.roll(x, n_rows - 1, 0))
    return prev, nxt


def _hy_body(p_ref, cw_ref, cb_ref, f_ref, g_ref, ps_ref, qs_ref, p2s_ref, o_ref, fh_ref, gh_ref, *, L):
    @pl.when(pl.program_id(0) == 0)
    def _():
        fh_ref[...] = f_ref[...].astype(MXU_DT)
        gh_ref[...] = g_ref[...].astype(MXU_DT)

    def conv(z, o):
        spec = _dot(fh_ref[...], z)
        xr = spec[:L]
        xi = spec[L:]
        p = ps_ref[o]
        q = qs_ref[o]
        yr = xr * p - xi * q
        yi = xr * q + xi * p2s_ref[o]
        return _dot(gh_ref[...], jnp.concatenate([yr, yi], axis=0))

    for i in range(p_ref.shape[0] // L):
        rows = slice(i * L, (i + 1) * L)
        x = p_ref[rows, :]
        prev, nxt = _shift_rows(x, L, L)
        hy = prev * cw_ref[0:1, :] + x * cw_ref[1:2, :] + nxt * cw_ref[2:3, :] + cb_ref[...]
        hv = hy[:, 0:HY_W]
        hx1 = hy[:, HY_W:2 * HY_W]
        hx2 = hy[:, 2 * HY_W:3 * HY_W]
        z = hx1 * conv(hv, 0)
        o_ref[rows, :] = (hx2 * conv(z, 1)).astype(o_ref.dtype)


def _hyena(ph, cw, cb, spectra, l, B, L, row0):
    fwd_table, inv_table = _dft_tables(L)
    ps, qs, p2s = spectra
    const = lambda shape: _resident(shape, lambda b: (0,) * len(shape))
    spec_blk = _resident((None, 2, L, HY_W), lambda b: (l, 0, 0, 0))
    nb = min(B, max(1, 2048 // L))
    off = row0 // (nb * L)
    return pl.pallas_call(
        functools.partial(_hy_body, L=L),
        grid=(B // nb,),
        in_specs=[pl.BlockSpec((nb * L, 3 * HY_W), lambda b: (b + off, 0)),
                  pl.BlockSpec((None, 8, 3 * HY_W), lambda b: (l, 0, 0)),
                  pl.BlockSpec((None, 1, 3 * HY_W), lambda b: (l, 0, 0)),
                  const((2 * L, L)), const((L, 2 * L)),
                  spec_blk, spec_blk, spec_blk],
        out_specs=pl.BlockSpec((nb * L, HY_W), lambda b: (b, 0)),
        out_shape=jax.ShapeDtypeStruct((B * L, HY_W), MXU_DT),
        scratch_shapes=[pltpu.VMEM((2 * L, L), MXU_DT), pltpu.VMEM((L, 2 * L), MXU_DT)],
        compiler_params=_params(),
        name="hyena_%d" % L,
    )(ph, cw, cb, fwd_table, inv_table, ps, qs, p2s)


def _out_body(mr_ref, mh_ref, mm_ref, w_ref, x_ref, g1_ref, sc2_ref, sh2_ref,
              gpost_ref, gpre_ref, x1_ref, h2_ref):
    w = w_ref[...].astype(MXU_DT)
    for r in range(x_ref.shape[0] // ROW_SUB):
        rs = slice(r * ROW_SUB, (r + 1) * ROW_SUB)
        mixed = jnp.concatenate([mr_ref[rs, :], mh_ref[rs, :], mm_ref[rs, :]], axis=1)
        mix = jnp.dot(mixed, w, preferred_element_type=F32)
        x1 = x_ref[rs, :] + g1_ref[...] * (_rms(mix, D_MODEL) * gpost_ref[...])
        x1_ref[rs, :] = x1
        h2 = _rms(x1, D_MODEL) * gpre_ref[...] * (1.0 + sc2_ref[...]) + sh2_ref[...]
        h2_ref[rs, :] = h2.astype(h2_ref.dtype)


def _ffn_body(mr_ref, mh_ref, mm_ref, wo_ref, x_ref, g1_ref, sc2_ref, sh2_ref, gmix_ref, gpre_ref,
              wa_ref, wb_ref, cw_ref, cb_ref, wd_ref, g2_ref, gpost_ref, o_ref, x1_ref, h_ref, *, seq_len, tm):
    j = pl.program_id(1)

    @pl.when(j == 0)
    def _():
        _out_body(mr_ref, mh_ref, mm_ref, wo_ref, x_ref, g1_ref, sc2_ref, sh2_ref,
                  gmix_ref, gpre_ref, x1_ref, h_ref)
        o_ref[...] = jnp.zeros_like(o_ref)

    h = h_ref[...]
    part = None
    for s in range(wa_ref.shape[1] // FFN_SUB):
        cs = slice(s * FFN_SUB, (s + 1) * FFN_SUB)
        a = jnp.dot(h, wa_ref[:, cs].astype(MXU_DT), preferred_element_type=F32)
        b = jnp.dot(h, wb_ref[:, cs].astype(MXU_DT), preferred_element_type=F32)
        prev, nxt = _shift_rows(a, seq_len, tm)
        a = prev * cw_ref[0:1, cs] + a * cw_ref[1:2, cs] + nxt * cw_ref[2:3, cs] + cb_ref[:, cs]
        gelu = 0.5 * a * (1.0 + jnp.tanh(math.sqrt(2.0 / math.pi) * (a + 0.044715 * (a * a * a))))
        p = jnp.dot((gelu * b).astype(MXU_DT), wd_ref[cs, :].astype(MXU_DT), preferred_element_type=F32)
        part = p if part is None else part + p
    o_ref[...] += part

    @pl.when(j == pl.num_programs(1) - 1)
    def _():
        o_ref[...] = x1_ref[...] + g2_ref[...] * (_rms(o_ref[...], D_MODEL) * gpost_ref[...])


def _mix_ffn(mr, mh, mm, w_out, x, mod, gmix, gpre, w_up, cw, cb, w_down, gpost, l, row_of_tile,
             seq_len, tm, tf):
    n = x.shape[0]
    nf = D_FF // tf
    tile = lambda w: pl.BlockSpec((tm, w), lambda i, j: (i, 0))
    layer = lambda r, c: pl.BlockSpec((None, r, c), lambda i, j: (l, 0, 0))
    mod_row = lambda which: pl.BlockSpec((None, None, None, 1, D_MODEL),
                                         lambda i, j: (l, which, row_of_tile(i), 0, 0))
    return pl.pallas_call(
        functools.partial(_ffn_body, seq_len=seq_len, tm=tm),
        grid=(n // tm, nf),
        in_specs=[tile(HW), tile(HY_W), tile(HW),
                  _resident((None, D_MODEL, D_MODEL), lambda i, j: (l, 0, 0)),
                  tile(D_MODEL), mod_row(2), mod_row(4), mod_row(3),
                  layer(1, D_MODEL), layer(1, D_MODEL),
                  pl.BlockSpec((None, D_MODEL, tf), lambda i, j: (l, 0, j)),
                  pl.BlockSpec((None, D_MODEL, tf), lambda i, j: (l, 0, j + nf)),
                  pl.BlockSpec((None, 8, tf), lambda i, j: (l, 0, j)),
                  pl.BlockSpec((None, 1, tf), lambda i, j: (l, 0, j)),
                  pl.BlockSpec((None, tf, D_MODEL), lambda i, j: (l, j, 0)),
                  mod_row(5), layer(1, D_MODEL)],
        out_specs=pl.BlockSpec((tm, D_MODEL), lambda i, j: (i, 0)),
        out_shape=jax.ShapeDtypeStruct((n, D_MODEL), F32),
        scratch_shapes=[pltpu.VMEM((tm, D_MODEL), F32), pltpu.VMEM((tm, D_MODEL), MXU_DT)],
        compiler_params=_params(),
        name="mix_ffn",
    )(mr, mh, mm, w_out, x, mod, mod, mod, gmix, gpre, w_up, w_up, cw, cb, w_down, mod, gpost)


def _pad_rows8(w):
    return jnp.concatenate([w, jnp.zeros((w.shape[0], 8 - w.shape[1], w.shape[2]), w.dtype)], axis=1)


def kernel(x_prompt, x_sample, c, state_ret, state_mlstm_c, state_mlstm_n, state_mlstm_m, c_ctx,
           norm_mix_pre, norm_mix_post, norm_ffn_pre, norm_ffn_post, w_mod, b_mod, w_in, w_out,
           ret_decay_logit, ret_norm_g, hy_conv_w, hy_conv_b, hy_f_w1, hy_f_b1, hy_f_w2, hy_f_b2,
           hy_f_w3, hy_f_b3, hy_sin_freq, hy_bias, ml_gate_bias, ml_norm_g,
           w_up, ffn_conv_w, ffn_conv_b, w_down):
    BP, LP, _ = x_prompt.shape
    BS, LS, _ = x_sample.shape

    w_in_p = _prep_w_in(jnp.swapaxes(w_in, 1, 2))
    ret_gain = _pad_heads(ret_norm_g).reshape(DEPTH, 1, HWP)
    ml_gain = _pad_heads(ml_norm_g).reshape(DEPTH, 1, HWP)
    dl = jnp.broadcast_to(ret_decay_logit.reshape(DEPTH, 8, 1), (DEPTH, 8, 128))
    gate_bias = jnp.concatenate([ml_gate_bias.reshape(DEPTH, 1, 16), jnp.zeros((DEPTH, 1, 112), F32)], axis=2)
    hy_cw = _pad_rows8(hy_conv_w)
    hy_cb = hy_conv_b.reshape(DEPTH, 1, 3 * HY_W)
    ffn_cw = _pad_rows8(ffn_conv_w)
    ffn_cb = ffn_conv_b.reshape(DEPTH, 1, D_FF)
    g_mix_pre = norm_mix_pre.reshape(DEPTH, 1, D_MODEL)
    g_mix_post = norm_mix_post.reshape(DEPTH, 1, D_MODEL)
    g_ffn_pre = norm_ffn_pre.reshape(DEPTH, 1, D_MODEL)
    g_ffn_post = norm_ffn_post.reshape(DEPTH, 1, D_MODEL)
    pad2 = lambda a, r, cdim: jnp.pad(a, ((0, 0), (0, r - a.shape[1]), (0, cdim - a.shape[2])))
    fw1 = pad2(hy_f_w1, 128, 128)
    fb1 = pad2(hy_f_b1.reshape(DEPTH, 1, -1), 1, 128)
    fw2 = pad2(hy_f_w2, 128, 128)
    fb2 = pad2(hy_f_b2.reshape(DEPTH, 1, -1), 1, 128)
    fw3 = pad2(hy_f_w3, 128, 4 * HY_W)
    fb3 = hy_f_b3.reshape(DEPTH, 1, 4 * HY_W)
    ffr = pad2(hy_sin_freq.reshape(DEPTH, 1, -1), 1, 128)

    m0 = jnp.broadcast_to(state_mlstm_m.reshape(BS, DEPTH, 8, 1), (BS, DEPTH, 8, 128))
    new_ret = jnp.zeros((BP, DEPTH, 2, HEADS, DH, DH), F32)
    new_c = jnp.zeros((BP, DEPTH, 2, HEADS, DH, DH), F32)
    new_n = jnp.zeros((BP, DEPTH, 2, HEADS, DH), F32)

    cvec8 = jnp.concatenate([c_ctx.reshape(1, D_MODEL), c, jnp.zeros((8 - 1 - BS, D_MODEL), F32)], axis=0)
    mod = _mod_all(cvec8, w_mod, b_mod)
    mod = mod.reshape(DEPTH, 8, 6, 1, D_MODEL).transpose(0, 2, 1, 3, 4)
    spectra_p = _filter_spectra(LP, fw1, fb1, fw2, fb2, fw3, fb3, ffr, hy_bias)
    spectra_s = _filter_spectra(LS, fw1, fb1, fw2, fb2, fw3, fb3, ffr, hy_bias)

    xp = x_prompt.reshape(BP * LP, D_MODEL)
    xs = x_sample.reshape(BS * LS, D_MODEL)
    tm = 512
    ffn_tm = 1024
    ffn_tf = 512
    n_ctx = BP * LP
    row_ctx = lambda i: 0
    row_both = lambda i: jnp.where(i < n_ctx // tm, 0, 1 + jnp.maximum(i - n_ctx // tm, 0) // (LS // tm))
    row_lat_ffn = lambda i: 1 + i // (LS // ffn_tm)
    new_m = []

    def mix_and_ffn(x, ph, mr, mm, l, B, L, row0, row_ffn, spectra):
        mh = _hyena(ph, hy_cw, hy_cb, spectra, l, B, L, row0)
        return _mix_ffn(mr, mh, mm, w_out, x, mod, g_mix_post, g_ffn_pre,
                        w_up, ffn_cw, ffn_cb, w_down, g_ffn_post, l, row_ffn, L, ffn_tm, ffn_tf)

    for l in range(DEPTH):
        pr, ph, pm, pg = _in_proj(xp, xs, mod, g_mix_pre, w_in_p, l, row_both, tm)
        mr, new_ret = _retention(pr, dl, ret_gain, l, BP, LP, False, new_ret, 0)
        mm, new_c, new_n, m_fin = _mlstm(pm, pg, gate_bias, ml_gain, l, BP, LP, False, new_c, new_n, 0)
        new_m.append(m_fin)
        xp = mix_and_ffn(xp, ph, mr, mm, l, BP, LP, 0, row_ctx, spectra_p)
        (mr,) = _retention(pr, dl, ret_gain, l, BS, LS, True, state_ret, n_ctx)
        (mm,) = _mlstm(pm, pg, gate_bias, ml_gain, l, BS, LS, True, state_mlstm_c, state_mlstm_n, n_ctx, m0)
        xs = mix_and_ffn(xs, ph, mr, mm, l, BS, LS, n_ctx, row_lat_ffn, spectra_s)

    out_m = jnp.stack(new_m, axis=1)[..., 0].reshape(BP, DEPTH, 2, HEADS)
    return (xp.reshape(BP, LP, D_MODEL), xs.reshape(BS, LS, D_MODEL), new_ret, new_c, new_n, out_m)
```

```python
import functools
import math

import numpy as np
import jax
import jax.numpy as jnp
from jax import lax
from jax.experimental import pallas as pl
from jax.experimental.pallas import tpu as pltpu

D_MODEL = 1024
DEPTH = 4
GRID_W = 64
HEADS = 4
DH = 96
DHP = 128
HW = HEADS * DH
HWP = HEADS * DHP
HY_W = 256
D_FF = 4 * D_MODEL
CHUNK = 128
N_BANDS = 16
FEAT_W = 1 + 2 * N_BANDS
HY_SHIFT = 0.05
HY_TARGET = 1e-2
HY_SHORT_DECAY_PCT = 0.3
HY_LONG_DECAY_PCT = 1.5
ROPE_BASE = 10000.0
EPS = 1e-6

MXU_DT = jnp.bfloat16
F32 = jnp.float32
VMEM_LIMIT = 56 * 1024 * 1024


def _dot(a, b):
    return jnp.dot(a.astype(MXU_DT), b.astype(MXU_DT), preferred_element_type=F32)


def _dot_nt(a, b):
    return lax.dot_general(a.astype(MXU_DT), b.astype(MXU_DT), (((1,), (1,)), ((), ())),
                           preferred_element_type=F32)


def _dot_tn(a, b):
    return lax.dot_general(a.astype(MXU_DT), b.astype(MXU_DT), (((0,), (0,)), ((), ())),
                           preferred_element_type=F32)


def _split2(x):
    hi = x.astype(MXU_DT)
    lo = (x - hi.astype(F32)).astype(MXU_DT)
    return hi, lo


def _dot_split(a_hi, a_lo, b):
    b_hi, b_lo = _split2(b)
    return (jnp.dot(a_hi, b_hi, preferred_element_type=F32)
            + jnp.dot(a_hi, b_lo, preferred_element_type=F32)
            + jnp.dot(a_lo, b_hi, preferred_element_type=F32))


def _dot3(a, b):
    a_hi, a_lo = _split2(a)
    return _dot_split(a_hi, a_lo, b)


def _dot_exact_lhs(t, x):
    x1 = x.astype(MXU_DT)
    r1 = x - x1.astype(F32)
    x2 = r1.astype(MXU_DT)
    x3 = (r1 - x2.astype(F32)).astype(MXU_DT)
    return (jnp.dot(t, x1, preferred_element_type=F32) + jnp.dot(t, x2, preferred_element_type=F32)
            + jnp.dot(t, x3, preferred_element_type=F32))


def _rms(x, n):
    return x * lax.rsqrt(jnp.sum(x * x, axis=-1, keepdims=True) * (1.0 / n) + EPS)


def _chunk_rows(c, size, base=0):
    if isinstance(c, int):
        return slice(base + c * size, base + (c + 1) * size)
    return pl.ds(pl.multiple_of(base + c * size, size), size)


def _log_sigmoid(x):
    return jnp.minimum(x, 0.0) - jnp.log1p(jnp.exp(-jnp.abs(x)))


def _resident(shape, index_map):
    return pl.BlockSpec(shape, index_map, pipeline_mode=pl.Buffered(1))


def _params(**kw):
    return pltpu.CompilerParams(vmem_limit_bytes=VMEM_LIMIT, **kw)


@functools.lru_cache(maxsize=None)
def _dft_tables(L):
    f = np.arange(L, dtype=np.int64)[:, None]
    s = np.arange(L, dtype=np.int64)[None, :]
    ang = np.pi * ((f * s) % (2 * L)).astype(np.float64) / L
    fr = np.cos(ang)
    fi = -np.sin(ang)
    fi[0, :] = np.where(np.arange(L) % 2 == 0, 1.0, -1.0)
    fwd = np.concatenate([fr, fi], axis=0)
    gr = np.cos(ang.T) * (2.0 / (2 * L))
    gr[:, 0] = 1.0 / (2 * L)
    gi = -np.sin(ang.T) * (2.0 / (2 * L))
    gi[:, 0] = np.where(np.arange(L) % 2 == 0, 1.0, -1.0) / (2 * L)
    inv = np.concatenate([gr, gi], axis=1)
    return fwd.astype(np.float32), inv.astype(np.float32)


@functools.lru_cache(maxsize=None)
def _filter_tables(L):
    tn = np.arange(L, dtype=np.float64) / L
    bands = np.linspace(1e-4, N_BANDS - 1, N_BANDS)
    ang = 2.0 * math.pi * tn[:, None] * bands[None, :]
    feat = np.zeros((L, 128), np.float32)
    feat[:, 0] = tn
    feat[:, 1:1 + N_BANDS] = np.cos(ang)
    feat[:, 1 + N_BANDS:FEAT_W] = np.sin(ang)
    deltas = np.abs(np.linspace(math.log(HY_TARGET) / HY_LONG_DECAY_PCT,
                                math.log(HY_TARGET) / HY_SHORT_DECAY_PCT, HY_W))
    window = (np.exp(-tn[:, None] * deltas[None, :]) + HY_SHIFT).astype(np.float32)
    return feat, window


@functools.lru_cache(maxsize=None)
def _rope_tables(L):
    rows = L // GRID_W
    row = np.repeat(np.arange(rows, dtype=np.float64), GRID_W)
    col = np.tile(np.arange(GRID_W, dtype=np.float64), rows)
    half = DH // 2
    n_freq = half // 2
    freqs = ROPE_BASE ** (-np.arange(n_freq, dtype=np.float64) / n_freq)
    ang = np.concatenate([row[:, None] * freqs, col[:, None] * freqs], axis=-1)
    cos = np.zeros((L, DHP), np.float32)
    sin = np.zeros((L, DHP), np.float32)
    cos[:, :half] = np.cos(ang)
    cos[:, 64:64 + half] = np.cos(ang)
    sin[:, :half] = -np.sin(ang)
    sin[:, 64:64 + half] = np.sin(ang)
    return cos, sin


@functools.lru_cache(maxsize=None)
def _tri_tables():
    i = np.arange(CHUNK)
    lower = (i[:, None] >= i[None, :]).astype(np.float32)
    upper = (i[:, None] <= i[None, :]).astype(np.float32)
    return lower, upper


def _pad_heads(w, axis=-1, rope=False):
    axis = axis % w.ndim
    parts = []

    def zeros(n):
        shp = list(w.shape)
        shp[axis] = n
        return jnp.zeros(shp, w.dtype)

    for h in range(HEADS):
        blk = lax.slice_in_dim(w, h * DH, (h + 1) * DH, axis=axis)
        if rope:
            half = DH // 2
            parts += [lax.slice_in_dim(blk, 0, half, axis=axis), zeros(64 - half),
                      lax.slice_in_dim(blk, half, DH, axis=axis), zeros(64 - half)]
        else:
            parts += [blk, zeros(DHP - DH)]
    return jnp.concatenate(parts, axis=axis)


def _unpad_heads_axis(x, axis, rope=False):
    if rope:
        half = DH // 2
        return jnp.concatenate([lax.slice_in_dim(x, 0, half, axis=axis),
                                lax.slice_in_dim(x, 64, 64 + half, axis=axis)], axis=axis)
    return lax.slice_in_dim(x, 0, DH, axis=axis)


def _mod_body(c_ref, w_ref, b_ref, o_ref):
    c = c_ref[...]
    s = c * (1.0 / (1.0 + jnp.exp(-c)))
    o_ref[...] = _dot(s, w_ref[...]) + b_ref[...]


def _mod_all(cvec8, w_mod, b_mod):
    tn = 1536
    nj = 6 * D_MODEL // tn
    return pl.pallas_call(
        _mod_body,
        grid=(DEPTH, nj),
        in_specs=[pl.BlockSpec((8, D_MODEL), lambda l, j: (0, 0)),
                  pl.BlockSpec((None, D_MODEL, tn), lambda l, j: (l, 0, j)),
                  pl.BlockSpec((None, 1, tn), lambda l, j: (l, 0, j))],
        out_specs=pl.BlockSpec((None, 8, tn), lambda l, j: (l, 0, j)),
        out_shape=jax.ShapeDtypeStruct((DEPTH, 8, 6 * D_MODEL), F32),
        compiler_params=_params(),
        name="mod_all",
    )(cvec8, w_mod, b_mod.reshape(DEPTH, 1, 6 * D_MODEL))


def _filt_body(feat_ref, win_ref, f_ref, w1_ref, b1_ref, w2_ref, b2_ref, w3_ref, b3_ref,
               fr_ref, hb_ref, p_ref, q_ref, p2_ref, *, L):
    fr = fr_ref[...]
    h = jnp.sin(fr * (_dot3(feat_ref[...], w1_ref[...]) + b1_ref[...]))
    h = jnp.sin(fr * (_dot3(h, w2_ref[...]) + b2_ref[...]))
    filt = _dot3(h, w3_ref[...]) + b3_ref[...]
    win = win_ref[...]
    row = lax.broadcasted_iota(jnp.int32, (L, 1), 0)
    taps = filt * jnp.concatenate([win, win, win, win], axis=1)
    lane = lax.broadcasted_iota(jnp.int32, (L, 4 * HY_W), 1)
    taps = jnp.where((row == 0) & (lane >= 2 * HY_W), 0.0, taps)
    spec = _dot(f_ref[...], taps)
    for o in range(2):
        a = spec[:, o * HY_W:(o + 1) * HY_W]
        b = spec[:, (2 + o) * HY_W:(3 + o) * HY_W]
        kr = a[:L] + b[:L]
        ki = a[L:] - b[L:]
        nyq = a[L:L + 1] + b[L:L + 1]
        bias = hb_ref[o:o + 1, :]
        p_ref[o] = kr + bias
        q_ref[o] = jnp.where(row == 0, 0.0, ki)
        p2_ref[o] = jnp.where(row == 0, nyq, kr) + bias


def _filter_spectra(L, fw1, fb1, fw2, fb2, fw3, fb3, ffr, hy_bias):
    feat, window = _filter_tables(L)
    fwd_table, _ = _dft_tables(L)
    const = lambda shape: _resident(shape, lambda l: (0,) * len(shape))
    per_layer = lambda shape: pl.BlockSpec((None,) + shape, lambda l: (l,) + (0,) * len(shape))
    out_spec = per_layer((2, L, HY_W))
    out_shape = jax.ShapeDtypeStruct((DEPTH, 2, L, HY_W), F32)
    return pl.pallas_call(
        functools.partial(_filt_body, L=L),
        grid=(DEPTH,),
        in_specs=[const((L, 128)), const((L, HY_W)), const((2 * L, L)),
                  per_layer((128, 128)), per_layer((1, 128)), per_layer((128, 128)), per_layer((1, 128)),
                  per_layer((128, 4 * HY_W)), per_layer((1, 4 * HY_W)), per_layer((1, 128)),
                  per_layer((2, HY_W))],
        out_specs=[out_spec, out_spec, out_spec],
        out_shape=[out_shape, out_shape, out_shape],
        compiler_params=_params(),
        name="hyena_filter_%d" % L,
    )(feat, window, fwd_table, fw1, fb1, fw2, fb2, fw3, fb3, ffr, hy_bias)


SEC_W = 3 * HWP + HW
W_R0 = 0
W_H0 = W_R0 + SEC_W
W_M0 = W_H0 + 3 * HY_W
W_G0 = W_M0 + SEC_W
W_END = W_G0
GATE_LANE0 = DH
IN_W = 8 * HW + 3 * HY_W + 16
FFN_SUB = 256
ROW_SUB = 256
SCAN_UNROLL = 4


def _prep_in_body(wt_ref, o_ref):
    kb = wt_ref.shape[1]
    half = DH // 2

    def zeros(n):
        return jnp.zeros((n, kb), F32)

    def head_tile(src, rope):
        if rope:
            return jnp.concatenate([wt_ref[src:src + half, :], zeros(64 - half),
                                    wt_ref[src + half:src + DH, :], zeros(64 - half)], axis=0)
        return jnp.concatenate([wt_ref[src:src + DH, :], zeros(DHP - DH)], axis=0)

    def dense_tiles(src, width):
        return [wt_ref[src + i * 128:src + (i + 1) * 128, :] for i in range(width // 128)]

    tiles = []
    for sec in range(3):
        tiles += [head_tile(sec * HW + h * DH, sec < 2) for h in range(HEADS)]
    tiles += dense_tiles(3 * HW, HW)
    tiles += dense_tiles(4 * HW, 3 * HY_W)
    m0 = 4 * HW + 3 * HY_W
    g0 = 8 * HW + 3 * HY_W
    for sec in range(3):
        tiles += [head_tile(m0 + sec * HW + h * DH, False) for h in range(HEADS)]
    tiles[-3 * HEADS] = jnp.concatenate([wt_ref[m0:m0 + DH, :], wt_ref[g0:g0 + 16, :], zeros(DHP - DH - 16)], axis=0)
    tiles += dense_tiles(m0 + 3 * HW, HW)
    for t, tile in enumerate(tiles):
        o_ref[:, t * 128:(t + 1) * 128] = tile.T.astype(o_ref.dtype)


def _prep_w_in(w_in_t):
    kb = 256
    return pl.pallas_call(
        _prep_in_body,
        grid=(DEPTH, D_MODEL // kb),
        in_specs=[pl.BlockSpec((None, IN_W, kb), lambda l, i: (l, 0, i))],
        out_specs=pl.BlockSpec((None, kb, W_END), lambda l, i: (l, i, 0)),
        out_shape=jax.ShapeDtypeStruct((DEPTH, D_MODEL, W_END), MXU_DT),
        compiler_params=_params(),
        name="prep_w_in",
    )(w_in_t)


def _in_body(xa_ref, xb_ref, sc_ref, sh_ref, g_ref, w_ref, pr_ref, ph_ref, pm_ref, *, tiles_a):
    x = jnp.where(pl.program_id(0) < tiles_a, xa_ref[...], xb_ref[...])
    h = _rms(x, D_MODEL) * g_ref[...]
    h = (h * (1.0 + sc_ref[...]) + sh_ref[...]).astype(MXU_DT)
    proj = jnp.dot(h, w_ref[...], preferred_element_type=F32)
    pr_ref[...] = proj[:, W_R0:W_H0]
    ph_ref[...] = proj[:, W_H0:W_M0]
    pm_ref[...] = proj[:, W_M0:W_G0]


def _mod_spec(l, which, row_of_tile):
    return pl.BlockSpec((None, None, None, 1, D_MODEL), lambda i: (l, which, row_of_tile(i), 0, 0))


def _in_proj(xa, xb, mod, gain, w_in, l, row_of_tile, tm):
    tiles_a = xa.shape[0] // tm
    n = xa.shape[0] + xb.shape[0]
    shapes = [SEC_W, 3 * HY_W, SEC_W]
    return pl.pallas_call(
        functools.partial(_in_body, tiles_a=tiles_a),
        grid=(n // tm,),
        in_specs=[pl.BlockSpec((tm, D_MODEL), lambda i: (jnp.minimum(i, tiles_a - 1), 0)),
                  pl.BlockSpec((tm, D_MODEL), lambda i: (jnp.maximum(i - tiles_a, 0), 0)),
                  _mod_spec(l, 1, row_of_tile), _mod_spec(l, 0, row_of_tile),
                  pl.BlockSpec((None, 1, D_MODEL), lambda i: (l, 0, 0)),
                  _resident((None, D_MODEL, W_END), lambda i: (l, 0, 0))],
        out_specs=[pl.BlockSpec((tm, w), lambda i: (i, 0)) for w in shapes],
        out_shape=[jax.ShapeDtypeStruct((n, w), F32) for w in shapes],
        compiler_params=_params(),
        name="in_proj",
    )(xa, xb, mod, mod, gain, w_in)


def _ret_body(*refs, L, latent, nb):
    if latent:
        (q_ref, k_ref, v_ref, gate_ref, dl_ref, gain_ref, s0_ref, cos_ref, sin_ref,
         o_ref, qs, ks, acc, sst) = refs
    else:
        (q_ref, k_ref, v_ref, gate_ref, dl_ref, gain_ref, _, o_ref, sfin_ref, qs, ks, acc, sst) = refs
    half = DH // 2
    C = CHUNK
    nchunk = L // C
    lg = _log_sigmoid(dl_ref[...])
    rel = (lax.broadcasted_iota(jnp.int32, (C, C), 0) - lax.broadcasted_iota(jnp.int32, (C, C), 1)).astype(F32)
    ri = lax.broadcasted_iota(jnp.int32, (C, 1), 0).astype(F32)

    for bi in range(nb):
        seq = slice(bi * L, (bi + 1) * L)
        for h in range(HEADS):
            cols = slice(h * DHP, (h + 1) * DHP)
            q = q_ref[seq, cols]
            k = k_ref[seq, cols]
            if latent:
                cos = cos_ref[...]
                sin = sin_ref[...]
                q = q * cos + pltpu.roll(q, 64, 1) * sin
                k = k * cos + pltpu.roll(k, 64, 1) * sin
            qs[seq, cols] = q
            ks[seq, cols] = k * (DH ** -0.5)
            for d in range(2):
                sst[bi, d, h] = jnp.zeros((DHP, DHP), F32)
                if latent:
                    sst[bi, d, h, 0:half, 0:DH] = s0_ref[bi, d, h, 0:half, :]
                    sst[bi, d, h, 64:64 + half, 0:DH] = s0_ref[bi, d, h, half:DH, :]

    acc[...] = jnp.zeros_like(acc)
    consts = []
    for h in range(HEADS):
        lgf = lg[h:h + 1, :]
        lgb = lg[HEADS + h:HEADS + h + 1, :]
        lgf1 = lgf[:, 0:1]
        lgb1 = lgb[:, 0:1]
        mask = (jnp.where(rel >= 0, jnp.exp(rel * lgf), 0.0)
                + jnp.where(rel <= 0, jnp.exp(-rel * lgb), 0.0))
        consts.append(dict(
            mask=mask,
            qdec_f=jnp.exp((ri + 1.0) * lgf1), qdec_b=jnp.exp((C - ri) * lgb1),
            kdec_f=jnp.exp((C - 1.0 - ri) * lgf1), kdec_b=jnp.exp(ri * lgb1),
            cdec_f=jnp.exp(C * lgf1), cdec_b=jnp.exp(C * lgb1)))

    def step(j):
        for bi in range(nb):
            rows_f = _chunk_rows(j, C, bi * L)
            rows_b = _chunk_rows(nchunk - 1 - j, C, bi * L)
            for h in range(HEADS):
                cols = slice(h * DHP, (h + 1) * DHP)
                cn = consts[h]
                q = qs[rows_f, cols]
                k = ks[rows_f, cols]
                v = v_ref[rows_f, cols]
                s_f = sst[bi, 0, h]
                sc = _dot_nt(q, k) * cn["mask"]
                acc[rows_f, cols] += _dot(sc, v) + _dot(q, s_f) * cn["qdec_f"]
                sst[bi, 0, h] = s_f * cn["cdec_f"] + _dot_tn(k * cn["kdec_f"], v)
                q = qs[rows_b, cols]
                k = ks[rows_b, cols]
                v = v_ref[rows_b, cols]
                s_b = sst[bi, 1, h]
                acc[rows_b, cols] += _dot(q, s_b) * cn["qdec_b"]
                sst[bi, 1, h] = s_b * cn["cdec_b"] + _dot_tn(k * cn["kdec_b"], v)

    if nchunk <= 2:
        for j in range(nchunk):
            step(j)
    else:
        pl.loop(0, nchunk, unroll=SCAN_UNROLL)(step)

    if not latent:
        for bi in range(nb):
            for d in range(2):
                for h in range(HEADS):
                    sfin_ref[bi, d, h, 0:half, :] = sst[bi, d, h, 0:half, 0:DH]
                    sfin_ref[bi, d, h, half:DH, :] = sst[bi, d, h, 64:64 + half, 0:DH]

    y = jnp.concatenate([(_rms(acc[:, h * DHP:(h + 1) * DHP], DH) * gain_ref[:, h * DHP:(h + 1) * DHP])[:, :DH]
                         for h in range(HEADS)], axis=1)
    g = gate_ref[...]
    o_ref[...] = (y * (g * (1.0 / (1.0 + jnp.exp(-g))))).astype(o_ref.dtype)


def _state_spec(l, nb, *tail):
    shape = (nb, None, 2, HEADS) + tail
    return pl.BlockSpec(shape, lambda b: (b, l) + (0,) * (len(shape) - 2))


def _seqs_per_step(B, L):
    return min(B, max(1, 512 // L))


def _retention(pr, dl, gain, l, B, L, latent, state, row0):
    nb = _seqs_per_step(B, L)
    off = row0 // (nb * L)
    blk = lambda j: pl.BlockSpec((nb * L, HWP), lambda b, j=j: (b + off, j))
    gate_blk = pl.BlockSpec((nb * L, HW), lambda b: (b + off, 3 * HWP // HW))
    in_specs = [blk(0), blk(1), blk(2), gate_blk,
                pl.BlockSpec((None, 8, 128), lambda b: (l, 0, 0)),
                pl.BlockSpec((None, 1, HWP), lambda b: (l, 0, 0))]
    args = [pr, pr, pr, pr, dl, gain]
    out_specs = [pl.BlockSpec((nb * L, HW), lambda b: (b, 0))]
    out_shape = [jax.ShapeDtypeStruct((B * L, HW), MXU_DT)]
    aliases = {}
    if latent:
        cos, sin = _rope_tables(L)
        in_specs += [_state_spec(l, nb, DH, DH),
                     pl.BlockSpec((L, DHP), lambda b: (0, 0)), pl.BlockSpec((L, DHP), lambda b: (0, 0))]
        args += [state, cos, sin]
    else:
        in_specs.append(pl.BlockSpec(memory_space=pl.ANY))
        args.append(state)
        aliases = {len(args) - 1: 1}
        out_specs.append(_state_spec(l, nb, DH, DH))
        out_shape.append(jax.ShapeDtypeStruct(state.shape, F32))
    return pl.pallas_call(
        functools.partial(_ret_body, L=L, latent=latent, nb=nb),
        grid=(B // nb,),
        in_specs=in_specs, out_specs=out_specs, out_shape=out_shape,
        input_output_aliases=aliases,
        scratch_shapes=[pltpu.VMEM((nb * L, HWP), F32), pltpu.VMEM((nb * L, HWP), F32),
                        pltpu.VMEM((nb * L, HWP), F32), pltpu.VMEM((nb, 2, HEADS, DHP, DHP), F32)],
        compiler_params=_params(),
        name="retention_%d" % L,
    )(*args)


def _ml_body(*refs, L, latent, nb):
    if latent:
        (q_ref, k_ref, v_ref, og_ref, gb_ref, gain_ref, tl_ref, tu_ref, c0_ref, n0_ref, m0_ref,
         o_ref, acc, bb_s, ub_s, pm_s, brow_s, ibrow_s, mc_s, mx_s, cn_s) = refs
    else:
        (q_ref, k_ref, v_ref, og_ref, gb_ref, gain_ref, tl_ref, tu_ref, _, _,
         o_ref, cfin_ref, nfin_ref, mfin_ref, acc, bb_s, ub_s, pm_s, brow_s, ibrow_s, mc_s, mx_s, cn_s) = refs
    C = CHUNK
    nchunk = L // C
    lane = lax.broadcasted_iota(jnp.int32, (C, DHP), 1)
    tri_r = lax.broadcasted_iota(jnp.int32, (C, C), 0)
    tri_c = lax.broadcasted_iota(jnp.int32, (C, C), 1)
    tl = tl_ref[...]
    tu = tu_ref[...]

    row_i = lax.broadcasted_iota(jnp.int32, (C, 128), 0)

    def cummax_rows(x, suffix):
        s = 1
        while s < C:
            if suffix:
                x = jnp.where(row_i < C - s, jnp.maximum(x, pltpu.roll(x, C - s, 0)), x)
            else:
                x = jnp.where(row_i >= s, jnp.maximum(x, pltpu.roll(x, s, 0)), x)
            s *= 2
        return x

    for c in range(nb * nchunk):
        rows = slice(c * C, (c + 1) * C)
        g = jnp.where(lane < 16, pltpu.roll(q_ref[rows, 0:DHP], DHP - GATE_LANE0, 1), 0.0) + gb_ref[...]
        lf = _log_sigmoid(g)
        pre = _dot_exact_lhs(tl, lf)
        suf = _dot_exact_lhs(tu, lf)
        bc = jnp.where(lane < 8, pre, suf)
        brow_s[c * 16:(c + 1) * 16, :] = bc.T[0:16, :]
        ibrow_s[c * 16:(c + 1) * 16, :] = g.T[0:16, :]
        for d in range(2):
            for h in range(HEADS):
                sr = d * HEADS + h
                b_b = jnp.broadcast_to(bc[:, d * 8 + 4 + h:d * 8 + 5 + h], (C, 128))
                u_b = jnp.broadcast_to(g[:, d * 8 + h:d * 8 + h + 1], (C, 128)) - b_b
                bb_s[sr, rows, :] = b_b
                ub_s[sr, rows, :] = u_b
                pm_s[sr, rows, :] = cummax_rows(u_b, suffix=(d == 1))

    for bi in range(nb):
        for d in range(2):
            for h in range(HEADS):
                sr = d * HEADS + h
                m = m0_ref[bi, sr:sr + 1, :] if latent else jnp.zeros((1, 128), F32)
                for j in range(nchunk):
                    c = bi * nchunk + (j if d == 0 else nchunk - 1 - j)
                    last = c * C + (C - 1 if d == 0 else 0)
                    mx = jnp.maximum(m, pm_s[sr, last:last + 1, :])
                    mc_s[c * 8 + sr:c * 8 + sr + 1, :] = m
                    mx_s[c * 8 + sr:c * 8 + sr + 1, :] = mx
                    m = bb_s[sr, last:last + 1, :] + mx
                if not latent:
                    mfin_ref[bi, sr:sr + 1, :] = m

    acc[...] = jnp.zeros_like(acc)
    cn_s[...] = jnp.zeros_like(cn_s)
    if latent:
        for bi in range(nb):
            for d in range(2):
                for h in range(HEADS):
                    cn_s[bi, d, h, 0:DH, 0:DH] = c0_ref[bi, d, h]
                    n_row = jnp.concatenate([n0_ref[bi, d, h:h + 1, :], jnp.zeros((1, DHP - DH), F32)], axis=1)
                    cn_s[bi, d, h, :, DHP:] = jnp.broadcast_to(n_row, (DHP, DHP)).T
    ones_blk = jnp.ones((C, DHP), MXU_DT)

    def chain(bi, d, h, c):
        rows = _chunk_rows(c, C)
        brs = brow_s[_chunk_rows(c, 16), :]
        ibrs = ibrow_s[_chunk_rows(c, 16), :]
        mcs = mc_s[_chunk_rows(c, 8), :]
        mxs = mx_s[_chunk_rows(c, 8), :]
        causal = (tri_r >= tri_c) if d == 0 else (tri_r <= tri_c)
        cols = slice(h * DHP, (h + 1) * DHP)
        ic = d * 8 + h
        fc = d * 8 + 4 + h
        sr = d * HEADS + h
        m_c = mcs[sr:sr + 1, :]
        mx = mxs[sr:sr + 1, :]
        q = q_ref[rows, cols].astype(MXU_DT)
        k = k_ref[rows, cols] * (DH ** -0.5)
        v1 = jnp.concatenate([v_ref[rows, cols].astype(MXU_DT), ones_blk], axis=1)
        cn = cn_s[bi, d, h]
        urow = ibrs[ic:ic + 1, :] - brs[fc:fc + 1, :]
        mb = jnp.maximum(m_c, pm_s[sr, rows, :])
        w_intra = jnp.exp(jnp.where(causal, urow - mb, -jnp.inf))
        w_inter = jnp.exp(m_c - mb)
        s = _dot_nt(q, k) * w_intra
        tot = (jnp.dot(s.astype(MXU_DT), v1, preferred_element_type=F32)
               + jnp.concatenate([w_inter, w_inter], axis=1) * _dot(q, cn))
        den = tot[:, DHP:]
        floor = jnp.exp(-(bb_s[sr, rows, :] + mb))
        acc[rows, cols] += tot[:, :DHP] * (1.0 / jnp.maximum(jnp.abs(den), floor))
        kw = k * jnp.exp(ub_s[sr, rows, :] - mx)
        w_prev = jnp.exp(m_c - mx)
        cn_s[bi, d, h] = jnp.concatenate([w_prev, w_prev], axis=1) * cn + _dot_tn(kw, v1)

    def step(j):
        for bi in range(nb):
            for d in range(2):
                c = bi * nchunk + (j if d == 0 else nchunk - 1 - j)
                for h in range(HEADS):
                    chain(bi, d, h, c)

    if nchunk <= 2:
        for j in range(nchunk):
            step(j)
    else:
        pl.loop(0, nchunk, unroll=SCAN_UNROLL)(step)

    if not latent:
        for bi in range(nb):
            for d in range(2):
                for h in range(HEADS):
                    cfin_ref[bi, d, h] = cn_s[bi, d, h, 0:DH, 0:DH]
                    nfin_ref[bi, d, h:h + 1, :] = cn_s[bi, d, h, :, DHP:].T[0:1, 0:DH]

    y = jnp.concatenate([(_rms(acc[:, h * DHP:(h + 1) * DHP], DH) * gain_ref[:, h * DHP:(h + 1) * DHP])[:, :DH]
                         for h in range(HEADS)], axis=1)
    o_ref[...] = (y * (1.0 / (1.0 + jnp.exp(-og_ref[...])))).astype(o_ref.dtype)


def _mlstm(pm, gb, gain, l, B, L, latent, c_state, n_state, row0, m0=None):
    tl, tu = _tri_tables()
    tl = tl.astype(MXU_DT)
    tu = tu.astype(MXU_DT)
    nb = _seqs_per_step(B, L)
    off = row0 // (nb * L)
    blk = lambda j: pl.BlockSpec((nb * L, HWP), lambda b, j=j: (b + off, j))
    gate_blk = pl.BlockSpec((nb * L, HW), lambda b: (b + off, 3 * HWP // HW))
    in_specs = [blk(0), blk(1), blk(2), gate_blk,
                pl.BlockSpec((None, 1, 128), lambda b: (l, 0, 0)),
                pl.BlockSpec((None, 1, HWP), lambda b: (l, 0, 0)),
                pl.BlockSpec((CHUNK, CHUNK), lambda b: (0, 0)),
                pl.BlockSpec((CHUNK, CHUNK), lambda b: (0, 0))]
    args = [pm, pm, pm, pm, gb, gain, tl, tu]
    out_specs = [pl.BlockSpec((nb * L, HW), lambda b: (b, 0))]
    out_shape = [jax.ShapeDtypeStruct((B * L, HW), MXU_DT)]
    aliases = {}
    if latent:
        in_specs += [_state_spec(l, nb, DH, DH), _state_spec(l, nb, DH),
                     pl.BlockSpec((nb, None, 8, 128), lambda b: (b, l, 0, 0))]
        args += [c_state, n_state, m0]
    else:
        in_specs += [pl.BlockSpec(memory_space=pl.ANY), pl.BlockSpec(memory_space=pl.ANY)]
        args += [c_state, n_state]
        aliases = {len(args) - 2: 1, len(args) - 1: 2}
        out_specs += [_state_spec(l, nb, DH, DH), _state_spec(l, nb, DH),
                      pl.BlockSpec((nb, 8, 128), lambda b: (b, 0, 0))]
        out_shape += [jax.ShapeDtypeStruct(c_state.shape, F32), jax.ShapeDtypeStruct(n_state.shape, F32),
                      jax.ShapeDtypeStruct((B, 8, 128), F32)]
    nchunk = L // CHUNK
    return pl.pallas_call(
        functools.partial(_ml_body, L=L, latent=latent, nb=nb),
        grid=(B // nb,),
        in_specs=in_specs, out_specs=out_specs, out_shape=out_shape,
        input_output_aliases=aliases,
        scratch_shapes=[pltpu.VMEM((nb * L, HWP), F32),
                        pltpu.VMEM((8, nb * L, 128), F32), pltpu.VMEM((8, nb * L, 128), F32),
                        pltpu.VMEM((8, nb * L, 128), F32),
                        pltpu.VMEM((nb * nchunk * 16, 128), F32), pltpu.VMEM((nb * nchunk * 16, 128), F32),
                        pltpu.VMEM((nb * nchunk * 8, 128), F32), pltpu.VMEM((nb * nchunk * 8, 128), F32),
                        pltpu.VMEM((nb, 2, HEADS, DHP, 2 * DHP), F32)],
        compiler_params=_params(),
        name="mlstm_%d" % L,
    )(*args)


def _shift_rows(x, seq_len, n_rows):
    pos = lax.broadcasted_iota(jnp.int32, (n_rows, 1), 0) & (seq_len - 1)
    prev = jnp.where(pos == 0, 0.0, pltpu.roll(x, 1, 0))
    nxt = jnp.where(pos == seq_len - 1, 0.0, pltpu.roll(x, n_rows - 1, 0))
    return prev, nxt


def _hy_body(p_ref, cw_ref, cb_ref, f_ref, g_ref, ps_ref, qs_ref, p2s_ref, o_ref, fh_ref, gh_ref, *, L):
    @pl.when(pl.program_id(0) == 0)
    def _():
        fh_ref[...] = f_ref[...].astype(MXU_DT)
        gh_ref[...] = g_ref[...].astype(MXU_DT)

    def conv(z, o):
        spec = _dot(fh_ref[...], z)
        xr = spec[:L]
        xi = spec[L:]
        p = ps_ref[o]
        q = qs_ref[o]
        yr = xr * p - xi * q
        yi = xr * q + xi * p2s_ref[o]
        return _dot(gh_ref[...], jnp.concatenate([yr, yi], axis=0))

    for i in range(p_ref.shape[0] // L):
        rows = slice(i * L, (i + 1) * L)
        x = p_ref[rows, :]
        prev, nxt = _shift_rows(x, L, L)
        hy = prev * cw_ref[0:1, :] + x * cw_ref[1:2, :] + nxt * cw_ref[2:3, :] + cb_ref[...]
        hv = hy[:, 0:HY_W]
        hx1 = hy[:, HY_W:2 * HY_W]
        hx2 = hy[:, 2 * HY_W:3 * HY_W]
        z = hx1 * conv(hv, 0)
        o_ref[rows, :] = (hx2 * conv(z, 1)).astype(o_ref.dtype)


def _hyena(ph, cw, cb, spectra, l, B, L, row0):
    fwd_table, inv_table = _dft_tables(L)
    ps, qs, p2s = spectra
    const = lambda shape: _resident(shape, lambda b: (0,) * len(shape))
    spec_blk = _resident((None, 2, L, HY_W), lambda b: (l, 0, 0, 0))
    nb = min(B, max(1, 2048 // L))
    off = row0 // (nb * L)
    return pl.pallas_call(
        functools.partial(_hy_body, L=L),
        grid=(B // nb,),
        in_specs=[pl.BlockSpec((nb * L, 3 * HY_W), lambda b: (b + off, 0)),
                  pl.BlockSpec((None, 8, 3 * HY_W), lambda b: (l, 0, 0)),
                  pl.BlockSpec((None, 1, 3 * HY_W), lambda b: (l, 0, 0)),
                  const((2 * L, L)), const((L, 2 * L)),
                  spec_blk, spec_blk, spec_blk],
        out_specs=pl.BlockSpec((nb * L, HY_W), lambda b: (b, 0)),
        out_shape=jax.ShapeDtypeStruct((B * L, HY_W), MXU_DT),
        scratch_shapes=[pltpu.VMEM((2 * L, L), MXU_DT), pltpu.VMEM((L, 2 * L), MXU_DT)],
        compiler_params=_params(),
        name="hyena_%d" % L,
    )(ph, cw, cb, fwd_table, inv_table, ps, qs, p2s)


def _out_body(mr_ref, mh_ref, mm_ref, w_ref, x_ref, g1_ref, sc2_ref, sh2_ref,
              gpost_ref, gpre_ref, x1_ref, h2_ref):
    w = w_ref[...].astype(MXU_DT)
    for r in range(x_ref.shape[0] // ROW_SUB):
        rs = slice(r * ROW_SUB, (r + 1) * ROW_SUB)
        mixed = jnp.concatenate([mr_ref[rs, :], mh_ref[rs, :], mm_ref[rs, :]], axis=1)
        mix = jnp.dot(mixed, w, preferred_element_type=F32)
        x1 = x_ref[rs, :] + g1_ref[...] * (_rms(mix, D_MODEL) * gpost_ref[...])
        x1_ref[rs, :] = x1
        h2 = _rms(x1, D_MODEL) * gpre_ref[...] * (1.0 + sc2_ref[...]) + sh2_ref[...]
        h2_ref[rs, :] = h2.astype(h2_ref.dtype)


def _ffn_body(mr_ref, mh_ref, mm_ref, wo_ref, x_ref, g1_ref, sc2_ref, sh2_ref, gmix_ref, gpre_ref,
              wa_ref, wb_ref, cw_ref, cb_ref, wd_ref, g2_ref, gpost_ref, o_ref, x1_ref, h_ref, *, seq_len, tm):
    j = pl.program_id(1)

    @pl.when(j == 0)
    def _():
        _out_body(mr_ref, mh_ref, mm_ref, wo_ref, x_ref, g1_ref, sc2_ref, sh2_ref,
                  gmix_ref, gpre_ref, x1_ref, h_ref)
        o_ref[...] = jnp.zeros_like(o_ref)

    h = h_ref[...]
    part = None
    for s in range(wa_ref.shape[1] // FFN_SUB):
        cs = slice(s * FFN_SUB, (s + 1) * FFN_SUB)
        a = jnp.dot(h, wa_ref[:, cs].astype(MXU_DT), preferred_element_type=F32)
        b = jnp.dot(h, wb_ref[:, cs].astype(MXU_DT), preferred_element_type=F32)
        prev, nxt = _shift_rows(a, seq_len, tm)
        a = prev * cw_ref[0:1, cs] + a * cw_ref[1:2, cs] + nxt * cw_ref[2:3, cs] + cb_ref[:, cs]
        gelu = 0.5 * a * (1.0 + jnp.tanh(math.sqrt(2.0 / math.pi) * (a + 0.044715 * (a * a * a))))
        p = jnp.dot((gelu * b).astype(MXU_DT), wd_ref[cs, :].astype(MXU_DT), preferred_element_type=F32)
        part = p if part is None else part + p
    o_ref[...] += part

    @pl.when(j == pl.num_programs(1) - 1)
    def _():
        o_ref[...] = x1_ref[...] + g2_ref[...] * (_rms(o_ref[...], D_MODEL) * gpost_ref[...])


def _mix_ffn(mr, mh, mm, w_out, x, mod, gmix, gpre, w_up, cw, cb, w_down, gpost, l, row_of_tile,
             seq_len, tm, tf):
    n = x.shape[0]
    nf = D_FF // tf
    tile = lambda w: pl.BlockSpec((tm, w), lambda i, j: (i, 0))
    layer = lambda r, c: pl.BlockSpec((None, r, c), lambda i, j: (l, 0, 0))
    mod_row = lambda which: pl.BlockSpec((None, None, None, 1, D_MODEL),
                                         lambda i, j: (l, which, row_of_tile(i), 0, 0))
    return pl.pallas_call(
        functools.partial(_ffn_body, seq_len=seq_len, tm=tm),
        grid=(n // tm, nf),
        in_specs=[tile(HW), tile(HY_W), tile(HW),
                  _resident((None, D_MODEL, D_MODEL), lambda i, j: (l, 0, 0)),
                  tile(D_MODEL), mod_row(2), mod_row(4), mod_row(3),
                  layer(1, D_MODEL), layer(1, D_MODEL),
                  pl.BlockSpec((None, D_MODEL, tf), lambda i, j: (l, 0, j)),
                  pl.BlockSpec((None, D_MODEL, tf), lambda i, j: (l, 0, j + nf)),
                  pl.BlockSpec((None, 8, tf), lambda i, j: (l, 0, j)),
                  pl.BlockSpec((None, 1, tf), lambda i, j: (l, 0, j)),
                  pl.BlockSpec((None, tf, D_MODEL), lambda i, j: (l, j, 0)),
                  mod_row(5), layer(1, D_MODEL)],
        out_specs=pl.BlockSpec((tm, D_MODEL), lambda i, j: (i, 0)),
        out_shape=jax.ShapeDtypeStruct((n, D_MODEL), F32),
        scratch_shapes=[pltpu.VMEM((tm, D_MODEL), F32), pltpu.VMEM((tm, D_MODEL), MXU_DT)],
        compiler_params=_params(),
        name="mix_ffn",
    )(mr, mh, mm, w_out, x, mod, mod, mod, gmix, gpre, w_up, w_up, cw, cb, w_down, mod, gpost)


def _pad_rows8(w):
    return jnp.concatenate([w, jnp.zeros((w.shape[0], 8 - w.shape[1], w.shape[2]), w.dtype)], axis=1)


def kernel(x_prompt, x_sample, c, state_ret, state_mlstm_c, state_mlstm_n, state_mlstm_m, c_ctx,
           norm_mix_pre, norm_mix_post, norm_ffn_pre, norm_ffn_post, w_mod, b_mod, w_in, w_out,
           ret_decay_logit, ret_norm_g, hy_conv_w, hy_conv_b, hy_f_w1, hy_f_b1, hy_f_w2, hy_f_b2,
           hy_f_w3, hy_f_b3, hy_sin_freq, hy_bias, ml_gate_bias, ml_norm_g,
           w_up, ffn_conv_w, ffn_conv_b, w_down):
    BP, LP, _ = x_prompt.shape
    BS, LS, _ = x_sample.shape

    w_in_p = _prep_w_in(jnp.swapaxes(w_in, 1, 2))
    ret_gain = _pad_heads(ret_norm_g).reshape(DEPTH, 1, HWP)
    ml_gain = _pad_heads(ml_norm_g).reshape(DEPTH, 1, HWP)
    dl = jnp.broadcast_to(ret_decay_logit.reshape(DEPTH, 8, 1), (DEPTH, 8, 128))
    gate_bias = jnp.concatenate([ml_gate_bias.reshape(DEPTH, 1, 16), jnp.zeros((DEPTH, 1, 112), F32)], axis=2)
    hy_cw = _pad_rows8(hy_conv_w)
    hy_cb = hy_conv_b.reshape(DEPTH, 1, 3 * HY_W)
    ffn_cw = _pad_rows8(ffn_conv_w)
    ffn_cb = ffn_conv_b.reshape(DEPTH, 1, D_FF)
    g_mix_pre = norm_mix_pre.reshape(DEPTH, 1, D_MODEL)
    g_mix_post = norm_mix_post.reshape(DEPTH, 1, D_MODEL)
    g_ffn_pre = norm_ffn_pre.reshape(DEPTH, 1, D_MODEL)
    g_ffn_post = norm_ffn_post.reshape(DEPTH, 1, D_MODEL)
    pad2 = lambda a, r, cdim: jnp.pad(a, ((0, 0), (0, r - a.shape[1]), (0, cdim - a.shape[2])))
    fw1 = pad2(hy_f_w1, 128, 128)
    fb1 = pad2(hy_f_b1.reshape(DEPTH, 1, -1), 1, 128)
    fw2 = pad2(hy_f_w2, 128, 128)
    fb2 = pad2(hy_f_b2.reshape(DEPTH, 1, -1), 1, 128)
    fw3 = pad2(hy_f_w3, 128, 4 * HY_W)
    fb3 = hy_f_b3.reshape(DEPTH, 1, 4 * HY_W)
    ffr = pad2(hy_sin_freq.reshape(DEPTH, 1, -1), 1, 128)

    m0 = jnp.broadcast_to(state_mlstm_m.reshape(BS, DEPTH, 8, 1), (BS, DEPTH, 8, 128))
    new_ret = jnp.zeros((BP, DEPTH, 2, HEADS, DH, DH), F32)
    new_c = jnp.zeros((BP, DEPTH, 2, HEADS, DH, DH), F32)
    new_n = jnp.zeros((BP, DEPTH, 2, HEADS, DH), F32)

    cvec8 = jnp.concatenate([c_ctx.reshape(1, D_MODEL), c, jnp.zeros((8 - 1 - BS, D_MODEL), F32)], axis=0)
    mod = _mod_all(cvec8, w_mod, b_mod)
    mod = mod.reshape(DEPTH, 8, 6, 1, D_MODEL).transpose(0, 2, 1, 3, 4)
    spectra_p = _filter_spectra(LP, fw1, fb1, fw2, fb2, fw3, fb3, ffr, hy_bias)
    spectra_s = _filter_spectra(LS, fw1, fb1, fw2, fb2, fw3, fb3, ffr, hy_bias)

    xp = x_prompt.reshape(BP * LP, D_MODEL)
    xs = x_sample.reshape(BS * LS, D_MODEL)
    tm = 512
    ffn_tm = 1024
    ffn_tf = 512
    n_ctx = BP * LP
    row_ctx = lambda i: 0
    row_both = lambda i: jnp.where(i < n_ctx // tm, 0, 1 + jnp.maximum(i - n_ctx // tm, 0) // (LS // tm))
    row_lat_ffn = lambda i: 1 + i // (LS // ffn_tm)
    new_m = []

    def mix_and_ffn(x, ph, mr, mm, l, B, L, row0, row_ffn, spectra):
        mh = _hyena(ph, hy_cw, hy_cb, spectra, l, B, L, row0)
        return _mix_ffn(mr, mh, mm, w_out, x, mod, g_mix_post, g_ffn_pre,
                        w_up, ffn_cw, ffn_cb, w_down, g_ffn_post, l, row_ffn, L, ffn_tm, ffn_tf)

    for l in range(DEPTH):
        pr, ph, pm = _in_proj(xp, xs, mod, g_mix_pre, w_in_p, l, row_both, tm)
        mr, new_ret = _retention(pr, dl, ret_gain, l, BP, LP, False, new_ret, 0)
        mm, new_c, new_n, m_fin = _mlstm(pm, gate_bias, ml_gain, l, BP, LP, False, new_c, new_n, 0)
        new_m.append(m_fin)
        xp = mix_and_ffn(xp, ph, mr, mm, l, BP, LP, 0, row_ctx, spectra_p)
        (mr,) = _retention(pr, dl, ret_gain, l, BS, LS, True, state_ret, n_ctx)
        (mm,) = _mlstm(pm, gate_bias, ml_gain, l, BS, LS, True, state_mlstm_c, state_mlstm_n, n_ctx, m0)
        xs = mix_and_ffn(xs, ph, mr, mm, l, BS, LS, n_ctx, row_lat_ffn, spectra_s)

    out_m = jnp.stack(new_m, axis=1)[..., 0].reshape(BP, DEPTH, 2, HEADS)
    return (xp.reshape(BP, LP, D_MODEL), xs.reshape(BS, LS, D_MODEL), new_ret, new_c, new_n, out_m)
```

```python
import functools
import math

import numpy as np
import jax
import jax.numpy as jnp
from jax import lax
from jax.experimental import pallas as pl
from jax.experimental.pallas import tpu as pltpu

D_MODEL = 1024
DEPTH = 4
GRID_W = 64
HEADS = 4
DH = 96
DHP = 128
HW = HEADS * DH
HWP = HEADS * DHP
HY_W = 256
D_FF = 4 * D_MODEL
CHUNK = 128
N_BANDS = 16
FEAT_W = 1 + 2 * N_BANDS
HY_SHIFT = 0.05
HY_TARGET = 1e-2
HY_SHORT_DECAY_PCT = 0.3
HY_LONG_DECAY_PCT = 1.5
ROPE_BASE = 10000.0
EPS = 1e-6

MXU_DT = jnp.bfloat16
F32 = jnp.float32
VMEM_LIMIT = 56 * 1024 * 1024


def _dot(a, b):
    return jnp.dot(a.astype(MXU_DT), b.astype(MXU_DT), preferred_element_type=F32)


def _dot_nt(a, b):
    return lax.dot_general(a.astype(MXU_DT), b.astype(MXU_DT), (((1,), (1,)), ((), ())),
                           preferred_element_type=F32)


def _dot_tn(a, b):
    return lax.dot_general(a.astype(MXU_DT), b.astype(MXU_DT), (((0,), (0,)), ((), ())),
                           preferred_element_type=F32)


def _split2(x):
    hi = x.astype(MXU_DT)
    lo = (x - hi.astype(F32)).astype(MXU_DT)
    return hi, lo


def _dot_split(a_hi, a_lo, b):
    b_hi, b_lo = _split2(b)
    return (jnp.dot(a_hi, b_hi, preferred_element_type=F32)
            + jnp.dot(a_hi, b_lo, preferred_element_type=F32)
            + jnp.dot(a_lo, b_hi, preferred_element_type=F32))


def _dot3(a, b):
    a_hi, a_lo = _split2(a)
    return _dot_split(a_hi, a_lo, b)


def _dot_exact_lhs(t, x):
    x1 = x.astype(MXU_DT)
    r1 = x - x1.astype(F32)
    x2 = r1.astype(MXU_DT)
    x3 = (r1 - x2.astype(F32)).astype(MXU_DT)
    return (jnp.dot(t, x1, preferred_element_type=F32) + jnp.dot(t, x2, preferred_element_type=F32)
            + jnp.dot(t, x3, preferred_element_type=F32))


def _rms(x, n):
    return x * lax.rsqrt(jnp.sum(x * x, axis=-1, keepdims=True) * (1.0 / n) + EPS)


def _chunk_rows(c, size, base=0):
    if isinstance(c, int):
        return slice(base + c * size, base + (c + 1) * size)
    return pl.ds(pl.multiple_of(base + c * size, size), size)


def _log_sigmoid(x):
    return jnp.minimum(x, 0.0) - jnp.log1p(jnp.exp(-jnp.abs(x)))


def _resident(shape, index_map):
    return pl.BlockSpec(shape, index_map, pipeline_mode=pl.Buffered(1))


def _params(**kw):
    return pltpu.CompilerParams(vmem_limit_bytes=VMEM_LIMIT, **kw)


@functools.lru_cache(maxsize=None)
def _dft_tables(L):
    f = np.arange(L, dtype=np.int64)[:, None]
    s = np.arange(L, dtype=np.int64)[None, :]
    ang = np.pi * ((f * s) % (2 * L)).astype(np.float64) / L
    fr = np.cos(ang)
    fi = -np.sin(ang)
    fi[0, :] = np.where(np.arange(L) % 2 == 0, 1.0, -1.0)
    fwd = np.concatenate([fr, fi], axis=0)
    gr = np.cos(ang.T) * (2.0 / (2 * L))
    gr[:, 0] = 1.0 / (2 * L)
    gi = -np.sin(ang.T) * (2.0 / (2 * L))
    gi[:, 0] = np.where(np.arange(L) % 2 == 0, 1.0, -1.0) / (2 * L)
    inv = np.concatenate([gr, gi], axis=1)
    return fwd.astype(np.float32), inv.astype(np.float32)


@functools.lru_cache(maxsize=None)
def _filter_tables(L):
    tn = np.arange(L, dtype=np.float64) / L
    bands = np.linspace(1e-4, N_BANDS - 1, N_BANDS)
    ang = 2.0 * math.pi * tn[:, None] * bands[None, :]
    feat = np.zeros((L, 128), np.float32)
    feat[:, 0] = tn
    feat[:, 1:1 + N_BANDS] = np.cos(ang)
    feat[:, 1 + N_BANDS:FEAT_W] = np.sin(ang)
    deltas = np.abs(np.linspace(math.log(HY_TARGET) / HY_LONG_DECAY_PCT,
                                math.log(HY_TARGET) / HY_SHORT_DECAY_PCT, HY_W))
    window = (np.exp(-tn[:, None] * deltas[None, :]) + HY_SHIFT).astype(np.float32)
    return feat, window


@functools.lru_cache(maxsize=None)
def _rope_tables(L):
    rows = L // GRID_W
    row = np.repeat(np.arange(rows, dtype=np.float64), GRID_W)
    col = np.tile(np.arange(GRID_W, dtype=np.float64), rows)
    half = DH // 2
    n_freq = half // 2
    freqs = ROPE_BASE ** (-np.arange(n_freq, dtype=np.float64) / n_freq)
    ang = np.concatenate([row[:, None] * freqs, col[:, None] * freqs], axis=-1)
    cos = np.zeros((L, DHP), np.float32)
    sin = np.zeros((L, DHP), np.float32)
    cos[:, :half] = np.cos(ang)
    cos[:, 64:64 + half] = np.cos(ang)
    sin[:, :half] = -np.sin(ang)
    sin[:, 64:64 + half] = np.sin(ang)
    return cos, sin


@functools.lru_cache(maxsize=None)
def _tri_tables():
    i = np.arange(CHUNK)
    lower = (i[:, None] >= i[None, :]).astype(np.float32)
    upper = (i[:, None] <= i[None, :]).astype(np.float32)
    return lower, upper


def _pad_heads(w, axis=-1, rope=False):
    axis = axis % w.ndim
    parts = []

    def zeros(n):
        shp = list(w.shape)
        shp[axis] = n
        return jnp.zeros(shp, w.dtype)

    for h in range(HEADS):
        blk = lax.slice_in_dim(w, h * DH, (h + 1) * DH, axis=axis)
        if rope:
            half = DH // 2
            parts += [lax.slice_in_dim(blk, 0, half, axis=axis), zeros(64 - half),
                      lax.slice_in_dim(blk, half, DH, axis=axis), zeros(64 - half)]
        else:
            parts += [blk, zeros(DHP - DH)]
    return jnp.concatenate(parts, axis=axis)


def _unpad_heads_axis(x, axis, rope=False):
    if rope:
        half = DH // 2
        return jnp.concatenate([lax.slice_in_dim(x, 0, half, axis=axis),
                                lax.slice_in_dim(x, 64, 64 + half, axis=axis)], axis=axis)
    return lax.slice_in_dim(x, 0, DH, axis=axis)


def _mod_body(c_ref, w_ref, b_ref, o_ref):
    c = c_ref[...]
    s = c * (1.0 / (1.0 + jnp.exp(-c)))
    o_ref[...] = _dot(s, w_ref[...]) + b_ref[...]


def _mod_all(cvec8, w_mod, b_mod):
    tn = 1536
    nj = 6 * D_MODEL // tn
    return pl.pallas_call(
        _mod_body,
        grid=(DEPTH, nj),
        in_specs=[pl.BlockSpec((8, D_MODEL), lambda l, j: (0, 0)),
                  pl.BlockSpec((None, D_MODEL, tn), lambda l, j: (l, 0, j)),
                  pl.BlockSpec((None, 1, tn), lambda l, j: (l, 0, j))],
        out_specs=pl.BlockSpec((None, 8, tn), lambda l, j: (l, 0, j)),
        out_shape=jax.ShapeDtypeStruct((DEPTH, 8, 6 * D_MODEL), F32),
        compiler_params=_params(),
        name="mod_all",
    )(cvec8, w_mod, b_mod.reshape(DEPTH, 1, 6 * D_MODEL))


def _filt_body(feat_ref, win_ref, f_ref, w1_ref, b1_ref, w2_ref, b2_ref, w3_ref, b3_ref,
               fr_ref, hb_ref, p_ref, q_ref, p2_ref, *, L):
    fr = fr_ref[...]
    h = jnp.sin(fr * (_dot3(feat_ref[...], w1_ref[...]) + b1_ref[...]))
    h = jnp.sin(fr * (_dot3(h, w2_ref[...]) + b2_ref[...]))
    filt = _dot3(h, w3_ref[...]) + b3_ref[...]
    win = win_ref[...]
    row = lax.broadcasted_iota(jnp.int32, (L, 1), 0)
    taps = filt * jnp.concatenate([win, win, win, win], axis=1)
    lane = lax.broadcasted_iota(jnp.int32, (L, 4 * HY_W), 1)
    taps = jnp.where((row == 0) & (lane >= 2 * HY_W), 0.0, taps)
    spec = _dot(f_ref[...], taps)
    for o in range(2):
        a = spec[:, o * HY_W:(o + 1) * HY_W]
        b = spec[:, (2 + o) * HY_W:(3 + o) * HY_W]
        kr = a[:L] + b[:L]
        ki = a[L:] - b[L:]
        nyq = a[L:L + 1] + b[L:L + 1]
        bias = hb_ref[o:o + 1, :]
        p_ref[o] = kr + bias
        q_ref[o] = jnp.where(row == 0, 0.0, ki)
        p2_ref[o] = jnp.where(row == 0, nyq, kr) + bias


def _filter_spectra(L, fw1, fb1, fw2, fb2, fw3, fb3, ffr, hy_bias):
    feat, window = _filter_tables(L)
    fwd_table, _ = _dft_tables(L)
    const = lambda shape: _resident(shape, lambda l: (0,) * len(shape))
    per_layer = lambda shape: pl.BlockSpec((None,) + shape, lambda l: (l,) + (0,) * len(shape))
    out_spec = per_layer((2, L, HY_W))
    out_shape = jax.ShapeDtypeStruct((DEPTH, 2, L, HY_W), F32)
    return pl.pallas_call(
        functools.partial(_filt_body, L=L),
        grid=(DEPTH,),
        in_specs=[const((L, 128)), const((L, HY_W)), const((2 * L, L)),
                  per_layer((128, 128)), per_layer((1, 128)), per_layer((128, 128)), per_layer((1, 128)),
                  per_layer((128, 4 * HY_W)), per_layer((1, 4 * HY_W)), per_layer((1, 128)),
                  per_layer((2, HY_W))],
        out_specs=[out_spec, out_spec, out_spec],
        out_shape=[out_shape, out_shape, out_shape],
        compiler_params=_params(),
        name="hyena_filter_%d" % L,
    )(feat, window, fwd_table, fw1, fb1, fw2, fb2, fw3, fb3, ffr, hy_bias)


SEC_W = 3 * HWP + HW
W_R0 = 0
W_H0 = W_R0 + SEC_W
W_M0 = W_H0 + 3 * HY_W
W_G0 = W_M0 + SEC_W
W_END = W_G0
GATE_LANE0 = DH
IN_W = 8 * HW + 3 * HY_W + 16
FFN_SUB = 256
ROW_SUB = 256
SCAN_UNROLL = 4


def _prep_in_body(wt_ref, o_ref):
    kb = wt_ref.shape[1]
    half = DH // 2

    def zeros(n):
        return jnp.zeros((n, kb), F32)

    def head_tile(src, rope):
        if rope:
            return jnp.concatenate([wt_ref[src:src + half, :], zeros(64 - half),
                                    wt_ref[src + half:src + DH, :], zeros(64 - half)], axis=0)
        return jnp.concatenate([wt_ref[src:src + DH, :], zeros(DHP - DH)], axis=0)

    def dense_tiles(src, width):
        return [wt_ref[src + i * 128:src + (i + 1) * 128, :] for i in range(width // 128)]

    tiles = []
    for sec in range(3):
        tiles += [head_tile(sec * HW + h * DH, sec < 2) for h in range(HEADS)]
    tiles += dense_tiles(3 * HW, HW)
    tiles += dense_tiles(4 * HW, 3 * HY_W)
    m0 = 4 * HW + 3 * HY_W
    g0 = 8 * HW + 3 * HY_W
    for sec in range(3):
        tiles += [head_tile(m0 + sec * HW + h * DH, False) for h in range(HEADS)]
    tiles[-3 * HEADS] = jnp.concatenate([wt_ref[m0:m0 + DH, :], wt_ref[g0:g0 + 16, :], zeros(DHP - DH - 16)], axis=0)
    tiles += dense_tiles(m0 + 3 * HW, HW)
    for t, tile in enumerate(tiles):
        o_ref[:, t * 128:(t + 1) * 128] = tile.T.astype(o_ref.dtype)


def _prep_w_in(w_in_t):
    kb = 256
    return pl.pallas_call(
        _prep_in_body,
        grid=(DEPTH, D_MODEL // kb),
        in_specs=[pl.BlockSpec((None, IN_W, kb), lambda l, i: (l, 0, i))],
        out_specs=pl.BlockSpec((None, kb, W_END), lambda l, i: (l, i, 0)),
        out_shape=jax.ShapeDtypeStruct((DEPTH, D_MODEL, W_END), MXU_DT),
        compiler_params=_params(),
        name="prep_w_in",
    )(w_in_t)


def _in_body(xa_ref, xb_ref, sc_ref, sh_ref, g_ref, w_ref, pr_ref, ph_ref, pm_ref, *, tiles_a):
    x = jnp.where(pl.program_id(0) < tiles_a, xa_ref[...], xb_ref[...])
    h = _rms(x, D_MODEL) * g_ref[...]
    h = (h * (1.0 + sc_ref[...]) + sh_ref[...]).astype(MXU_DT)
    proj = jnp.dot(h, w_ref[...], preferred_element_type=F32)
    pr_ref[...] = proj[:, W_R0:W_H0]
    ph_ref[...] = proj[:, W_H0:W_M0]
    pm_ref[...] = proj[:, W_M0:W_G0]


def _mod_spec(l, which, row_of_tile):
    return pl.BlockSpec((None, None, None, 1, D_MODEL), lambda i: (l, which, row_of_tile(i), 0, 0))


def _in_proj(xa, xb, mod, gain, w_in, l, row_of_tile, tm):
    tiles_a = xa.shape[0] // tm
    n = xa.shape[0] + xb.shape[0]
    shapes = [SEC_W, 3 * HY_W, SEC_W]
    return pl.pallas_call(
        functools.partial(_in_body, tiles_a=tiles_a),
        grid=(n // tm,),
        in_specs=[pl.BlockSpec((tm, D_MODEL), lambda i: (jnp.minimum(i, tiles_a - 1), 0)),
                  pl.BlockSpec((tm, D_MODEL), lambda i: (jnp.maximum(i - tiles_a, 0), 0)),
                  _mod_spec(l, 1, row_of_tile), _mod_spec(l, 0, row_of_tile),
                  pl.BlockSpec((None, 1, D_MODEL), lambda i: (l, 0, 0)),
                  _resident((None, D_MODEL, W_END), lambda i: (l, 0, 0))],
        out_specs=[pl.BlockSpec((tm, w), lambda i: (i, 0)) for w in shapes],
        out_shape=[jax.ShapeDtypeStruct((n, w), F32) for w in shapes],
        compiler_params=_params(),
        name="in_proj",
    )(xa, xb, mod, mod, gain, w_in)


def _ret_body(*refs, L, latent, nb):
    if latent:
        (q_ref, k_ref, v_ref, gate_ref, dl_ref, gain_ref, s0_ref, cos_ref, sin_ref,
         o_ref, qs, ks, acc, sst) = refs
    else:
        (q_ref, k_ref, v_ref, gate_ref, dl_ref, gain_ref, _, o_ref, sfin_ref, qs, ks, acc, sst) = refs
    half = DH // 2
    C = CHUNK
    nchunk = L // C
    lg = _log_sigmoid(dl_ref[...])
    rel = (lax.broadcasted_iota(jnp.int32, (C, C), 0) - lax.broadcasted_iota(jnp.int32, (C, C), 1)).astype(F32)
    ri = lax.broadcasted_iota(jnp.int32, (C, 1), 0).astype(F32)

    for bi in range(nb):
        seq = slice(bi * L, (bi + 1) * L)
        for h in range(HEADS):
            cols = slice(h * DHP, (h + 1) * DHP)
            q = q_ref[seq, cols]
            k = k_ref[seq, cols]
            if latent:
                cos = cos_ref[...]
                sin = sin_ref[...]
                q = q * cos + pltpu.roll(q, 64, 1) * sin
                k = k * cos + pltpu.roll(k, 64, 1) * sin
            qs[seq, cols] = q
            ks[seq, cols] = k * (DH ** -0.5)
            for d in range(2):
                sst[bi, d, h] = jnp.zeros((DHP, DHP), F32)
                if latent:
                    sst[bi, d, h, 0:half, 0:DH] = s0_ref[bi, d, h, 0:half, :]
                    sst[bi, d, h, 64:64 + half, 0:DH] = s0_ref[bi, d, h, half:DH, :]

    acc[...] = jnp.zeros_like(acc)
    consts = []
    for h in range(HEADS):
        lgf = lg[h:h + 1, :]
        lgb = lg[HEADS + h:HEADS + h + 1, :]
        lgf1 = lgf[:, 0:1]
        lgb1 = lgb[:, 0:1]
        mask = (jnp.where(rel >= 0, jnp.exp(rel * lgf), 0.0)
                + jnp.where(rel <= 0, jnp.exp(-rel * lgb), 0.0))
        consts.append(dict(
            mask=mask,
            qdec_f=jnp.exp((ri + 1.0) * lgf1), qdec_b=jnp.exp((C - ri) * lgb1),
            kdec_f=jnp.exp((C - 1.0 - ri) * lgf1), kdec_b=jnp.exp(ri * lgb1),
            cdec_f=jnp.exp(C * lgf1), cdec_b=jnp.exp(C * lgb1)))

    def step(j):
        for bi in range(nb):
            rows_f = _chunk_rows(j, C, bi * L)
            rows_b = _chunk_rows(nchunk - 1 - j, C, bi * L)
            for h in range(HEADS):
                cols = slice(h * DHP, (h + 1) * DHP)
                cn = consts[h]
                q = qs[rows_f, cols]
                k = ks[rows_f, cols]
                v = v_ref[rows_f, cols]
                s_f = sst[bi, 0, h]
                sc = _dot_nt(q, k) * cn["mask"]
                acc[rows_f, cols] += _dot(sc, v) + _dot(q, s_f) * cn["qdec_f"]
                sst[bi, 0, h] = s_f * cn["cdec_f"] + _dot_tn(k * cn["kdec_f"], v)
                q = qs[rows_b, cols]
                k = ks[rows_b, cols]
                v = v_ref[rows_b, cols]
                s_b = sst[bi, 1, h]
                acc[rows_b, cols] += _dot(q, s_b) * cn["qdec_b"]
                sst[bi, 1, h] = s_b * cn["cdec_b"] + _dot_tn(k * cn["kdec_b"], v)

    if nchunk <= 2:
        for j in range(nchunk):
            step(j)
    else:
        pl.loop(0, nchunk, unroll=SCAN_UNROLL)(step)

    if not latent:
        for bi in range(nb):
            for d in range(2):
                for h in range(HEADS):
                    sfin_ref[bi, d, h, 0:half, :] = sst[bi, d, h, 0:half, 0:DH]
                    sfin_ref[bi, d, h, half:DH, :] = sst[bi, d, h, 64:64 + half, 0:DH]

    y = jnp.concatenate([(_rms(acc[:, h * DHP:(h + 1) * DHP], DH) * gain_ref[:, h * DHP:(h + 1) * DHP])[:, :DH]
                         for h in range(HEADS)], axis=1)
    g = gate_ref[...]
    o_ref[...] = (y * (g * (1.0 / (1.0 + jnp.exp(-g))))).astype(o_ref.dtype)


def _state_spec(l, nb, *tail):
    shape = (nb, None, 2, HEADS) + tail
    return pl.BlockSpec(shape, lambda b: (b, l) + (0,) * (len(shape) - 2))


def _seqs_per_step(B, L):
    return min(B, max(1, 512 // L))


def _retention(pr, dl, gain, l, B, L, latent, state, row0):
    nb = _seqs_per_step(B, L)
    off = row0 // (nb * L)
    blk = lambda j: pl.BlockSpec((nb * L, HWP), lambda b, j=j: (b + off, j))
    gate_blk = pl.BlockSpec((nb * L, HW), lambda b: (b + off, 3 * HWP // HW))
    in_specs = [blk(0), blk(1), blk(2), gate_blk,
                pl.BlockSpec((None, 8, 128), lambda b: (l, 0, 0)),
                pl.BlockSpec((None, 1, HWP), lambda b: (l, 0, 0))]
    args = [pr, pr, pr, pr, dl, gain]
    out_specs = [pl.BlockSpec((nb * L, HW), lambda b: (b, 0))]
    out_shape = [jax.ShapeDtypeStruct((B * L, HW), MXU_DT)]
    aliases = {}
    if latent:
        cos, sin = _rope_tables(L)
        in_specs += [_state_spec(l, nb, DH, DH),
                     pl.BlockSpec((L, DHP), lambda b: (0, 0)), pl.BlockSpec((L, DHP), lambda b: (0, 0))]
        args += [state, cos, sin]
    else:
        in_specs.append(pl.BlockSpec(memory_space=pl.ANY))
        args.append(state)
        aliases = {len(args) - 1: 1}
        out_specs.append(_state_spec(l, nb, DH, DH))
        out_shape.append(jax.ShapeDtypeStruct(state.shape, F32))
    return pl.pallas_call(
        functools.partial(_ret_body, L=L, latent=latent, nb=nb),
        grid=(B // nb,),
        in_specs=in_specs, out_specs=out_specs, out_shape=out_shape,
        input_output_aliases=aliases,
        scratch_shapes=[pltpu.VMEM((nb * L, HWP), F32), pltpu.VMEM((nb * L, HWP), F32),
                        pltpu.VMEM((nb * L, HWP), F32), pltpu.VMEM((nb, 2, HEADS, DHP, DHP), F32)],
        compiler_params=_params(),
        name="retention_%d" % L,
    )(*args)


def _ml_body(*refs, L, latent, nb):
    if latent:
        (q_ref, k_ref, v_ref, og_ref, gb_ref, gain_ref, tl_ref, tu_ref, c0_ref, n0_ref, m0_ref,
         o_ref, acc, bb_s, ub_s, pm_s, brow_s, ibrow_s, mc_s, mx_s, cn_s, gate_s) = refs
    else:
        (q_ref, k_ref, v_ref, og_ref, gb_ref, gain_ref, tl_ref, tu_ref, _, _,
         o_ref, cfin_ref, nfin_ref, mfin_ref, acc, bb_s, ub_s, pm_s, brow_s, ibrow_s, mc_s, mx_s, cn_s,
         gate_s) = refs
    C = CHUNK
    nchunk = L // C
    lane = lax.broadcasted_iota(jnp.int32, (C, DHP), 1)
    tri_r = lax.broadcasted_iota(jnp.int32, (C, C), 0)
    tri_c = lax.broadcasted_iota(jnp.int32, (C, C), 1)
    tl = tl_ref[...]
    tu = tu_ref[...]

    row_i = lax.broadcasted_iota(jnp.int32, (C, 128), 0)

    def cummax_rows(x, suffix):
        s = 1
        while s < C:
            if suffix:
                x = jnp.where(row_i < C - s, jnp.maximum(x, pltpu.roll(x, C - s, 0)), x)
            else:
                x = jnp.where(row_i >= s, jnp.maximum(x, pltpu.roll(x, s, 0)), x)
            s *= 2
        return x

    lane_all = lax.broadcasted_iota(jnp.int32, (nb * L, DHP), 1)
    gate_s[...] = jnp.where(lane_all < 16, pltpu.roll(q_ref[:, 0:DHP], DHP - GATE_LANE0, 1), 0.0)

    for c in range(nb * nchunk):
        rows = slice(c * C, (c + 1) * C)
        g = gate_s[rows, :] + gb_ref[...]
        lf = _log_sigmoid(g)
        pre = _dot_exact_lhs(tl, lf)
        suf = _dot_exact_lhs(tu, lf)
        bc = jnp.where(lane < 8, pre, suf)
        brow_s[c * 16:(c + 1) * 16, :] = bc.T[0:16, :]
        ibrow_s[c * 16:(c + 1) * 16, :] = g.T[0:16, :]
        for d in range(2):
            for h in range(HEADS):
                sr = d * HEADS + h
                b_b = jnp.broadcast_to(bc[:, d * 8 + 4 + h:d * 8 + 5 + h], (C, 128))
                u_b = jnp.broadcast_to(g[:, d * 8 + h:d * 8 + h + 1], (C, 128)) - b_b
                bb_s[sr, rows, :] = b_b
                ub_s[sr, rows, :] = u_b
                pm_s[sr, rows, :] = cummax_rows(u_b, suffix=(d == 1))

    for bi in range(nb):
        for d in range(2):
            for h in range(HEADS):
                sr = d * HEADS + h
                m = m0_ref[bi, sr:sr + 1, :] if latent else jnp.zeros((1, 128), F32)
                for j in range(nchunk):
                    c = bi * nchunk + (j if d == 0 else nchunk - 1 - j)
                    last = c * C + (C - 1 if d == 0 else 0)
                    mx = jnp.maximum(m, pm_s[sr, last:last + 1, :])
                    mc_s[c * 8 + sr:c * 8 + sr + 1, :] = m
                    mx_s[c * 8 + sr:c * 8 + sr + 1, :] = mx
                    m = bb_s[sr, last:last + 1, :] + mx
                if not latent:
                    mfin_ref[bi, sr:sr + 1, :] = m

    acc[...] = jnp.zeros_like(acc)
    cn_s[...] = jnp.zeros_like(cn_s)
    if latent:
        for bi in range(nb):
            for d in range(2):
                for h in range(HEADS):
                    cn_s[bi, d, h, 0:DH, 0:DH] = c0_ref[bi, d, h]
                    n_row = jnp.concatenate([n0_ref[bi, d, h:h + 1, :], jnp.zeros((1, DHP - DH), F32)], axis=1)
                    cn_s[bi, d, h, :, DHP:] = jnp.broadcast_to(n_row, (DHP, DHP)).T
    ones_blk = jnp.ones((C, DHP), MXU_DT)

    def chain(bi, d, h, c):
        rows = _chunk_rows(c, C)
        brs = brow_s[_chunk_rows(c, 16), :]
        ibrs = ibrow_s[_chunk_rows(c, 16), :]
        mcs = mc_s[_chunk_rows(c, 8), :]
        mxs = mx_s[_chunk_rows(c, 8), :]
        causal = (tri_r >= tri_c) if d == 0 else (tri_r <= tri_c)
        cols = slice(h * DHP, (h + 1) * DHP)
        ic = d * 8 + h
        fc = d * 8 + 4 + h
        sr = d * HEADS + h
        m_c = mcs[sr:sr + 1, :]
        mx = mxs[sr:sr + 1, :]
        q = q_ref[rows, cols].astype(MXU_DT)
        k = k_ref[rows, cols] * (DH ** -0.5)
        v1 = jnp.concatenate([v_ref[rows, cols].astype(MXU_DT), ones_blk], axis=1)
        cn = cn_s[bi, d, h]
        urow = ibrs[ic:ic + 1, :] - brs[fc:fc + 1, :]
        mb = jnp.maximum(m_c, pm_s[sr, rows, :])
        w_intra = jnp.exp(jnp.where(causal, urow - mb, -jnp.inf))
        w_inter = jnp.exp(m_c - mb)
        s = _dot_nt(q, k) * w_intra
        tot = (jnp.dot(s.astype(MXU_DT), v1, preferred_element_type=F32)
               + jnp.concatenate([w_inter, w_inter], axis=1) * _dot(q, cn))
        den = tot[:, DHP:]
        floor = jnp.exp(-(bb_s[sr, rows, :] + mb))
        acc[rows, cols] += tot[:, :DHP] * (1.0 / jnp.maximum(jnp.abs(den), floor))
        kw = k * jnp.exp(ub_s[sr, rows, :] - mx)
        w_prev = jnp.exp(m_c - mx)
        cn_s[bi, d, h] = jnp.concatenate([w_prev, w_prev], axis=1) * cn + _dot_tn(kw, v1)

    def step(j):
        for bi in range(nb):
            for d in range(2):
                c = bi * nchunk + (j if d == 0 else nchunk - 1 - j)
                for h in range(HEADS):
                    chain(bi, d, h, c)

    if nchunk <= 2:
        for j in range(nchunk):
            step(j)
    else:
        pl.loop(0, nchunk, unroll=SCAN_UNROLL)(step)

    if not latent:
        for bi in range(nb):
            for d in range(2):
                for h in range(HEADS):
                    cfin_ref[bi, d, h] = cn_s[bi, d, h, 0:DH, 0:DH]
                    nfin_ref[bi, d, h:h + 1, :] = cn_s[bi, d, h, :, DHP:].T[0:1, 0:DH]

    y = jnp.concatenate([(_rms(acc[:, h * DHP:(h + 1) * DHP], DH) * gain_ref[:, h * DHP:(h + 1) * DHP])[:, :DH]
                         for h in range(HEADS)], axis=1)
    o_ref[...] = (y * (1.0 / (1.0 + jnp.exp(-og_ref[...])))).astype(o_ref.dtype)


def _mlstm(pm, gb, gain, l, B, L, latent, c_state, n_state, row0, m0=None):
    tl, tu = _tri_tables()
    tl = tl.astype(MXU_DT)
    tu = tu.astype(MXU_DT)
    nb = _seqs_per_step(B, L)
    off = row0 // (nb * L)
    blk = lambda j: pl.BlockSpec((nb * L, HWP), lambda b, j=j: (b + off, j))
    gate_blk = pl.BlockSpec((nb * L, HW), lambda b: (b + off, 3 * HWP // HW))
    in_specs = [blk(0), blk(1), blk(2), gate_blk,
                pl.BlockSpec((None, 1, 128), lambda b: (l, 0, 0)),
                pl.BlockSpec((None, 1, HWP), lambda b: (l, 0, 0)),
                pl.BlockSpec((CHUNK, CHUNK), lambda b: (0, 0)),
                pl.BlockSpec((CHUNK, CHUNK), lambda b: (0, 0))]
    args = [pm, pm, pm, pm, gb, gain, tl, tu]
    out_specs = [pl.BlockSpec((nb * L, HW), lambda b: (b, 0))]
    out_shape = [jax.ShapeDtypeStruct((B * L, HW), MXU_DT)]
    aliases = {}
    if latent:
        in_specs += [_state_spec(l, nb, DH, DH), _state_spec(l, nb, DH),
                     pl.BlockSpec((nb, None, 8, 128), lambda b: (b, l, 0, 0))]
        args += [c_state, n_state, m0]
    else:
        in_specs += [pl.BlockSpec(memory_space=pl.ANY), pl.BlockSpec(memory_space=pl.ANY)]
        args += [c_state, n_state]
        aliases = {len(args) - 2: 1, len(args) - 1: 2}
        out_specs += [_state_spec(l, nb, DH, DH), _state_spec(l, nb, DH),
                      pl.BlockSpec((nb, 8, 128), lambda b: (b, 0, 0))]
        out_shape += [jax.ShapeDtypeStruct(c_state.shape, F32), jax.ShapeDtypeStruct(n_state.shape, F32),
                      jax.ShapeDtypeStruct((B, 8, 128), F32)]
    nchunk = L // CHUNK
    return pl.pallas_call(
        functools.partial(_ml_body, L=L, latent=latent, nb=nb),
        grid=(B // nb,),
        in_specs=in_specs, out_specs=out_specs, out_shape=out_shape,
        input_output_aliases=aliases,
        scratch_shapes=[pltpu.VMEM((nb * L, HWP), F32),
                        pltpu.VMEM((8, nb * L, 128), F32), pltpu.VMEM((8, nb * L, 128), F32),
                        pltpu.VMEM((8, nb * L, 128), F32),
                        pltpu.VMEM((nb * nchunk * 16, 128), F32), pltpu.VMEM((nb * nchunk * 16, 128), F32),
                        pltpu.VMEM((nb * nchunk * 8, 128), F32), pltpu.VMEM((nb * nchunk * 8, 128), F32),
                        pltpu.VMEM((nb, 2, HEADS, DHP, 2 * DHP), F32), pltpu.VMEM((nb * L, 128), F32)],
        compiler_params=_params(),
        name="mlstm_%d" % L,
    )(*args)


def _shift_rows(x, seq_len, n_rows):
    pos = lax.broadcasted_iota(jnp.int32, (n_rows, 1), 0) & (seq_len - 1)
    prev = jnp.where(pos == 0, 0.0, pltpu.roll(x, 1, 0))
    nxt = jnp.where(pos == seq_len - 1, 0.0, pltpu.roll(x, n_rows - 1, 0))
    return prev, nxt


def _hy_body(p_ref, cw_ref, cb_ref, f_ref, g_ref, ps_ref, qs_ref, p2s_ref, o_ref, fh_ref, gh_ref, *, L):
    @pl.when(pl.program_id(0) == 0)
    def _():
        fh_ref[...] = f_ref[...].astype(MXU_DT)
        gh_ref[...] = g_ref[...].astype(MXU_DT)

    def conv(z, o):
        spec = _dot(fh_ref[...], z)
        xr = spec[:L]
        xi = spec[L:]
        p = ps_ref[o]
        q = qs_ref[o]
        yr = xr * p - xi * q
        yi = xr * q + xi * p2s_ref[o]
        return _dot(gh_ref[...], jnp.concatenate([yr, yi], axis=0))

    for i in range(p_ref.shape[0] // L):
        rows = slice(i * L, (i + 1) * L)
        x = p_ref[rows, :]
        prev, nxt = _shift_rows(x, L, L)
        hy = prev * cw_ref[0:1, :] + x * cw_ref[1:2, :] + nxt * cw_ref[2:3, :] + cb_ref[...]
        hv = hy[:, 0:HY_W]
        hx1 = hy[:, HY_W:2 * HY_W]
        hx2 = hy[:, 2 * HY_W:3 * HY_W]
        z = hx1 * conv(hv, 0)
        o_ref[rows, :] = (hx2 * conv(z, 1)).astype(o_ref.dtype)


def _hyena(ph, cw, cb, spectra, l, B, L, row0):
    fwd_table, inv_table = _dft_tables(L)
    ps, qs, p2s = spectra
    const = lambda shape: _resident(shape, lambda b: (0,) * len(shape))
    spec_blk = _resident((None, 2, L, HY_W), lambda b: (l, 0, 0, 0))
    nb = min(B, max(1, 2048 // L))
    off = row0 // (nb * L)
    return pl.pallas_call(
        functools.partial(_hy_body, L=L),
        grid=(B // nb,),
        in_specs=[pl.BlockSpec((nb * L, 3 * HY_W), lambda b: (b + off, 0)),
                  pl.BlockSpec((None, 8, 3 * HY_W), lambda b: (l, 0, 0)),
                  pl.BlockSpec((None, 1, 3 * HY_W), lambda b: (l, 0, 0)),
                  const((2 * L, L)), const((L, 2 * L)),
                  spec_blk, spec_blk, spec_blk],
        out_specs=pl.BlockSpec((nb * L, HY_W), lambda b: (b, 0)),
        out_shape=jax.ShapeDtypeStruct((B * L, HY_W), MXU_DT),
        scratch_shapes=[pltpu.VMEM((2 * L, L), MXU_DT), pltpu.VMEM((L, 2 * L), MXU_DT)],
        compiler_params=_params(),
        name="hyena_%d" % L,
    )(ph, cw, cb, fwd_table, inv_table, ps, qs, p2s)


def _out_body(mr_ref, mh_ref, mm_ref, w_ref, x_ref, g1_ref, sc2_ref, sh2_ref,
              gpost_ref, gpre_ref, x1_ref, h2_ref):
    w = w_ref[...].astype(MXU_DT)
    for r in range(x_ref.shape[0] // ROW_SUB):
        rs = slice(r * ROW_SUB, (r + 1) * ROW_SUB)
        mixed = jnp.concatenate([mr_ref[rs, :], mh_ref[rs, :], mm_ref[rs, :]], axis=1)
        mix = jnp.dot(mixed, w, preferred_element_type=F32)
        x1 = x_ref[rs, :] + g1_ref[...] * (_rms(mix, D_MODEL) * gpost_ref[...])
        x1_ref[rs, :] = x1
        h2 = _rms(x1, D_MODEL) * gpre_ref[...] * (1.0 + sc2_ref[...]) + sh2_ref[...]
        h2_ref[rs, :] = h2.astype(h2_ref.dtype)


def _ffn_body(mr_ref, mh_ref, mm_ref, wo_ref, x_ref, g1_ref, sc2_ref, sh2_ref, gmix_ref, gpre_ref,
              wa_ref, wb_ref, cw_ref, cb_ref, wd_ref, g2_ref, gpost_ref, o_ref, x1_ref, h_ref, *, seq_len, tm):
    j = pl.program_id(1)

    @pl.when(j == 0)
    def _():
        _out_body(mr_ref, mh_ref, mm_ref, wo_ref, x_ref, g1_ref, sc2_ref, sh2_ref,
                  gmix_ref, gpre_ref, x1_ref, h_ref)
        o_ref[...] = jnp.zeros_like(o_ref)

    h = h_ref[...]
    part = None
    for s in range(wa_ref.shape[1] // FFN_SUB):
        cs = slice(s * FFN_SUB, (s + 1) * FFN_SUB)
        a = jnp.dot(h, wa_ref[:, cs].astype(MXU_DT), preferred_element_type=F32)
        b = jnp.dot(h, wb_ref[:, cs].astype(MXU_DT), preferred_element_type=F32)
        prev, nxt = _shift_rows(a, seq_len, tm)
        a = prev * cw_ref[0:1, cs] + a * cw_ref[1:2, cs] + nxt * cw_ref[2:3, cs] + cb_ref[:, cs]
        gelu = 0.5 * a * (1.0 + jnp.tanh(math.sqrt(2.0 / math.pi) * (a + 0.044715 * (a * a * a))))
        p = jnp.dot((gelu * b).astype(MXU_DT), wd_ref[cs, :].astype(MXU_DT), preferred_element_type=F32)
        part = p if part is None else part + p
    o_ref[...] += part

    @pl.when(j == pl.num_programs(1) - 1)
    def _():
        o_ref[...] = x1_ref[...] + g2_ref[...] * (_rms(o_ref[...], D_MODEL) * gpost_ref[...])


def _mix_ffn(mr, mh, mm, w_out, x, mod, gmix, gpre, w_up, cw, cb, w_down, gpost, l, row_of_tile,
             seq_len, tm, tf):
    n = x.shape[0]
    nf = D_FF // tf
    tile = lambda w: pl.BlockSpec((tm, w), lambda i, j: (i, 0))
    layer = lambda r, c: pl.BlockSpec((None, r, c), lambda i, j: (l, 0, 0))
    mod_row = lambda which: pl.BlockSpec((None, None, None, 1, D_MODEL),
                                         lambda i, j: (l, which, row_of_tile(i), 0, 0))
    return pl.pallas_call(
        functools.partial(_ffn_body, seq_len=seq_len, tm=tm),
        grid=(n // tm, nf),
        in_specs=[tile(HW), tile(HY_W), tile(HW),
                  _resident((None, D_MODEL, D_MODEL), lambda i, j: (l, 0, 0)),
                  tile(D_MODEL), mod_row(2), mod_row(4), mod_row(3),
                  layer(1, D_MODEL), layer(1, D_MODEL),
                  pl.BlockSpec((None, D_MODEL, tf), lambda i, j: (l, 0, j)),
                  pl.BlockSpec((None, D_MODEL, tf), lambda i, j: (l, 0, j + nf)),
                  pl.BlockSpec((None, 8, tf), lambda i, j: (l, 0, j)),
                  pl.BlockSpec((None, 1, tf), lambda i, j: (l, 0, j)),
                  pl.BlockSpec((None, tf, D_MODEL), lambda i, j: (l, j, 0)),
                  mod_row(5), layer(1, D_MODEL)],
        out_specs=pl.BlockSpec((tm, D_MODEL), lambda i, j: (i, 0)),
        out_shape=jax.ShapeDtypeStruct((n, D_MODEL), F32),
        scratch_shapes=[pltpu.VMEM((tm, D_MODEL), F32), pltpu.VMEM((tm, D_MODEL), MXU_DT)],
        compiler_params=_params(),
        name="mix_ffn",
    )(mr, mh, mm, w_out, x, mod, mod, mod, gmix, gpre, w_up, w_up, cw, cb, w_down, mod, gpost)


def _pad_rows8(w):
    return jnp.concatenate([w, jnp.zeros((w.shape[0], 8 - w.shape[1], w.shape[2]), w.dtype)], axis=1)


def kernel(x_prompt, x_sample, c, state_ret, state_mlstm_c, state_mlstm_n, state_mlstm_m, c_ctx,
           norm_mix_pre, norm_mix_post, norm_ffn_pre, norm_ffn_post, w_mod, b_mod, w_in, w_out,
           ret_decay_logit, ret_norm_g, hy_conv_w, hy_conv_b, hy_f_w1, hy_f_b1, hy_f_w2, hy_f_b2,
           hy_f_w3, hy_f_b3, hy_sin_freq, hy_bias, ml_gate_bias, ml_norm_g,
           w_up, ffn_conv_w, ffn_conv_b, w_down):
    BP, LP, _ = x_prompt.shape
    BS, LS, _ = x_sample.shape

    w_in_p = _prep_w_in(jnp.swapaxes(w_in, 1, 2))
    ret_gain = _pad_heads(ret_norm_g).reshape(DEPTH, 1, HWP)
    ml_gain = _pad_heads(ml_norm_g).reshape(DEPTH, 1, HWP)
    dl = jnp.broadcast_to(ret_decay_logit.reshape(DEPTH, 8, 1), (DEPTH, 8, 128))
    gate_bias = jnp.concatenate([ml_gate_bias.reshape(DEPTH, 1, 16), jnp.zeros((DEPTH, 1, 112), F32)], axis=2)
    hy_cw = _pad_rows8(hy_conv_w)
    hy_cb = hy_conv_b.reshape(DEPTH, 1, 3 * HY_W)
    ffn_cw = _pad_rows8(ffn_conv_w)
    ffn_cb = ffn_conv_b.reshape(DEPTH, 1, D_FF)
    g_mix_pre = norm_mix_pre.reshape(DEPTH, 1, D_MODEL)
    g_mix_post = norm_mix_post.reshape(DEPTH, 1, D_MODEL)
    g_ffn_pre = norm_ffn_pre.reshape(DEPTH, 1, D_MODEL)
    g_ffn_post = norm_ffn_post.reshape(DEPTH, 1, D_MODEL)
    pad2 = lambda a, r, cdim: jnp.pad(a, ((0, 0), (0, r - a.shape[1]), (0, cdim - a.shape[2])))
    fw1 = pad2(hy_f_w1, 128, 128)
    fb1 = pad2(hy_f_b1.reshape(DEPTH, 1, -1), 1, 128)
    fw2 = pad2(hy_f_w2, 128, 128)
    fb2 = pad2(hy_f_b2.reshape(DEPTH, 1, -1), 1, 128)
    fw3 = pad2(hy_f_w3, 128, 4 * HY_W)
    fb3 = hy_f_b3.reshape(DEPTH, 1, 4 * HY_W)
    ffr = pad2(hy_sin_freq.reshape(DEPTH, 1, -1), 1, 128)

    m0 = jnp.broadcast_to(state_mlstm_m.reshape(BS, DEPTH, 8, 1), (BS, DEPTH, 8, 128))
    new_ret = jnp.zeros((BP, DEPTH, 2, HEADS, DH, DH), F32)
    new_c = jnp.zeros((BP, DEPTH, 2, HEADS, DH, DH), F32)
    new_n = jnp.zeros((BP, DEPTH, 2, HEADS, DH), F32)

    cvec8 = jnp.concatenate([c_ctx.reshape(1, D_MODEL), c, jnp.zeros((8 - 1 - BS, D_MODEL), F32)], axis=0)
    mod = _mod_all(cvec8, w_mod, b_mod)
    mod = mod.reshape(DEPTH, 8, 6, 1, D_MODEL).transpose(0, 2, 1, 3, 4)
    spectra_p = _filter_spectra(LP, fw1, fb1, fw2, fb2, fw3, fb3, ffr, hy_bias)
    spectra_s = _filter_spectra(LS, fw1, fb1, fw2, fb2, fw3, fb3, ffr, hy_bias)

    xp = x_prompt.reshape(BP * LP, D_MODEL)
    xs = x_sample.reshape(BS * LS, D_MODEL)
    tm = 512
    ffn_tm = 1024
    ffn_tf = 512
    n_ctx = BP * LP
    row_ctx = lambda i: 0
    row_both = lambda i: jnp.where(i < n_ctx // tm, 0, 1 + jnp.maximum(i - n_ctx // tm, 0) // (LS // tm))
    row_lat_ffn = lambda i: 1 + i // (LS // ffn_tm)
    new_m = []

    def mix_and_ffn(x, ph, mr, mm, l, B, L, row0, row_ffn, spectra):
        mh = _hyena(ph, hy_cw, hy_cb, spectra, l, B, L, row0)
        return _mix_ffn(mr, mh, mm, w_out, x, mod, g_mix_post, g_ffn_pre,
                        w_up, ffn_cw, ffn_cb, w_down, g_ffn_post, l, row_ffn, L, ffn_tm, ffn_tf)

    for l in range(DEPTH):
        pr, ph, pm = _in_proj(xp, xs, mod, g_mix_pre, w_in_p, l, row_both, tm)
        mr, new_ret = _retention(pr, dl, ret_gain, l, BP, LP, False, new_ret, 0)
        mm, new_c, new_n, m_fin = _mlstm(pm, gate_bias, ml_gain, l, BP, LP, False, new_c, new_n, 0)
        new_m.append(m_fin)
        xp = mix_and_ffn(xp, ph, mr, mm, l, BP, LP, 0, row_ctx, spectra_p)
        (mr,) = _retention(pr, dl, ret_gain, l, BS, LS, True, state_ret, n_ctx)
        (mm,) = _mlstm(pm, gate_bias, ml_gain, l, BS, LS, True, state_mlstm_c, state_mlstm_n, n_ctx, m0)
        xs = mix_and_ffn(xs, ph, mr, mm, l, BS, LS, n_ctx, row_lat_ffn, spectra_s)

    out_m = jnp.stack(new_m, axis=1)[..., 0].reshape(BP, DEPTH, 2, HEADS)
    return (xp.reshape(BP, LP, D_MODEL), xs.reshape(BS, LS, D_MODEL), new_ret, new_c, new_n, out_m)
```

```python
import functools
import math

import numpy as np
import jax
import jax.numpy as jnp
from jax import lax
from jax.experimental import pallas as pl
from jax.experimental.pallas import tpu as pltpu

D_MODEL = 1024
DEPTH = 4
GRID_W = 64
HEADS = 4
DH = 96
DHP = 128
HW = HEADS * DH
HWP = HEADS * DHP
HY_W = 256
D_FF = 4 * D_MODEL
CHUNK = 128
N_BANDS = 16
FEAT_W = 1 + 2 * N_BANDS
HY_SHIFT = 0.05
HY_TARGET = 1e-2
HY_SHORT_DECAY_PCT = 0.3
HY_LONG_DECAY_PCT = 1.5
ROPE_BASE = 10000.0
EPS = 1e-6

MXU_DT = jnp.bfloat16
F32 = jnp.float32
VMEM_LIMIT = 56 * 1024 * 1024


def _dot(a, b):
    return jnp.dot(a.astype(MXU_DT), b.astype(MXU_DT), preferred_element_type=F32)


def _dot_nt(a, b):
    return lax.dot_general(a.astype(MXU_DT), b.astype(MXU_DT), (((1,), (1,)), ((), ())),
                           preferred_element_type=F32)


def _dot_tn(a, b):
    return lax.dot_general(a.astype(MXU_DT), b.astype(MXU_DT), (((0,), (0,)), ((), ())),
                           preferred_element_type=F32)


def _split2(x):
    hi = x.astype(MXU_DT)
    lo = (x - hi.astype(F32)).astype(MXU_DT)
    return hi, lo


def _dot_split(a_hi, a_lo, b):
    b_hi, b_lo = _split2(b)
    return (jnp.dot(a_hi, b_hi, preferred_element_type=F32)
            + jnp.dot(a_hi, b_lo, preferred_element_type=F32)
            + jnp.dot(a_lo, b_hi, preferred_element_type=F32))


def _dot3(a, b):
    a_hi, a_lo = _split2(a)
    return _dot_split(a_hi, a_lo, b)


def _dot_exact_lhs(t, x):
    x1 = x.astype(MXU_DT)
    r1 = x - x1.astype(F32)
    x2 = r1.astype(MXU_DT)
    x3 = (r1 - x2.astype(F32)).astype(MXU_DT)
    return (jnp.dot(t, x1, preferred_element_type=F32) + jnp.dot(t, x2, preferred_element_type=F32)
            + jnp.dot(t, x3, preferred_element_type=F32))


def _rms(x, n):
    return x * lax.rsqrt(jnp.sum(x * x, axis=-1, keepdims=True) * (1.0 / n) + EPS)


def _chunk_rows(c, size, base=0):
    if isinstance(c, int):
        return slice(base + c * size, base + (c + 1) * size)
    return pl.ds(pl.multiple_of(base + c * size, size), size)


def _log_sigmoid(x):
    return jnp.minimum(x, 0.0) - jnp.log1p(jnp.exp(-jnp.abs(x)))


def _resident(shape, index_map):
    return pl.BlockSpec(shape, index_map, pipeline_mode=pl.Buffered(1))


def _params(**kw):
    return pltpu.CompilerParams(vmem_limit_bytes=VMEM_LIMIT, **kw)


@functools.lru_cache(maxsize=None)
def _dft_tables(L):
    f = np.arange(L, dtype=np.int64)[:, None]
    s = np.arange(L, dtype=np.int64)[None, :]
    ang = np.pi * ((f * s) % (2 * L)).astype(np.float64) / L
    fr = np.cos(ang)
    fi = -np.sin(ang)
    fi[0, :] = np.where(np.arange(L) % 2 == 0, 1.0, -1.0)
    fwd = np.concatenate([fr, fi], axis=0)
    gr = np.cos(ang.T) * (2.0 / (2 * L))
    gr[:, 0] = 1.0 / (2 * L)
    gi = -np.sin(ang.T) * (2.0 / (2 * L))
    gi[:, 0] = np.where(np.arange(L) % 2 == 0, 1.0, -1.0) / (2 * L)
    inv = np.concatenate([gr, gi], axis=1)
    return fwd.astype(np.float32), inv.astype(np.float32)


@functools.lru_cache(maxsize=None)
def _filter_tables(L):
    tn = np.arange(L, dtype=np.float64) / L
    bands = np.linspace(1e-4, N_BANDS - 1, N_BANDS)
    ang = 2.0 * math.pi * tn[:, None] * bands[None, :]
    feat = np.zeros((L, 128), np.float32)
    feat[:, 0] = tn
    feat[:, 1:1 + N_BANDS] = np.cos(ang)
    feat[:, 1 + N_BANDS:FEAT_W] = np.sin(ang)
    deltas = np.abs(np.linspace(math.log(HY_TARGET) / HY_LONG_DECAY_PCT,
                                math.log(HY_TARGET) / HY_SHORT_DECAY_PCT, HY_W))
    window = (np.exp(-tn[:, None] * deltas[None, :]) + HY_SHIFT).astype(np.float32)
    return feat, window


@functools.lru_cache(maxsize=None)
def _rope_tables(L):
    rows = L // GRID_W
    row = np.repeat(np.arange(rows, dtype=np.float64), GRID_W)
    col = np.tile(np.arange(GRID_W, dtype=np.float64), rows)
    half = DH // 2
    n_freq = half // 2
    freqs = ROPE_BASE ** (-np.arange(n_freq, dtype=np.float64) / n_freq)
    ang = np.concatenate([row[:, None] * freqs, col[:, None] * freqs], axis=-1)
    cos = np.zeros((L, DHP), np.float32)
    sin = np.zeros((L, DHP), np.float32)
    cos[:, :half] = np.cos(ang)
    cos[:, 64:64 + half] = np.cos(ang)
    sin[:, :half] = -np.sin(ang)
    sin[:, 64:64 + half] = np.sin(ang)
    return cos, sin


@functools.lru_cache(maxsize=None)
def _tri_tables():
    i = np.arange(CHUNK)
    lower = (i[:, None] >= i[None, :]).astype(np.float32)
    upper = (i[:, None] <= i[None, :]).astype(np.float32)
    return lower, upper


def _pad_heads(w, axis=-1, rope=False):
    axis = axis % w.ndim
    parts = []

    def zeros(n):
        shp = list(w.shape)
        shp[axis] = n
        return jnp.zeros(shp, w.dtype)

    for h in range(HEADS):
        blk = lax.slice_in_dim(w, h * DH, (h + 1) * DH, axis=axis)
        if rope:
            half = DH // 2
            parts += [lax.slice_in_dim(blk, 0, half, axis=axis), zeros(64 - half),
                      lax.slice_in_dim(blk, half, DH, axis=axis), zeros(64 - half)]
        else:
            parts += [blk, zeros(DHP - DH)]
    return jnp.concatenate(parts, axis=axis)


def _unpad_heads_axis(x, axis, rope=False):
    if rope:
        half = DH // 2
        return jnp.concatenate([lax.slice_in_dim(x, 0, half, axis=axis),
                                lax.slice_in_dim(x, 64, 64 + half, axis=axis)], axis=axis)
    return lax.slice_in_dim(x, 0, DH, axis=axis)


def _mod_body(c_ref, w_ref, b_ref, o_ref):
    c = c_ref[...]
    s = c * (1.0 / (1.0 + jnp.exp(-c)))
    o_ref[...] = _dot(s, w_ref[...]) + b_ref[...]


def _mod_all(cvec8, w_mod, b_mod):
    tn = 1536
    nj = 6 * D_MODEL // tn
    return pl.pallas_call(
        _mod_body,
        grid=(DEPTH, nj),
        in_specs=[pl.BlockSpec((8, D_MODEL), lambda l, j: (0, 0)),
                  pl.BlockSpec((None, D_MODEL, tn), lambda l, j: (l, 0, j)),
                  pl.BlockSpec((None, 1, tn), lambda l, j: (l, 0, j))],
        out_specs=pl.BlockSpec((None, 8, tn), lambda l, j: (l, 0, j)),
        out_shape=jax.ShapeDtypeStruct((DEPTH, 8, 6 * D_MODEL), F32),
        compiler_params=_params(),
        name="mod_all",
    )(cvec8, w_mod, b_mod.reshape(DEPTH, 1, 6 * D_MODEL))


def _filt_body(feat_ref, win_ref, f_ref, w1_ref, b1_ref, w2_ref, b2_ref, w3_ref, b3_ref,
               fr_ref, hb_ref, p_ref, q_ref, p2_ref, *, L):
    fr = fr_ref[...]
    h = jnp.sin(fr * (_dot3(feat_ref[...], w1_ref[...]) + b1_ref[...]))
    h = jnp.sin(fr * (_dot3(h, w2_ref[...]) + b2_ref[...]))
    filt = _dot3(h, w3_ref[...]) + b3_ref[...]
    win = win_ref[...]
    row = lax.broadcasted_iota(jnp.int32, (L, 1), 0)
    taps = filt * jnp.concatenate([win, win, win, win], axis=1)
    lane = lax.broadcasted_iota(jnp.int32, (L, 4 * HY_W), 1)
    taps = jnp.where((row == 0) & (lane >= 2 * HY_W), 0.0, taps)
    spec = _dot(f_ref[...], taps)
    for o in range(2):
        a = spec[:, o * HY_W:(o + 1) * HY_W]
        b = spec[:, (2 + o) * HY_W:(3 + o) * HY_W]
        kr = a[:L] + b[:L]
        ki = a[L:] - b[L:]
        nyq = a[L:L + 1] + b[L:L + 1]
        bias = hb_ref[o:o + 1, :]
        p_ref[o] = kr + bias
        q_ref[o] = jnp.where(row == 0, 0.0, ki)
        p2_ref[o] = jnp.where(row == 0, nyq, kr) + bias


def _filter_spectra(L, fw1, fb1, fw2, fb2, fw3, fb3, ffr, hy_bias):
    feat, window = _filter_tables(L)
    fwd_table, _ = _dft_tables(L)
    const = lambda shape: _resident(shape, lambda l: (0,) * len(shape))
    per_layer = lambda shape: pl.BlockSpec((None,) + shape, lambda l: (l,) + (0,) * len(shape))
    out_spec = per_layer((2, L, HY_W))
    out_shape = jax.ShapeDtypeStruct((DEPTH, 2, L, HY_W), F32)
    return pl.pallas_call(
        functools.partial(_filt_body, L=L),
        grid=(DEPTH,),
        in_specs=[const((L, 128)), const((L, HY_W)), const((2 * L, L)),
                  per_layer((128, 128)), per_layer((1, 128)), per_layer((128, 128)), per_layer((1, 128)),
                  per_layer((128, 4 * HY_W)), per_layer((1, 4 * HY_W)), per_layer((1, 128)),
                  per_layer((2, HY_W))],
        out_specs=[out_spec, out_spec, out_spec],
        out_shape=[out_shape, out_shape, out_shape],
        compiler_params=_params(),
        name="hyena_filter_%d" % L,
    )(feat, window, fwd_table, fw1, fb1, fw2, fb2, fw3, fb3, ffr, hy_bias)


SEC_W = 3 * HWP + HW
W_R0 = 0
W_H0 = W_R0 + SEC_W
W_M0 = W_H0 + 3 * HY_W
W_G0 = W_M0 + SEC_W
W_END = W_G0
GATE_LANE0 = DH
IN_W = 8 * HW + 3 * HY_W + 16
FFN_SUB = 256
ROW_SUB = 256
SCAN_UNROLL = 8


def _prep_in_body(wt_ref, o_ref):
    kb = wt_ref.shape[1]
    half = DH // 2

    def zeros(n):
        return jnp.zeros((n, kb), F32)

    def head_tile(src, rope):
        if rope:
            return jnp.concatenate([wt_ref[src:src + half, :], zeros(64 - half),
                                    wt_ref[src + half:src + DH, :], zeros(64 - half)], axis=0)
        return jnp.concatenate([wt_ref[src:src + DH, :], zeros(DHP - DH)], axis=0)

    def dense_tiles(src, width):
        return [wt_ref[src + i * 128:src + (i + 1) * 128, :] for i in range(width // 128)]

    tiles = []
    for sec in range(3):
        tiles += [head_tile(sec * HW + h * DH, sec < 2) for h in range(HEADS)]
    tiles += dense_tiles(3 * HW, HW)
    tiles += dense_tiles(4 * HW, 3 * HY_W)
    m0 = 4 * HW + 3 * HY_W
    g0 = 8 * HW + 3 * HY_W
    for sec in range(3):
        tiles += [head_tile(m0 + sec * HW + h * DH, False) for h in range(HEADS)]
    tiles[-3 * HEADS] = jnp.concatenate([wt_ref[m0:m0 + DH, :], wt_ref[g0:g0 + 16, :], zeros(DHP - DH - 16)], axis=0)
    tiles += dense_tiles(m0 + 3 * HW, HW)
    for t, tile in enumerate(tiles):
        o_ref[:, t * 128:(t + 1) * 128] = tile.T.astype(o_ref.dtype)


def _prep_w_in(w_in_t):
    kb = 256
    return pl.pallas_call(
        _prep_in_body,
        grid=(DEPTH, D_MODEL // kb),
        in_specs=[pl.BlockSpec((None, IN_W, kb), lambda l, i: (l, 0, i))],
        out_specs=pl.BlockSpec((None, kb, W_END), lambda l, i: (l, i, 0)),
        out_shape=jax.ShapeDtypeStruct((DEPTH, D_MODEL, W_END), MXU_DT),
        compiler_params=_params(),
        name="prep_w_in",
    )(w_in_t)


def _in_body(xa_ref, xb_ref, sc_ref, sh_ref, g_ref, w_ref, pr_ref, ph_ref, pm_ref, *, tiles_a):
    x = jnp.where(pl.program_id(0) < tiles_a, xa_ref[...], xb_ref[...])
    h = _rms(x, D_MODEL) * g_ref[...]
    h = (h * (1.0 + sc_ref[...]) + sh_ref[...]).astype(MXU_DT)
    proj = jnp.dot(h, w_ref[...], preferred_element_type=F32)
    pr_ref[...] = proj[:, W_R0:W_H0]
    ph_ref[...] = proj[:, W_H0:W_M0]
    pm_ref[...] = proj[:, W_M0:W_G0]


def _mod_spec(l, which, row_of_tile):
    return pl.BlockSpec((None, None, None, 1, D_MODEL), lambda i: (l, which, row_of_tile(i), 0, 0))


def _in_proj(xa, xb, mod, gain, w_in, l, row_of_tile, tm):
    tiles_a = xa.shape[0] // tm
    n = xa.shape[0] + xb.shape[0]
    shapes = [SEC_W, 3 * HY_W, SEC_W]
    return pl.pallas_call(
        functools.partial(_in_body, tiles_a=tiles_a),
        grid=(n // tm,),
        in_specs=[pl.BlockSpec((tm, D_MODEL), lambda i: (jnp.minimum(i, tiles_a - 1), 0)),
                  pl.BlockSpec((tm, D_MODEL), lambda i: (jnp.maximum(i - tiles_a, 0), 0)),
                  _mod_spec(l, 1, row_of_tile), _mod_spec(l, 0, row_of_tile),
                  pl.BlockSpec((None, 1, D_MODEL), lambda i: (l, 0, 0)),
                  _resident((None, D_MODEL, W_END), lambda i: (l, 0, 0))],
        out_specs=[pl.BlockSpec((tm, w), lambda i: (i, 0)) for w in shapes],
        out_shape=[jax.ShapeDtypeStruct((n, w), F32) for w in shapes],
        compiler_params=_params(),
        name="in_proj",
    )(xa, xb, mod, mod, gain, w_in)


def _ret_body(*refs, L, latent, nb):
    if latent:
        (q_ref, k_ref, v_ref, gate_ref, dl_ref, gain_ref, s0_ref, cos_ref, sin_ref,
         o_ref, qs, ks, acc, sst) = refs
    else:
        (q_ref, k_ref, v_ref, gate_ref, dl_ref, gain_ref, _, o_ref, sfin_ref, qs, ks, acc, sst) = refs
    half = DH // 2
    C = CHUNK
    nchunk = L // C
    lg = _log_sigmoid(dl_ref[...])
    rel = (lax.broadcasted_iota(jnp.int32, (C, C), 0) - lax.broadcasted_iota(jnp.int32, (C, C), 1)).astype(F32)
    ri = lax.broadcasted_iota(jnp.int32, (C, 1), 0).astype(F32)

    for bi in range(nb):
        seq = slice(bi * L, (bi + 1) * L)
        for h in range(HEADS):
            cols = slice(h * DHP, (h + 1) * DHP)
            q = q_ref[seq, cols]
            k = k_ref[seq, cols]
            if latent:
                cos = cos_ref[...]
                sin = sin_ref[...]
                q = q * cos + pltpu.roll(q, 64, 1) * sin
                k = k * cos + pltpu.roll(k, 64, 1) * sin
            qs[seq, cols] = q
            ks[seq, cols] = k * (DH ** -0.5)
            for d in range(2):
                sst[bi, d, h] = jnp.zeros((DHP, DHP), F32)
                if latent:
                    sst[bi, d, h, 0:half, 0:DH] = s0_ref[bi, d, h, 0:half, :]
                    sst[bi, d, h, 64:64 + half, 0:DH] = s0_ref[bi, d, h, half:DH, :]

    acc[...] = jnp.zeros_like(acc)
    consts = []
    for h in range(HEADS):
        lgf = lg[h:h + 1, :]
        lgb = lg[HEADS + h:HEADS + h + 1, :]
        lgf1 = lgf[:, 0:1]
        lgb1 = lgb[:, 0:1]
        mask = (jnp.where(rel >= 0, jnp.exp(rel * lgf), 0.0)
                + jnp.where(rel <= 0, jnp.exp(-rel * lgb), 0.0))
        consts.append(dict(
            mask=mask,
            qdec_f=jnp.exp((ri + 1.0) * lgf1), qdec_b=jnp.exp((C - ri) * lgb1),
            kdec_f=jnp.exp((C - 1.0 - ri) * lgf1), kdec_b=jnp.exp(ri * lgb1),
            cdec_f=jnp.exp(C * lgf1), cdec_b=jnp.exp(C * lgb1)))

    def step(j):
        for bi in range(nb):
            rows_f = _chunk_rows(j, C, bi * L)
            rows_b = _chunk_rows(nchunk - 1 - j, C, bi * L)
            for h in range(HEADS):
                cols = slice(h * DHP, (h + 1) * DHP)
                cn = consts[h]
                q = qs[rows_f, cols]
                k = ks[rows_f, cols]
                v = v_ref[rows_f, cols]
                s_f = sst[bi, 0, h]
                sc = _dot_nt(q, k) * cn["mask"]
                acc[rows_f, cols] += _dot(sc, v) + _dot(q, s_f) * cn["qdec_f"]
                sst[bi, 0, h] = s_f * cn["cdec_f"] + _dot_tn(k * cn["kdec_f"], v)
                q = qs[rows_b, cols]
                k = ks[rows_b, cols]
                v = v_ref[rows_b, cols]
                s_b = sst[bi, 1, h]
                acc[rows_b, cols] += _dot(q, s_b) * cn["qdec_b"]
                sst[bi, 1, h] = s_b * cn["cdec_b"] + _dot_tn(k * cn["kdec_b"], v)

    if nchunk <= 2:
        for j in range(nchunk):
            step(j)
    else:
        pl.loop(0, nchunk, unroll=SCAN_UNROLL)(step)

    if not latent:
        for bi in range(nb):
            for d in range(2):
                for h in range(HEADS):
                    sfin_ref[bi, d, h, 0:half, :] = sst[bi, d, h, 0:half, 0:DH]
                    sfin_ref[bi, d, h, half:DH, :] = sst[bi, d, h, 64:64 + half, 0:DH]

    y = jnp.concatenate([(_rms(acc[:, h * DHP:(h + 1) * DHP], DH) * gain_ref[:, h * DHP:(h + 1) * DHP])[:, :DH]
                         for h in range(HEADS)], axis=1)
    g = gate_ref[...]
    o_ref[...] = (y * (g * (1.0 / (1.0 + jnp.exp(-g))))).astype(o_ref.dtype)


def _state_spec(l, nb, *tail):
    shape = (nb, None, 2, HEADS) + tail
    return pl.BlockSpec(shape, lambda b: (b, l) + (0,) * (len(shape) - 2))


def _seqs_per_step(B, L):
    return min(B, max(1, 512 // L))


def _retention(pr, dl, gain, l, B, L, latent, state, row0):
    nb = _seqs_per_step(B, L)
    off = row0 // (nb * L)
    blk = lambda j: pl.BlockSpec((nb * L, HWP), lambda b, j=j: (b + off, j))
    gate_blk = pl.BlockSpec((nb * L, HW), lambda b: (b + off, 3 * HWP // HW))
    in_specs = [blk(0), blk(1), blk(2), gate_blk,
                pl.BlockSpec((None, 8, 128), lambda b: (l, 0, 0)),
                pl.BlockSpec((None, 1, HWP), lambda b: (l, 0, 0))]
    args = [pr, pr, pr, pr, dl, gain]
    out_specs = [pl.BlockSpec((nb * L, HW), lambda b: (b, 0))]
    out_shape = [jax.ShapeDtypeStruct((B * L, HW), MXU_DT)]
    aliases = {}
    if latent:
        cos, sin = _rope_tables(L)
        in_specs += [_state_spec(l, nb, DH, DH),
                     pl.BlockSpec((L, DHP), lambda b: (0, 0)), pl.BlockSpec((L, DHP), lambda b: (0, 0))]
        args += [state, cos, sin]
    else:
        in_specs.append(pl.BlockSpec(memory_space=pl.ANY))
        args.append(state)
        aliases = {len(args) - 1: 1}
        out_specs.append(_state_spec(l, nb, DH, DH))
        out_shape.append(jax.ShapeDtypeStruct(state.shape, F32))
    return pl.pallas_call(
        functools.partial(_ret_body, L=L, latent=latent, nb=nb),
        grid=(B // nb,),
        in_specs=in_specs, out_specs=out_specs, out_shape=out_shape,
        input_output_aliases=aliases,
        scratch_shapes=[pltpu.VMEM((nb * L, HWP), F32), pltpu.VMEM((nb * L, HWP), F32),
                        pltpu.VMEM((nb * L, HWP), F32), pltpu.VMEM((nb, 2, HEADS, DHP, DHP), F32)],
        compiler_params=_params(),
        name="retention_%d" % L,
    )(*args)


def _ml_body(*refs, L, latent, nb):
    if latent:
        (q_ref, k_ref, v_ref, og_ref, gb_ref, gain_ref, tl_ref, tu_ref, c0_ref, n0_ref, m0_ref,
         o_ref, acc, bb_s, ub_s, pm_s, brow_s, ibrow_s, mc_s, mx_s, cn_s, gate_s) = refs
    else:
        (q_ref, k_ref, v_ref, og_ref, gb_ref, gain_ref, tl_ref, tu_ref, _, _,
         o_ref, cfin_ref, nfin_ref, mfin_ref, acc, bb_s, ub_s, pm_s, brow_s, ibrow_s, mc_s, mx_s, cn_s,
         gate_s) = refs
    C = CHUNK
    nchunk = L // C
    lane = lax.broadcasted_iota(jnp.int32, (C, DHP), 1)
    tri_r = lax.broadcasted_iota(jnp.int32, (C, C), 0)
    tri_c = lax.broadcasted_iota(jnp.int32, (C, C), 1)
    tl = tl_ref[...]
    tu = tu_ref[...]

    row_i = lax.broadcasted_iota(jnp.int32, (C, 128), 0)

    def cummax_rows(x, suffix):
        s = 1
        while s < C:
            if suffix:
                x = jnp.where(row_i < C - s, jnp.maximum(x, pltpu.roll(x, C - s, 0)), x)
            else:
                x = jnp.where(row_i >= s, jnp.maximum(x, pltpu.roll(x, s, 0)), x)
            s *= 2
        return x

    lane_all = lax.broadcasted_iota(jnp.int32, (nb * L, DHP), 1)
    gate_s[...] = jnp.where(lane_all < 16, pltpu.roll(q_ref[:, 0:DHP], DHP - GATE_LANE0, 1), 0.0)

    for c in range(nb * nchunk):
        rows = slice(c * C, (c + 1) * C)
        g = gate_s[rows, :] + gb_ref[...]
        lf = _log_sigmoid(g)
        pre = _dot_exact_lhs(tl, lf)
        suf = _dot_exact_lhs(tu, lf)
        bc = jnp.where(lane < 8, pre, suf)
        brow_s[c * 16:(c + 1) * 16, :] = bc.T[0:16, :]
        ibrow_s[c * 16:(c + 1) * 16, :] = g.T[0:16, :]
        for d in range(2):
            for h in range(HEADS):
                sr = d * HEADS + h
                b_b = jnp.broadcast_to(bc[:, d * 8 + 4 + h:d * 8 + 5 + h], (C, 128))
                u_b = jnp.broadcast_to(g[:, d * 8 + h:d * 8 + h + 1], (C, 128)) - b_b
                bb_s[sr, rows, :] = b_b
                ub_s[sr, rows, :] = u_b
                pm_s[sr, rows, :] = cummax_rows(u_b, suffix=(d == 1))

    for bi in range(nb):
        for d in range(2):
            for h in range(HEADS):
                sr = d * HEADS + h
                m = m0_ref[bi, sr:sr + 1, :] if latent else jnp.zeros((1, 128), F32)
                for j in range(nchunk):
                    c = bi * nchunk + (j if d == 0 else nchunk - 1 - j)
                    last = c * C + (C - 1 if d == 0 else 0)
                    mx = jnp.maximum(m, pm_s[sr, last:last + 1, :])
                    mc_s[c * 8 + sr:c * 8 + sr + 1, :] = m
                    mx_s[c * 8 + sr:c * 8 + sr + 1, :] = mx
                    m = bb_s[sr, last:last + 1, :] + mx
                if not latent:
                    mfin_ref[bi, sr:sr + 1, :] = m

    acc[...] = jnp.zeros_like(acc)
    cn_s[...] = jnp.zeros_like(cn_s)
    if latent:
        for bi in range(nb):
            for d in range(2):
                for h in range(HEADS):
                    cn_s[bi, d, h, 0:DH, 0:DH] = c0_ref[bi, d, h]
                    n_row = jnp.concatenate([n0_ref[bi, d, h:h + 1, :], jnp.zeros((1, DHP - DH), F32)], axis=1)
                    cn_s[bi, d, h, :, DHP:] = jnp.broadcast_to(n_row, (DHP, DHP)).T
    ones_blk = jnp.ones((C, DHP), MXU_DT)

    def chain(bi, d, h, c):
        rows = _chunk_rows(c, C)
        brs = brow_s[_chunk_rows(c, 16), :]
        ibrs = ibrow_s[_chunk_rows(c, 16), :]
        mcs = mc_s[_chunk_rows(c, 8), :]
        mxs = mx_s[_chunk_rows(c, 8), :]
        causal = (tri_r >= tri_c) if d == 0 else (tri_r <= tri_c)
        cols = slice(h * DHP, (h + 1) * DHP)
        ic = d * 8 + h
        fc = d * 8 + 4 + h
        sr = d * HEADS + h
        m_c = mcs[sr:sr + 1, :]
        mx = mxs[sr:sr + 1, :]
        q = q_ref[rows, cols].astype(MXU_DT)
        k = k_ref[rows, cols] * (DH ** -0.5)
        v1 = jnp.concatenate([v_ref[rows, cols].astype(MXU_DT), ones_blk], axis=1)
        cn = cn_s[bi, d, h]
        urow = ibrs[ic:ic + 1, :] - brs[fc:fc + 1, :]
        mb = jnp.maximum(m_c, pm_s[sr, rows, :])
        w_intra = jnp.exp(jnp.where(causal, urow - mb, -jnp.inf))
        w_inter = jnp.exp(m_c - mb)
        s = _dot_nt(q, k) * w_intra
        tot = (jnp.dot(s.astype(MXU_DT), v1, preferred_element_type=F32)
               + jnp.concatenate([w_inter, w_inter], axis=1) * _dot(q, cn))
        den = tot[:, DHP:]
        floor = jnp.exp(-(bb_s[sr, rows, :] + mb))
        acc[rows, cols] += tot[:, :DHP] * (1.0 / jnp.maximum(jnp.abs(den), floor))
        kw = k * jnp.exp(ub_s[sr, rows, :] - mx)
        w_prev = jnp.exp(m_c - mx)
        cn_s[bi, d, h] = jnp.concatenate([w_prev, w_prev], axis=1) * cn + _dot_tn(kw, v1)

    def step(j):
        for bi in range(nb):
            for d in range(2):
                c = bi * nchunk + (j if d == 0 else nchunk - 1 - j)
                for h in range(HEADS):
                    chain(bi, d, h, c)

    if nchunk <= 2:
        for j in range(nchunk):
            step(j)
    else:
        pl.loop(0, nchunk, unroll=SCAN_UNROLL)(step)

    if not latent:
        for bi in range(nb):
            for d in range(2):
                for h in range(HEADS):
                    cfin_ref[bi, d, h] = cn_s[bi, d, h, 0:DH, 0:DH]
                    nfin_ref[bi, d, h:h + 1, :] = cn_s[bi, d, h, :, DHP:].T[0:1, 0:DH]

    y = jnp.concatenate([(_rms(acc[:, h * DHP:(h + 1) * DHP], DH) * gain_ref[:, h * DHP:(h + 1) * DHP])[:, :DH]
                         for h in range(HEADS)], axis=1)
    o_ref[...] = (y * (1.0 / (1.0 + jnp.exp(-og_ref[...])))).astype(o_ref.dtype)


def _mlstm(pm, gb, gain, l, B, L, latent, c_state, n_state, row0, m0=None):
    tl, tu = _tri_tables()
    tl = tl.astype(MXU_DT)
    tu = tu.astype(MXU_DT)
    nb = _seqs_per_step(B, L)
    off = row0 // (nb * L)
    blk = lambda j: pl.BlockSpec((nb * L, HWP), lambda b, j=j: (b + off, j))
    gate_blk = pl.BlockSpec((nb * L, HW), lambda b: (b + off, 3 * HWP // HW))
    in_specs = [blk(0), blk(1), blk(2), gate_blk,
                pl.BlockSpec((None, 1, 128), lambda b: (l, 0, 0)),
                pl.BlockSpec((None, 1, HWP), lambda b: (l, 0, 0)),
                pl.BlockSpec((CHUNK, CHUNK), lambda b: (0, 0)),
                pl.BlockSpec((CHUNK, CHUNK), lambda b: (0, 0))]
    args = [pm, pm, pm, pm, gb, gain, tl, tu]
    out_specs = [pl.BlockSpec((nb * L, HW), lambda b: (b, 0))]
    out_shape = [jax.ShapeDtypeStruct((B * L, HW), MXU_DT)]
    aliases = {}
    if latent:
        in_specs += [_state_spec(l, nb, DH, DH), _state_spec(l, nb, DH),
                     pl.BlockSpec((nb, None, 8, 128), lambda b: (b, l, 0, 0))]
        args += [c_state, n_state, m0]
    else:
        in_specs += [pl.BlockSpec(memory_space=pl.ANY), pl.BlockSpec(memory_space=pl.ANY)]
        args += [c_state, n_state]
        aliases = {len(args) - 2: 1, len(args) - 1: 2}
        out_specs += [_state_spec(l, nb, DH, DH), _state_spec(l, nb, DH),
                      pl.BlockSpec((nb, 8, 128), lambda b: (b, 0, 0))]
        out_shape += [jax.ShapeDtypeStruct(c_state.shape, F32), jax.ShapeDtypeStruct(n_state.shape, F32),
                      jax.ShapeDtypeStruct((B, 8, 128), F32)]
    nchunk = L // CHUNK
    return pl.pallas_call(
        functools.partial(_ml_body, L=L, latent=latent, nb=nb),
        grid=(B // nb,),
        in_specs=in_specs, out_specs=out_specs, out_shape=out_shape,
        input_output_aliases=aliases,
        scratch_shapes=[pltpu.VMEM((nb * L, HWP), F32),
                        pltpu.VMEM((8, nb * L, 128), F32), pltpu.VMEM((8, nb * L, 128), F32),
                        pltpu.VMEM((8, nb * L, 128), F32),
                        pltpu.VMEM((nb * nchunk * 16, 128), F32), pltpu.VMEM((nb * nchunk * 16, 128), F32),
                        pltpu.VMEM((nb * nchunk * 8, 128), F32), pltpu.VMEM((nb * nchunk * 8, 128), F32),
                        pltpu.VMEM((nb, 2, HEADS, DHP, 2 * DHP), F32), pltpu.VMEM((nb * L, 128), F32)],
        compiler_params=_params(),
        name="mlstm_%d" % L,
    )(*args)


def _shift_rows(x, seq_len, n_rows):
    pos = lax.broadcasted_iota(jnp.int32, (n_rows, 1), 0) & (seq_len - 1)
    prev = jnp.where(pos == 0, 0.0, pltpu.roll(x, 1, 0))
    nxt = jnp.where(pos == seq_len - 1, 0.0, pltpu.roll(x, n_rows - 1, 0))
    return prev, nxt


def _hy_body(p_ref, cw_ref, cb_ref, f_ref, g_ref, ps_ref, qs_ref, p2s_ref, o_ref, fh_ref, gh_ref, *, L):
    @pl.when(pl.program_id(0) == 0)
    def _():
        fh_ref[...] = f_ref[...].astype(MXU_DT)
        gh_ref[...] = g_ref[...].astype(MXU_DT)

    def conv(z, o):
        spec = _dot(fh_ref[...], z)
        xr = spec[:L]
        xi = spec[L:]
        p = ps_ref[o]
        q = qs_ref[o]
        yr = xr * p - xi * q
        yi = xr * q + xi * p2s_ref[o]
        return _dot(gh_ref[...], jnp.concatenate([yr, yi], axis=0))

    for i in range(p_ref.shape[0] // L):
        rows = slice(i * L, (i + 1) * L)
        x = p_ref[rows, :]
        prev, nxt = _shift_rows(x, L, L)
        hy = prev * cw_ref[0:1, :] + x * cw_ref[1:2, :] + nxt * cw_ref[2:3, :] + cb_ref[...]
        hv = hy[:, 0:HY_W]
        hx1 = hy[:, HY_W:2 * HY_W]
        hx2 = hy[:, 2 * HY_W:3 * HY_W]
        z = hx1 * conv(hv, 0)
        o_ref[rows, :] = (hx2 * conv(z, 1)).astype(o_ref.dtype)


def _hyena(ph, cw, cb, spectra, l, B, L, row0):
    fwd_table, inv_table = _dft_tables(L)
    ps, qs, p2s = spectra
    const = lambda shape: _resident(shape, lambda b: (0,) * len(shape))
    spec_blk = _resident((None, 2, L, HY_W), lambda b: (l, 0, 0, 0))
    nb = min(B, max(1, 2048 // L))
    off = row0 // (nb * L)
    return pl.pallas_call(
        functools.partial(_hy_body, L=L),
        grid=(B // nb,),
        in_specs=[pl.BlockSpec((nb * L, 3 * HY_W), lambda b: (b + off, 0)),
                  pl.BlockSpec((None, 8, 3 * HY_W), lambda b: (l, 0, 0)),
                  pl.BlockSpec((None, 1, 3 * HY_W), lambda b: (l, 0, 0)),
                  const((2 * L, L)), const((L, 2 * L)),
                  spec_blk, spec_blk, spec_blk],
        out_specs=pl.BlockSpec((nb * L, HY_W), lambda b: (b, 0)),
        out_shape=jax.ShapeDtypeStruct((B * L, HY_W), MXU_DT),
        scratch_shapes=[pltpu.VMEM((2 * L, L), MXU_DT), pltpu.VMEM((L, 2 * L), MXU_DT)],
        compiler_params=_params(),
        name="hyena_%d" % L,
    )(ph, cw, cb, fwd_table, inv_table, ps, qs, p2s)


def _out_body(mr_ref, mh_ref, mm_ref, w_ref, x_ref, g1_ref, sc2_ref, sh2_ref,
              gpost_ref, gpre_ref, x1_ref, h2_ref):
    w = w_ref[...].astype(MXU_DT)
    for r in range(x_ref.shape[0] // ROW_SUB):
        rs = slice(r * ROW_SUB, (r + 1) * ROW_SUB)
        mixed = jnp.concatenate([mr_ref[rs, :], mh_ref[rs, :], mm_ref[rs, :]], axis=1)
        mix = jnp.dot(mixed, w, preferred_element_type=F32)
        x1 = x_ref[rs, :] + g1_ref[...] * (_rms(mix, D_MODEL) * gpost_ref[...])
        x1_ref[rs, :] = x1
        h2 = _rms(x1, D_MODEL) * gpre_ref[...] * (1.0 + sc2_ref[...]) + sh2_ref[...]
        h2_ref[rs, :] = h2.astype(h2_ref.dtype)


def _ffn_body(mr_ref, mh_ref, mm_ref, wo_ref, x_ref, g1_ref, sc2_ref, sh2_ref, gmix_ref, gpre_ref,
              wa_ref, wb_ref, cw_ref, cb_ref, wd_ref, g2_ref, gpost_ref, o_ref, x1_ref, h_ref, *, seq_len, tm):
    j = pl.program_id(1)

    @pl.when(j == 0)
    def _():
        _out_body(mr_ref, mh_ref, mm_ref, wo_ref, x_ref, g1_ref, sc2_ref, sh2_ref,
                  gmix_ref, gpre_ref, x1_ref, h_ref)
        o_ref[...] = jnp.zeros_like(o_ref)

    h = h_ref[...]
    part = None
    for s in range(wa_ref.shape[1] // FFN_SUB):
        cs = slice(s * FFN_SUB, (s + 1) * FFN_SUB)
        a = jnp.dot(h, wa_ref[:, cs].astype(MXU_DT), preferred_element_type=F32)
        b = jnp.dot(h, wb_ref[:, cs].astype(MXU_DT), preferred_element_type=F32)
        prev, nxt = _shift_rows(a, seq_len, tm)
        a = prev * cw_ref[0:1, cs] + a * cw_ref[1:2, cs] + nxt * cw_ref[2:3, cs] + cb_ref[:, cs]
        gelu = 0.5 * a * (1.0 + jnp.tanh(math.sqrt(2.0 / math.pi) * (a + 0.044715 * (a * a * a))))
        p = jnp.dot((gelu * b).astype(MXU_DT), wd_ref[cs, :].astype(MXU_DT), preferred_element_type=F32)
        part = p if part is None else part + p
    o_ref[...] += part

    @pl.when(j == pl.num_programs(1) - 1)
    def _():
        o_ref[...] = x1_ref[...] + g2_ref[...] * (_rms(o_ref[...], D_MODEL) * gpost_ref[...])


def _mix_ffn(mr, mh, mm, w_out, x, mod, gmix, gpre, w_up, cw, cb, w_down, gpost, l, row_of_tile,
             seq_len, tm, tf):
    n = x.shape[0]
    nf = D_FF // tf
    tile = lambda w: pl.BlockSpec((tm, w), lambda i, j: (i, 0))
    layer = lambda r, c: pl.BlockSpec((None, r, c), lambda i, j: (l, 0, 0))
    mod_row = lambda which: pl.BlockSpec((None, None, None, 1, D_MODEL),
                                         lambda i, j: (l, which, row_of_tile(i), 0, 0))
    return pl.pallas_call(
        functools.partial(_ffn_body, seq_len=seq_len, tm=tm),
        grid=(n // tm, nf),
        in_specs=[tile(HW), tile(HY_W), tile(HW),
                  _resident((None, D_MODEL, D_MODEL), lambda i, j: (l, 0, 0)),
                  tile(D_MODEL), mod_row(2), mod_row(4), mod_row(3),
                  layer(1, D_MODEL), layer(1, D_MODEL),
                  pl.BlockSpec((None, D_MODEL, tf), lambda i, j: (l, 0, j)),
                  pl.BlockSpec((None, D_MODEL, tf), lambda i, j: (l, 0, j + nf)),
                  pl.BlockSpec((None, 8, tf), lambda i, j: (l, 0, j)),
                  pl.BlockSpec((None, 1, tf), lambda i, j: (l, 0, j)),
                  pl.BlockSpec((None, tf, D_MODEL), lambda i, j: (l, j, 0)),
                  mod_row(5), layer(1, D_MODEL)],
        out_specs=pl.BlockSpec((tm, D_MODEL), lambda i, j: (i, 0)),
        out_shape=jax.ShapeDtypeStruct((n, D_MODEL), F32),
        scratch_shapes=[pltpu.VMEM((tm, D_MODEL), F32), pltpu.VMEM((tm, D_MODEL), MXU_DT)],
        compiler_params=_params(),
        name="mix_ffn",
    )(mr, mh, mm, w_out, x, mod, mod, mod, gmix, gpre, w_up, w_up, cw, cb, w_down, mod, gpost)


def _pad_rows8(w):
    return jnp.concatenate([w, jnp.zeros((w.shape[0], 8 - w.shape[1], w.shape[2]), w.dtype)], axis=1)


def kernel(x_prompt, x_sample, c, state_ret, state_mlstm_c, state_mlstm_n, state_mlstm_m, c_ctx,
           norm_mix_pre, norm_mix_post, norm_ffn_pre, norm_ffn_post, w_mod, b_mod, w_in, w_out,
           ret_decay_logit, ret_norm_g, hy_conv_w, hy_conv_b, hy_f_w1, hy_f_b1, hy_f_w2, hy_f_b2,
           hy_f_w3, hy_f_b3, hy_sin_freq, hy_bias, ml_gate_bias, ml_norm_g,
           w_up, ffn_conv_w, ffn_conv_b, w_down):
    BP, LP, _ = x_prompt.shape
    BS, LS, _ = x_sample.shape

    w_in_p = _prep_w_in(jnp.swapaxes(w_in, 1, 2))
    ret_gain = _pad_heads(ret_norm_g).reshape(DEPTH, 1, HWP)
    ml_gain = _pad_heads(ml_norm_g).reshape(DEPTH, 1, HWP)
    dl = jnp.broadcast_to(ret_decay_logit.reshape(DEPTH, 8, 1), (DEPTH, 8, 128))
    gate_bias = jnp.concatenate([ml_gate_bias.reshape(DEPTH, 1, 16), jnp.zeros((DEPTH, 1, 112), F32)], axis=2)
    hy_cw = _pad_rows8(hy_conv_w)
    hy_cb = hy_conv_b.reshape(DEPTH, 1, 3 * HY_W)
    ffn_cw = _pad_rows8(ffn_conv_w)
    ffn_cb = ffn_conv_b.reshape(DEPTH, 1, D_FF)
    g_mix_pre = norm_mix_pre.reshape(DEPTH, 1, D_MODEL)
    g_mix_post = norm_mix_post.reshape(DEPTH, 1, D_MODEL)
    g_ffn_pre = norm_ffn_pre.reshape(DEPTH, 1, D_MODEL)
    g_ffn_post = norm_ffn_post.reshape(DEPTH, 1, D_MODEL)
    pad2 = lambda a, r, cdim: jnp.pad(a, ((0, 0), (0, r - a.shape[1]), (0, cdim - a.shape[2])))
    fw1 = pad2(hy_f_w1, 128, 128)
    fb1 = pad2(hy_f_b1.reshape(DEPTH, 1, -1), 1, 128)
    fw2 = pad2(hy_f_w2, 128, 128)
    fb2 = pad2(hy_f_b2.reshape(DEPTH, 1, -1), 1, 128)
    fw3 = pad2(hy_f_w3, 128, 4 * HY_W)
    fb3 = hy_f_b3.reshape(DEPTH, 1, 4 * HY_W)
    ffr = pad2(hy_sin_freq.reshape(DEPTH, 1, -1), 1, 128)

    m0 = jnp.broadcast_to(state_mlstm_m.reshape(BS, DEPTH, 8, 1), (BS, DEPTH, 8, 128))
    new_ret = jnp.zeros((BP, DEPTH, 2, HEADS, DH, DH), F32)
    new_c = jnp.zeros((BP, DEPTH, 2, HEADS, DH, DH), F32)
    new_n = jnp.zeros((BP, DEPTH, 2, HEADS, DH), F32)

    cvec8 = jnp.concatenate([c_ctx.reshape(1, D_MODEL), c, jnp.zeros((8 - 1 - BS, D_MODEL), F32)], axis=0)
    mod = _mod_all(cvec8, w_mod, b_mod)
    mod = mod.reshape(DEPTH, 8, 6, 1, D_MODEL).transpose(0, 2, 1, 3, 4)
    spectra_p = _filter_spectra(LP, fw1, fb1, fw2, fb2, fw3, fb3, ffr, hy_bias)
    spectra_s = _filter_spectra(LS, fw1, fb1, fw2, fb2, fw3, fb3, ffr, hy_bias)

    xp = x_prompt.reshape(BP * LP, D_MODEL)
    xs = x_sample.reshape(BS * LS, D_MODEL)
    tm = 512
    ffn_tm = 1024
    ffn_tf = 512
    n_ctx = BP * LP
    row_ctx = lambda i: 0
    row_both = lambda i: jnp.where(i < n_ctx // tm, 0, 1 + jnp.maximum(i - n_ctx // tm, 0) // (LS // tm))
    row_lat_ffn = lambda i: 1 + i // (LS // ffn_tm)
    new_m = []

    def mix_and_ffn(x, ph, mr, mm, l, B, L, row0, row_ffn, spectra):
        mh = _hyena(ph, hy_cw, hy_cb, spectra, l, B, L, row0)
        return _mix_ffn(mr, mh, mm, w_out, x, mod, g_mix_post, g_ffn_pre,
                        w_up, ffn_cw, ffn_cb, w_down, g_ffn_post, l, row_ffn, L, ffn_tm, ffn_tf)

    for l in range(DEPTH):
        pr, ph, pm = _in_proj(xp, xs, mod, g_mix_pre, w_in_p, l, row_both, tm)
        mr, new_ret = _retention(pr, dl, ret_gain, l, BP, LP, False, new_ret, 0)
        mm, new_c, new_n, m_fin = _mlstm(pm, gate_bias, ml_gain, l, BP, LP, False, new_c, new_n, 0)
        new_m.append(m_fin)
        xp = mix_and_ffn(xp, ph, mr, mm, l, BP, LP, 0, row_ctx, spectra_p)
        (mr,) = _retention(pr, dl, ret_gain, l, BS, LS, True, state_ret, n_ctx)
        (mm,) = _mlstm(pm, gate_bias, ml_gain, l, BS, LS, True, state_mlstm_c, state_mlstm_n, n_ctx, m0)
        xs = mix_and_ffn(xs, ph, mr, mm, l, BS, LS, n_ctx, row_lat_ffn, spectra_s)

    out_m = jnp.stack(new_m, axis=1)[..., 0].reshape(BP, DEPTH, 2, HEADS)
    return (xp.reshape(BP, LP, D_MODEL), xs.reshape(BS, LS, D_MODEL), new_ret, new_c, new_n, out_m)
```
